```python
import math
import jax
import jax.numpy as jnp
from jax import lax
import numpy as np

D_MODEL = 1024
BATCH = 2
SEQ = 8192
DEPTH = 1
DEC_BATCH = 128
DEC_SEQ = 1
PAST_LEN = 8192
PAGE_SIZE = 128

HEAD_DIM = 64
N_HEADS_A = 8
N_KV_A = 2
HEADS_PER_GROUP = N_HEADS_A // N_KV_A
CMP_BLOCK = 32
CMP_STRIDE = 16
CMP_RATIO = CMP_BLOCK // CMP_STRIDE
CMP_HIDDEN = 128
SLC_BLOCK = 64
SLC_PER_CMP = SLC_BLOCK // CMP_STRIDE
N_SLC = 16
N_LOCAL_SLC = 2
WINDOW = 512
N_HEADS_B = 8
Q_LORA = 192
KV_LORA = 128
QK_NOPE = 64
QK_ROPE = 32
V_DIM = 64
LATENT_DIM = KV_LORA + QK_ROPE
ROPE_THETA = 10000.0
MLA_SCALE = (QK_NOPE + QK_ROPE) ** -0.5
N_BUCKETS = 32
MAX_DISTANCE = 128
W_Q_A = N_HEADS_A * HEAD_DIM
W_KV_A = 6 * N_KV_A * HEAD_DIM
W_GATE_A = 3 * N_HEADS_A
IN_DIM = W_Q_A + W_KV_A + W_GATE_A + Q_LORA + LATENT_DIM
MIX_DIM = N_HEADS_A * HEAD_DIM + N_HEADS_B * V_DIM
N_EXPERTS = 256
TOP_K = 8
N_GROUPS = 8
TOP_GROUPS = 4
D_EXPERT = 256
D_SHARED = 256
ROUTED_SCALE = 2.5
MOE_BLOCK = 128
Q_BLOCK = 128
EPS = 1e-6
NEG = -1e30
FORCED = 1e30

kernel_name = 'hybrid_nsa_mla_moe_step'


def rmsnorm(x, g):
    x32 = x.astype(jnp.float32)
    y = x32 * lax.rsqrt(jnp.mean(x32 * x32, axis=-1, keepdims=True) + EPS) * g.astype(jnp.float32)
    return y.astype(x.dtype)


def adaln(c, w_ada, b_ada):
    return (jax.nn.silu(c) @ w_ada + b_ada).reshape(c.shape[0], 6, -1)


def modulate(h, shift, scale):
    return h * (1 + scale[:, None, :]) + shift[:, None, :]


def t5_bucket(rel):
    max_exact = N_BUCKETS // 2
    n = jnp.maximum(rel, 0)
    nf = jnp.maximum(n, 1).astype(jnp.float32)
    large = max_exact + (jnp.log(nf / max_exact) / math.log(MAX_DISTANCE / max_exact)
                         * (N_BUCKETS - max_exact)).astype(jnp.int32)
    large = jnp.minimum(large, N_BUCKETS - 1)
    return jnp.where(n < max_exact, n, large)


def head_bias(rel_bias, rel):
    b = rel_bias.astype(jnp.float32)[t5_bucket(rel)]
    return jnp.moveaxis(b, -1, 0).reshape(N_KV_A, HEADS_PER_GROUP, *rel.shape)


def rope(x, pos):
    half = QK_ROPE // 2
    inv = ROPE_THETA ** (-jnp.arange(half, dtype=jnp.float32) / half)
    ang = pos.astype(jnp.float32)[:, None] * inv[None, :]
    cos = jnp.cos(ang)[:, None, :]
    sin = jnp.sin(ang)[:, None, :]
    x32 = x.astype(jnp.float32)
    x1, x2 = x32[..., :half], x32[..., half:]
    return jnp.concatenate([x1 * cos - x2 * sin, x2 * cos + x1 * sin], -1).astype(x.dtype)


def split_proj(p):
    B, T = p.shape[:2]
    o1 = W_Q_A
    o2 = o1 + W_KV_A
    o3 = o2 + W_GATE_A
    o4 = o3 + Q_LORA
    q_a = p[..., :o1].reshape(B, T, N_HEADS_A, HEAD_DIM)
    kv_a = p[..., o1:o2].reshape(B, T, 6, N_KV_A, HEAD_DIM)
    gates = jax.nn.sigmoid(p[..., o2:o3]).reshape(B, T, 3, N_KV_A, HEADS_PER_GROUP)
    return q_a, kv_a, gates, p[..., o3:o4], p[..., o4:]


def compress_kv(k, v, cmp_pe, cmp_w1, cmp_w2):
    B, L = k.shape[:2]
    n_cmp = (L - CMP_BLOCK) // CMP_STRIDE + 1
    n_chunk = n_cmp + CMP_RATIO - 1

    def phi(x, j):
        ch = x[:, :n_chunk * CMP_STRIDE].reshape(B, n_chunk, CMP_STRIDE, N_KV_A, HEAD_DIM)
        ch = ch.transpose(0, 1, 3, 2, 4).reshape(B, n_chunk, N_KV_A, CMP_STRIDE * HEAD_DIM)
        w1 = cmp_w1[j].reshape(CMP_RATIO, CMP_STRIDE * HEAD_DIM, CMP_HIDDEN)
        hid = cmp_pe[j].reshape(-1) @ cmp_w1[j]
        for r in range(CMP_RATIO):
            hid = hid + ch[:, r:r + n_cmp] @ w1[r]
        return jax.nn.silu(hid) @ cmp_w2[j]

    c_end = jnp.arange(n_cmp) * CMP_STRIDE + CMP_BLOCK - 1
    return phi(k, 0), phi(v, 1), c_end


def nsa_core(q, gates, q_pos, kc, vc, c_end, gather_slc, n_sblk, kw, vw, w_pos, rel_bias):
    B, Tq = q.shape[:2]
    qg = q.reshape(B, Tq, N_KV_A, HEADS_PER_GROUP, HEAD_DIM) * (HEAD_DIM ** -0.5)
    rel_c = q_pos[:, None] - c_end[None, :]
    valid_c = rel_c >= 0
    s_c = jnp.einsum('btghd,bngd->bghtn', qg, kc).astype(jnp.float32) + head_bias(rel_bias, rel_c)
    p_c = jnp.where(valid_c, jax.nn.softmax(jnp.where(valid_c, s_c, NEG), axis=-1), 0.0)
    o_c = jnp.einsum('bghtn,bngd->btghd', p_c.astype(vc.dtype), vc)
    imp = p_c.sum(axis=2)
    n_cmp = imp.shape[-1]
    imp = jnp.pad(imp, ((0, 0), (0, 0), (0, 0), (0, n_sblk * SLC_PER_CMP - n_cmp)))
    imp = imp.reshape(B, N_KV_A, Tq, n_sblk, SLC_PER_CMP).sum(-1)
    j = jnp.arange(n_sblk)
    dist = (q_pos // SLC_BLOCK)[:, None] - j[None, :]
    forced = (j[None, :] == 0) | ((dist >= 0) & (dist < N_LOCAL_SLC))
    score = jnp.where(dist >= 0, jnp.where(forced, FORCED, imp), NEG)
    _, idx = lax.top_k(score, min(N_SLC, n_sblk))
    idx = idx.transpose(0, 2, 1, 3)
    kb, vb = gather_slc(idx)
    kpos = idx[..., None] * SLC_BLOCK + jnp.arange(SLC_BLOCK)
    rel_s = q_pos[None, :, None, None, None] - kpos
    tbl = rel_bias.astype(jnp.float32).reshape(N_BUCKETS, N_KV_A, HEADS_PER_GROUP)
    bias_s = tbl[t5_bucket(rel_s), jnp.arange(N_KV_A)[None, None, :, None, None]]
    s_s = jnp.einsum('btghd,btgkjd->btghkj', qg, kb).astype(jnp.float32) + jnp.moveaxis(bias_s, -1, 3)
    s_s = jnp.where((rel_s >= 0)[:, :, :, None], s_s, NEG)
    sh = s_s.shape
    p_s = jax.nn.softmax(s_s.reshape(*sh[:4], -1), axis=-1).reshape(sh)
    o_s = jnp.einsum('btghkj,btgkjd->btghd', p_s.astype(vb.dtype), vb)
    rel_w = q_pos[:, None] - w_pos[None, :]
    valid_w = (rel_w >= 0) & (rel_w < WINDOW) & (w_pos[None, :] >= 0)
    s_w = jnp.einsum('btghd,bsgd->bghts', qg, kw).astype(jnp.float32) + head_bias(rel_bias, rel_w)
    p_w = jax.nn.softmax(jnp.where(valid_w, s_w, NEG), axis=-1)
    o_w = jnp.einsum('bghts,bsgd->btghd', p_w.astype(vw.dtype), vw)
    out = (gates[:, :, 0][..., None] * o_c + gates[:, :, 1][..., None] * o_s
           + gates[:, :, 2][..., None] * o_w)
    return out.reshape(B, Tq, N_HEADS_A * HEAD_DIM)


def nsa_prompt(q_a, kv_a, gates, cmp_pe, cmp_w1, cmp_w2, rel_bias):
    B, S = q_a.shape[:2]
    kc, vc, c_end = compress_kv(kv_a[:, :, 0], kv_a[:, :, 1], cmp_pe, cmp_w1, cmp_w2)
    n_sblk = S // SLC_BLOCK
    slc_r = kv_a[:, :, 2:4].reshape(B, n_sblk, SLC_BLOCK, 2, N_KV_A, HEAD_DIM).transpose(0, 4, 1, 2, 3, 5)
    bidx = jnp.arange(B)[:, None, None, None]
    gidx = jnp.arange(N_KV_A)[None, None, :, None]

    def gather_slc(idx):
        kv = slc_r[bidx, gidx, idx]
        return kv[..., 0, :], kv[..., 1, :]

    win_pad = jnp.pad(kv_a[:, :, 4:6], ((0, 0), (WINDOW, 0), (0, 0), (0, 0), (0, 0)))
    n_qb = S // Q_BLOCK

    def block(args):
        qb, gb, t0 = args
        q_pos = t0 + jnp.arange(Q_BLOCK)
        win = lax.dynamic_slice_in_dim(win_pad, t0, WINDOW + Q_BLOCK, axis=1)
        w_pos = t0 - WINDOW + jnp.arange(WINDOW + Q_BLOCK)
        return nsa_core(qb, gb, q_pos, kc, vc, c_end, gather_slc, n_sblk,
                        win[:, :, 0], win[:, :, 1], w_pos, rel_bias)

    qs = q_a.reshape(B, n_qb, Q_BLOCK, N_HEADS_A, HEAD_DIM).swapaxes(0, 1)
    gs = gates.reshape(B, n_qb, Q_BLOCK, 3, N_KV_A, HEADS_PER_GROUP).swapaxes(0, 1)
    out = lax.map(block, (qs, gs, jnp.arange(n_qb) * Q_BLOCK))
    return out.swapaxes(0, 1).reshape(B, S, -1)


def nsa_sample(q_a, kv_a, gates, pool_cmp, pool_slc, win_buf, page_table, cmp_pe, cmp_w1, cmp_w2, rel_bias):
    Bd, T = q_a.shape[:2]
    q_pos = PAST_LEN + jnp.arange(T)
    past_cmp = pool_cmp[page_table].reshape(Bd, PAST_LEN, 2, N_KV_A, HEAD_DIM)
    cmp_all = jnp.concatenate([past_cmp, kv_a[:, :, 0:2]], axis=1)
    kc, vc, c_end = compress_kv(cmp_all[:, :, 0], cmp_all[:, :, 1], cmp_pe, cmp_w1, cmp_w2)
    n_past_blk = PAST_LEN // SLC_BLOCK
    n_new_blk = -(-T // SLC_BLOCK)
    n_sblk = n_past_blk + n_new_blk
    bpp = PAGE_SIZE // SLC_BLOCK
    pool_r = pool_slc.reshape(pool_slc.shape[0], bpp, SLC_BLOCK, 2, N_KV_A, HEAD_DIM)
    new_slc = jnp.pad(kv_a[:, :, 2:4], ((0, 0), (0, n_new_blk * SLC_BLOCK - T), (0, 0), (0, 0), (0, 0)))
    new_r = new_slc.reshape(Bd, n_new_blk, SLC_BLOCK, 2, N_KV_A, HEAD_DIM).transpose(0, 4, 1, 2, 3, 5)
    bidx = jnp.arange(Bd)[:, None, None, None]
    gidx = jnp.arange(N_KV_A)[None, None, :, None]

    def gather_slc(idx):
        past_i = jnp.minimum(idx, n_past_blk - 1)
        phys = page_table[bidx, past_i // bpp]
        from_past = pool_r[phys, past_i % bpp, :, :, gidx]
        from_new = new_r[bidx, gidx, jnp.clip(idx - n_past_blk, 0, n_new_blk - 1)]
        kv = jnp.where((idx >= n_past_blk)[..., None, None, None], from_new, from_past)
        return kv[..., 0, :], kv[..., 1, :]

    w_len = win_buf.shape[1]
    win_all = jnp.concatenate([win_buf, kv_a[:, :, 4:6]], axis=1)
    w_pos = PAST_LEN - w_len + jnp.arange(w_len + T)
    out = nsa_core(q_a, gates, q_pos, kc, vc, c_end, gather_slc, n_sblk,
                   win_all[:, :, 0], win_all[:, :, 1], w_pos, rel_bias)
    return out, win_all[:, -w_len:]


def mla_project(q_down, kv_down, pos, q_norm, w_q_up, kv_norm, w_kv_up):
    B, T = q_down.shape[:2]
    q = (rmsnorm(q_down, q_norm) @ w_q_up).reshape(B, T, N_HEADS_B, QK_NOPE + QK_ROPE)
    q_nope = q[..., :QK_NOPE]
    q_rope = rope(q[..., QK_NOPE:], pos)
    ckv = rmsnorm(kv_down[..., :KV_LORA], kv_norm)
    k_rope = rope(kv_down[..., None, KV_LORA:], pos)[:, :, 0]
    w_ukv = w_kv_up.reshape(KV_LORA, N_HEADS_B, QK_NOPE + V_DIM)
    q_abs = jnp.einsum('bthn,chn->bthc', q_nope, w_ukv[..., :QK_NOPE])
    latent = jnp.concatenate([ckv, k_rope], axis=-1)
    return q_abs, q_rope, latent, w_ukv[..., QK_NOPE:]


def mla_attend(q_abs, q_rope, q_pos, lat, k_pos):
    ckv, krope = lat[..., :KV_LORA], lat[..., KV_LORA:]
    s = (jnp.einsum('bthc,bsc->bhts', q_abs, ckv)
         + jnp.einsum('bthr,bsr->bhts', q_rope, krope)).astype(jnp.float32) * MLA_SCALE
    mask = k_pos[None, :] <= q_pos[:, None]
    p = jax.nn.softmax(jnp.where(mask, s, NEG), axis=-1)
    return jnp.einsum('bhts,bsc->bthc', p.astype(ckv.dtype), ckv)


def merge_heads(o_a, o_b, out_norm_a, out_norm_b, w_out):
    return jnp.concatenate([rmsnorm(o_a, out_norm_a), rmsnorm(o_b, out_norm_b)], axis=-1) @ w_out


def token_mix_prompt(h, w_in, cmp_pe, cmp_w1, cmp_w2, q_norm, w_q_up, kv_norm, w_kv_up,
                     out_norm_a, out_norm_b, w_out, rel_bias):
    B, S, _ = h.shape
    pos = jnp.arange(S)
    q_a, kv_a, gates, q_down, kv_down = split_proj(h @ w_in)
    o_a = nsa_prompt(q_a, kv_a, gates, cmp_pe, cmp_w1, cmp_w2, rel_bias)
    q_abs, q_rope, lat, w_uv = mla_project(q_down, kv_down, pos, q_norm, w_q_up, kv_norm, w_kv_up)
    n_qb = S // Q_BLOCK

    def block(args):
        qa, qr, t0 = args
        return mla_attend(qa, qr, t0 + jnp.arange(Q_BLOCK), lat, pos)

    qa_b = q_abs.reshape(B, n_qb, Q_BLOCK, N_HEADS_B, KV_LORA).swapaxes(0, 1)
    qr_b = q_rope.reshape(B, n_qb, Q_BLOCK, N_HEADS_B, QK_ROPE).swapaxes(0, 1)
    o_lat = lax.map(block, (qa_b, qr_b, jnp.arange(n_qb) * Q_BLOCK)).swapaxes(0, 1)
    o_lat = o_lat.reshape(B, S, N_HEADS_B, KV_LORA)
    o_b = jnp.einsum('bthc,chv->bthv', o_lat, w_uv).reshape(B, S, -1)
    w_keep = min(WINDOW, S)
    out = merge_heads(o_a, o_b, out_norm_a, out_norm_b, w_out)
    return out, kv_a[:, :, 0:2], kv_a[:, :, 2:4], kv_a[:, S - w_keep:, 4:6], lat


def token_mix_sample(h, pool_cmp, pool_slc, win_buf, pool_mla, page_table, w_in, cmp_pe, cmp_w1, cmp_w2,
                     q_norm, w_q_up, kv_norm, w_kv_up, out_norm_a, out_norm_b, w_out, rel_bias):
    Bd, T, _ = h.shape
    pos = PAST_LEN + jnp.arange(T)
    q_a, kv_a, gates, q_down, kv_down = split_proj(h @ w_in)
    o_a, new_win = nsa_sample(q_a, kv_a, gates, pool_cmp, pool_slc, win_buf, page_table,
                              cmp_pe, cmp_w1, cmp_w2, rel_bias)
    q_abs, q_rope, lat, w_uv = mla_project(q_down, kv_down, pos, q_norm, w_q_up, kv_norm, w_kv_up)
    past = pool_mla[page_table].reshape(Bd, PAST_LEN, LATENT_DIM)
    lat_all = jnp.concatenate([past, lat], axis=1)
    o_lat = mla_attend(q_abs, q_rope, pos, lat_all, jnp.arange(PAST_LEN + T))
    o_b = jnp.einsum('bthc,chv->bthv', o_lat, w_uv).reshape(Bd, T, -1)
    out = merge_heads(o_a, o_b, out_norm_a, out_norm_b, w_out)
    return out, kv_a[:, :, 0:2], kv_a[:, :, 2:4], new_win, lat


def moe_ffn(h, w_router, router_bias, w_gate_e, w_up_e, w_down_e, w_gate_s, w_up_s, w_down_s):
    n_tok = h.shape[0]
    scores = jax.nn.sigmoid((h @ w_router).astype(jnp.float32))
    biased = scores + router_bias.astype(jnp.float32)
    grp = biased.reshape(n_tok, N_GROUPS, N_EXPERTS // N_GROUPS)
    grp_score = lax.top_k(grp, 2)[0].sum(-1)
    _, top_g = lax.top_k(grp_score, TOP_GROUPS)
    gmask = jnp.zeros((n_tok, N_GROUPS), bool).at[jnp.arange(n_tok)[:, None], top_g].set(True)
    biased = jnp.where(jnp.repeat(gmask, N_EXPERTS // N_GROUPS, axis=1), biased, NEG)
    _, idx = lax.top_k(biased, TOP_K)
    wts = jnp.take_along_axis(scores, idx, axis=1)
    wts = wts / wts.sum(-1, keepdims=True) * ROUTED_SCALE
    n_asg = n_tok * TOP_K
    flat_e = idx.reshape(n_asg)
    order = jnp.argsort(flat_e)
    se = flat_e[order]
    st = jnp.repeat(jnp.arange(n_tok, dtype=jnp.int32), TOP_K)[order]
    sw = wts.reshape(n_asg)[order]
    counts = jnp.bincount(flat_e, length=N_EXPERTS)
    padded = (counts + MOE_BLOCK - 1) // MOE_BLOCK * MOE_BLOCK
    pad_end = jnp.cumsum(padded)
    dest = (pad_end - padded)[se] + jnp.arange(n_asg) - (jnp.cumsum(counts) - counts)[se]
    n_blocks = -(-(n_asg + N_EXPERTS * (MOE_BLOCK - 1)) // MOE_BLOCK)
    n_rows = n_blocks * MOE_BLOCK
    row_tok = jnp.full((n_rows,), n_tok, jnp.int32).at[dest].set(st)
    row_w = jnp.zeros((n_rows,), jnp.float32).at[dest].set(sw)
    blk_exp = jnp.minimum(jnp.searchsorted(pad_end, jnp.arange(n_blocks) * MOE_BLOCK, side='right'),
                          N_EXPERTS - 1)
    h_pad = jnp.concatenate([h, jnp.zeros((1, h.shape[1]), h.dtype)], axis=0)
    xb = h_pad[row_tok].reshape(n_blocks, MOE_BLOCK, -1)

    def expert_block(args):
        xe, e = args
        return (jax.nn.silu(xe @ w_gate_e[e]) * (xe @ w_up_e[e])) @ w_down_e[e]

    yb = lax.map(expert_block, (xb, blk_exp)).reshape(n_rows, -1)
    routed = jax.ops.segment_sum(yb * row_w[:, None].astype(yb.dtype), row_tok,
                                 num_segments=n_tok + 1)[:n_tok]
    shared = (jax.nn.silu(h @ w_gate_s) * (h @ w_up_s)) @ w_down_s
    return routed + shared


def setup_inputs(seed: int = 0) -> dict:
    key = jax.random.key(seed)
    ks = iter(jax.random.split(key, 40))

    def nrm(shape, scale):
        return jax.random.normal(next(ks), shape, jnp.float32) * scale

    def gain(shape):
        return 1.0 + nrm(shape, 0.1)

    D = D_MODEL
    n_pages = PAST_LEN // PAGE_SIZE
    n_used = DEC_BATCH * n_pages
    n_pool = n_used + max(n_used // 4, 1)
    page_table = jax.random.permutation(next(ks), n_pool)[:n_used].reshape(DEC_BATCH, n_pages).astype(jnp.int32)
    w_buf = min(WINDOW, PAST_LEN)
    x_prompt = nrm((BATCH, SEQ, D), 1.0)
    x_sample = nrm((DEC_BATCH, DEC_SEQ, D), 1.0)
    cache_nsa_cmp = nrm((DEPTH, n_pool, PAGE_SIZE, 2, N_KV_A, HEAD_DIM), 1.0)
    cache_nsa_slc = nrm((DEPTH, n_pool, PAGE_SIZE, 2, N_KV_A, HEAD_DIM), 1.0)
    cache_nsa_win = nrm((DEPTH, DEC_BATCH, w_buf, 2, N_KV_A, HEAD_DIM), 1.0)
    cache_mla = nrm((DEPTH, n_pool, PAGE_SIZE, LATENT_DIM), 1.0)
    return {
        'x_prompt': x_prompt,
        'x_sample': x_sample,
        'cache_nsa_cmp': cache_nsa_cmp,
        'cache_nsa_slc': cache_nsa_slc,
        'cache_nsa_win': cache_nsa_win,
        'cache_mla': cache_mla,
        'page_table': page_table,
        'c_prompt': nrm((BATCH, D), 1.0),
        'c_sample': nrm((DEC_BATCH, D), 1.0),
        'rel_bias': nrm((N_BUCKETS, N_HEADS_A), 0.5),
        'w_ada': nrm((DEPTH, D, 6 * D), 0.5 * D ** -0.5),
        'b_ada': nrm((DEPTH, 6 * D), 0.02),
        'norm_attn': gain((DEPTH, D)),
        'norm_ffn': gain((DEPTH, D)),
        'w_in': nrm((DEPTH, D, IN_DIM), D ** -0.5),
        'cmp_pe': nrm((DEPTH, 2, CMP_BLOCK, HEAD_DIM), 0.5),
        'cmp_w1': nrm((DEPTH, 2, CMP_BLOCK * HEAD_DIM, CMP_HIDDEN), (CMP_BLOCK * HEAD_DIM) ** -0.5),
        'cmp_w2': nrm((DEPTH, 2, CMP_HIDDEN, HEAD_DIM), CMP_HIDDEN ** -0.5),
        'q_norm': gain((DEPTH, Q_LORA)),
        'w_q_up': nrm((DEPTH, Q_LORA, N_HEADS_B * (QK_NOPE + QK_ROPE)), Q_LORA ** -0.5),
        'kv_norm': gain((DEPTH, KV_LORA)),
        'w_kv_up': nrm((DEPTH, KV_LORA, N_HEADS_B * (QK_NOPE + V_DIM)), KV_LORA ** -0.5),
        'out_norm_a': gain((DEPTH, N_HEADS_A * HEAD_DIM)),
        'out_norm_b': gain((DEPTH, N_HEADS_B * V_DIM)),
        'w_out': nrm((DEPTH, MIX_DIM, D), MIX_DIM ** -0.5),
        'w_router': nrm((DEPTH, D, N_EXPERTS), D ** -0.5),
        'router_bias': nrm((DEPTH, N_EXPERTS), 0.01),
        'w_gate_e': nrm((DEPTH, N_EXPERTS, D, D_EXPERT), D ** -0.5),
        'w_up_e': nrm((DEPTH, N_EXPERTS, D, D_EXPERT), D ** -0.5),
        'w_down_e': nrm((DEPTH, N_EXPERTS, D_EXPERT, D), D_EXPERT ** -0.5),
        'w_gate_s': nrm((DEPTH, D, D_SHARED), D ** -0.5),
        'w_up_s': nrm((DEPTH, D, D_SHARED), D ** -0.5),
        'w_down_s': nrm((DEPTH, D_SHARED, D), D_SHARED ** -0.5),
        'norm_final': gain((D,)),
    }


def reference(x_prompt, x_sample, cache_nsa_cmp, cache_nsa_slc, cache_nsa_win, cache_mla, page_table,
              c_prompt, c_sample, rel_bias, w_ada, b_ada, norm_attn, norm_ffn, w_in, cmp_pe, cmp_w1, cmp_w2,
              q_norm, w_q_up, kv_norm, w_kv_up, out_norm_a, out_norm_b, w_out, w_router, router_bias,
              w_gate_e, w_up_e, w_down_e, w_gate_s, w_up_s, w_down_s, norm_final):
    xp, xs = x_prompt, x_sample
    n_p = xp.shape[0] * xp.shape[1]
    cmp_p, cmp_s, slc_p, slc_s, win_p, win_s, mla_p, mla_s = [], [], [], [], [], [], [], []
    for l in range(DEPTH):
        ap = adaln(c_prompt, w_ada[l], b_ada[l])
        asm = adaln(c_sample, w_ada[l], b_ada[l])
        mix_w = (w_in[l], cmp_pe[l], cmp_w1[l], cmp_w2[l], q_norm[l], w_q_up[l], kv_norm[l], w_kv_up[l],
                 out_norm_a[l], out_norm_b[l], w_out[l], rel_bias)
        hp = modulate(rmsnorm(xp, norm_attn[l]), ap[:, 0], ap[:, 1])
        hs = modulate(rmsnorm(xs, norm_attn[l]), asm[:, 0], asm[:, 1])
        mp, kc_p, ks_p, kw_p, lat_p = token_mix_prompt(hp, *mix_w)
        ms, kc_s, ks_s, kw_s, lat_s = token_mix_sample(hs, cache_nsa_cmp[l], cache_nsa_slc[l], cache_nsa_win[l],
                                                       cache_mla[l], page_table, *mix_w)
        xp = xp + ap[:, 2][:, None, :] * mp
        xs = xs + asm[:, 2][:, None, :] * ms
        fp = modulate(rmsnorm(xp, norm_ffn[l]), ap[:, 3], ap[:, 4])
        fs = modulate(rmsnorm(xs, norm_ffn[l]), asm[:, 3], asm[:, 4])
        f = moe_ffn(jnp.concatenate([fp.reshape(-1, D_MODEL), fs.reshape(-1, D_MODEL)], axis=0),
                    w_router[l], router_bias[l], w_gate_e[l], w_up_e[l], w_down_e[l],
                    w_gate_s[l], w_up_s[l], w_down_s[l])
        xp = xp + ap[:, 5][:, None, :] * f[:n_p].reshape(xp.shape)
        xs = xs + asm[:, 5][:, None, :] * f[n_p:].reshape(xs.shape)
        cmp_p.append(kc_p)
        cmp_s.append(kc_s)
        slc_p.append(ks_p)
        slc_s.append(ks_s)
        win_p.append(kw_p)
        win_s.append(kw_s)
        mla_p.append(lat_p)
        mla_s.append(lat_s)
    y_prompt = rmsnorm(xp, norm_final)
    y_sample = rmsnorm(xs, norm_final)
    return (y_prompt, y_sample, jnp.stack(cmp_p), jnp.stack(cmp_s), jnp.stack(slc_p), jnp.stack(slc_s),
            jnp.stack(win_p), jnp.stack(win_s), jnp.stack(mla_p), jnp.stack(mla_s))
```

```python
import functools
import math

import jax
import jax.numpy as jnp
from jax import lax
from jax.experimental import pallas as pl
from jax.experimental.pallas import tpu as pltpu

F32 = jnp.float32
BF16 = jnp.bfloat16

LANE = 128
VMEM_LIMIT = 56 * 1024 * 1024

HEAD_DIM = 64
N_HEADS_A = 8
N_KV_A = 2
HPG = N_HEADS_A // N_KV_A
CMP_BLOCK = 32
CMP_STRIDE = 16
CMP_HIDDEN = 128
SLC_BLOCK = 64
SLC_PER_CMP = SLC_BLOCK // CMP_STRIDE
N_SLC = 16
N_LOCAL_SLC = 2
WINDOW = 512
N_HEADS_B = 8
Q_LORA = 192
KV_LORA = 128
QK_NOPE = 64
QK_ROPE = 32
V_DIM = 64
LATENT_DIM = KV_LORA + QK_ROPE
ROPE_THETA = 10000.0
MLA_SCALE = (QK_NOPE + QK_ROPE) ** -0.5
N_BUCKETS = 32
MAX_DISTANCE = 128
N_EXPERTS = 256
TOP_K = 8
N_GROUPS = 8
TOP_GROUPS = 4
ROUTED_SCALE = 2.5
MOE_BLOCK = 128
EPS = 1e-6
NEG = -1e30
FORCED = 1e30

QB = 128
KT = 512
CMP_PAD = 16
LOC_W = 24


def _dot(a, b):
    return jnp.dot(a, b, preferred_element_type=F32)


def _dot_t(a, b):
    return lax.dot_general(a, b, (((1,), (1,)), ((), ())), preferred_element_type=F32)


def _const_spec(shape):
    nd = len(shape)
    return pl.BlockSpec(shape, lambda *_: (0,) * nd)


def _inproj_kernel(x_ref, sh_ref, sc_ref, g_ref, cs_ref, sn_ref, wq_ref, wkv_ref, wg_ref, wqd_ref,
                   wkvd_ref, qn_ref, wqup_ref, bd_ref, plc_ref, kvn_ref,
                   qa_ref, cmp_ref, slc_ref, win_ref, kv16_ref, gt_ref, qm_ref, lat_ref, lat16_ref):
    x = x_ref[0]
    ms = jnp.mean(x * x, axis=-1, keepdims=True)
    xn = x * lax.rsqrt(ms + EPS) * g_ref[...]
    h = xn * (1.0 + sc_ref[0]) + sh_ref[0]
    hb = h.astype(BF16)
    qa_ref[0] = _dot(hb, wq_ref[...]).astype(BF16)
    kv = _dot(hb, wkv_ref[...])
    cmp_ref[0] = kv[:, 0:256]
    slc_ref[0] = kv[:, 256:512]
    win_ref[0] = kv[:, 512:768]
    kv16_ref[0] = kv.astype(BF16)
    gl = _dot(hb, wg_ref[...])
    gt_ref[0] = 1.0 / (1.0 + jnp.exp(-gl))
    qd = _dot(hb, wqd_ref[...])
    qn = qd * lax.rsqrt(jnp.sum(qd * qd, axis=-1, keepdims=True) * (1.0 / Q_LORA) + EPS) * qn_ref[...]
    qu = _dot(qn.astype(BF16), wqup_ref[...])
    cs = cs_ref[...]
    sn = sn_ref[...]
    qr = qu[:, 512:768] * cs + qu[:, 768:1024] * sn
    qm = _dot(qu[:, 0:512].astype(BF16), bd_ref[...]) + _dot(qr.astype(BF16), plc_ref[...])
    qm_ref[0] = (qm * MLA_SCALE).astype(BF16)
    kvd = _dot(hb, wkvd_ref[...])
    c = kvd[:, 0:128]
    ckv = c * lax.rsqrt(jnp.mean(c * c, axis=-1, keepdims=True) + EPS) * kvn_ref[...]
    kr = kvd[:, 128:256] * cs[:, 0:128] + kvd[:, 256:384] * sn[:, 0:128]
    lat_ref[0, :, 0:128] = ckv
    lat_ref[0, :, 128:160] = kr[:, 0:32]
    lat16_ref[0, :, 0:128] = ckv.astype(BF16)
    lat16_ref[0, :, 128:256] = kr.astype(BF16)


def _inproj_weights(w_in, q_norm, w_q_up, kv_norm, w_kv_up):
    D = w_in.shape[0]
    o1 = N_HEADS_A * HEAD_DIM
    o2 = o1 + 6 * N_KV_A * HEAD_DIM
    o3 = o2 + 3 * N_HEADS_A
    o4 = o3 + Q_LORA
    wq = w_in[:, :o1].reshape(D, N_HEADS_A, HEAD_DIM) * (HEAD_DIM ** -0.5)
    z = jnp.zeros_like(wq)
    grp = (jnp.arange(N_HEADS_A) // HPG)[None, :, None]
    wq_pad = jnp.concatenate([jnp.where(grp == 0, wq, z), jnp.where(grp == 1, wq, z)], axis=-1)
    wq_pad = wq_pad.reshape(D, N_HEADS_A * 2 * HEAD_DIM)
    wkv = w_in[:, o1:o2]
    wg = jnp.pad(w_in[:, o2:o3], ((0, 0), (0, LANE - 3 * N_HEADS_A)))
    wqd = jnp.pad(w_in[:, o3:o4], ((0, 0), (0, 256 - Q_LORA)))
    wkd = w_in[:, o4:]
    half = QK_ROPE // 2
    wc = wkd[:, :KV_LORA]
    wr = wkd[:, KV_LORA:]
    wrot = jnp.concatenate([-wr[:, half:], wr[:, :half]], axis=1)
    padr = ((0, 0), (0, LANE - QK_ROPE))
    wkvd = jnp.concatenate([wc, jnp.pad(wr, padr), jnp.pad(wrot, padr)], axis=1)
    qn = jnp.pad(q_norm, (0, 256 - Q_LORA)).reshape(1, 256)
    wu = jnp.pad(w_q_up, ((0, 256 - Q_LORA), (0, 0))).reshape(256, N_HEADS_B, QK_NOPE + QK_ROPE)
    wu_n = wu[:, :, :QK_NOPE].reshape(256, N_HEADS_B * QK_NOPE)
    wu_r = wu[:, :, QK_NOPE:]
    wu_rot = jnp.concatenate([-wu_r[:, :, half:], wu_r[:, :, :half]], axis=-1)
    wqup = jnp.concatenate([wu_n, wu_r.reshape(256, -1), wu_rot.reshape(256, -1)], axis=1)
    w_ukv = w_kv_up.reshape(KV_LORA, N_HEADS_B, QK_NOPE + V_DIM)
    w_uk = w_ukv[:, :, :QK_NOPE]
    eye = jnp.eye(N_HEADS_B, dtype=F32)
    bd = jnp.einsum('chn,hk->hnkc', w_uk, eye)
    bd = jnp.pad(bd, ((0, 0), (0, 0), (0, 0), (0, 256 - KV_LORA))).reshape(N_HEADS_B * QK_NOPE, N_HEADS_B * 256)
    plc = jnp.einsum('hk,rs->hrks', eye, jnp.eye(QK_ROPE, dtype=F32))
    plc = jnp.pad(plc, ((0, 0), (0, 0), (0, 0), (KV_LORA, 256 - KV_LORA - QK_ROPE)))
    plc = plc.reshape(N_HEADS_B * QK_ROPE, N_HEADS_B * 256)
    w_uv = jnp.pad(w_ukv[:, :, QK_NOPE:].transpose(1, 0, 2), ((0, 0), (0, 0), (0, LANE - V_DIM)))
    bf = lambda a: a.astype(BF16)
    return dict(wq=bf(wq_pad), wkv=bf(wkv), wg=bf(wg), wqd=bf(wqd), wkvd=bf(wkvd), qn=qn, wqup=bf(wqup),
                bd=bf(bd), plc=bf(plc), kvn=kv_norm.reshape(1, KV_LORA), w_uv=bf(w_uv))


def _rope_tables(pos):
    half = QK_ROPE // 2
    inv = ROPE_THETA ** (-jnp.arange(half, dtype=F32) / half)
    ang = pos.astype(F32)[:, None] * inv[None, :]
    cos = jnp.tile(jnp.cos(ang), (1, 2 * N_HEADS_B))
    sin = jnp.tile(jnp.sin(ang), (1, 2 * N_HEADS_B))
    return cos, sin


def _inproj(x, shift, scale, gain, cos, sin, w, tr):
    B, T, D = x.shape
    tm = shift.shape[1]
    mod_spec = pl.BlockSpec((1, tr if tm > 1 else 1, D), (lambda b, t: (b, t, 0)) if tm > 1 else (lambda b, t: (b, 0, 0)))
    row = lambda n: pl.BlockSpec((1, tr, n), lambda b, t: (b, t, 0))
    tab = pl.BlockSpec((tr, 256), lambda b, t: (t, 0))
    wnames = ['wq', 'wkv', 'wg', 'wqd', 'wkvd', 'qn', 'wqup', 'bd', 'plc', 'kvn']
    out_shape = [
        jax.ShapeDtypeStruct((B, T, 1024), BF16), jax.ShapeDtypeStruct((B, T, 256), F32),
        jax.ShapeDtypeStruct((B, T, 256), F32), jax.ShapeDtypeStruct((B, T, 256), F32),
        jax.ShapeDtypeStruct((B, T, 768), BF16), jax.ShapeDtypeStruct((B, T, LANE), F32),
        jax.ShapeDtypeStruct((B, T, 2048), BF16), jax.ShapeDtypeStruct((B, T, LATENT_DIM), F32),
        jax.ShapeDtypeStruct((B, T, 256), BF16)]
    in_specs = [row(D), mod_spec, mod_spec, _const_spec((1, D)), tab, tab]
    in_specs += [_const_spec(w[n].shape) for n in wnames[:5]]
    in_specs += [_const_spec(w['qn'].shape)] + [_const_spec(w[n].shape) for n in wnames[6:9]]
    in_specs += [_const_spec(w['kvn'].shape)]
    return pl.pallas_call(
        _inproj_kernel, grid=(B, T // tr), in_specs=in_specs,
        out_specs=[row(s.shape[-1]) for s in out_shape], out_shape=out_shape, name='inproj',
        compiler_params=pltpu.CompilerParams(dimension_semantics=('parallel', 'parallel'),
                                             vmem_limit_bytes=VMEM_LIMIT),
    )(x, shift, scale, gain.reshape(1, D), cos, sin, *[w[n] for n in wnames])


def _compress_kernel(x0_ref, x1_ref, w1_ref, pe_ref, w2a_ref, w2b_ref, o_ref, *, n_cmp):
    w1 = w1_ref[0]
    pp = _dot(pe_ref[0], w1)
    peh = pp[0:1, 0:CMP_HIDDEN] + pp[1:2, CMP_HIDDEN:]
    nc = x0_ref.shape[2]
    out = jnp.zeros((nc, LANE), F32)
    for x_ref, w2_ref in ((x0_ref, w2a_ref), (x1_ref, w2b_ref)):
        ab = _dot(x_ref[0, 0], w1)
        hid = ab[:, 0:CMP_HIDDEN] + pltpu.roll(ab[:, CMP_HIDDEN:], nc - 1, 0) + peh
        act = hid * (1.0 / (1.0 + jnp.exp(-hid)))
        out = out + _dot(act.astype(BF16), w2_ref[0])
    rows = lax.broadcasted_iota(jnp.int32, (nc, LANE), 0)
    out = jnp.where(rows < n_cmp, out, 0.0)
    o_ref[0, 0, 0:CMP_PAD] = jnp.zeros((CMP_PAD, LANE), F32)
    o_ref[0, 0, CMP_PAD:CMP_PAD + nc] = out
    o_ref[0, 0, CMP_PAD + nc:] = jnp.zeros((o_ref.shape[2] - CMP_PAD - nc, LANE), F32)


def _compress_weights(cmp_pe, cmp_w1, cmp_w2):
    kin = CMP_STRIDE * HEAD_DIM
    w1 = cmp_w1.reshape(2, 2, kin, CMP_HIDDEN)
    w1cat = jnp.concatenate([w1[:, 0], w1[:, 1]], axis=-1).astype(BF16)
    pe = jnp.pad(cmp_pe.reshape(2, 2, kin), ((0, 0), (0, 6), (0, 0))).astype(BF16)
    w2a = jnp.pad(cmp_w2, ((0, 0), (0, 0), (0, HEAD_DIM))).astype(BF16)
    w2b = jnp.pad(cmp_w2, ((0, 0), (0, 0), (HEAD_DIM, 0))).astype(BF16)
    return w1cat, pe, w2a, w2b


def _compress(xc, cw, n_cmp):
    B, _, nc, kin = xc.shape
    w1cat, pe, w2a, w2b = cw
    return pl.pallas_call(
        functools.partial(_compress_kernel, n_cmp=n_cmp), grid=(B, 2),
        in_specs=[pl.BlockSpec((1, 1, nc, kin), lambda b, j: (b, 2 * j, 0, 0)),
                  pl.BlockSpec((1, 1, nc, kin), lambda b, j: (b, 2 * j + 1, 0, 0)),
                  pl.BlockSpec((1, kin, 2 * CMP_HIDDEN), lambda b, j: (j, 0, 0)),
                  pl.BlockSpec((1, 8, kin), lambda b, j: (j, 0, 0)),
                  pl.BlockSpec((1, CMP_HIDDEN, LANE), lambda b, j: (j, 0, 0)),
                  pl.BlockSpec((1, CMP_HIDDEN, LANE), lambda b, j: (j, 0, 0))],
        out_specs=pl.BlockSpec((1, 1, nc + LANE, LANE), lambda b, j: (b, j, 0, 0)),
        out_shape=jax.ShapeDtypeStruct((B, 2, nc + LANE, LANE), F32), name='compress',
        compiler_params=pltpu.CompilerParams(dimension_semantics=('parallel', 'parallel'),
                                             vmem_limit_bytes=VMEM_LIMIT),
    )(xc, xc, w1cat, pe, w2a, w2b)


def _t5_bucket(rel):
    max_exact = N_BUCKETS // 2
    n = jnp.maximum(rel, 0)
    nf = jnp.maximum(n, 1).astype(F32)
    large = max_exact + (jnp.log(nf / max_exact) / math.log(MAX_DISTANCE / max_exact)
                         * (N_BUCKETS - max_exact)).astype(jnp.int32)
    large = jnp.minimum(large, N_BUCKETS - 1)
    return jnp.where(n < max_exact, n, large)


def _bias_tables(rel_bias):
    tbl = rel_bias.astype(F32)
    const = tbl[N_BUCKETS - 1]
    i = jnp.arange(QB)[:, None]
    j = jnp.arange(LANE)[None, :]

    def tab(rel):
        b = tbl[_t5_bucket(rel)]
        return jnp.moveaxis(b, -1, 0) - const[:, None, None]

    t0 = jnp.where((i - j >= 0)[None], tab(i - j), NEG)
    t1 = tab(QB + i - j)
    zero = jnp.zeros_like(t1)
    t4 = jnp.broadcast_to(jnp.where(j > i, 0.0, NEG)[None], t1.shape)
    tb = jnp.stack([t0, t1, zero, jnp.full_like(t1, NEG), t4]).reshape(5, N_HEADS_A * QB, LANE)
    rel_l = i - CMP_STRIDE * (j - CMP_PAD) - (CMP_BLOCK - 1)
    lb = jnp.where(((j < LOC_W) & (rel_l >= 0))[None], tab(rel_l), NEG).reshape(N_HEADS_A * QB, LANE)
    return tb, lb


def _hilo_dot(x, m):
    hi = x.astype(BF16)
    lo = (x - hi.astype(F32)).astype(BF16)
    return _dot(hi, m) + _dot(lo, m)


def _nsa_kernel(q_ref, gt_ref, ks_ref, vs_ref, kw_ref, vw_ref, kc_ref, vc_ref, tb_ref, lb_ref, pf_ref, e_ref,
                o_ref, m_scr, l_scr, acc_scr, *, n_cmp):
    qb = pl.program_id(1)
    nh = N_HEADS_A
    rows = nh * QB
    q = q_ref[0]
    q8 = jnp.concatenate([q[:, h * LANE:(h + 1) * LANE] for h in range(nh)], axis=0)
    nc = pf_ref.shape[0]

    kcf = kc_ref[0, 0, 0:nc].astype(BF16)
    vcf = vc_ref[0, 0, 0:nc].astype(BF16)
    l0 = pl.multiple_of(qb * (QB // CMP_STRIDE), 8)
    kcl = kc_ref[0, 0, pl.ds(l0, LANE)].astype(BF16)
    vcl = vc_ref[0, 0, pl.ds(l0, LANE)].astype(BF16)
    colf = lax.broadcasted_iota(jnp.int32, (1, nc), 1)
    far_ok = jnp.where(colf >= CMP_PAD, jnp.where(colf < l0, 0.0, NEG), NEG)
    coll = lax.broadcasted_iota(jnp.int32, (1, LANE), 1) + (l0 - CMP_PAD)
    loc_ok = jnp.where(coll >= 0, jnp.where(coll < n_cmp, 0.0, NEG), NEG)
    s_far = _dot_t(q8, kcf) + far_ok
    s_loc = _dot_t(q8, kcl) + lb_ref[...] + loc_ok
    mrow = jnp.maximum(jnp.max(s_far, axis=1, keepdims=True), jnp.max(s_loc, axis=1, keepdims=True))
    p_far = jnp.exp(s_far - mrow)
    p_loc = jnp.exp(s_loc - mrow)
    lsum = jnp.sum(p_far, axis=1, keepdims=True) + jnp.sum(p_loc, axis=1, keepdims=True)
    inv = jnp.where(mrow > 0.5 * NEG, 1.0 / lsum, 0.0)
    p_far = p_far * inv
    p_loc = p_loc * inv
    o_c = _dot(p_far.astype(BF16), vcf) + _dot(p_loc.astype(BF16), vcl)

    r_i = lax.broadcasted_iota(jnp.int32, (LANE, LANE), 0)
    c_i = lax.broadcasted_iota(jnp.int32, (LANE, LANE), 1)
    pool_loc = jnp.where(r_i < LOC_W,
                         jnp.where(c_i == (r_i >> 2) + (2 * qb - CMP_PAD // SLC_PER_CMP), 1.0, 0.0),
                         0.0).astype(BF16)
    tq = 2 * qb + jnp.where(r_i >= SLC_BLOCK, 1, 0)
    dist = tq - c_i
    c_f = c_i.astype(F32)
    sel = []
    for g in range(N_KV_A):
        pgf = p_far[(g * HPG) * QB:(g * HPG + 1) * QB]
        pgl = p_loc[(g * HPG) * QB:(g * HPG + 1) * QB]
        for hh in range(1, HPG):
            pgf = pgf + p_far[(g * HPG + hh) * QB:(g * HPG + hh + 1) * QB]
            pgl = pgl + p_loc[(g * HPG + hh) * QB:(g * HPG + hh + 1) * QB]
        imp = _hilo_dot(pgf, pf_ref[...]) + _hilo_dot(pgl, pool_loc)
        score = jnp.where(dist < 0, NEG, jnp.where(dist < N_LOCAL_SLC, FORCED, jnp.where(c_i == 0, FORCED, imp)))
        chosen = jnp.zeros((QB, LANE), F32)
        for _ in range(N_SLC):
            mx = jnp.max(score, axis=1, keepdims=True)
            first = jnp.min(jnp.where(score == mx, c_f, float(LANE)), axis=1, keepdims=True)
            hit = c_f == first
            chosen = jnp.where(hit, 1.0, chosen)
            score = jnp.where(hit, -jnp.inf, score)
        sel.append(chosen.astype(BF16))

    m_scr[...] = jnp.full(m_scr.shape, -jnp.inf, F32)
    l_scr[...] = jnp.zeros(l_scr.shape, F32)
    acc_scr[...] = jnp.zeros(acc_scr.shape, F32)
    nsub = KT // LANE

    def slc_tile(kt, carry):
        k0 = pl.multiple_of(kt * KT, KT)
        s = _dot_t(q8, ks_ref[0, pl.ds(k0, KT), :])
        bias = []
        for c in range(nsub):
            d = qb - (kt * nsub + c)
            bias.append(tb_ref[jnp.where(d < 0, 3, jnp.minimum(d, 2))])
        e_t = e_ref[:, pl.ds(k0, KT)]
        madd = [(_dot(sel[g], e_t) - 1.0) * (-NEG) for g in range(N_KV_A)]
        madd = jnp.concatenate([madd[g] for g in range(N_KV_A) for _ in range(HPG)], axis=0)
        s = s + jnp.concatenate(bias, axis=1) + madd
        m_old = m_scr[...]
        m_new = jnp.maximum(m_old, jnp.max(s, axis=1, keepdims=True))
        alpha = jnp.exp(m_old - m_new)
        p = jnp.exp(s - m_new)
        l_scr[...] = alpha * l_scr[...] + jnp.sum(p, axis=1, keepdims=True)
        acc_scr[...] = alpha * acc_scr[...] + _dot(p.astype(BF16), vs_ref[0, pl.ds(k0, KT), :])
        m_scr[...] = m_new
        return carry

    lax.fori_loop(0, qb // nsub + 1, slc_tile, 0)
    o_s = acc_scr[...] * (1.0 / l_scr[...])

    nwin = WINDOW // QB + 1
    w0 = jnp.maximum(qb - (nwin - 1), 0)
    k0 = pl.multiple_of(w0 * QB, QB)
    s = _dot_t(q8, kw_ref[0, pl.ds(k0, nwin * QB), :])
    bias = []
    for c in range(nwin):
        d = qb - (w0 + c)
        bias.append(tb_ref[jnp.where(d < 0, 3, jnp.where(d >= nwin - 1, 4, jnp.minimum(d, 2)))])
    s = s + jnp.concatenate(bias, axis=1)
    p = jnp.exp(s - jnp.max(s, axis=1, keepdims=True))
    o_w = _dot(p.astype(BF16), vw_ref[0, pl.ds(k0, nwin * QB), :]) * (1.0 / jnp.sum(p, axis=1, keepdims=True))

    gt = gt_ref[0]
    for h in range(nh):
        r = slice(h * QB, (h + 1) * QB)
        comb = (gt[:, h:h + 1] * o_c[r] + gt[:, nh + h:nh + h + 1] * o_s[r]
                + gt[:, 2 * nh + h:2 * nh + h + 1] * o_w[r])
        keep = (c_i >= HEAD_DIM) if h // HPG else (c_i < HEAD_DIM)
        o_ref[0, :, h * LANE:(h + 1) * LANE] = jnp.where(keep, comb, 0.0)


def _nsa_prompt(qa, gates, kv16, kc, tb, lb):
    B, S, _ = qa.shape
    nc = S // CMP_STRIDE
    n_cmp = (S - CMP_BLOCK) // CMP_STRIDE + 1
    m = jnp.arange(nc)
    pool_far = ((m[:, None] // SLC_PER_CMP - CMP_PAD // SLC_PER_CMP == jnp.arange(LANE)[None, :])
                & (m[:, None] >= CMP_PAD)).astype(BF16)
    expand = (jnp.arange(S)[None, :] // SLC_BLOCK == jnp.arange(LANE)[:, None]).astype(BF16)
    rows = N_HEADS_A * QB
    one = pl.Buffered(1)
    kvs = lambda c: pl.BlockSpec((1, S, LANE), lambda b, t: (b, 0, c), pipeline_mode=one)
    cspec = pl.BlockSpec((1, 1, nc + LANE, LANE), lambda b, t: (b, 0, 0, 0), pipeline_mode=one)
    vspec = pl.BlockSpec((1, 1, nc + LANE, LANE), lambda b, t: (b, 1, 0, 0), pipeline_mode=one)
    cst = lambda shape: pl.BlockSpec(shape, lambda b, t: (0,) * len(shape), pipeline_mode=one)
    return pl.pallas_call(
        functools.partial(_nsa_kernel, n_cmp=n_cmp), grid=(B, S // QB),
        in_specs=[pl.BlockSpec((1, QB, 1024), lambda b, t: (b, t, 0)),
                  pl.BlockSpec((1, QB, LANE), lambda b, t: (b, t, 0)),
                  kvs(2), kvs(3), kvs(4), kvs(5), cspec, vspec,
                  cst(tb.shape), cst(lb.shape), cst(pool_far.shape), cst(expand.shape)],
        out_specs=pl.BlockSpec((1, QB, 1024), lambda b, t: (b, t, 0)),
        out_shape=jax.ShapeDtypeStruct((B, S, 1024), F32),
        scratch_shapes=[pltpu.VMEM((rows, 1), F32), pltpu.VMEM((rows, 1), F32), pltpu.VMEM((rows, LANE), F32)],
        name='nsa_prompt',
        compiler_params=pltpu.CompilerParams(dimension_semantics=('parallel', 'arbitrary'),
                                             vmem_limit_bytes=VMEM_LIMIT),
    )(qa, gates, kv16, kv16, kv16, kv16, kc, kc, tb, lb, pool_far, expand)


def _mla_kernel(q_ref, lat_ref, wuv_ref, o_ref, m_scr, l_scr, acc_scr):
    qb = pl.program_id(1)
    nh = N_HEADS_B
    rows = nh * QB
    q = q_ref[0]
    q8 = jnp.concatenate([q[:, h * 256:(h + 1) * 256] for h in range(nh)], axis=0)
    m_scr[...] = jnp.full(m_scr.shape, -jnp.inf, F32)
    l_scr[...] = jnp.zeros(l_scr.shape, F32)
    acc_scr[...] = jnp.zeros(acc_scr.shape, F32)
    nsub = KT // QB

    def tile(kt, masked):
        k0 = pl.multiple_of(kt * KT, KT)
        lat = lat_ref[0, pl.ds(k0, KT), :]
        s = _dot_t(q8, lat)
        if masked:
            col = lax.broadcasted_iota(jnp.int32, (rows, KT), 1)
            row = lax.broadcasted_iota(jnp.int32, (rows, KT), 0) & (QB - 1)
            s = jnp.where(col - row <= qb * QB - kt * KT, s, NEG)
        m_old = m_scr[...]
        m_new = jnp.maximum(m_old, jnp.max(s, axis=1, keepdims=True))
        alpha = jnp.exp(m_old - m_new)
        p = jnp.exp(s - m_new)
        l_scr[...] = alpha * l_scr[...] + jnp.sum(p, axis=1, keepdims=True)
        acc_scr[...] = alpha * acc_scr[...] + _dot(p.astype(BF16), lat[:, 0:KV_LORA])
        m_scr[...] = m_new

    def full_tile(kt, carry):
        tile(kt, False)
        return carry

    lax.fori_loop(0, qb // nsub, full_tile, 0)
    tile(qb // nsub, True)
    o_lat = (acc_scr[...] * (1.0 / l_scr[...])).astype(BF16)
    for h in range(nh):
        o_ref[0, :, h * LANE:(h + 1) * LANE] = _dot(o_lat[h * QB:(h + 1) * QB], wuv_ref[h])


def _mla_prompt(qm, lat16, w_uv):
    B, S, _ = qm.shape
    rows = N_HEADS_B * QB
    one = pl.Buffered(1)
    return pl.pallas_call(
        _mla_kernel, grid=(B, S // QB),
        in_specs=[pl.BlockSpec((1, QB, 2048), lambda b, t: (b, t, 0)),
                  pl.BlockSpec((1, S, 256), lambda b, t: (b, 0, 0), pipeline_mode=one),
                  pl.BlockSpec(w_uv.shape, lambda b, t: (0, 0, 0), pipeline_mode=one)],
        out_specs=pl.BlockSpec((1, QB, 1024), lambda b, t: (b, t, 0)),
        out_shape=jax.ShapeDtypeStruct((B, S, 1024), F32),
        scratch_shapes=[pltpu.VMEM((rows, 1), F32), pltpu.VMEM((rows, 1), F32), pltpu.VMEM((rows, KV_LORA), F32)],
        name='mla_prompt',
        compiler_params=pltpu.CompilerParams(dimension_semantics=('parallel', 'arbitrary'),
                                             vmem_limit_bytes=VMEM_LIMIT),
    )(qm, lat16, w_uv)


def _adaln_kernel(c_ref, w_ref, b_ref, o_ref):
    c = c_ref[...]
    a = (c * (1.0 / (1.0 + jnp.exp(-c)))).astype(BF16)
    o_ref[...] = _dot(a, w_ref[...].astype(BF16)) + b_ref[...]


def _adaln(c, w_ada, b_ada, tn=512):
    R_, D = c.shape
    N = w_ada.shape[1]
    return pl.pallas_call(
        _adaln_kernel, grid=(N // tn,),
        in_specs=[pl.BlockSpec((R_, D), lambda j: (0, 0)), pl.BlockSpec((D, tn), lambda j: (0, j)),
                  pl.BlockSpec((1, tn), lambda j: (0, j))],
        out_specs=pl.BlockSpec((R_, tn), lambda j: (0, j)),
        out_shape=jax.ShapeDtypeStruct((R_, N), F32), name='adaln',
        compiler_params=pltpu.CompilerParams(dimension_semantics=('parallel',), vmem_limit_bytes=VMEM_LIMIT),
    )(c, w_ada, b_ada.reshape(1, N))


def _merge_kernel(x_ref, oa_ref, ob_ref, ga_ref, shf_ref, scf_ref, gf_ref, na_ref, nb_ref, nffn_ref,
                  wa_ref, wb_ref, wr_ref, wgs_ref, wus_ref, wds_ref, xs_ref, f_ref, sc_ref):
    n_real = N_HEADS_A * HEAD_DIM
    oa = oa_ref[0]
    ob = ob_ref[0]
    na = oa * lax.rsqrt(jnp.sum(oa * oa, axis=-1, keepdims=True) * (1.0 / n_real) + EPS) * na_ref[...]
    nb = ob * lax.rsqrt(jnp.sum(ob * ob, axis=-1, keepdims=True) * (1.0 / n_real) + EPS) * nb_ref[...]
    mix = _dot(na.astype(BF16), wa_ref[...]) + _dot(nb.astype(BF16), wb_ref[...])
    x1 = x_ref[0] + ga_ref[0] * mix
    f = x1 * lax.rsqrt(jnp.mean(x1 * x1, axis=-1, keepdims=True) + EPS) * nffn_ref[...]
    f = f * (1.0 + scf_ref[0]) + shf_ref[0]
    f_ref[0] = f
    fb = f.astype(BF16)
    sc_ref[0] = 1.0 / (1.0 + jnp.exp(-_dot(fb, wr_ref[...])))
    g = _dot(fb, wgs_ref[...])
    u = _dot(fb, wus_ref[...])
    hsh = (g * (1.0 / (1.0 + jnp.exp(-g))) * u).astype(BF16)
    xs_ref[0] = x1 + gf_ref[0] * _dot(hsh, wds_ref[...])


def _merge_weights(out_norm_a, out_norm_b, w_out, norm_ffn, w_router, w_gate_s, w_up_s, w_down_s):
    D = w_out.shape[1]
    na = out_norm_a.reshape(N_HEADS_A, 1, HEAD_DIM)
    grp = (jnp.arange(N_HEADS_A) // HPG)[:, None, None]
    half = jnp.arange(2)[None, :, None]
    na_pad = jnp.where(grp == half, na, 0.0).reshape(1, -1)
    nb_pad = jnp.pad(out_norm_b.reshape(N_HEADS_B, V_DIM), ((0, 0), (0, LANE - V_DIM))).reshape(1, -1)
    wa = w_out[:N_HEADS_A * HEAD_DIM].reshape(N_HEADS_A, 1, HEAD_DIM, D)
    wa_pad = jnp.where((grp == half)[..., None], wa, 0.0).reshape(-1, D)
    wb = w_out[N_HEADS_A * HEAD_DIM:].reshape(N_HEADS_B, V_DIM, D)
    wb_pad = jnp.pad(wb, ((0, 0), (0, LANE - V_DIM), (0, 0))).reshape(-1, D)
    bf = lambda a: a.astype(BF16)
    return [na_pad, nb_pad, norm_ffn.reshape(1, D), bf(wa_pad), bf(wb_pad), bf(w_router), bf(w_gate_s),
            bf(w_up_s), bf(w_down_s)]


def _mod_spec(a, tr):
    if a.shape[1] > 1:
        return pl.BlockSpec((1, tr, a.shape[2]), lambda b, t: (b, t, 0))
    return pl.BlockSpec((1, 1, a.shape[2]), lambda b, t: (b, 0, 0))


def _merge(x, oa, ob, gate_a, shift_f, scale_f, gate_f, mw, tr):
    B, T, D = x.shape
    row = lambda n: pl.BlockSpec((1, tr, n), lambda b, t: (b, t, 0))
    out_shape = [jax.ShapeDtypeStruct((B, T, D), F32), jax.ShapeDtypeStruct((B, T, D), F32),
                 jax.ShapeDtypeStruct((B, T, N_EXPERTS), F32)]
    return pl.pallas_call(
        _merge_kernel, grid=(B, T // tr),
        in_specs=[row(D), row(1024), row(1024)] + [_mod_spec(a, tr) for a in (gate_a, shift_f, scale_f, gate_f)]
        + [_const_spec(a.shape) for a in mw],
        out_specs=[row(D), row(D), row(N_EXPERTS)], out_shape=out_shape, name='merge',
        compiler_params=pltpu.CompilerParams(dimension_semantics=('parallel', 'parallel'),
                                             vmem_limit_bytes=VMEM_LIMIT),
    )(x, oa, ob, gate_a, shift_f, scale_f, gate_f, *mw)


def _gather_rows(idx_ref, idx_row, src_hbm, dst, sem):
    def body(r, c):
        pltpu.make_async_copy(src_hbm.at[pl.ds(idx_ref[idx_row, r], 1)], dst.at[pl.ds(r, 1)], sem).start()
        return c
    lax.fori_loop(0, MOE_BLOCK, body, 0, unroll=8)


def _moe_kernel(nused_ref, bexp_ref, rtok_ref, f_hbm, wg_ref, wu_ref, wd_ref, y_ref, xbuf, wgb, wub, wdb, sem):
    i = pl.program_id(0)
    nused = nused_ref[0]

    @pl.when((i == 0) & (nused > 0))
    def _():
        _gather_rows(rtok_ref, 0, f_hbm, xbuf.at[0], sem.at[0])

    @pl.when(i + 1 < nused)
    def _():
        _gather_rows(rtok_ref, i + 1, f_hbm, xbuf.at[(i + 1) % 2], sem.at[(i + 1) % 2])

    @pl.when(i < nused)
    def _():
        slot = i % 2
        pltpu.make_async_copy(f_hbm.at[pl.ds(0, MOE_BLOCK)], xbuf.at[slot], sem.at[slot]).wait()

        @pl.when((i == 0) | (bexp_ref[i] != bexp_ref[jnp.maximum(i - 1, 0)]))
        def _():
            wgb[...] = wg_ref[0].astype(BF16)
            wub[...] = wu_ref[0].astype(BF16)
            wdb[...] = wd_ref[0].astype(BF16)

        x = xbuf[slot].astype(BF16)
        g = _dot(x, wgb[...])
        u = _dot(x, wub[...])
        h = (g * (1.0 / (1.0 + jnp.exp(-g))) * u).astype(BF16)
        y_ref[...] = _dot(h, wdb[...])

    @pl.when(i >= nused)
    def _():
        y_ref[...] = jnp.zeros(y_ref.shape, F32)


def _moe_experts(nused, blk_exp, row_tok, f, w_gate_e, w_up_e, w_down_e):
    n_blocks = blk_exp.shape[0]
    D = f.shape[1]
    de = w_gate_e.shape[2]
    wspec = lambda shp: pl.BlockSpec((1,) + shp, lambda i, nu, be, rt: (be[i], 0, 0))
    return pl.pallas_call(
        _moe_kernel,
        grid_spec=pltpu.PrefetchScalarGridSpec(
            num_scalar_prefetch=3, grid=(n_blocks,),
            in_specs=[pl.BlockSpec(memory_space=pl.ANY), wspec((D, de)), wspec((D, de)), wspec((de, D))],
            out_specs=pl.BlockSpec((MOE_BLOCK, D), lambda i, nu, be, rt: (i, 0)),
            scratch_shapes=[pltpu.VMEM((2, MOE_BLOCK, D), F32), pltpu.VMEM((D, de), BF16),
                            pltpu.VMEM((D, de), BF16), pltpu.VMEM((de, D), BF16),
                            pltpu.SemaphoreType.DMA((2,))]),
        out_shape=jax.ShapeDtypeStruct((n_blocks * MOE_BLOCK, D), F32), name='moe_experts',
        compiler_params=pltpu.CompilerParams(dimension_semantics=('arbitrary',), vmem_limit_bytes=VMEM_LIMIT),
    )(nused, blk_exp, row_tok.reshape(n_blocks, MOE_BLOCK), f, w_gate_e, w_up_e, w_down_e)


def _combine_kernel(pos_ref, yb_hbm, xs_ref, gf_ref, w_ref, nf_ref, o_ref, buf, sem):
    tile = pl.program_id(0) * pl.num_programs(1) + pl.program_id(1)
    ntile = pl.num_programs(0) * pl.num_programs(1)

    def start(t, slot):
        for k in range(TOP_K):
            _gather_rows(pos_ref, t * TOP_K + k, yb_hbm, buf.at[slot, k], sem.at[slot])

    @pl.when(tile == 0)
    def _():
        start(0, 0)

    @pl.when(tile + 1 < ntile)
    def _():
        start(tile + 1, (tile + 1) % 2)

    slot = tile % 2
    w = w_ref[0]
    routed = jnp.zeros(xs_ref.shape[1:], F32)
    for k in range(TOP_K):
        pltpu.make_async_copy(yb_hbm.at[pl.ds(0, MOE_BLOCK)], buf.at[slot, k], sem.at[slot]).wait()
    for k in range(TOP_K):
        routed = routed + w[:, k:k + 1] * buf[slot, k]
    x2 = xs_ref[0] + gf_ref[0] * routed
    o_ref[0] = x2 * lax.rsqrt(jnp.mean(x2 * x2, axis=-1, keepdims=True) + EPS) * nf_ref[...]


def _combine(pos, yb, xs, gate_f, wts, norm_final):
    B, T, D = xs.shape
    tr = MOE_BLOCK
    gspec = (pl.BlockSpec((1, tr, D), lambda b, t, p: (b, t, 0)) if gate_f.shape[1] > 1
             else pl.BlockSpec((1, 1, D), lambda b, t, p: (b, 0, 0)))
    return pl.pallas_call(
        _combine_kernel,
        grid_spec=pltpu.PrefetchScalarGridSpec(
            num_scalar_prefetch=1, grid=(B, T // tr),
            in_specs=[pl.BlockSpec(memory_space=pl.ANY), pl.BlockSpec((1, tr, D), lambda b, t, p: (b, t, 0)), gspec,
                      pl.BlockSpec((1, tr, TOP_K), lambda b, t, p: (b, t, 0)),
                      pl.BlockSpec((1, D), lambda b, t, p: (0, 0))],
            out_specs=pl.BlockSpec((1, tr, D), lambda b, t, p: (b, t, 0)),
            scratch_shapes=[pltpu.VMEM((2, TOP_K, tr, D), F32), pltpu.SemaphoreType.DMA((2,))]),
        out_shape=jax.ShapeDtypeStruct((B, T, D), F32), name='combine',
        compiler_params=pltpu.CompilerParams(dimension_semantics=('arbitrary', 'arbitrary'),
                                             vmem_limit_bytes=VMEM_LIMIT),
    )(pos, yb, xs, gate_f, wts, norm_final.reshape(1, D))


def _route(scores, router_bias):
    n_tok = scores.shape[0]
    biased = scores + router_bias.astype(F32)
    grp = biased.reshape(n_tok, N_GROUPS, N_EXPERTS // N_GROUPS)
    grp_score = lax.top_k(grp, 2)[0].sum(-1)
    _, top_g = lax.top_k(grp_score, TOP_GROUPS)
    gmask = (top_g[:, :, None] == jnp.arange(N_GROUPS)[None, None, :]).any(axis=1)
    biased = jnp.where(jnp.repeat(gmask, N_EXPERTS // N_GROUPS, axis=1), biased, NEG)
    _, idx = lax.top_k(biased, TOP_K)
    wts = jnp.take_along_axis(scores, idx, axis=1)
    wts = wts / wts.sum(-1, keepdims=True) * ROUTED_SCALE
    n_asg = n_tok * TOP_K
    flat_e = idx.reshape(n_asg)
    order = jnp.argsort(flat_e)
    se = flat_e[order]
    st = (order // TOP_K).astype(jnp.int32)
    counts = jnp.sum((flat_e[:, None] == jnp.arange(N_EXPERTS)[None, :]).astype(jnp.int32), axis=0)
    padded = (counts + MOE_BLOCK - 1) // MOE_BLOCK * MOE_BLOCK
    pad_end = jnp.cumsum(padded)
    dest = ((pad_end - padded)[se] + jnp.arange(n_asg) - (jnp.cumsum(counts) - counts)[se]).astype(jnp.int32)
    n_blocks = -(-(n_asg + N_EXPERTS * (MOE_BLOCK - 1)) // MOE_BLOCK)
    n_rows = n_blocks * MOE_BLOCK
    row_tok = jnp.zeros((n_rows,), jnp.int32).at[dest].set(st)
    pos = jnp.zeros((n_asg,), jnp.int32).at[order].set(dest).reshape(n_tok, TOP_K)
    blk_exp = jnp.minimum(jnp.searchsorted(pad_end, jnp.arange(n_blocks) * MOE_BLOCK, side='right'),
                          N_EXPERTS - 1).astype(jnp.int32)
    nused = (pad_end[-1] // MOE_BLOCK).astype(jnp.int32).reshape(1)
    return wts, pos, row_tok, blk_exp, nused


PAGE = 128


def _softmax_with_new(s, s_new):
    m = jnp.maximum(jnp.max(s, axis=1, keepdims=True), s_new)
    p = jnp.exp(s - m)
    pn = jnp.exp(s_new - m)
    return p, pn, 1.0 / (jnp.sum(p, axis=1, keepdims=True) + pn)


def _samp_cmp_kernel(pt_ref, pool_hbm, q_ref, wbd_ref, pe_ref, w1c_ref, w2_ref, bc_ref, pf_ref,
                     oc_ref, idx_ref, buf, peh_scr, sem, *, n_pages, n_cmp):
    s = pl.program_id(0)
    ns = pl.num_programs(0)
    nc = n_pages * PAGE // CMP_STRIDE

    def start(smp, slot):
        def body(p, c):
            r0 = pl.multiple_of(pt_ref[smp, p] * PAGE, PAGE)
            for j in range(2):
                pltpu.make_async_copy(pool_hbm.at[pl.ds(r0, PAGE), pl.ds(j * LANE, LANE)],
                                      buf.at[slot, j, pl.ds(p * PAGE, PAGE)], sem.at[slot]).start()
            return c
        lax.fori_loop(0, n_pages, body, 0)

    @pl.when(s == 0)
    def _():
        start(0, 0)
        for j in range(2):
            pp = _dot(pe_ref[j], w1c_ref[j])
            peh_scr[j] = jnp.broadcast_to(pp[0:1, 0:CMP_HIDDEN] + pp[1:2, CMP_HIDDEN:], (8, CMP_HIDDEN))

    @pl.when(s + 1 < ns)
    def _():
        start(s + 1, (s + 1) % 2)

    slot = s % 2
    for j in range(2):
        pltpu.make_async_copy(pool_hbm.at[pl.ds(0, n_pages * PAGE), pl.ds(j * LANE, LANE)], buf.at[slot, j],
                              sem.at[slot]).wait()

    rows = lax.broadcasted_iota(jnp.int32, (nc, LANE), 0)
    kvc = []
    for j in range(2):
        acc = jnp.zeros((nc, 4 * CMP_HIDDEN), F32)
        for r in range(CMP_STRIDE):
            x = buf[slot, j, pl.ds(r, nc, stride=CMP_STRIDE), :].astype(BF16)
            acc = acc + _dot(x, wbd_ref[j, r])
        out = jnp.zeros((nc, LANE), F32)
        for g in range(N_KV_A):
            a = acc[:, g * 2 * CMP_HIDDEN:g * 2 * CMP_HIDDEN + CMP_HIDDEN]
            bm = acc[:, g * 2 * CMP_HIDDEN + CMP_HIDDEN:(g + 1) * 2 * CMP_HIDDEN]
            hid = a + pltpu.roll(bm, nc - 1, 0) + peh_scr[j, 0:1]
            act = hid * (1.0 / (1.0 + jnp.exp(-hid)))
            out = out + _dot(act.astype(BF16), w2_ref[j, g])
        kvc.append(jnp.where(rows < n_cmp, out, 0.0).astype(BF16))
    kc, vc = kvc

    q8 = q_ref[0]
    sc = _dot_t(q8, kc) + bc_ref[...]
    p = jnp.exp(sc - jnp.max(sc, axis=1, keepdims=True))
    p = p * (1.0 / jnp.sum(p, axis=1, keepdims=True))
    oc_ref[0] = _dot(p.astype(BF16), vc)

    pg = jnp.concatenate([jnp.sum(p[0:HPG], axis=0, keepdims=True), jnp.sum(p[HPG:2 * HPG], axis=0, keepdims=True),
                          jnp.zeros((8 - N_KV_A, nc), F32)], axis=0)
    imp = _hilo_dot(pg, pf_ref[...])
    lane = lax.broadcasted_iota(jnp.int32, (8, LANE), 1).astype(F32)
    score = jnp.where(lane == 0.0, FORCED, jnp.where(lane >= float(LANE - N_LOCAL_SLC + 1), FORCED, imp))
    picks = jnp.full((8, LANE), float(LANE), F32)
    for k in range(N_SLC - 1):
        mx = jnp.max(score, axis=1, keepdims=True)
        first = jnp.min(jnp.where(score == mx, lane, float(LANE)), axis=1, keepdims=True)
        picks = jnp.where(lane == float(k), first, picks)
        score = jnp.where(lane == first, -jnp.inf, score)
    idx_ref[0] = picks.astype(jnp.int32)


def _samp_cmp(page_table, pool2d, q3, scw, bc, n_cmp):
    Bd, n_pages = page_table.shape
    nc = n_pages * PAGE // CMP_STRIDE
    wbd, pe, w1c, w2g = scw
    m = jnp.arange(nc)
    pool_m = ((m[:, None] // SLC_PER_CMP == jnp.arange(LANE)[None, :]) & (m[:, None] < n_cmp)).astype(BF16)
    one = pl.Buffered(1)
    cst = lambda a: pl.BlockSpec(a.shape, lambda s, pt: (0,) * a.ndim, pipeline_mode=one)
    return pl.pallas_call(
        functools.partial(_samp_cmp_kernel, n_pages=n_pages, n_cmp=n_cmp),
        grid_spec=pltpu.PrefetchScalarGridSpec(
            num_scalar_prefetch=1, grid=(Bd,),
            in_specs=[pl.BlockSpec(memory_space=pl.ANY), pl.BlockSpec((1, 8, LANE), lambda s, pt: (s, 0, 0)),
                      cst(wbd), cst(pe), cst(w1c), cst(w2g), cst(bc), cst(pool_m)],
            out_specs=[pl.BlockSpec((1, 8, LANE), lambda s, pt: (s, 0, 0)),
                       pl.BlockSpec((1, 8, LANE), lambda s, pt: (s, 0, 0))],
            scratch_shapes=[pltpu.VMEM((2, 2, n_pages * PAGE, LANE), F32), pltpu.VMEM((2, 8, CMP_HIDDEN), F32),
                            pltpu.SemaphoreType.DMA((2,))]),
        out_shape=[jax.ShapeDtypeStruct((Bd, 8, LANE), F32), jax.ShapeDtypeStruct((Bd, 8, LANE), jnp.int32)],
        name='sample_cmp',
        compiler_params=pltpu.CompilerParams(dimension_semantics=('arbitrary',), vmem_limit_bytes=VMEM_LIMIT),
    )(page_table, pool2d, q3, wbd, pe, w1c, w2g, bc, pool_m)


def _samp_cmp_weights(cw):
    w1cat, pe, w2a, w2b = cw
    w = w1cat.reshape(2, CMP_STRIDE, HEAD_DIM, 2 * CMP_HIDDEN)
    z = jnp.zeros_like(w)
    wbd = jnp.concatenate([jnp.concatenate([w, z], axis=-1), jnp.concatenate([z, w], axis=-1)], axis=2)
    return wbd, pe, w1cat, jnp.stack([w2a, w2b], axis=1)


def _t5_bias_rows(rel_bias, rel, valid):
    b = rel_bias.astype(F32)[_t5_bucket(rel)]
    return jnp.where(valid[None, :], b.T, NEG)


def _bucket_bias(rel, tbl_t):
    max_exact = N_BUCKETS // 2
    nf = jnp.maximum(rel, 1).astype(F32)
    large = max_exact + (jnp.log(nf / max_exact) / math.log(MAX_DISTANCE / max_exact)
                         * (N_BUCKETS - max_exact)).astype(jnp.int32)
    bucket = jnp.where(rel < max_exact, rel, jnp.minimum(large, N_BUCKETS - 1))
    bias = jnp.zeros(rel.shape, F32)
    for b in range(N_BUCKETS):
        bias = jnp.where(bucket == b, tbl_t[:, b:b + 1], bias)
    return bias


def _samp_sw_kernel(pt_ref, idx_ref, pool_hbm, q_ref, knew_ref, win_ref, wnew_ref, oc_ref, gcol_ref, tblt_ref,
                    bw_ref, oa_ref, nwin_ref, kvbuf, sem, *, past_len):
    s = pl.program_id(0)
    ns = pl.num_programs(0)
    npb = past_len // SLC_BLOCK
    bpp = PAGE // SLC_BLOCK
    nk = N_SLC * SLC_BLOCK

    def start(smp, slot):
        for g in range(N_KV_A):
            for k in range(N_SLC):
                j = jnp.minimum(idx_ref[(smp * N_KV_A + g) * N_SLC + k], npb - 1)
                r0 = pl.multiple_of(pt_ref[smp, j // bpp] * PAGE + (j % bpp) * SLC_BLOCK, SLC_BLOCK)
                pltpu.make_async_copy(pool_hbm.at[pl.ds(r0, SLC_BLOCK)],
                                      kvbuf.at[slot, g, pl.ds(k * SLC_BLOCK, SLC_BLOCK)], sem.at[slot]).start()

    @pl.when(s == 0)
    def _():
        start(0, 0)

    @pl.when(s + 1 < ns)
    def _():
        start(s + 1, (s + 1) % 2)

    slot = s % 2
    q8 = q_ref[0]
    q32 = q8.astype(F32)
    tbl_t = tblt_ref[...]
    row = lax.broadcasted_iota(jnp.int32, (8, LANE), 0)
    lane = lax.broadcasted_iota(jnp.int32, (8, LANE), 1)
    grp0 = row[:, 0:1] < HPG

    def new_token(kv_row):
        kn = kv_row[:, 0:LANE].astype(BF16).astype(F32)
        vn = kv_row[:, LANE:].astype(BF16).astype(F32)
        return jnp.sum(q32 * kn, axis=1, keepdims=True) + tbl_t[:, 0:1], vn

    w = win_ref[0]
    s_new, v_new = new_token(wnew_ref[0])
    sw = _dot_t(q8, w[:, 0:LANE].astype(BF16)) + bw_ref[...]
    p, pn, inv = _softmax_with_new(sw, s_new)
    o_w = (_dot(p.astype(BF16), w[:, LANE:].astype(BF16)) + pn * v_new) * inv
    wrow = lax.broadcasted_iota(jnp.int32, w.shape, 0)
    nwin_ref[0] = jnp.where(wrow == w.shape[0] - 1, wnew_ref[0], pltpu.roll(w, w.shape[0] - 1, 0))

    for g in range(N_KV_A):
        pltpu.make_async_copy(pool_hbm.at[pl.ds(0, nk)], kvbuf.at[slot, g], sem.at[slot]).wait()
    kl = lax.broadcasted_iota(jnp.int32, (8, nk), 1)
    kslot = kl >> 6
    kpos = kl & (SLC_BLOCK - 1)
    rowk = lax.broadcasted_iota(jnp.int32, (8, nk), 0) < HPG
    base = jnp.zeros((8, nk), jnp.int32)
    for k in range(N_SLC):
        j0 = idx_ref[(s * N_KV_A) * N_SLC + k] * SLC_BLOCK
        j1 = idx_ref[(s * N_KV_A + 1) * N_SLC + k] * SLC_BLOCK
        base = jnp.where(kslot == k, jnp.where(rowk, j0, j1), base)
    rel = past_len - (base + kpos)
    ks = [kvbuf[slot, g, :, 0:LANE].astype(BF16) for g in range(N_KV_A)]
    vs = [kvbuf[slot, g, :, LANE:].astype(BF16) for g in range(N_KV_A)]
    ss = jnp.where(rowk, _dot_t(q8, ks[0]), _dot_t(q8, ks[1])) + _bucket_bias(jnp.maximum(rel, 0), tbl_t)
    ss = jnp.where(rel > 0, ss, NEG)
    s_new, v_new = new_token(knew_ref[0])
    p, pn, inv = _softmax_with_new(ss, s_new)
    pb = p.astype(BF16)
    o_s = (jnp.where(grp0, _dot(pb, vs[0]), _dot(pb, vs[1])) + pn * v_new) * inv

    gc = gcol_ref[0]
    o = gc[:, 0:1] * oc_ref[0] + gc[:, 1:2] * o_s + gc[:, 2:3] * o_w
    oa_ref[0] = jnp.where((lane >= HEAD_DIM) == (row >= HPG), o, 0.0)


def _samp_sw(page_table, idx_flat, pool2d, q3, knew, win_buf, wnew, o_c, gcol, tbl_t, bw, past_len):
    Bd = page_table.shape[0]
    wl = win_buf.shape[1]
    per = lambda shp: pl.BlockSpec((1,) + shp, lambda s, pt, ix: (s, 0, 0))
    cst = lambda a: pl.BlockSpec(a.shape, lambda s, pt, ix: (0,) * a.ndim)
    return pl.pallas_call(
        functools.partial(_samp_sw_kernel, past_len=past_len),
        grid_spec=pltpu.PrefetchScalarGridSpec(
            num_scalar_prefetch=2, grid=(Bd,),
            in_specs=[pl.BlockSpec(memory_space=pl.ANY), per((8, LANE)), per((1, 2 * LANE)), per((wl, 2 * LANE)),
                      per((1, 2 * LANE)), per((8, LANE)), per((8, LANE)), cst(tbl_t), cst(bw)],
            out_specs=[per((8, LANE)), per((wl, 2 * LANE))],
            scratch_shapes=[pltpu.VMEM((2, N_KV_A, N_SLC * SLC_BLOCK, 2 * LANE), F32),
                            pltpu.SemaphoreType.DMA((2,))]),
        out_shape=[jax.ShapeDtypeStruct((Bd, 8, LANE), F32), jax.ShapeDtypeStruct(win_buf.shape, F32)],
        name='sample_slc_win',
        compiler_params=pltpu.CompilerParams(dimension_semantics=('arbitrary',), vmem_limit_bytes=VMEM_LIMIT),
    )(page_table, idx_flat, pool2d, q3, knew, win_buf, wnew, o_c, gcol, tbl_t, bw)


def _samp_mla_kernel(pt_ref, pool_hbm, q_ref, lnew_ref, o_ref, buf, sem, *, n_pages):
    s = pl.program_id(0)
    ns = pl.num_programs(0)

    def start(smp, slot):
        def body(p, c):
            r0 = pl.multiple_of(pt_ref[smp, p] * PAGE, PAGE)
            pltpu.make_async_copy(pool_hbm.at[pl.ds(r0, PAGE)], buf.at[slot, pl.ds(p * PAGE, PAGE)],
                                  sem.at[slot]).start()
            return c
        lax.fori_loop(0, n_pages, body, 0)

    @pl.when(s == 0)
    def _():
        start(0, 0)

    @pl.when(s + 1 < ns)
    def _():
        start(s + 1, (s + 1) % 2)

    slot = s % 2
    pltpu.make_async_copy(pool_hbm.at[pl.ds(0, n_pages * PAGE)], buf.at[slot], sem.at[slot]).wait()
    q8 = q_ref[0]
    ckv = buf[slot, :, 0:KV_LORA].astype(BF16)
    kr = buf[slot, :, KV_LORA:LATENT_DIM].astype(BF16)
    ln = lnew_ref[0].astype(F32)
    sc = _dot_t(q8[:, 0:KV_LORA], ckv) + _dot_t(q8[:, KV_LORA:LATENT_DIM], kr)
    s_new = jnp.sum(q8.astype(F32) * ln, axis=1, keepdims=True)
    p, pn, inv = _softmax_with_new(sc, s_new)
    o_ref[0] = (_dot(p.astype(BF16), ckv) + pn * ln[:, 0:KV_LORA]) * inv


def _samp_mla(page_table, pool2d, qm3, lnew):
    Bd, n_pages = page_table.shape
    per = lambda shp: pl.BlockSpec((1,) + shp, lambda s, pt: (s, 0, 0))
    return pl.pallas_call(
        functools.partial(_samp_mla_kernel, n_pages=n_pages),
        grid_spec=pltpu.PrefetchScalarGridSpec(
            num_scalar_prefetch=1, grid=(Bd,),
            in_specs=[pl.BlockSpec(memory_space=pl.ANY), per((8, 256)), per((1, 256))],
            out_specs=per((8, KV_LORA)),
            scratch_shapes=[pltpu.VMEM((2, n_pages * PAGE, LATENT_DIM), F32), pltpu.SemaphoreType.DMA((2,))]),
        out_shape=jax.ShapeDtypeStruct((Bd, 8, KV_LORA), F32), name='sample_mla',
        compiler_params=pltpu.CompilerParams(dimension_semantics=('arbitrary',), vmem_limit_bytes=VMEM_LIMIT),
    )(page_table, pool2d, qm3, lnew)


def _uv_kernel(o_ref, w_ref, y_ref):
    y_ref[...] = _dot(o_ref[...].astype(BF16), w_ref[0])


def _samp_uv(o_lat2d, w_uv):
    Bd = o_lat2d.shape[0]
    return pl.pallas_call(
        _uv_kernel, grid=(N_HEADS_B,),
        in_specs=[pl.BlockSpec((Bd, LANE), lambda h: (0, h)), pl.BlockSpec((1, KV_LORA, LANE), lambda h: (h, 0, 0))],
        out_specs=pl.BlockSpec((Bd, LANE), lambda h: (0, h)),
        out_shape=jax.ShapeDtypeStruct((Bd, N_HEADS_B * LANE), F32), name='sample_uv',
        compiler_params=pltpu.CompilerParams(dimension_semantics=('parallel',), vmem_limit_bytes=VMEM_LIMIT),
    )(o_lat2d, w_uv)


def _sample_mix(xs3, msm, pool_cmp, pool_slc, win_buf, pool_mla, page_table, gain, rel_bias, w, cw):
    Bd = xs3.shape[1]
    past_len = page_table.shape[1] * PAGE
    cos_s, sin_s = _rope_tables(jnp.full((Bd,), past_len, jnp.int32))
    qa_s, cmp_s, slc_s, win_s, _, gt_s, qm_s, lat_s, lat16_s = _inproj(xs3, msm[0], msm[1], gain, cos_s, sin_s, w, Bd)
    q3 = qa_s.reshape(Bd, N_HEADS_A, LANE)
    nc = past_len // CMP_STRIDE
    n_cmp = (past_len + 1 - CMP_BLOCK) // CMP_STRIDE + 1
    m = jnp.arange(nc)
    bc = _t5_bias_rows(rel_bias, past_len - (m * CMP_STRIDE + CMP_BLOCK - 1), m < n_cmp)
    o_c, idx = _samp_cmp(page_table, pool_cmp.reshape(-1, 2 * LANE), q3, _samp_cmp_weights(cw), bc, n_cmp)
    idx_flat = idx[:, :N_KV_A, :N_SLC].reshape(-1)
    wl = win_buf.shape[1]
    wi = jnp.arange(wl)
    bw = _t5_bias_rows(rel_bias, wl - wi, (wl - wi < WINDOW) & (past_len - wl + wi >= 0))
    tbl_t = jnp.pad(rel_bias.astype(F32).T, ((0, 0), (0, LANE - N_BUCKETS)))
    gcol = gt_s[0, :, :3 * N_HEADS_A].reshape(Bd, 3, N_HEADS_A).transpose(0, 2, 1)
    gcol = jnp.pad(gcol, ((0, 0), (0, 0), (0, LANE - 3)))
    oa_s, new_win = _samp_sw(page_table, idx_flat, pool_slc.reshape(-1, 2 * LANE), q3, slc_s.reshape(Bd, 1, 2 * LANE),
                             win_buf.reshape(Bd, wl, 2 * LANE), win_s.reshape(Bd, 1, 2 * LANE), o_c, gcol, tbl_t, bw,
                             past_len)
    o_lat = _samp_mla(page_table, pool_mla.reshape(-1, LATENT_DIM), qm_s.reshape(Bd, N_HEADS_B, 256),
                      lat16_s.reshape(Bd, 1, 256))
    ob_s = _samp_uv(o_lat.reshape(Bd, N_HEADS_B * KV_LORA), w['w_uv'])
    return (oa_s.reshape(1, Bd, -1), ob_s.reshape(1, Bd, -1), cmp_s, slc_s,
            new_win.reshape(Bd, wl, 2, N_KV_A, HEAD_DIM), lat_s)


def kernel(x_prompt, x_sample, cache_nsa_cmp, cache_nsa_slc, cache_nsa_win, cache_mla, page_table, c_prompt, c_sample, rel_bias, w_ada, b_ada, norm_attn, norm_ffn, w_in, cmp_pe, cmp_w1, cmp_w2, q_norm, w_q_up, kv_norm, w_kv_up, out_norm_a, out_norm_b, w_out, w_router, router_bias, w_gate_e, w_up_e, w_down_e, w_gate_s, w_up_s, w_down_s, norm_final):
    B, S, D = x_prompt.shape
    Bd = x_sample.shape[0]
    l = 0
    n_mod = B + Bd
    c_all = jnp.pad(jnp.concatenate([c_prompt, c_sample], axis=0), ((0, -n_mod % 8), (0, 0)))
    mod = _adaln(c_all, w_ada[l], b_ada[l]).reshape(-1, 6, D)
    mp = [mod[:B, i][:, None, :] for i in range(6)]
    msm = [mod[B:n_mod, i][None] for i in range(6)]

    w = _inproj_weights(w_in[l], q_norm[l], w_q_up[l], kv_norm[l], w_kv_up[l])
    cw = _compress_weights(cmp_pe[l], cmp_w1[l], cmp_w2[l])
    mw = _merge_weights(out_norm_a[l], out_norm_b[l], w_out[l], norm_ffn[l], w_router[l], w_gate_s[l], w_up_s[l],
                        w_down_s[l])
    tb, lb = _bias_tables(rel_bias)

    cos, sin = _rope_tables(jnp.arange(S))
    qa, cmp32, slc32, win32, kv16, gt, qm, lat32, lat16 = _inproj(
        x_prompt, mp[0], mp[1], norm_attn[l], cos, sin, w, 256)
    nc = S // CMP_STRIDE
    n_cmp = (S - CMP_BLOCK) // CMP_STRIDE + 1
    xc = cmp32.reshape(B, nc, CMP_STRIDE, 4, HEAD_DIM).transpose(0, 3, 1, 2, 4).reshape(B, 4, nc, -1).astype(BF16)
    kcv = _compress(xc, cw, n_cmp)
    oa_p = _nsa_prompt(qa, gt, kv16, kcv, tb, lb)
    ob_p = _mla_prompt(qm, lat16, w['w_uv'])
    xs_p, f_p, sc_p = _merge(x_prompt, oa_p, ob_p, mp[2], mp[3], mp[4], mp[5], mw, 256)

    xs3 = x_sample.reshape(1, Bd, D)
    oa_s, ob_s, cmp_s, slc_s, new_win, lat_s = _sample_mix(
        xs3, msm, cache_nsa_cmp[l], cache_nsa_slc[l], cache_nsa_win[l], cache_mla[l], page_table, norm_attn[l],
        rel_bias, w, cw)
    xs_s, f_s, sc_s = _merge(xs3, oa_s, ob_s, msm[2], msm[3], msm[4], msm[5], mw, Bd)

    n_p = B * S
    f_all = jnp.concatenate([f_p.reshape(n_p, D), f_s.reshape(Bd, D)], axis=0)
    sc_all = jnp.concatenate([sc_p.reshape(n_p, N_EXPERTS), sc_s.reshape(Bd, N_EXPERTS)], axis=0)
    wts, pos, row_tok, blk_exp, nused = _route(sc_all, router_bias[l])
    yb = _moe_experts(nused, blk_exp, row_tok, f_all, w_gate_e[l], w_up_e[l], w_down_e[l])
    tile_pos = lambda p: p.reshape(-1, MOE_BLOCK, TOP_K).transpose(0, 2, 1).reshape(-1, MOE_BLOCK)
    y_p = _combine(tile_pos(pos[:n_p]), yb, xs_p, mp[5], wts[:n_p].reshape(B, S, TOP_K), norm_final)
    y_s = _combine(tile_pos(pos[n_p:]), yb, xs_s, msm[5], wts[n_p:].reshape(1, Bd, TOP_K), norm_final)

    sh6 = lambda a, b, t: a.reshape(1, b, t, 2, N_KV_A, HEAD_DIM)
    return (y_p, y_s.reshape(Bd, 1, D), sh6(cmp32, B, S), sh6(cmp_s, Bd, 1), sh6(slc32, B, S), sh6(slc_s, Bd, 1),
            sh6(win32[:, S - WINDOW:], B, WINDOW), new_win[None], lat32[None], lat_s.reshape(1, Bd, 1, LATENT_DIM))
```

```python
import functools
import math

import jax
import jax.numpy as jnp
from jax import lax
from jax.experimental import pallas as pl
from jax.experimental.pallas import tpu as pltpu

F32 = jnp.float32
BF16 = jnp.bfloat16

LANE = 128
VMEM_LIMIT = 56 * 1024 * 1024

HEAD_DIM = 64
N_HEADS_A = 8
N_KV_A = 2
HPG = N_HEADS_A // N_KV_A
CMP_BLOCK = 32
CMP_STRIDE = 16
CMP_HIDDEN = 128
SLC_BLOCK = 64
SLC_PER_CMP = SLC_BLOCK // CMP_STRIDE
N_SLC = 16
N_LOCAL_SLC = 2
WINDOW = 512
N_HEADS_B = 8
Q_LORA = 192
KV_LORA = 128
QK_NOPE = 64
QK_ROPE = 32
V_DIM = 64
LATENT_DIM = KV_LORA + QK_ROPE
ROPE_THETA = 10000.0
MLA_SCALE = (QK_NOPE + QK_ROPE) ** -0.5
N_BUCKETS = 32
MAX_DISTANCE = 128
N_EXPERTS = 256
TOP_K = 8
N_GROUPS = 8
TOP_GROUPS = 4
ROUTED_SCALE = 2.5
MOE_BLOCK = 128
EPS = 1e-6
NEG = -1e30
FORCED = 1e30

QB = 128
KT = 512
CMP_PAD = 16
LOC_W = 24


def _dot(a, b):
    return jnp.dot(a, b, preferred_element_type=F32)


def _dot_t(a, b):
    return lax.dot_general(a, b, (((1,), (1,)), ((), ())), preferred_element_type=F32)


def _const_spec(shape):
    nd = len(shape)
    return pl.BlockSpec(shape, lambda *_: (0,) * nd)


def _inproj_kernel(x_ref, sh_ref, sc_ref, g_ref, cs_ref, sn_ref, wq_ref, wkv_ref, wg_ref, wqd_ref,
                   wkvd_ref, qn_ref, wqup_ref, bd_ref, plc_ref, kvn_ref,
                   qa_ref, cmp_ref, slc_ref, win_ref, kv16_ref, gt_ref, qm_ref, lat_ref, lat16_ref):
    x = x_ref[0]
    ms = jnp.mean(x * x, axis=-1, keepdims=True)
    xn = x * lax.rsqrt(ms + EPS) * g_ref[...]
    h = xn * (1.0 + sc_ref[0]) + sh_ref[0]
    hb = h.astype(BF16)
    qa_ref[0] = _dot(hb, wq_ref[...]).astype(BF16)
    kv = _dot(hb, wkv_ref[...])
    cmp_ref[0] = kv[:, 0:256]
    slc_ref[0] = kv[:, 256:512]
    win_ref[0] = kv[:, 512:768]
    kv16_ref[0] = kv.astype(BF16)
    gl = _dot(hb, wg_ref[...])
    gt_ref[0] = 1.0 / (1.0 + jnp.exp(-gl))
    qd = _dot(hb, wqd_ref[...])
    qn = qd * lax.rsqrt(jnp.sum(qd * qd, axis=-1, keepdims=True) * (1.0 / Q_LORA) + EPS) * qn_ref[...]
    qu = _dot(qn.astype(BF16), wqup_ref[...])
    cs = cs_ref[...]
    sn = sn_ref[...]
    qr = qu[:, 512:768] * cs + qu[:, 768:1024] * sn
    qm = _dot(qu[:, 0:512].astype(BF16), bd_ref[...]) + _dot(qr.astype(BF16), plc_ref[...])
    qm_ref[0] = (qm * MLA_SCALE).astype(BF16)
    kvd = _dot(hb, wkvd_ref[...])
    c = kvd[:, 0:128]
    ckv = c * lax.rsqrt(jnp.mean(c * c, axis=-1, keepdims=True) + EPS) * kvn_ref[...]
    kr = kvd[:, 128:256] * cs[:, 0:128] + kvd[:, 256:384] * sn[:, 0:128]
    lat_ref[0, :, 0:128] = ckv
    lat_ref[0, :, 128:160] = kr[:, 0:32]
    lat16_ref[0, :, 0:128] = ckv.astype(BF16)
    lat16_ref[0, :, 128:256] = kr.astype(BF16)


def _inproj_weights(w_in, q_norm, w_q_up, kv_norm, w_kv_up):
    D = w_in.shape[0]
    o1 = N_HEADS_A * HEAD_DIM
    o2 = o1 + 6 * N_KV_A * HEAD_DIM
    o3 = o2 + 3 * N_HEADS_A
    o4 = o3 + Q_LORA
    wq = w_in[:, :o1].reshape(D, N_HEADS_A, HEAD_DIM) * (HEAD_DIM ** -0.5)
    z = jnp.zeros_like(wq)
    grp = (jnp.arange(N_HEADS_A) // HPG)[None, :, None]
    wq_pad = jnp.concatenate([jnp.where(grp == 0, wq, z), jnp.where(grp == 1, wq, z)], axis=-1)
    wq_pad = wq_pad.reshape(D, N_HEADS_A * 2 * HEAD_DIM)
    wkv = w_in[:, o1:o2]
    wg = jnp.pad(w_in[:, o2:o3], ((0, 0), (0, LANE - 3 * N_HEADS_A)))
    wqd = jnp.pad(w_in[:, o3:o4], ((0, 0), (0, 256 - Q_LORA)))
    wkd = w_in[:, o4:]
    half = QK_ROPE // 2
    wc = wkd[:, :KV_LORA]
    wr = wkd[:, KV_LORA:]
    wrot = jnp.concatenate([-wr[:, half:], wr[:, :half]], axis=1)
    padr = ((0, 0), (0, LANE - QK_ROPE))
    wkvd = jnp.concatenate([wc, jnp.pad(wr, padr), jnp.pad(wrot, padr)], axis=1)
    qn = jnp.pad(q_norm, (0, 256 - Q_LORA)).reshape(1, 256)
    wu = jnp.pad(w_q_up, ((0, 256 - Q_LORA), (0, 0))).reshape(256, N_HEADS_B, QK_NOPE + QK_ROPE)
    wu_n = wu[:, :, :QK_NOPE].reshape(256, N_HEADS_B * QK_NOPE)
    wu_r = wu[:, :, QK_NOPE:]
    wu_rot = jnp.concatenate([-wu_r[:, :, half:], wu_r[:, :, :half]], axis=-1)
    wqup = jnp.concatenate([wu_n, wu_r.reshape(256, -1), wu_rot.reshape(256, -1)], axis=1)
    w_ukv = w_kv_up.reshape(KV_LORA, N_HEADS_B, QK_NOPE + V_DIM)
    w_uk = w_ukv[:, :, :QK_NOPE]
    eye = jnp.eye(N_HEADS_B, dtype=F32)
    bd = jnp.einsum('chn,hk->hnkc', w_uk, eye)
    bd = jnp.pad(bd, ((0, 0), (0, 0), (0, 0), (0, 256 - KV_LORA))).reshape(N_HEADS_B * QK_NOPE, N_HEADS_B * 256)
    plc = jnp.einsum('hk,rs->hrks', eye, jnp.eye(QK_ROPE, dtype=F32))
    plc = jnp.pad(plc, ((0, 0), (0, 0), (0, 0), (KV_LORA, 256 - KV_LORA - QK_ROPE)))
    plc = plc.reshape(N_HEADS_B * QK_ROPE, N_HEADS_B * 256)
    w_uv = jnp.pad(w_ukv[:, :, QK_NOPE:].transpose(1, 0, 2), ((0, 0), (0, 0), (0, LANE - V_DIM)))
    bf = lambda a: a.astype(BF16)
    return dict(wq=bf(wq_pad), wkv=bf(wkv), wg=bf(wg), wqd=bf(wqd), wkvd=bf(wkvd), qn=qn, wqup=bf(wqup),
                bd=bf(bd), plc=bf(plc), kvn=kv_norm.reshape(1, KV_LORA), w_uv=bf(w_uv))


def _rope_tables(pos):
    half = QK_ROPE // 2
    inv = ROPE_THETA ** (-jnp.arange(half, dtype=F32) / half)
    ang = pos.astype(F32)[:, None] * inv[None, :]
    cos = jnp.tile(jnp.cos(ang), (1, 2 * N_HEADS_B))
    sin = jnp.tile(jnp.sin(ang), (1, 2 * N_HEADS_B))
    return cos, sin


def _inproj(x, shift, scale, gain, cos, sin, w, tr):
    B, T, D = x.shape
    tm = shift.shape[1]
    mod_spec = pl.BlockSpec((1, tr if tm > 1 else 1, D), (lambda b, t: (b, t, 0)) if tm > 1 else (lambda b, t: (b, 0, 0)))
    row = lambda n: pl.BlockSpec((1, tr, n), lambda b, t: (b, t, 0))
    tab = pl.BlockSpec((tr, 256), lambda b, t: (t, 0))
    wnames = ['wq', 'wkv', 'wg', 'wqd', 'wkvd', 'qn', 'wqup', 'bd', 'plc', 'kvn']
    out_shape = [
        jax.ShapeDtypeStruct((B, T, 1024), BF16), jax.ShapeDtypeStruct((B, T, 256), F32),
        jax.ShapeDtypeStruct((B, T, 256), F32), jax.ShapeDtypeStruct((B, T, 256), F32),
        jax.ShapeDtypeStruct((B, T, 768), BF16), jax.ShapeDtypeStruct((B, T, LANE), F32),
        jax.ShapeDtypeStruct((B, T, 2048), BF16), jax.ShapeDtypeStruct((B, T, LATENT_DIM), F32),
        jax.ShapeDtypeStruct((B, T, 256), BF16)]
    in_specs = [row(D), mod_spec, mod_spec, _const_spec((1, D)), tab, tab]
    in_specs += [_const_spec(w[n].shape) for n in wnames[:5]]
    in_specs += [_const_spec(w['qn'].shape)] + [_const_spec(w[n].shape) for n in wnames[6:9]]
    in_specs += [_const_spec(w['kvn'].shape)]
    return pl.pallas_call(
        _inproj_kernel, grid=(B, T // tr), in_specs=in_specs,
        out_specs=[row(s.shape[-1]) for s in out_shape], out_shape=out_shape, name='inproj',
        compiler_params=pltpu.CompilerParams(dimension_semantics=('parallel', 'parallel'),
                                             vmem_limit_bytes=VMEM_LIMIT),
    )(x, shift, scale, gain.reshape(1, D), cos, sin, *[w[n] for n in wnames])


def _compress_kernel(x0_ref, x1_ref, w1_ref, pe_ref, w2a_ref, w2b_ref, o_ref, *, n_cmp):
    w1 = w1_ref[0]
    pp = _dot(pe_ref[0], w1)
    peh = pp[0:1, 0:CMP_HIDDEN] + pp[1:2, CMP_HIDDEN:]
    nc = x0_ref.shape[2]
    out = jnp.zeros((nc, LANE), F32)
    for x_ref, w2_ref in ((x0_ref, w2a_ref), (x1_ref, w2b_ref)):
        ab = _dot(x_ref[0, 0], w1)
        hid = ab[:, 0:CMP_HIDDEN] + pltpu.roll(ab[:, CMP_HIDDEN:], nc - 1, 0) + peh
        act = hid * (1.0 / (1.0 + jnp.exp(-hid)))
        out = out + _dot(act.astype(BF16), w2_ref[0])
    rows = lax.broadcasted_iota(jnp.int32, (nc, LANE), 0)
    out = jnp.where(rows < n_cmp, out, 0.0)
    o_ref[0, 0, 0:CMP_PAD] = jnp.zeros((CMP_PAD, LANE), F32)
    o_ref[0, 0, CMP_PAD:CMP_PAD + nc] = out
    o_ref[0, 0, CMP_PAD + nc:] = jnp.zeros((o_ref.shape[2] - CMP_PAD - nc, LANE), F32)


def _compress_weights(cmp_pe, cmp_w1, cmp_w2):
    kin = CMP_STRIDE * HEAD_DIM
    w1 = cmp_w1.reshape(2, 2, kin, CMP_HIDDEN)
    w1cat = jnp.concatenate([w1[:, 0], w1[:, 1]], axis=-1).astype(BF16)
    pe = jnp.pad(cmp_pe.reshape(2, 2, kin), ((0, 0), (0, 6), (0, 0))).astype(BF16)
    w2a = jnp.pad(cmp_w2, ((0, 0), (0, 0), (0, HEAD_DIM))).astype(BF16)
    w2b = jnp.pad(cmp_w2, ((0, 0), (0, 0), (HEAD_DIM, 0))).astype(BF16)
    return w1cat, pe, w2a, w2b


def _compress(xc, cw, n_cmp):
    B, _, nc, kin = xc.shape
    w1cat, pe, w2a, w2b = cw
    return pl.pallas_call(
        functools.partial(_compress_kernel, n_cmp=n_cmp), grid=(B, 2),
        in_specs=[pl.BlockSpec((1, 1, nc, kin), lambda b, j: (b, 2 * j, 0, 0)),
                  pl.BlockSpec((1, 1, nc, kin), lambda b, j: (b, 2 * j + 1, 0, 0)),
                  pl.BlockSpec((1, kin, 2 * CMP_HIDDEN), lambda b, j: (j, 0, 0)),
                  pl.BlockSpec((1, 8, kin), lambda b, j: (j, 0, 0)),
                  pl.BlockSpec((1, CMP_HIDDEN, LANE), lambda b, j: (j, 0, 0)),
                  pl.BlockSpec((1, CMP_HIDDEN, LANE), lambda b, j: (j, 0, 0))],
        out_specs=pl.BlockSpec((1, 1, nc + LANE, LANE), lambda b, j: (b, j, 0, 0)),
        out_shape=jax.ShapeDtypeStruct((B, 2, nc + LANE, LANE), F32), name='compress',
        compiler_params=pltpu.CompilerParams(dimension_semantics=('parallel', 'parallel'),
                                             vmem_limit_bytes=VMEM_LIMIT),
    )(xc, xc, w1cat, pe, w2a, w2b)


def _t5_bucket(rel):
    max_exact = N_BUCKETS // 2
    n = jnp.maximum(rel, 0)
    nf = jnp.maximum(n, 1).astype(F32)
    large = max_exact + (jnp.log(nf / max_exact) / math.log(MAX_DISTANCE / max_exact)
                         * (N_BUCKETS - max_exact)).astype(jnp.int32)
    large = jnp.minimum(large, N_BUCKETS - 1)
    return jnp.where(n < max_exact, n, large)


def _bias_tables(rel_bias):
    tbl = rel_bias.astype(F32)
    const = tbl[N_BUCKETS - 1]
    i = jnp.arange(QB)[:, None]
    j = jnp.arange(LANE)[None, :]

    def tab(rel):
        b = tbl[_t5_bucket(rel)]
        return jnp.moveaxis(b, -1, 0) - const[:, None, None]

    t0 = jnp.where((i - j >= 0)[None], tab(i - j), NEG)
    t1 = tab(QB + i - j)
    zero = jnp.zeros_like(t1)
    t4 = jnp.broadcast_to(jnp.where(j > i, 0.0, NEG)[None], t1.shape)
    tb = jnp.stack([t0, t1, zero, jnp.full_like(t1, NEG), t4]).reshape(5, N_HEADS_A * QB, LANE)
    rel_l = i - CMP_STRIDE * (j - CMP_PAD) - (CMP_BLOCK - 1)
    lb = jnp.where(((j < LOC_W) & (rel_l >= 0))[None], tab(rel_l), NEG).reshape(N_HEADS_A * QB, LANE)
    return tb, lb


def _hilo_dot(x, m):
    hi = x.astype(BF16)
    lo = (x - hi.astype(F32)).astype(BF16)
    return _dot(hi, m) + _dot(lo, m)


def _nsa_kernel(q_ref, gt_ref, ks_ref, vs_ref, kw_ref, vw_ref, kc_ref, vc_ref, tb_ref, lb_ref, pf_ref, e_ref,
                o_ref, m_scr, l_scr, acc_scr, *, n_cmp):
    qb = pl.program_id(1)
    nh = N_HEADS_A
    rows = nh * QB
    q = q_ref[0]
    q8 = jnp.concatenate([q[:, h * LANE:(h + 1) * LANE] for h in range(nh)], axis=0)
    nc = pf_ref.shape[0]

    kcf = kc_ref[0, 0, 0:nc].astype(BF16)
    vcf = vc_ref[0, 0, 0:nc].astype(BF16)
    l0 = pl.multiple_of(qb * (QB // CMP_STRIDE), 8)
    kcl = kc_ref[0, 0, pl.ds(l0, LANE)].astype(BF16)
    vcl = vc_ref[0, 0, pl.ds(l0, LANE)].astype(BF16)
    colf = lax.broadcasted_iota(jnp.int32, (1, nc), 1)
    far_ok = jnp.where(colf >= CMP_PAD, jnp.where(colf < l0, 0.0, NEG), NEG)
    coll = lax.broadcasted_iota(jnp.int32, (1, LANE), 1) + (l0 - CMP_PAD)
    loc_ok = jnp.where(coll >= 0, jnp.where(coll < n_cmp, 0.0, NEG), NEG)
    s_far = _dot_t(q8, kcf) + far_ok
    s_loc = _dot_t(q8, kcl) + lb_ref[...] + loc_ok
    mrow = jnp.maximum(jnp.max(s_far, axis=1, keepdims=True), jnp.max(s_loc, axis=1, keepdims=True))
    p_far = jnp.exp(s_far - mrow)
    p_loc = jnp.exp(s_loc - mrow)
    lsum = jnp.sum(p_far, axis=1, keepdims=True) + jnp.sum(p_loc, axis=1, keepdims=True)
    inv = jnp.where(mrow > 0.5 * NEG, 1.0 / lsum, 0.0)
    p_far = p_far * inv
    p_loc = p_loc * inv
    o_c = _dot(p_far.astype(BF16), vcf) + _dot(p_loc.astype(BF16), vcl)

    r_i = lax.broadcasted_iota(jnp.int32, (LANE, LANE), 0)
    c_i = lax.broadcasted_iota(jnp.int32, (LANE, LANE), 1)
    pool_loc = jnp.where(r_i < LOC_W,
                         jnp.where(c_i == (r_i >> 2) + (2 * qb - CMP_PAD // SLC_PER_CMP), 1.0, 0.0),
                         0.0).astype(BF16)
    tq = 2 * qb + jnp.where(r_i >= SLC_BLOCK, 1, 0)
    dist = tq - c_i
    c_f = c_i.astype(F32)
    sel = []
    for g in range(N_KV_A):
        pgf = p_far[(g * HPG) * QB:(g * HPG + 1) * QB]
        pgl = p_loc[(g * HPG) * QB:(g * HPG + 1) * QB]
        for hh in range(1, HPG):
            pgf = pgf + p_far[(g * HPG + hh) * QB:(g * HPG + hh + 1) * QB]
            pgl = pgl + p_loc[(g * HPG + hh) * QB:(g * HPG + hh + 1) * QB]
        imp = _hilo_dot(pgf, pf_ref[...]) + _hilo_dot(pgl, pool_loc)
        score = jnp.where(dist < 0, NEG, jnp.where(dist < N_LOCAL_SLC, FORCED, jnp.where(c_i == 0, FORCED, imp)))
        chosen = jnp.zeros((QB, LANE), F32)
        for _ in range(N_SLC):
            mx = jnp.max(score, axis=1, keepdims=True)
            first = jnp.min(jnp.where(score == mx, c_f, float(LANE)), axis=1, keepdims=True)
            hit = c_f == first
            chosen = jnp.where(hit, 1.0, chosen)
            score = jnp.where(hit, -jnp.inf, score)
        sel.append(chosen.astype(BF16))

    m_scr[...] = jnp.full(m_scr.shape, -jnp.inf, F32)
    l_scr[...] = jnp.zeros(l_scr.shape, F32)
    acc_scr[...] = jnp.zeros(acc_scr.shape, F32)
    nsub = KT // LANE

    def slc_tile(kt, carry):
        k0 = pl.multiple_of(kt * KT, KT)
        s = _dot_t(q8, ks_ref[0, pl.ds(k0, KT), :])
        bias = []
        for c in range(nsub):
            d = qb - (kt * nsub + c)
            bias.append(tb_ref[jnp.where(d < 0, 3, jnp.minimum(d, 2))])
        e_t = e_ref[:, pl.ds(k0, KT)]
        madd = [(_dot(sel[g], e_t) - 1.0) * (-NEG) for g in range(N_KV_A)]
        madd = jnp.concatenate([madd[g] for g in range(N_KV_A) for _ in range(HPG)], axis=0)
        s = s + jnp.concatenate(bias, axis=1) + madd
        m_old = m_scr[...]
        m_new = jnp.maximum(m_old, jnp.max(s, axis=1, keepdims=True))
        alpha = jnp.exp(m_old - m_new)
        p = jnp.exp(s - m_new)
        l_scr[...] = alpha * l_scr[...] + jnp.sum(p, axis=1, keepdims=True)
        acc_scr[...] = alpha * acc_scr[...] + _dot(p.astype(BF16), vs_ref[0, pl.ds(k0, KT), :])
        m_scr[...] = m_new
        return carry

    lax.fori_loop(0, qb // nsub + 1, slc_tile, 0)
    o_s = acc_scr[...] * (1.0 / l_scr[...])

    nwin = WINDOW // QB + 1
    w0 = jnp.maximum(qb - (nwin - 1), 0)
    k0 = pl.multiple_of(w0 * QB, QB)
    s = _dot_t(q8, kw_ref[0, pl.ds(k0, nwin * QB), :])
    bias = []
    for c in range(nwin):
        d = qb - (w0 + c)
        bias.append(tb_ref[jnp.where(d < 0, 3, jnp.where(d >= nwin - 1, 4, jnp.minimum(d, 2)))])
    s = s + jnp.concatenate(bias, axis=1)
    p = jnp.exp(s - jnp.max(s, axis=1, keepdims=True))
    o_w = _dot(p.astype(BF16), vw_ref[0, pl.ds(k0, nwin * QB), :]) * (1.0 / jnp.sum(p, axis=1, keepdims=True))

    gt = gt_ref[0]
    for h in range(nh):
        r = slice(h * QB, (h + 1) * QB)
        comb = (gt[:, h:h + 1] * o_c[r] + gt[:, nh + h:nh + h + 1] * o_s[r]
                + gt[:, 2 * nh + h:2 * nh + h + 1] * o_w[r])
        keep = (c_i >= HEAD_DIM) if h // HPG else (c_i < HEAD_DIM)
        o_ref[0, :, h * LANE:(h + 1) * LANE] = jnp.where(keep, comb, 0.0)


def _nsa_prompt(qa, gates, kv16, kc, tb, lb):
    B, S, _ = qa.shape
    nc = S // CMP_STRIDE
    n_cmp = (S - CMP_BLOCK) // CMP_STRIDE + 1
    m = jnp.arange(nc)
    pool_far = ((m[:, None] // SLC_PER_CMP - CMP_PAD // SLC_PER_CMP == jnp.arange(LANE)[None, :])
                & (m[:, None] >= CMP_PAD)).astype(BF16)
    expand = (jnp.arange(S)[None, :] // SLC_BLOCK == jnp.arange(LANE)[:, None]).astype(BF16)
    rows = N_HEADS_A * QB
    one = pl.Buffered(1)
    kvs = lambda c: pl.BlockSpec((1, S, LANE), lambda b, t: (b, 0, c), pipeline_mode=one)
    cspec = pl.BlockSpec((1, 1, nc + LANE, LANE), lambda b, t: (b, 0, 0, 0), pipeline_mode=one)
    vspec = pl.BlockSpec((1, 1, nc + LANE, LANE), lambda b, t: (b, 1, 0, 0), pipeline_mode=one)
    cst = lambda shape: pl.BlockSpec(shape, lambda b, t: (0,) * len(shape), pipeline_mode=one)
    return pl.pallas_call(
        functools.partial(_nsa_kernel, n_cmp=n_cmp), grid=(B, S // QB),
        in_specs=[pl.BlockSpec((1, QB, 1024), lambda b, t: (b, t, 0)),
                  pl.BlockSpec((1, QB, LANE), lambda b, t: (b, t, 0)),
                  kvs(2), kvs(3), kvs(4), kvs(5), cspec, vspec,
                  cst(tb.shape), cst(lb.shape), cst(pool_far.shape), cst(expand.shape)],
        out_specs=pl.BlockSpec((1, QB, 1024), lambda b, t: (b, t, 0)),
        out_shape=jax.ShapeDtypeStruct((B, S, 1024), F32),
        scratch_shapes=[pltpu.VMEM((rows, 1), F32), pltpu.VMEM((rows, 1), F32), pltpu.VMEM((rows, LANE), F32)],
        name='nsa_prompt',
        compiler_params=pltpu.CompilerParams(dimension_semantics=('parallel', 'arbitrary'),
                                             vmem_limit_bytes=VMEM_LIMIT),
    )(qa, gates, kv16, kv16, kv16, kv16, kc, kc, tb, lb, pool_far, expand)


def _mla_kernel(q_ref, lat_ref, wuv_ref, o_ref, m_scr, l_scr, acc_scr):
    qb = pl.program_id(1)
    nh = N_HEADS_B
    rows = nh * QB
    q = q_ref[0]
    q8 = jnp.concatenate([q[:, h * 256:(h + 1) * 256] for h in range(nh)], axis=0)
    m_scr[...] = jnp.full(m_scr.shape, -jnp.inf, F32)
    l_scr[...] = jnp.zeros(l_scr.shape, F32)
    acc_scr[...] = jnp.zeros(acc_scr.shape, F32)
    nsub = KT // QB

    def tile(kt, masked):
        k0 = pl.multiple_of(kt * KT, KT)
        lat = lat_ref[0, pl.ds(k0, KT), :]
        s = _dot_t(q8, lat)
        if masked:
            col = lax.broadcasted_iota(jnp.int32, (rows, KT), 1)
            row = lax.broadcasted_iota(jnp.int32, (rows, KT), 0) & (QB - 1)
            s = jnp.where(col - row <= qb * QB - kt * KT, s, NEG)
        m_old = m_scr[...]
        m_new = jnp.maximum(m_old, jnp.max(s, axis=1, keepdims=True))
        alpha = jnp.exp(m_old - m_new)
        p = jnp.exp(s - m_new)
        l_scr[...] = alpha * l_scr[...] + jnp.sum(p, axis=1, keepdims=True)
        acc_scr[...] = alpha * acc_scr[...] + _dot(p.astype(BF16), lat[:, 0:KV_LORA])
        m_scr[...] = m_new

    def full_tile(kt, carry):
        tile(kt, False)
        return carry

    lax.fori_loop(0, qb // nsub, full_tile, 0)
    tile(qb // nsub, True)
    o_lat = (acc_scr[...] * (1.0 / l_scr[...])).astype(BF16)
    for h in range(nh):
        o_ref[0, :, h * LANE:(h + 1) * LANE] = _dot(o_lat[h * QB:(h + 1) * QB], wuv_ref[h])


def _mla_prompt(qm, lat16, w_uv):
    B, S, _ = qm.shape
    rows = N_HEADS_B * QB
    one = pl.Buffered(1)
    return pl.pallas_call(
        _mla_kernel, grid=(B, S // QB),
        in_specs=[pl.BlockSpec((1, QB, 2048), lambda b, t: (b, t, 0)),
                  pl.BlockSpec((1, S, 256), lambda b, t: (b, 0, 0), pipeline_mode=one),
                  pl.BlockSpec(w_uv.shape, lambda b, t: (0, 0, 0), pipeline_mode=one)],
        out_specs=pl.BlockSpec((1, QB, 1024), lambda b, t: (b, t, 0)),
        out_shape=jax.ShapeDtypeStruct((B, S, 1024), F32),
        scratch_shapes=[pltpu.VMEM((rows, 1), F32), pltpu.VMEM((rows, 1), F32), pltpu.VMEM((rows, KV_LORA), F32)],
        name='mla_prompt',
        compiler_params=pltpu.CompilerParams(dimension_semantics=('parallel', 'arbitrary'),
                                             vmem_limit_bytes=VMEM_LIMIT),
    )(qm, lat16, w_uv)


def _adaln_kernel(c_ref, w_ref, b_ref, o_ref):
    c = c_ref[...]
    a = (c * (1.0 / (1.0 + jnp.exp(-c)))).astype(BF16)
    o_ref[...] = _dot(a, w_ref[...].astype(BF16)) + b_ref[...]


def _adaln(c, w_ada, b_ada, tn=512):
    R_, D = c.shape
    N = w_ada.shape[1]
    return pl.pallas_call(
        _adaln_kernel, grid=(N // tn,),
        in_specs=[pl.BlockSpec((R_, D), lambda j: (0, 0)), pl.BlockSpec((D, tn), lambda j: (0, j)),
                  pl.BlockSpec((1, tn), lambda j: (0, j))],
        out_specs=pl.BlockSpec((R_, tn), lambda j: (0, j)),
        out_shape=jax.ShapeDtypeStruct((R_, N), F32), name='adaln',
        compiler_params=pltpu.CompilerParams(dimension_semantics=('parallel',), vmem_limit_bytes=VMEM_LIMIT),
    )(c, w_ada, b_ada.reshape(1, N))


def _merge_kernel(x_ref, oa_ref, ob_ref, ga_ref, shf_ref, scf_ref, gf_ref, na_ref, nb_ref, nffn_ref,
                  wa_ref, wb_ref, wr_ref, wgs_ref, wus_ref, wds_ref, xs_ref, f_ref, sc_ref):
    n_real = N_HEADS_A * HEAD_DIM
    oa = oa_ref[0]
    ob = ob_ref[0]
    na = oa * lax.rsqrt(jnp.sum(oa * oa, axis=-1, keepdims=True) * (1.0 / n_real) + EPS) * na_ref[...]
    nb = ob * lax.rsqrt(jnp.sum(ob * ob, axis=-1, keepdims=True) * (1.0 / n_real) + EPS) * nb_ref[...]
    mix = _dot(na.astype(BF16), wa_ref[...]) + _dot(nb.astype(BF16), wb_ref[...])
    x1 = x_ref[0] + ga_ref[0] * mix
    f = x1 * lax.rsqrt(jnp.mean(x1 * x1, axis=-1, keepdims=True) + EPS) * nffn_ref[...]
    f = f * (1.0 + scf_ref[0]) + shf_ref[0]
    f_ref[0] = f
    fb = f.astype(BF16)
    sc_ref[0] = 1.0 / (1.0 + jnp.exp(-_dot(fb, wr_ref[...])))
    g = _dot(fb, wgs_ref[...])
    u = _dot(fb, wus_ref[...])
    hsh = (g * (1.0 / (1.0 + jnp.exp(-g))) * u).astype(BF16)
    xs_ref[0] = x1 + gf_ref[0] * _dot(hsh, wds_ref[...])


def _merge_weights(out_norm_a, out_norm_b, w_out, norm_ffn, w_router, w_gate_s, w_up_s, w_down_s):
    D = w_out.shape[1]
    na = out_norm_a.reshape(N_HEADS_A, 1, HEAD_DIM)
    grp = (jnp.arange(N_HEADS_A) // HPG)[:, None, None]
    half = jnp.arange(2)[None, :, None]
    na_pad = jnp.where(grp == half, na, 0.0).reshape(1, -1)
    nb_pad = jnp.pad(out_norm_b.reshape(N_HEADS_B, V_DIM), ((0, 0), (0, LANE - V_DIM))).reshape(1, -1)
    wa = w_out[:N_HEADS_A * HEAD_DIM].reshape(N_HEADS_A, 1, HEAD_DIM, D)
    wa_pad = jnp.where((grp == half)[..., None], wa, 0.0).reshape(-1, D)
    wb = w_out[N_HEADS_A * HEAD_DIM:].reshape(N_HEADS_B, V_DIM, D)
    wb_pad = jnp.pad(wb, ((0, 0), (0, LANE - V_DIM), (0, 0))).reshape(-1, D)
    bf = lambda a: a.astype(BF16)
    return [na_pad, nb_pad, norm_ffn.reshape(1, D), bf(wa_pad), bf(wb_pad), bf(w_router), bf(w_gate_s),
            bf(w_up_s), bf(w_down_s)]


def _mod_spec(a, tr):
    if a.shape[1] > 1:
        return pl.BlockSpec((1, tr, a.shape[2]), lambda b, t: (b, t, 0))
    return pl.BlockSpec((1, 1, a.shape[2]), lambda b, t: (b, 0, 0))


def _merge(x, oa, ob, gate_a, shift_f, scale_f, gate_f, mw, tr):
    B, T, D = x.shape
    row = lambda n: pl.BlockSpec((1, tr, n), lambda b, t: (b, t, 0))
    out_shape = [jax.ShapeDtypeStruct((B, T, D), F32), jax.ShapeDtypeStruct((B, T, D), F32),
                 jax.ShapeDtypeStruct((B, T, N_EXPERTS), F32)]
    return pl.pallas_call(
        _merge_kernel, grid=(B, T // tr),
        in_specs=[row(D), row(1024), row(1024)] + [_mod_spec(a, tr) for a in (gate_a, shift_f, scale_f, gate_f)]
        + [_const_spec(a.shape) for a in mw],
        out_specs=[row(D), row(D), row(N_EXPERTS)], out_shape=out_shape, name='merge',
        compiler_params=pltpu.CompilerParams(dimension_semantics=('parallel', 'parallel'),
                                             vmem_limit_bytes=VMEM_LIMIT),
    )(x, oa, ob, gate_a, shift_f, scale_f, gate_f, *mw)


def _gather_rows(idx_ref, idx_row, src_hbm, dst, sem):
    def body(r, c):
        pltpu.make_async_copy(src_hbm.at[pl.ds(idx_ref[idx_row, r], 1)], dst.at[pl.ds(r, 1)], sem).start()
        return c
    lax.fori_loop(0, MOE_BLOCK, body, 0, unroll=8)


MOE_CHUNK = 256


def _moe_kernel(nused_ref, bexp_ref, rtok_hbm, f_hbm, wg_ref, wu_ref, wd_ref, y_ref, xbuf, wgb, wub, wdb, rtok, sem,
                isem):
    i = pl.program_id(0)
    nused = nused_ref[0]
    nchunk = rtok_hbm.shape[0] // MOE_CHUNK

    def ids_copy(c):
        return pltpu.make_async_copy(rtok_hbm.at[pl.ds(c * MOE_CHUNK, MOE_CHUNK)], rtok.at[c % 2], isem.at[c % 2])

    def gather(blk):
        c = blk // MOE_CHUNK
        _gather_rows(rtok.at[c % 2], blk % MOE_CHUNK, f_hbm, xbuf.at[blk % 2], sem.at[blk % 2])

    @pl.when(i == 0)
    def _():
        ids_copy(0).start()
        ids_copy(0).wait()
        if nchunk > 1:
            ids_copy(1).start()

        @pl.when(nused > 0)
        def _():
            gather(0)

    @pl.when(((i + 1) % MOE_CHUNK == 0) & (i + 1 < pl.num_programs(0)))
    def _():
        c = (i + 1) // MOE_CHUNK
        ids_copy(c).wait()

        @pl.when(c + 1 < nchunk)
        def _():
            ids_copy(c + 1).start()

    @pl.when((i + 1 < nused) & (i + 1 < pl.num_programs(0)))
    def _():
        gather(i + 1)

    @pl.when(i < nused)
    def _():
        slot = i % 2
        pltpu.make_async_copy(f_hbm.at[pl.ds(0, MOE_BLOCK)], xbuf.at[slot], sem.at[slot]).wait()

        @pl.when((i == 0) | (bexp_ref[i] != bexp_ref[jnp.maximum(i - 1, 0)]))
        def _():
            wgb[...] = wg_ref[0].astype(BF16)
            wub[...] = wu_ref[0].astype(BF16)
            wdb[...] = wd_ref[0].astype(BF16)

        x = xbuf[slot].astype(BF16)
        g = _dot(x, wgb[...])
        u = _dot(x, wub[...])
        h = (g * (1.0 / (1.0 + jnp.exp(-g))) * u).astype(BF16)
        y_ref[...] = _dot(h, wdb[...])

    @pl.when(i >= nused)
    def _():
        y_ref[...] = jnp.zeros(y_ref.shape, F32)


def _moe_experts(nused, blk_exp, row_tok, f, w_gate_e, w_up_e, w_down_e):
    n_blocks = blk_exp.shape[0]
    nb_pad = -(-n_blocks // MOE_CHUNK) * MOE_CHUNK
    row_tok = jnp.pad(row_tok, (0, (nb_pad - n_blocks) * MOE_BLOCK)).reshape(nb_pad, MOE_BLOCK)
    D = f.shape[1]
    de = w_gate_e.shape[2]
    wspec = lambda shp: pl.BlockSpec((1,) + shp, lambda i, nu, be: (be[i], 0, 0))
    return pl.pallas_call(
        _moe_kernel,
        grid_spec=pltpu.PrefetchScalarGridSpec(
            num_scalar_prefetch=2, grid=(n_blocks,),
            in_specs=[pl.BlockSpec(memory_space=pl.ANY), pl.BlockSpec(memory_space=pl.ANY),
                      wspec((D, de)), wspec((D, de)), wspec((de, D))],
            out_specs=pl.BlockSpec((MOE_BLOCK, D), lambda i, nu, be: (i, 0)),
            scratch_shapes=[pltpu.VMEM((2, MOE_BLOCK, D), F32), pltpu.VMEM((D, de), BF16),
                            pltpu.VMEM((D, de), BF16), pltpu.VMEM((de, D), BF16),
                            pltpu.SMEM((2, MOE_CHUNK, MOE_BLOCK), jnp.int32),
                            pltpu.SemaphoreType.DMA((2,)), pltpu.SemaphoreType.DMA((2,))]),
        out_shape=jax.ShapeDtypeStruct((n_blocks * MOE_BLOCK, D), F32), name='moe_experts',
        compiler_params=pltpu.CompilerParams(dimension_semantics=('arbitrary',), vmem_limit_bytes=VMEM_LIMIT),
    )(nused, blk_exp, row_tok, f, w_gate_e, w_up_e, w_down_e)


def _combine_kernel(pos_ref, yb_hbm, xs_ref, gf_ref, w_ref, nf_ref, o_ref, buf, sem):
    tile = pl.program_id(0) * pl.num_programs(1) + pl.program_id(1)
    ntile = pl.num_programs(0) * pl.num_programs(1)

    def start(t, slot):
        for k in range(TOP_K):
            _gather_rows(pos_ref, t * TOP_K + k, yb_hbm, buf.at[slot, k], sem.at[slot])

    @pl.when(tile == 0)
    def _():
        start(0, 0)

    @pl.when(tile + 1 < ntile)
    def _():
        start(tile + 1, (tile + 1) % 2)

    slot = tile % 2
    w = w_ref[0]
    routed = jnp.zeros(xs_ref.shape[1:], F32)
    for k in range(TOP_K):
        pltpu.make_async_copy(yb_hbm.at[pl.ds(0, MOE_BLOCK)], buf.at[slot, k], sem.at[slot]).wait()
    for k in range(TOP_K):
        routed = routed + w[:, k:k + 1] * buf[slot, k]
    x2 = xs_ref[0] + gf_ref[0] * routed
    o_ref[0] = x2 * lax.rsqrt(jnp.mean(x2 * x2, axis=-1, keepdims=True) + EPS) * nf_ref[...]


def _combine(pos, yb, xs, gate_f, wts, norm_final):
    B, T, D = xs.shape
    tr = MOE_BLOCK
    gspec = (pl.BlockSpec((1, tr, D), lambda b, t, p: (b, t, 0)) if gate_f.shape[1] > 1
             else pl.BlockSpec((1, 1, D), lambda b, t, p: (b, 0, 0)))
    return pl.pallas_call(
        _combine_kernel,
        grid_spec=pltpu.PrefetchScalarGridSpec(
            num_scalar_prefetch=1, grid=(B, T // tr),
            in_specs=[pl.BlockSpec(memory_space=pl.ANY), pl.BlockSpec((1, tr, D), lambda b, t, p: (b, t, 0)), gspec,
                      pl.BlockSpec((1, tr, TOP_K), lambda b, t, p: (b, t, 0)),
                      pl.BlockSpec((1, D), lambda b, t, p: (0, 0))],
            out_specs=pl.BlockSpec((1, tr, D), lambda b, t, p: (b, t, 0)),
            scratch_shapes=[pltpu.VMEM((2, TOP_K, tr, D), F32), pltpu.SemaphoreType.DMA((2,))]),
        out_shape=jax.ShapeDtypeStruct((B, T, D), F32), name='combine',
        compiler_params=pltpu.CompilerParams(dimension_semantics=('arbitrary', 'arbitrary'),
                                             vmem_limit_bytes=VMEM_LIMIT),
    )(pos, yb, xs, gate_f, wts, norm_final.reshape(1, D))


def _route(scores, router_bias):
    n_tok = scores.shape[0]
    biased = scores + router_bias.astype(F32)
    grp = biased.reshape(n_tok, N_GROUPS, N_EXPERTS // N_GROUPS)
    grp_score = lax.top_k(grp, 2)[0].sum(-1)
    _, top_g = lax.top_k(grp_score, TOP_GROUPS)
    gmask = (top_g[:, :, None] == jnp.arange(N_GROUPS)[None, None, :]).any(axis=1)
    biased = jnp.where(jnp.repeat(gmask, N_EXPERTS // N_GROUPS, axis=1), biased, NEG)
    _, idx = lax.top_k(biased, TOP_K)
    wts = jnp.take_along_axis(scores, idx, axis=1)
    wts = wts / wts.sum(-1, keepdims=True) * ROUTED_SCALE
    n_asg = n_tok * TOP_K
    idx = idx.astype(jnp.int32)
    rank, counts = _expert_rank(idx)
    padded = (counts + MOE_BLOCK - 1) // MOE_BLOCK * MOE_BLOCK
    pad_end = jnp.cumsum(padded)
    pos = (pad_end - padded)[idx] + rank
    n_blocks = -(-(n_asg + N_EXPERTS * (MOE_BLOCK - 1)) // MOE_BLOCK)
    n_rows = n_blocks * MOE_BLOCK
    tok = jnp.broadcast_to(jnp.arange(n_tok, dtype=jnp.int32)[:, None], pos.shape)
    row_tok = jnp.zeros((n_rows,), jnp.int32).at[pos.reshape(-1)].set(tok.reshape(-1))
    blk_exp = jnp.minimum(jnp.searchsorted(pad_end, jnp.arange(n_blocks) * MOE_BLOCK, side='right'),
                          N_EXPERTS - 1).astype(jnp.int32)
    nused = (pad_end[-1] // MOE_BLOCK).astype(jnp.int32).reshape(1)
    return wts, pos, row_tok, blk_exp, nused


def _rank_kernel(idx_ref, rank_ref, cnt_ref, carry):
    i = pl.program_id(0)

    @pl.when(i == 0)
    def _():
        carry[...] = jnp.zeros(carry.shape, F32)

    idx = idx_ref[...]
    lane = lax.broadcasted_iota(jnp.int32, (MOE_BLOCK, N_EXPERTS), 1)
    hot = [lane == idx[:, k:k + 1] for k in range(TOP_K)]
    onehot = jnp.zeros((MOE_BLOCK, N_EXPERTS), F32)
    for k in range(TOP_K):
        onehot = jnp.where(hot[k], 1.0, onehot)
    r_i = lax.broadcasted_iota(jnp.int32, (MOE_BLOCK, MOE_BLOCK), 0)
    c_i = lax.broadcasted_iota(jnp.int32, (MOE_BLOCK, MOE_BLOCK), 1)
    lower = jnp.where(c_i < r_i, 1.0, 0.0).astype(BF16)
    before = _dot(lower, onehot.astype(BF16)) + carry[0:1, :]
    kcol = lax.broadcasted_iota(jnp.int32, (MOE_BLOCK, TOP_K), 1)
    rank = jnp.zeros((MOE_BLOCK, TOP_K), F32)
    for k in range(TOP_K):
        rk = jnp.sum(jnp.where(hot[k], before, 0.0), axis=1, keepdims=True)
        rank = jnp.where(kcol == k, rk, rank)
    rank_ref[...] = rank.astype(jnp.int32)
    total = carry[0:1, :] + jnp.sum(onehot, axis=0, keepdims=True)
    carry[...] = jnp.broadcast_to(total, carry.shape)
    cnt_ref[...] = jnp.broadcast_to(total, cnt_ref.shape).astype(jnp.int32)


def _expert_rank(idx):
    n_tok = idx.shape[0]
    rank, cnt = pl.pallas_call(
        _rank_kernel, grid=(n_tok // MOE_BLOCK,),
        in_specs=[pl.BlockSpec((MOE_BLOCK, TOP_K), lambda i: (i, 0))],
        out_specs=[pl.BlockSpec((MOE_BLOCK, TOP_K), lambda i: (i, 0)), pl.BlockSpec((8, N_EXPERTS), lambda i: (0, 0))],
        out_shape=[jax.ShapeDtypeStruct((n_tok, TOP_K), jnp.int32), jax.ShapeDtypeStruct((8, N_EXPERTS), jnp.int32)],
        scratch_shapes=[pltpu.VMEM((8, N_EXPERTS), F32)], name='expert_rank',
        compiler_params=pltpu.CompilerParams(dimension_semantics=('arbitrary',), vmem_limit_bytes=VMEM_LIMIT),
    )(idx)
    return rank, cnt[0]


PAGE = 128


def _softmax_with_new(s, s_new):
    m = jnp.maximum(jnp.max(s, axis=1, keepdims=True), s_new)
    p = jnp.exp(s - m)
    pn = jnp.exp(s_new - m)
    return p, pn, 1.0 / (jnp.sum(p, axis=1, keepdims=True) + pn)


def _samp_cmp_kernel(pt_ref, pool_hbm, q_ref, wbd_ref, pe_ref, w1c_ref, w2_ref, bc_ref, pf_ref,
                     oc_ref, idx_ref, buf_t, buf, peh_scr, sem, *, n_pages, n_cmp):
    s = pl.program_id(0)
    ns = pl.num_programs(0)
    nc = n_pages * PAGE // CMP_STRIDE

    def page_copy(page, slot, p):
        return pltpu.make_async_copy(pool_hbm.at[page], buf_t.at[slot, :, :, pl.ds(p * PAGE, PAGE)], sem.at[slot])

    def start(smp, slot):
        def body(p, c):
            page_copy(pt_ref[smp, p], slot, p).start()
            return c
        lax.fori_loop(0, n_pages, body, 0)

    @pl.when(s == 0)
    def _():
        start(0, 0)
        for j in range(2):
            pp = _dot(pe_ref[j], w1c_ref[j])
            peh_scr[j] = jnp.broadcast_to(pp[0:1, 0:CMP_HIDDEN] + pp[1:2, CMP_HIDDEN:], (8, CMP_HIDDEN))

    @pl.when(s + 1 < ns)
    def _():
        start(s + 1, (s + 1) % 2)

    slot = s % 2

    def wait_page(p, c):
        page_copy(0, slot, p).wait()
        return c
    lax.fori_loop(0, n_pages, wait_page, 0)

    tw = 4 * PAGE

    def to_rows(c, carry):
        l0 = pl.multiple_of(c * tw, tw)
        for j in range(2):
            xt = buf_t[slot, pl.ds(2 * j, 2), :, pl.ds(l0, tw)].reshape(2 * HEAD_DIM, tw)
            buf[j, pl.ds(l0, tw), :] = xt.T
        return carry
    lax.fori_loop(0, n_pages * PAGE // tw, to_rows, 0)

    rows = lax.broadcasted_iota(jnp.int32, (nc, LANE), 0)
    kvc = []
    for j in range(2):
        acc = jnp.zeros((nc, 4 * CMP_HIDDEN), F32)
        for r in range(CMP_STRIDE):
            x = buf[j, pl.ds(r, nc, stride=CMP_STRIDE), :].astype(BF16)
            acc = acc + _dot(x, wbd_ref[j, r])
        out = jnp.zeros((nc, LANE), F32)
        for g in range(N_KV_A):
            a = acc[:, g * 2 * CMP_HIDDEN:g * 2 * CMP_HIDDEN + CMP_HIDDEN]
            bm = acc[:, g * 2 * CMP_HIDDEN + CMP_HIDDEN:(g + 1) * 2 * CMP_HIDDEN]
            hid = a + pltpu.roll(bm, nc - 1, 0) + peh_scr[j, 0:1]
            act = hid * (1.0 / (1.0 + jnp.exp(-hid)))
            out = out + _dot(act.astype(BF16), w2_ref[j, g])
        kvc.append(jnp.where(rows < n_cmp, out, 0.0).astype(BF16))
    kc, vc = kvc

    q8 = q_ref[0]
    sc = _dot_t(q8, kc) + bc_ref[...]
    p = jnp.exp(sc - jnp.max(sc, axis=1, keepdims=True))
    p = p * (1.0 / jnp.sum(p, axis=1, keepdims=True))
    oc_ref[0] = _dot(p.astype(BF16), vc)

    pg = jnp.concatenate([jnp.sum(p[0:HPG], axis=0, keepdims=True), jnp.sum(p[HPG:2 * HPG], axis=0, keepdims=True),
                          jnp.zeros((8 - N_KV_A, nc), F32)], axis=0)
    imp = _hilo_dot(pg, pf_ref[...])
    lane = lax.broadcasted_iota(jnp.int32, (8, LANE), 1).astype(F32)
    score = jnp.where(lane == 0.0, FORCED, jnp.where(lane >= float(LANE - N_LOCAL_SLC + 1), FORCED, imp))
    picks = jnp.full((8, LANE), float(LANE), F32)
    for k in range(N_SLC - 1):
        mx = jnp.max(score, axis=1, keepdims=True)
        first = jnp.min(jnp.where(score == mx, lane, float(LANE)), axis=1, keepdims=True)
        picks = jnp.where(lane == float(k), first, picks)
        score = jnp.where(lane == first, -jnp.inf, score)
    idx_ref[0] = picks.astype(jnp.int32)


def _samp_cmp(page_table, pool2d, q3, scw, bc, n_cmp):
    Bd, n_pages = page_table.shape
    nc = n_pages * PAGE // CMP_STRIDE
    wbd, pe, w1c, w2g = scw
    m = jnp.arange(nc)
    pool_m = ((m[:, None] // SLC_PER_CMP == jnp.arange(LANE)[None, :]) & (m[:, None] < n_cmp)).astype(BF16)
    one = pl.Buffered(1)
    cst = lambda a: pl.BlockSpec(a.shape, lambda s, pt: (0,) * a.ndim, pipeline_mode=one)
    return pl.pallas_call(
        functools.partial(_samp_cmp_kernel, n_pages=n_pages, n_cmp=n_cmp),
        grid_spec=pltpu.PrefetchScalarGridSpec(
            num_scalar_prefetch=1, grid=(Bd,),
            in_specs=[pl.BlockSpec(memory_space=pl.ANY), pl.BlockSpec((1, 8, LANE), lambda s, pt: (s, 0, 0)),
                      cst(wbd), cst(pe), cst(w1c), cst(w2g), cst(bc), cst(pool_m)],
            out_specs=[pl.BlockSpec((1, 8, LANE), lambda s, pt: (s, 0, 0)),
                       pl.BlockSpec((1, 8, LANE), lambda s, pt: (s, 0, 0))],
            scratch_shapes=[pltpu.VMEM((2, 4, HEAD_DIM, n_pages * PAGE), F32),
                            pltpu.VMEM((2, n_pages * PAGE, LANE), F32), pltpu.VMEM((2, 8, CMP_HIDDEN), F32),
                            pltpu.SemaphoreType.DMA((2,))]),
        out_shape=[jax.ShapeDtypeStruct((Bd, 8, LANE), F32), jax.ShapeDtypeStruct((Bd, 8, LANE), jnp.int32)],
        name='sample_cmp',
        compiler_params=pltpu.CompilerParams(dimension_semantics=('arbitrary',), vmem_limit_bytes=VMEM_LIMIT),
    )(page_table, pool2d, q3, wbd, pe, w1c, w2g, bc, pool_m)


def _samp_cmp_weights(cw):
    w1cat, pe, w2a, w2b = cw
    w = w1cat.reshape(2, CMP_STRIDE, HEAD_DIM, 2 * CMP_HIDDEN)
    z = jnp.zeros_like(w)
    wbd = jnp.concatenate([jnp.concatenate([w, z], axis=-1), jnp.concatenate([z, w], axis=-1)], axis=2)
    return wbd, pe, w1cat, jnp.stack([w2a, w2b], axis=1)


def _t5_bias_rows(rel_bias, rel, valid):
    b = rel_bias.astype(F32)[_t5_bucket(rel)]
    return jnp.where(valid[None, :], b.T, NEG)


def _bucket_bias(rel, tbl_t):
    max_exact = N_BUCKETS // 2
    nf = jnp.maximum(rel, 1).astype(F32)
    large = max_exact + (jnp.log(nf / max_exact) / math.log(MAX_DISTANCE / max_exact)
                         * (N_BUCKETS - max_exact)).astype(jnp.int32)
    bucket = jnp.where(rel < max_exact, rel, jnp.minimum(large, N_BUCKETS - 1))
    bias = jnp.zeros(rel.shape, F32)
    for b in range(N_BUCKETS):
        bias = jnp.where(bucket == b, tbl_t[:, b:b + 1], bias)
    return bias


def _samp_sw_kernel(pt_ref, idx_ref, pool_hbm, q_ref, knew_ref, win_ref, wnew_ref, wcol_ref, oc_ref, gcol_ref,
                    tblt_ref, bw_ref, oa_ref, nwin_ref, kvbuf, sem, *, past_len):
    s = pl.program_id(0)
    ns = pl.num_programs(0)
    npb = past_len // SLC_BLOCK
    bpp = PAGE // SLC_BLOCK
    nk = N_SLC * PAGE

    def block_copy(page, slot, g, k, kv):
        return pltpu.make_async_copy(pool_hbm.at[page, kv * N_KV_A + g],
                                     kvbuf.at[slot, kv * N_KV_A + g, :, pl.ds(k * PAGE, PAGE)], sem.at[slot])

    def start(smp, slot):
        for g in range(N_KV_A):
            for k in range(N_SLC):
                j = jnp.minimum(idx_ref[(smp * N_KV_A + g) * N_SLC + k], npb - 1)
                page = pt_ref[smp, j // bpp]
                for kv in range(2):
                    block_copy(page, slot, g, k, kv).start()

    @pl.when(s == 0)
    def _():
        start(0, 0)

    @pl.when(s + 1 < ns)
    def _():
        start(s + 1, (s + 1) % 2)

    slot = s % 2
    q8 = q_ref[0]
    q32 = q8.astype(F32)
    qg = [q8[:, g * HEAD_DIM:(g + 1) * HEAD_DIM] for g in range(N_KV_A)]
    tbl_t = tblt_ref[...]
    row = lax.broadcasted_iota(jnp.int32, (8, LANE), 0)
    lane = lax.broadcasted_iota(jnp.int32, (8, LANE), 1)
    grp0 = row[:, 0:1] < HPG

    def new_token(kv_row):
        kn = kv_row[:, 0:LANE].astype(BF16).astype(F32)
        vn = kv_row[:, LANE:].astype(BF16).astype(F32)
        return jnp.sum(q32 * kn, axis=1, keepdims=True) + tbl_t[:, 0:1], vn

    def by_group(a0, a1):
        return jnp.concatenate([jnp.where(grp0, a0, 0.0), jnp.where(grp0, 0.0, a1)], axis=1)

    w = win_ref[0]
    wl = w.shape[-1]
    s_new, v_new = new_token(wnew_ref[0])
    sw = jnp.where(grp0, _dot(qg[0], w[0].astype(BF16)), _dot(qg[1], w[1].astype(BF16))) + bw_ref[...]
    p, pn, inv = _softmax_with_new(sw, s_new)
    pb = p.astype(BF16)
    o_w = (by_group(_dot_t(pb, w[2].astype(BF16)), _dot_t(pb, w[3].astype(BF16))) + pn * v_new) * inv
    wcol = lax.broadcasted_iota(jnp.int32, (HEAD_DIM, wl), 1)
    for c in range(2 * N_KV_A):
        nwin_ref[0, c] = jnp.where(wcol == wl - 1, wcol_ref[0, c], pltpu.roll(w[c], wl - 1, 1))

    for g in range(N_KV_A):
        for k in range(N_SLC):
            for kv in range(2):
                block_copy(0, slot, g, k, kv).wait()
    kl = lax.broadcasted_iota(jnp.int32, (8, nk), 1)
    kslot = kl >> 7
    kin = kl & (PAGE - 1)
    rowk = lax.broadcasted_iota(jnp.int32, (8, nk), 0) < HPG
    blk = jnp.zeros((8, nk), jnp.int32)
    for k in range(N_SLC):
        j0 = idx_ref[(s * N_KV_A) * N_SLC + k]
        j1 = idx_ref[(s * N_KV_A + 1) * N_SLC + k]
        blk = jnp.where(kslot == k, jnp.where(rowk, j0, j1), blk)
    rel = past_len - ((blk // bpp) * PAGE + kin)
    ok = (kin // SLC_BLOCK) == jnp.where(blk < npb, blk % bpp, -1)
    ss = jnp.where(rowk, _dot(qg[0], kvbuf[slot, 0].astype(BF16)), _dot(qg[1], kvbuf[slot, 1].astype(BF16)))
    ss = jnp.where(ok, ss + _bucket_bias(jnp.maximum(rel, 0), tbl_t), NEG)
    s_new, v_new = new_token(knew_ref[0])
    p, pn, inv = _softmax_with_new(ss, s_new)
    pb = p.astype(BF16)
    o_s = (by_group(_dot_t(pb, kvbuf[slot, 2].astype(BF16)), _dot_t(pb, kvbuf[slot, 3].astype(BF16)))
           + pn * v_new) * inv

    gc = gcol_ref[0]
    o = gc[:, 0:1] * oc_ref[0] + gc[:, 1:2] * o_s + gc[:, 2:3] * o_w
    oa_ref[0] = jnp.where((lane >= HEAD_DIM) == (row >= HPG), o, 0.0)


def _samp_sw(page_table, idx_flat, pool_t, q3, knew, win_t, wnew, wcol, o_c, gcol, tbl_t, bw, past_len):
    Bd = page_table.shape[0]
    wl = win_t.shape[-1]
    per = lambda shp: pl.BlockSpec((1,) + shp, lambda s, pt, ix: (s,) + (0,) * len(shp))
    cst = lambda a: pl.BlockSpec(a.shape, lambda s, pt, ix: (0,) * a.ndim)
    return pl.pallas_call(
        functools.partial(_samp_sw_kernel, past_len=past_len),
        grid_spec=pltpu.PrefetchScalarGridSpec(
            num_scalar_prefetch=2, grid=(Bd,),
            in_specs=[pl.BlockSpec(memory_space=pl.ANY), per((8, LANE)), per((1, 2 * LANE)),
                      per((2 * N_KV_A, HEAD_DIM, wl)), per((1, 2 * LANE)), per((2 * N_KV_A, HEAD_DIM, 1)),
                      per((8, LANE)), per((8, LANE)), cst(tbl_t), cst(bw)],
            out_specs=[per((8, LANE)), per((2 * N_KV_A, HEAD_DIM, wl))],
            scratch_shapes=[pltpu.VMEM((2, 2 * N_KV_A, HEAD_DIM, N_SLC * PAGE), F32),
                            pltpu.SemaphoreType.DMA((2,))]),
        out_shape=[jax.ShapeDtypeStruct((Bd, 8, LANE), F32), jax.ShapeDtypeStruct(win_t.shape, F32)],
        name='sample_slc_win',
        compiler_params=pltpu.CompilerParams(dimension_semantics=('arbitrary',), vmem_limit_bytes=VMEM_LIMIT),
    )(page_table, idx_flat, pool_t, q3, knew, win_t, wnew, wcol, o_c, gcol, tbl_t, bw)


def _samp_mla_kernel(pt_ref, pool_hbm, q_ref, lnew_ref, o_ref, buf, sem, *, n_pages):
    s = pl.program_id(0)
    ns = pl.num_programs(0)

    def page_copy(page, slot, p):
        return pltpu.make_async_copy(pool_hbm.at[page], buf.at[slot, :, pl.ds(p * PAGE, PAGE)], sem.at[slot])

    def start(smp, slot):
        def body(p, c):
            page_copy(pt_ref[smp, p], slot, p).start()
            return c
        lax.fori_loop(0, n_pages, body, 0)

    @pl.when(s == 0)
    def _():
        start(0, 0)

    @pl.when(s + 1 < ns)
    def _():
        start(s + 1, (s + 1) % 2)

    slot = s % 2

    def wait_page(p, c):
        page_copy(0, slot, p).wait()
        return c
    lax.fori_loop(0, n_pages, wait_page, 0)

    q8 = q_ref[0]
    ckv_t = buf[slot, 0:KV_LORA, :].astype(BF16)
    kr_t = buf[slot, KV_LORA:LATENT_DIM, :].astype(BF16)
    ln = lnew_ref[0].astype(F32)
    sc = _dot(q8[:, 0:KV_LORA], ckv_t) + _dot(q8[:, KV_LORA:LATENT_DIM], kr_t)
    s_new = jnp.sum(q8.astype(F32) * ln, axis=1, keepdims=True)
    p, pn, inv = _softmax_with_new(sc, s_new)
    o_ref[0] = (_dot_t(p.astype(BF16), ckv_t) + pn * ln[:, 0:KV_LORA]) * inv


def _samp_mla(page_table, pool2d, qm3, lnew):
    Bd, n_pages = page_table.shape
    per = lambda shp: pl.BlockSpec((1,) + shp, lambda s, pt: (s, 0, 0))
    return pl.pallas_call(
        functools.partial(_samp_mla_kernel, n_pages=n_pages),
        grid_spec=pltpu.PrefetchScalarGridSpec(
            num_scalar_prefetch=1, grid=(Bd,),
            in_specs=[pl.BlockSpec(memory_space=pl.ANY), per((8, 256)), per((1, 256))],
            out_specs=per((8, KV_LORA)),
            scratch_shapes=[pltpu.VMEM((2, LATENT_DIM, n_pages * PAGE), F32), pltpu.SemaphoreType.DMA((2,))]),
        out_shape=jax.ShapeDtypeStruct((Bd, 8, KV_LORA), F32), name='sample_mla',
        compiler_params=pltpu.CompilerParams(dimension_semantics=('arbitrary',), vmem_limit_bytes=VMEM_LIMIT),
    )(page_table, pool2d, qm3, lnew)


def _uv_kernel(o_ref, w_ref, y_ref):
    y_ref[...] = _dot(o_ref[...].astype(BF16), w_ref[0])


def _samp_uv(o_lat2d, w_uv):
    Bd = o_lat2d.shape[0]
    return pl.pallas_call(
        _uv_kernel, grid=(N_HEADS_B,),
        in_specs=[pl.BlockSpec((Bd, LANE), lambda h: (0, h)), pl.BlockSpec((1, KV_LORA, LANE), lambda h: (h, 0, 0))],
        out_specs=pl.BlockSpec((Bd, LANE), lambda h: (0, h)),
        out_shape=jax.ShapeDtypeStruct((Bd, N_HEADS_B * LANE), F32), name='sample_uv',
        compiler_params=pltpu.CompilerParams(dimension_semantics=('parallel',), vmem_limit_bytes=VMEM_LIMIT),
    )(o_lat2d, w_uv)


def _sample_mix(xs3, msm, pool_cmp, pool_slc, win_buf, pool_mla, page_table, gain, rel_bias, w, cw):
    Bd = xs3.shape[1]
    past_len = page_table.shape[1] * PAGE
    cos_s, sin_s = _rope_tables(jnp.full((Bd,), past_len, jnp.int32))
    qa_s, cmp_s, slc_s, win_s, _, gt_s, qm_s, lat_s, lat16_s = _inproj(xs3, msm[0], msm[1], gain, cos_s, sin_s, w, Bd)
    q3 = qa_s.reshape(Bd, N_HEADS_A, LANE)
    nc = past_len // CMP_STRIDE
    n_cmp = (past_len + 1 - CMP_BLOCK) // CMP_STRIDE + 1
    m = jnp.arange(nc)
    bc = _t5_bias_rows(rel_bias, past_len - (m * CMP_STRIDE + CMP_BLOCK - 1), m < n_cmp)
    fm = lambda a: a.transpose(0, 2, 3, 4, 1).reshape(a.shape[0], 2 * N_KV_A, HEAD_DIM, a.shape[1])
    o_c, idx = _samp_cmp(page_table, fm(pool_cmp), q3, _samp_cmp_weights(cw), bc, n_cmp)
    idx_flat = idx[:, :N_KV_A, :N_SLC].reshape(-1)
    wl = win_buf.shape[1]
    wi = jnp.arange(wl)
    bw = _t5_bias_rows(rel_bias, wl - wi, (wl - wi < WINDOW) & (past_len - wl + wi >= 0))
    tbl_t = jnp.pad(rel_bias.astype(F32).T, ((0, 0), (0, LANE - N_BUCKETS)))
    gcol = gt_s[0, :, :3 * N_HEADS_A].reshape(Bd, 3, N_HEADS_A).transpose(0, 2, 1)
    gcol = jnp.pad(gcol, ((0, 0), (0, 0), (0, LANE - 3)))
    oa_s, new_win = _samp_sw(page_table, idx_flat, fm(pool_slc), q3, slc_s.reshape(Bd, 1, 2 * LANE), fm(win_buf),
                             win_s.reshape(Bd, 1, 2 * LANE), win_s.reshape(Bd, 2 * N_KV_A, HEAD_DIM, 1), o_c, gcol,
                             tbl_t, bw, past_len)
    o_lat = _samp_mla(page_table, pool_mla.transpose(0, 2, 1), qm_s.reshape(Bd, N_HEADS_B, 256),
                      lat16_s.reshape(Bd, 1, 256))
    ob_s = _samp_uv(o_lat.reshape(Bd, N_HEADS_B * KV_LORA), w['w_uv'])
    new_win = new_win.reshape(Bd, 2, N_KV_A, HEAD_DIM, wl).transpose(0, 4, 1, 2, 3)
    return oa_s.reshape(1, Bd, -1), ob_s.reshape(1, Bd, -1), cmp_s, slc_s, new_win, lat_s


def kernel(x_prompt, x_sample, cache_nsa_cmp, cache_nsa_slc, cache_nsa_win, cache_mla, page_table, c_prompt, c_sample, rel_bias, w_ada, b_ada, norm_attn, norm_ffn, w_in, cmp_pe, cmp_w1, cmp_w2, q_norm, w_q_up, kv_norm, w_kv_up, out_norm_a, out_norm_b, w_out, w_router, router_bias, w_gate_e, w_up_e, w_down_e, w_gate_s, w_up_s, w_down_s, norm_final):
    B, S, D = x_prompt.shape
    Bd = x_sample.shape[0]
    l = 0
    n_mod = B + Bd
    c_all = jnp.pad(jnp.concatenate([c_prompt, c_sample], axis=0), ((0, -n_mod % 8), (0, 0)))
    mod = _adaln(c_all, w_ada[l], b_ada[l]).reshape(-1, 6, D)
    mp = [mod[:B, i][:, None, :] for i in range(6)]
    msm = [mod[B:n_mod, i][None] for i in range(6)]

    w = _inproj_weights(w_in[l], q_norm[l], w_q_up[l], kv_norm[l], w_kv_up[l])
    cw = _compress_weights(cmp_pe[l], cmp_w1[l], cmp_w2[l])
    mw = _merge_weights(out_norm_a[l], out_norm_b[l], w_out[l], norm_ffn[l], w_router[l], w_gate_s[l], w_up_s[l],
                        w_down_s[l])
    tb, lb = _bias_tables(rel_bias)

    cos, sin = _rope_tables(jnp.arange(S))
    qa, cmp32, slc32, win32, kv16, gt, qm, lat32, lat16 = _inproj(
        x_prompt, mp[0], mp[1], norm_attn[l], cos, sin, w, 256)
    nc = S // CMP_STRIDE
    n_cmp = (S - CMP_BLOCK) // CMP_STRIDE + 1
    xc = cmp32.reshape(B, nc, CMP_STRIDE, 4, HEAD_DIM).transpose(0, 3, 1, 2, 4).reshape(B, 4, nc, -1).astype(BF16)
    kcv = _compress(xc, cw, n_cmp)
    oa_p = _nsa_prompt(qa, gt, kv16, kcv, tb, lb)
    ob_p = _mla_prompt(qm, lat16, w['w_uv'])
    xs_p, f_p, sc_p = _merge(x_prompt, oa_p, ob_p, mp[2], mp[3], mp[4], mp[5], mw, 256)

    xs3 = x_sample.reshape(1, Bd, D)
    oa_s, ob_s, cmp_s, slc_s, new_win, lat_s = _sample_mix(
        xs3, msm, cache_nsa_cmp[l], cache_nsa_slc[l], cache_nsa_win[l], cache_mla[l], page_table, norm_attn[l],
        rel_bias, w, cw)
    xs_s, f_s, sc_s = _merge(xs3, oa_s, ob_s, msm[2], msm[3], msm[4], msm[5], mw, Bd)

    n_p = B * S
    f_all = jnp.concatenate([f_p.reshape(n_p, D), f_s.reshape(Bd, D)], axis=0)
    sc_all = jnp.concatenate([sc_p.reshape(n_p, N_EXPERTS), sc_s.reshape(Bd, N_EXPERTS)], axis=0)
    wts, pos, row_tok, blk_exp, nused = _route(sc_all, router_bias[l])
    yb = _moe_experts(nused, blk_exp, row_tok, f_all, w_gate_e[l], w_up_e[l], w_down_e[l])
    tile_pos = lambda p: p.reshape(-1, MOE_BLOCK, TOP_K).transpose(0, 2, 1).reshape(-1, MOE_BLOCK)
    y_p = _combine(tile_pos(pos[:n_p]), yb, xs_p, mp[5], wts[:n_p].reshape(B, S, TOP_K), norm_final)
    y_s = _combine(tile_pos(pos[n_p:]), yb, xs_s, msm[5], wts[n_p:].reshape(1, Bd, TOP_K), norm_final)

    sh6 = lambda a, b, t: a.reshape(1, b, t, 2, N_KV_A, HEAD_DIM)
    return (y_p, y_s.reshape(Bd, 1, D), sh6(cmp32, B, S), sh6(cmp_s, Bd, 1), sh6(slc32, B, S), sh6(slc_s, Bd, 1),
            sh6(win32[:, S - WINDOW:], B, WINDOW), new_win[None], lat32[None], lat_s.reshape(1, Bd, 1, LATENT_DIM))
```

```python
import functools
import math

import jax
import jax.numpy as jnp
from jax import lax
from jax.experimental import pallas as pl
from jax.experimental.pallas import tpu as pltpu

F32 = jnp.float32
BF16 = jnp.bfloat16

LANE = 128
VMEM_LIMIT = 56 * 1024 * 1024

HEAD_DIM = 64
N_HEADS_A = 8
N_KV_A = 2
HPG = N_HEADS_A // N_KV_A
CMP_BLOCK = 32
CMP_STRIDE = 16
CMP_HIDDEN = 128
SLC_BLOCK = 64
SLC_PER_CMP = SLC_BLOCK // CMP_STRIDE
N_SLC = 16
N_LOCAL_SLC = 2
WINDOW = 512
N_HEADS_B = 8
Q_LORA = 192
KV_LORA = 128
QK_NOPE = 64
QK_ROPE = 32
V_DIM = 64
LATENT_DIM = KV_LORA + QK_ROPE
ROPE_THETA = 10000.0
MLA_SCALE = (QK_NOPE + QK_ROPE) ** -0.5
N_BUCKETS = 32
MAX_DISTANCE = 128
N_EXPERTS = 256
TOP_K = 8
N_GROUPS = 8
TOP_GROUPS = 4
ROUTED_SCALE = 2.5
MOE_BLOCK = 128
EPS = 1e-6
NEG = -1e30
FORCED = 1e30

QB = 128
KT = 512
CMP_PAD = 16
LOC_W = 24


def _dot(a, b):
    return jnp.dot(a, b, preferred_element_type=F32)


def _dot_t(a, b):
    return lax.dot_general(a, b, (((1,), (1,)), ((), ())), preferred_element_type=F32)


def _lane_tiles(x):
    return [x[:, c * LANE:(c + 1) * LANE] for c in range(x.shape[1] // LANE)]


def _row_max(x):
    return jnp.max(functools.reduce(jnp.maximum, _lane_tiles(x)), axis=1, keepdims=True)


def _row_sum(x):
    return jnp.sum(functools.reduce(jnp.add, _lane_tiles(x)), axis=1, keepdims=True)


def _const_spec(shape):
    nd = len(shape)
    return pl.BlockSpec(shape, lambda *_: (0,) * nd)


def _inproj_kernel(x_ref, sh_ref, sc_ref, g_ref, cs_ref, sn_ref, wq_ref, wkv_ref, wg_ref, wqd_ref,
                   wkvd_ref, qn_ref, wqup_ref, bd_ref, plc_ref, kvn_ref,
                   qa_ref, cmp_ref, slc_ref, win_ref, kv16_ref, gt_ref, qm_ref, lat_ref, lat16_ref):
    x = x_ref[0]
    ms = jnp.mean(x * x, axis=-1, keepdims=True)
    xn = x * lax.rsqrt(ms + EPS) * g_ref[...]
    h = xn * (1.0 + sc_ref[0]) + sh_ref[0]
    hb = h.astype(BF16)
    qa_ref[0] = _dot(hb, wq_ref[...]).astype(BF16)
    kv = _dot(hb, wkv_ref[...])
    cmp_ref[0] = kv[:, 0:256]
    slc_ref[0] = kv[:, 256:512]
    win_ref[0] = kv[:, 512:768]
    kv16_ref[0] = kv.astype(BF16)
    gl = _dot(hb, wg_ref[...])
    gt_ref[0] = 1.0 / (1.0 + jnp.exp(-gl))
    qd = _dot(hb, wqd_ref[...])
    qn = qd * lax.rsqrt(jnp.sum(qd * qd, axis=-1, keepdims=True) * (1.0 / Q_LORA) + EPS) * qn_ref[...]
    qu = _dot(qn.astype(BF16), wqup_ref[...])
    cs = cs_ref[...]
    sn = sn_ref[...]
    qr = qu[:, 512:768] * cs + qu[:, 768:1024] * sn
    qm = _dot(qu[:, 0:512].astype(BF16), bd_ref[...]) + _dot(qr.astype(BF16), plc_ref[...])
    qm_ref[0] = (qm * MLA_SCALE).astype(BF16)
    kvd = _dot(hb, wkvd_ref[...])
    c = kvd[:, 0:128]
    ckv = c * lax.rsqrt(jnp.mean(c * c, axis=-1, keepdims=True) + EPS) * kvn_ref[...]
    kr = kvd[:, 128:256] * cs[:, 0:128] + kvd[:, 256:384] * sn[:, 0:128]
    lat_ref[0, :, 0:128] = ckv
    lat_ref[0, :, 128:160] = kr[:, 0:32]
    lat16_ref[0, :, 0:128] = ckv.astype(BF16)
    lat16_ref[0, :, 128:256] = kr.astype(BF16)


def _inproj_weights(w_in, q_norm, w_q_up, kv_norm, w_kv_up):
    D = w_in.shape[0]
    o1 = N_HEADS_A * HEAD_DIM
    o2 = o1 + 6 * N_KV_A * HEAD_DIM
    o3 = o2 + 3 * N_HEADS_A
    o4 = o3 + Q_LORA
    wq = w_in[:, :o1].reshape(D, N_HEADS_A, HEAD_DIM) * (HEAD_DIM ** -0.5)
    z = jnp.zeros_like(wq)
    grp = (jnp.arange(N_HEADS_A) // HPG)[None, :, None]
    wq_pad = jnp.concatenate([jnp.where(grp == 0, wq, z), jnp.where(grp == 1, wq, z)], axis=-1)
    wq_pad = wq_pad.reshape(D, N_HEADS_A * 2 * HEAD_DIM)
    wkv = w_in[:, o1:o2]
    wg = jnp.pad(w_in[:, o2:o3], ((0, 0), (0, LANE - 3 * N_HEADS_A)))
    wqd = jnp.pad(w_in[:, o3:o4], ((0, 0), (0, 256 - Q_LORA)))
    wkd = w_in[:, o4:]
    half = QK_ROPE // 2
    wc = wkd[:, :KV_LORA]
    wr = wkd[:, KV_LORA:]
    wrot = jnp.concatenate([-wr[:, half:], wr[:, :half]], axis=1)
    padr = ((0, 0), (0, LANE - QK_ROPE))
    wkvd = jnp.concatenate([wc, jnp.pad(wr, padr), jnp.pad(wrot, padr)], axis=1)
    qn = jnp.pad(q_norm, (0, 256 - Q_LORA)).reshape(1, 256)
    wu = jnp.pad(w_q_up, ((0, 256 - Q_LORA), (0, 0))).reshape(256, N_HEADS_B, QK_NOPE + QK_ROPE)
    wu_n = wu[:, :, :QK_NOPE].reshape(256, N_HEADS_B * QK_NOPE)
    wu_r = wu[:, :, QK_NOPE:]
    wu_rot = jnp.concatenate([-wu_r[:, :, half:], wu_r[:, :, :half]], axis=-1)
    wqup = jnp.concatenate([wu_n, wu_r.reshape(256, -1), wu_rot.reshape(256, -1)], axis=1)
    w_ukv = w_kv_up.reshape(KV_LORA, N_HEADS_B, QK_NOPE + V_DIM)
    w_uk = w_ukv[:, :, :QK_NOPE]
    eye = jnp.eye(N_HEADS_B, dtype=F32)
    bd = jnp.einsum('chn,hk->hnkc', w_uk, eye)
    bd = jnp.pad(bd, ((0, 0), (0, 0), (0, 0), (0, 256 - KV_LORA))).reshape(N_HEADS_B * QK_NOPE, N_HEADS_B * 256)
    plc = jnp.einsum('hk,rs->hrks', eye, jnp.eye(QK_ROPE, dtype=F32))
    plc = jnp.pad(plc, ((0, 0), (0, 0), (0, 0), (KV_LORA, 256 - KV_LORA - QK_ROPE)))
    plc = plc.reshape(N_HEADS_B * QK_ROPE, N_HEADS_B * 256)
    w_uv = jnp.pad(w_ukv[:, :, QK_NOPE:].transpose(1, 0, 2), ((0, 0), (0, 0), (0, LANE - V_DIM)))
    bf = lambda a: a.astype(BF16)
    return dict(wq=bf(wq_pad), wkv=bf(wkv), wg=bf(wg), wqd=bf(wqd), wkvd=bf(wkvd), qn=qn, wqup=bf(wqup),
                bd=bf(bd), plc=bf(plc), kvn=kv_norm.reshape(1, KV_LORA), w_uv=bf(w_uv))


def _rope_tables(pos):
    half = QK_ROPE // 2
    inv = ROPE_THETA ** (-jnp.arange(half, dtype=F32) / half)
    ang = pos.astype(F32)[:, None] * inv[None, :]
    cos = jnp.tile(jnp.cos(ang), (1, 2 * N_HEADS_B))
    sin = jnp.tile(jnp.sin(ang), (1, 2 * N_HEADS_B))
    return cos, sin


def _inproj(x, shift, scale, gain, cos, sin, w, tr):
    B, T, D = x.shape
    tm = shift.shape[1]
    mod_spec = pl.BlockSpec((1, tr if tm > 1 else 1, D), (lambda b, t: (b, t, 0)) if tm > 1 else (lambda b, t: (b, 0, 0)))
    row = lambda n: pl.BlockSpec((1, tr, n), lambda b, t: (b, t, 0))
    tab = pl.BlockSpec((tr, 256), lambda b, t: (t, 0))
    wnames = ['wq', 'wkv', 'wg', 'wqd', 'wkvd', 'qn', 'wqup', 'bd', 'plc', 'kvn']
    out_shape = [
        jax.ShapeDtypeStruct((B, T, 1024), BF16), jax.ShapeDtypeStruct((B, T, 256), F32),
        jax.ShapeDtypeStruct((B, T, 256), F32), jax.ShapeDtypeStruct((B, T, 256), F32),
        jax.ShapeDtypeStruct((B, T, 768), BF16), jax.ShapeDtypeStruct((B, T, LANE), F32),
        jax.ShapeDtypeStruct((B, T, 2048), BF16), jax.ShapeDtypeStruct((B, T, LATENT_DIM), F32),
        jax.ShapeDtypeStruct((B, T, 256), BF16)]
    in_specs = [row(D), mod_spec, mod_spec, _const_spec((1, D)), tab, tab]
    in_specs += [_const_spec(w[n].shape) for n in wnames[:5]]
    in_specs += [_const_spec(w['qn'].shape)] + [_const_spec(w[n].shape) for n in wnames[6:9]]
    in_specs += [_const_spec(w['kvn'].shape)]
    return pl.pallas_call(
        _inproj_kernel, grid=(B, T // tr), in_specs=in_specs,
        out_specs=[row(s.shape[-1]) for s in out_shape], out_shape=out_shape, name='inproj',
        compiler_params=pltpu.CompilerParams(dimension_semantics=('parallel', 'parallel'),
                                             vmem_limit_bytes=VMEM_LIMIT),
    )(x, shift, scale, gain.reshape(1, D), cos, sin, *[w[n] for n in wnames])


def _compress_kernel(x0_ref, x1_ref, w1_ref, pe_ref, w2a_ref, w2b_ref, o_ref, *, n_cmp):
    w1 = w1_ref[0]
    pp = _dot(pe_ref[0], w1)
    peh = pp[0:1, 0:CMP_HIDDEN] + pp[1:2, CMP_HIDDEN:]
    nc = x0_ref.shape[2]
    out = jnp.zeros((nc, LANE), F32)
    for x_ref, w2_ref in ((x0_ref, w2a_ref), (x1_ref, w2b_ref)):
        ab = _dot(x_ref[0, 0], w1)
        hid = ab[:, 0:CMP_HIDDEN] + pltpu.roll(ab[:, CMP_HIDDEN:], nc - 1, 0) + peh
        act = hid * (1.0 / (1.0 + jnp.exp(-hid)))
        out = out + _dot(act.astype(BF16), w2_ref[0])
    rows = lax.broadcasted_iota(jnp.int32, (nc, LANE), 0)
    out = jnp.where(rows < n_cmp, out, 0.0)
    o_ref[0, 0, 0:CMP_PAD] = jnp.zeros((CMP_PAD, LANE), F32)
    o_ref[0, 0, CMP_PAD:CMP_PAD + nc] = out
    o_ref[0, 0, CMP_PAD + nc:] = jnp.zeros((o_ref.shape[2] - CMP_PAD - nc, LANE), F32)


def _compress_weights(cmp_pe, cmp_w1, cmp_w2):
    kin = CMP_STRIDE * HEAD_DIM
    w1 = cmp_w1.reshape(2, 2, kin, CMP_HIDDEN)
    w1cat = jnp.concatenate([w1[:, 0], w1[:, 1]], axis=-1).astype(BF16)
    pe = jnp.pad(cmp_pe.reshape(2, 2, kin), ((0, 0), (0, 6), (0, 0))).astype(BF16)
    w2a = jnp.pad(cmp_w2, ((0, 0), (0, 0), (0, HEAD_DIM))).astype(BF16)
    w2b = jnp.pad(cmp_w2, ((0, 0), (0, 0), (HEAD_DIM, 0))).astype(BF16)
    return w1cat, pe, w2a, w2b


def _compress(xc, cw, n_cmp):
    B, _, nc, kin = xc.shape
    w1cat, pe, w2a, w2b = cw
    return pl.pallas_call(
        functools.partial(_compress_kernel, n_cmp=n_cmp), grid=(B, 2),
        in_specs=[pl.BlockSpec((1, 1, nc, kin), lambda b, j: (b, 2 * j, 0, 0)),
                  pl.BlockSpec((1, 1, nc, kin), lambda b, j: (b, 2 * j + 1, 0, 0)),
                  pl.BlockSpec((1, kin, 2 * CMP_HIDDEN), lambda b, j: (j, 0, 0)),
                  pl.BlockSpec((1, 8, kin), lambda b, j: (j, 0, 0)),
                  pl.BlockSpec((1, CMP_HIDDEN, LANE), lambda b, j: (j, 0, 0)),
                  pl.BlockSpec((1, CMP_HIDDEN, LANE), lambda b, j: (j, 0, 0))],
        out_specs=pl.BlockSpec((1, 1, nc + LANE, LANE), lambda b, j: (b, j, 0, 0)),
        out_shape=jax.ShapeDtypeStruct((B, 2, nc + LANE, LANE), F32), name='compress',
        compiler_params=pltpu.CompilerParams(dimension_semantics=('parallel', 'parallel'),
                                             vmem_limit_bytes=VMEM_LIMIT),
    )(xc, xc, w1cat, pe, w2a, w2b)


def _t5_bucket(rel):
    max_exact = N_BUCKETS // 2
    n = jnp.maximum(rel, 0)
    nf = jnp.maximum(n, 1).astype(F32)
    large = max_exact + (jnp.log(nf / max_exact) / math.log(MAX_DISTANCE / max_exact)
                         * (N_BUCKETS - max_exact)).astype(jnp.int32)
    large = jnp.minimum(large, N_BUCKETS - 1)
    return jnp.where(n < max_exact, n, large)


def _bias_tables(rel_bias):
    tbl = rel_bias.astype(F32)
    const = tbl[N_BUCKETS - 1]
    i = jnp.arange(QB)[:, None]
    j = jnp.arange(LANE)[None, :]

    def tab(rel):
        b = tbl[_t5_bucket(rel)]
        return jnp.moveaxis(b, -1, 0) - const[:, None, None]

    t0 = jnp.where((i - j >= 0)[None], tab(i - j), NEG)
    t1 = tab(QB + i - j)
    zero = jnp.zeros_like(t1)
    t4 = jnp.broadcast_to(jnp.where(j > i, 0.0, NEG)[None], t1.shape)
    tb = jnp.stack([t0, t1, zero, jnp.full_like(t1, NEG), t4]).reshape(5, N_HEADS_A * QB, LANE)
    rel_l = i - CMP_STRIDE * (j - CMP_PAD) - (CMP_BLOCK - 1)
    lb = jnp.where(((j < LOC_W) & (rel_l >= 0))[None], tab(rel_l), NEG).reshape(N_HEADS_A * QB, LANE)
    return tb, lb


def _hilo_dot(x, m):
    hi = x.astype(BF16)
    lo = (x - hi.astype(F32)).astype(BF16)
    return _dot(hi, m) + _dot(lo, m)


def _nsa_kernel(q_ref, gt_ref, ks_ref, vs_ref, kw_ref, vw_ref, kc_ref, vc_ref, tb_ref, lb_ref, pf_ref, e_ref,
                o_ref, m_scr, l_scr, acc_scr, *, n_cmp):
    qb = pl.program_id(1)
    nh = N_HEADS_A
    rows = nh * QB
    q = q_ref[0]
    q8 = jnp.concatenate([q[:, h * LANE:(h + 1) * LANE] for h in range(nh)], axis=0)
    nc = pf_ref.shape[0]

    kcf = kc_ref[0, 0, 0:nc].astype(BF16)
    vcf = vc_ref[0, 0, 0:nc].astype(BF16)
    l0 = pl.multiple_of(qb * (QB // CMP_STRIDE), 8)
    kcl = kc_ref[0, 0, pl.ds(l0, LANE)].astype(BF16)
    vcl = vc_ref[0, 0, pl.ds(l0, LANE)].astype(BF16)
    colf = lax.broadcasted_iota(jnp.int32, (1, nc), 1)
    far_ok = jnp.where(colf >= CMP_PAD, jnp.where(colf < l0, 0.0, NEG), NEG)
    coll = lax.broadcasted_iota(jnp.int32, (1, LANE), 1) + (l0 - CMP_PAD)
    loc_ok = jnp.where(coll >= 0, jnp.where(coll < n_cmp, 0.0, NEG), NEG)
    s_far = _dot_t(q8, kcf) + far_ok
    s_loc = _dot_t(q8, kcl) + lb_ref[...] + loc_ok
    mrow = jnp.maximum(jnp.max(s_far, axis=1, keepdims=True), jnp.max(s_loc, axis=1, keepdims=True))
    p_far = jnp.exp(s_far - mrow)
    p_loc = jnp.exp(s_loc - mrow)
    lsum = jnp.sum(p_far, axis=1, keepdims=True) + jnp.sum(p_loc, axis=1, keepdims=True)
    inv = jnp.where(mrow > 0.5 * NEG, 1.0 / lsum, 0.0)
    p_far = p_far * inv
    p_loc = p_loc * inv
    o_c = _dot(p_far.astype(BF16), vcf) + _dot(p_loc.astype(BF16), vcl)

    r_i = lax.broadcasted_iota(jnp.int32, (LANE, LANE), 0)
    c_i = lax.broadcasted_iota(jnp.int32, (LANE, LANE), 1)
    pool_loc = jnp.where(r_i < LOC_W,
                         jnp.where(c_i == (r_i >> 2) + (2 * qb - CMP_PAD // SLC_PER_CMP), 1.0, 0.0),
                         0.0).astype(BF16)
    tq = 2 * qb + jnp.where(r_i >= SLC_BLOCK, 1, 0)
    dist = tq - c_i
    c_f = c_i.astype(F32)
    sel = []
    for g in range(N_KV_A):
        pgf = p_far[(g * HPG) * QB:(g * HPG + 1) * QB]
        pgl = p_loc[(g * HPG) * QB:(g * HPG + 1) * QB]
        for hh in range(1, HPG):
            pgf = pgf + p_far[(g * HPG + hh) * QB:(g * HPG + hh + 1) * QB]
            pgl = pgl + p_loc[(g * HPG + hh) * QB:(g * HPG + hh + 1) * QB]
        imp = _hilo_dot(pgf, pf_ref[...]) + _hilo_dot(pgl, pool_loc)
        score = jnp.where(dist < 0, NEG, jnp.where(dist < N_LOCAL_SLC, FORCED, jnp.where(c_i == 0, FORCED, imp)))
        chosen = jnp.zeros((QB, LANE), F32)
        for _ in range(N_SLC):
            mx = jnp.max(score, axis=1, keepdims=True)
            first = jnp.min(jnp.where(score == mx, c_f, float(LANE)), axis=1, keepdims=True)
            hit = c_f == first
            chosen = jnp.where(hit, 1.0, chosen)
            score = jnp.where(hit, -jnp.inf, score)
        sel.append(chosen.astype(BF16))

    m_scr[...] = jnp.full(m_scr.shape, -jnp.inf, F32)
    l_scr[...] = jnp.zeros(l_scr.shape, F32)
    acc_scr[...] = jnp.zeros(acc_scr.shape, F32)
    nsub = KT // LANE

    def slc_tile(kt, carry):
        k0 = pl.multiple_of(kt * KT, KT)
        s = _dot_t(q8, ks_ref[0, pl.ds(k0, KT), :])
        bias = []
        for c in range(nsub):
            d = qb - (kt * nsub + c)
            bias.append(tb_ref[jnp.where(d < 0, 3, jnp.minimum(d, 2))])
        e_t = e_ref[:, pl.ds(k0, KT)]
        madd = [(_dot(sel[g], e_t) - 1.0) * (-NEG) for g in range(N_KV_A)]
        madd = jnp.concatenate([madd[g] for g in range(N_KV_A) for _ in range(HPG)], axis=0)
        s = s + jnp.concatenate(bias, axis=1) + madd
        m_old = m_scr[...]
        m_new = jnp.maximum(m_old, jnp.max(s, axis=1, keepdims=True))
        alpha = jnp.exp(m_old - m_new)
        p = jnp.exp(s - m_new)
        l_scr[...] = alpha * l_scr[...] + jnp.sum(p, axis=1, keepdims=True)
        acc_scr[...] = alpha * acc_scr[...] + _dot(p.astype(BF16), vs_ref[0, pl.ds(k0, KT), :])
        m_scr[...] = m_new
        return carry

    lax.fori_loop(0, qb // nsub + 1, slc_tile, 0)
    o_s = acc_scr[...] * (1.0 / l_scr[...])

    nwin = WINDOW // QB + 1
    w0 = jnp.maximum(qb - (nwin - 1), 0)
    k0 = pl.multiple_of(w0 * QB, QB)
    s = _dot_t(q8, kw_ref[0, pl.ds(k0, nwin * QB), :])
    bias = []
    for c in range(nwin):
        d = qb - (w0 + c)
        bias.append(tb_ref[jnp.where(d < 0, 3, jnp.where(d >= nwin - 1, 4, jnp.minimum(d, 2)))])
    s = s + jnp.concatenate(bias, axis=1)
    p = jnp.exp(s - jnp.max(s, axis=1, keepdims=True))
    o_w = _dot(p.astype(BF16), vw_ref[0, pl.ds(k0, nwin * QB), :]) * (1.0 / jnp.sum(p, axis=1, keepdims=True))

    gt = gt_ref[0]
    for h in range(nh):
        r = slice(h * QB, (h + 1) * QB)
        comb = (gt[:, h:h + 1] * o_c[r] + gt[:, nh + h:nh + h + 1] * o_s[r]
                + gt[:, 2 * nh + h:2 * nh + h + 1] * o_w[r])
        keep = (c_i >= HEAD_DIM) if h // HPG else (c_i < HEAD_DIM)
        o_ref[0, :, h * LANE:(h + 1) * LANE] = jnp.where(keep, comb, 0.0)


def _nsa_prompt(qa, gates, kv16, kc, tb, lb):
    B, S, _ = qa.shape
    nc = S // CMP_STRIDE
    n_cmp = (S - CMP_BLOCK) // CMP_STRIDE + 1
    m = jnp.arange(nc)
    pool_far = ((m[:, None] // SLC_PER_CMP - CMP_PAD // SLC_PER_CMP == jnp.arange(LANE)[None, :])
                & (m[:, None] >= CMP_PAD)).astype(BF16)
    expand = (jnp.arange(S)[None, :] // SLC_BLOCK == jnp.arange(LANE)[:, None]).astype(BF16)
    rows = N_HEADS_A * QB
    one = pl.Buffered(1)
    kvs = lambda c: pl.BlockSpec((1, S, LANE), lambda b, t: (b, 0, c), pipeline_mode=one)
    cspec = pl.BlockSpec((1, 1, nc + LANE, LANE), lambda b, t: (b, 0, 0, 0), pipeline_mode=one)
    vspec = pl.BlockSpec((1, 1, nc + LANE, LANE), lambda b, t: (b, 1, 0, 0), pipeline_mode=one)
    cst = lambda shape: pl.BlockSpec(shape, lambda b, t: (0,) * len(shape), pipeline_mode=one)
    return pl.pallas_call(
        functools.partial(_nsa_kernel, n_cmp=n_cmp), grid=(B, S // QB),
        in_specs=[pl.BlockSpec((1, QB, 1024), lambda b, t: (b, t, 0)),
                  pl.BlockSpec((1, QB, LANE), lambda b, t: (b, t, 0)),
                  kvs(2), kvs(3), kvs(4), kvs(5), cspec, vspec,
                  cst(tb.shape), cst(lb.shape), cst(pool_far.shape), cst(expand.shape)],
        out_specs=pl.BlockSpec((1, QB, 1024), lambda b, t: (b, t, 0)),
        out_shape=jax.ShapeDtypeStruct((B, S, 1024), F32),
        scratch_shapes=[pltpu.VMEM((rows, 1), F32), pltpu.VMEM((rows, 1), F32), pltpu.VMEM((rows, LANE), F32)],
        name='nsa_prompt',
        compiler_params=pltpu.CompilerParams(dimension_semantics=('parallel', 'arbitrary'),
                                             vmem_limit_bytes=VMEM_LIMIT),
    )(qa, gates, kv16, kv16, kv16, kv16, kc, kc, tb, lb, pool_far, expand)


def _mla_kernel(q_ref, lat_ref, wuv_ref, o_ref, m_scr, l_scr, acc_scr):
    qb = pl.program_id(1)
    nh = N_HEADS_B
    rows = nh * QB
    q = q_ref[0]
    q8 = jnp.concatenate([q[:, h * 256:(h + 1) * 256] for h in range(nh)], axis=0)
    m_scr[...] = jnp.full(m_scr.shape, -jnp.inf, F32)
    l_scr[...] = jnp.zeros(l_scr.shape, F32)
    acc_scr[...] = jnp.zeros(acc_scr.shape, F32)
    nsub = KT // QB

    def tile(kt, masked):
        k0 = pl.multiple_of(kt * KT, KT)
        lat = lat_ref[0, pl.ds(k0, KT), :]
        s = _dot_t(q8, lat)
        if masked:
            col = lax.broadcasted_iota(jnp.int32, (rows, KT), 1)
            row = lax.broadcasted_iota(jnp.int32, (rows, KT), 0) & (QB - 1)
            s = jnp.where(col - row <= qb * QB - kt * KT, s, NEG)
        m_old = m_scr[...]
        m_new = jnp.maximum(m_old, _row_max(s))
        alpha = jnp.exp(m_old - m_new)
        p = jnp.exp(s - m_new)
        l_scr[...] = alpha * l_scr[...] + _row_sum(p)
        acc_scr[...] = alpha * acc_scr[...] + _dot(p.astype(BF16), lat[:, 0:KV_LORA])
        m_scr[...] = m_new

    def full_tile(kt, carry):
        tile(kt, False)
        return carry

    lax.fori_loop(0, qb // nsub, full_tile, 0)
    tile(qb // nsub, True)
    o_lat = (acc_scr[...] * (1.0 / l_scr[...])).astype(BF16)
    for h in range(nh):
        o_ref[0, :, h * LANE:(h + 1) * LANE] = _dot(o_lat[h * QB:(h + 1) * QB], wuv_ref[h])


def _mla_prompt(qm, lat16, w_uv):
    B, S, _ = qm.shape
    rows = N_HEADS_B * QB
    one = pl.Buffered(1)
    return pl.pallas_call(
        _mla_kernel, grid=(B, S // QB),
        in_specs=[pl.BlockSpec((1, QB, 2048), lambda b, t: (b, t, 0)),
                  pl.BlockSpec((1, S, 256), lambda b, t: (b, 0, 0), pipeline_mode=one),
                  pl.BlockSpec(w_uv.shape, lambda b, t: (0, 0, 0), pipeline_mode=one)],
        out_specs=pl.BlockSpec((1, QB, 1024), lambda b, t: (b, t, 0)),
        out_shape=jax.ShapeDtypeStruct((B, S, 1024), F32),
        scratch_shapes=[pltpu.VMEM((rows, 1), F32), pltpu.VMEM((rows, 1), F32), pltpu.VMEM((rows, KV_LORA), F32)],
        name='mla_prompt',
        compiler_params=pltpu.CompilerParams(dimension_semantics=('parallel', 'arbitrary'),
                                             vmem_limit_bytes=VMEM_LIMIT),
    )(qm, lat16, w_uv)


def _adaln_kernel(c_ref, w_ref, b_ref, o_ref):
    c = c_ref[...]
    a = (c * (1.0 / (1.0 + jnp.exp(-c)))).astype(BF16)
    o_ref[...] = _dot(a, w_ref[...].astype(BF16)) + b_ref[...]


def _adaln(c, w_ada, b_ada, tn=512):
    R_, D = c.shape
    N = w_ada.shape[1]
    return pl.pallas_call(
        _adaln_kernel, grid=(N // tn,),
        in_specs=[pl.BlockSpec((R_, D), lambda j: (0, 0)), pl.BlockSpec((D, tn), lambda j: (0, j)),
                  pl.BlockSpec((1, tn), lambda j: (0, j))],
        out_specs=pl.BlockSpec((R_, tn), lambda j: (0, j)),
        out_shape=jax.ShapeDtypeStruct((R_, N), F32), name='adaln',
        compiler_params=pltpu.CompilerParams(dimension_semantics=('parallel',), vmem_limit_bytes=VMEM_LIMIT),
    )(c, w_ada, b_ada.reshape(1, N))


def _merge_kernel(x_ref, oa_ref, ob_ref, ga_ref, shf_ref, scf_ref, gf_ref, na_ref, nb_ref, nffn_ref,
                  wa_ref, wb_ref, wr_ref, wgs_ref, wus_ref, wds_ref, xs_ref, f_ref, sc_ref):
    n_real = N_HEADS_A * HEAD_DIM
    oa = oa_ref[0]
    ob = ob_ref[0]
    na = oa * lax.rsqrt(jnp.sum(oa * oa, axis=-1, keepdims=True) * (1.0 / n_real) + EPS) * na_ref[...]
    nb = ob * lax.rsqrt(jnp.sum(ob * ob, axis=-1, keepdims=True) * (1.0 / n_real) + EPS) * nb_ref[...]
    mix = _dot(na.astype(BF16), wa_ref[...]) + _dot(nb.astype(BF16), wb_ref[...])
    x1 = x_ref[0] + ga_ref[0] * mix
    f = x1 * lax.rsqrt(jnp.mean(x1 * x1, axis=-1, keepdims=True) + EPS) * nffn_ref[...]
    f = f * (1.0 + scf_ref[0]) + shf_ref[0]
    f_ref[0] = f
    fb = f.astype(BF16)
    sc_ref[0] = 1.0 / (1.0 + jnp.exp(-_dot(fb, wr_ref[...])))
    g = _dot(fb, wgs_ref[...])
    u = _dot(fb, wus_ref[...])
    hsh = (g * (1.0 / (1.0 + jnp.exp(-g))) * u).astype(BF16)
    xs_ref[0] = x1 + gf_ref[0] * _dot(hsh, wds_ref[...])


def _merge_weights(out_norm_a, out_norm_b, w_out, norm_ffn, w_router, w_gate_s, w_up_s, w_down_s):
    D = w_out.shape[1]
    na = out_norm_a.reshape(N_HEADS_A, 1, HEAD_DIM)
    grp = (jnp.arange(N_HEADS_A) // HPG)[:, None, None]
    half = jnp.arange(2)[None, :, None]
    na_pad = jnp.where(grp == half, na, 0.0).reshape(1, -1)
    nb_pad = jnp.pad(out_norm_b.reshape(N_HEADS_B, V_DIM), ((0, 0), (0, LANE - V_DIM))).reshape(1, -1)
    wa = w_out[:N_HEADS_A * HEAD_DIM].reshape(N_HEADS_A, 1, HEAD_DIM, D)
    wa_pad = jnp.where((grp == half)[..., None], wa, 0.0).reshape(-1, D)
    wb = w_out[N_HEADS_A * HEAD_DIM:].reshape(N_HEADS_B, V_DIM, D)
    wb_pad = jnp.pad(wb, ((0, 0), (0, LANE - V_DIM), (0, 0))).reshape(-1, D)
    bf = lambda a: a.astype(BF16)
    return [na_pad, nb_pad, norm_ffn.reshape(1, D), bf(wa_pad), bf(wb_pad), bf(w_router), bf(w_gate_s),
            bf(w_up_s), bf(w_down_s)]


def _mod_spec(a, tr):
    if a.shape[1] > 1:
        return pl.BlockSpec((1, tr, a.shape[2]), lambda b, t: (b, t, 0))
    return pl.BlockSpec((1, 1, a.shape[2]), lambda b, t: (b, 0, 0))


def _merge(x, oa, ob, gate_a, shift_f, scale_f, gate_f, mw, tr):
    B, T, D = x.shape
    row = lambda n: pl.BlockSpec((1, tr, n), lambda b, t: (b, t, 0))
    out_shape = [jax.ShapeDtypeStruct((B, T, D), F32), jax.ShapeDtypeStruct((B, T, D), F32),
                 jax.ShapeDtypeStruct((B, T, N_EXPERTS), F32)]
    return pl.pallas_call(
        _merge_kernel, grid=(B, T // tr),
        in_specs=[row(D), row(1024), row(1024)] + [_mod_spec(a, tr) for a in (gate_a, shift_f, scale_f, gate_f)]
        + [_const_spec(a.shape) for a in mw],
        out_specs=[row(D), row(D), row(N_EXPERTS)], out_shape=out_shape, name='merge',
        compiler_params=pltpu.CompilerParams(dimension_semantics=('parallel', 'parallel'),
                                             vmem_limit_bytes=VMEM_LIMIT),
    )(x, oa, ob, gate_a, shift_f, scale_f, gate_f, *mw)


def _gather_rows(idx_ref, idx_row, src_hbm, dst, sem):
    def body(r, c):
        pltpu.make_async_copy(src_hbm.at[pl.ds(idx_ref[idx_row, r], 1)], dst.at[pl.ds(r, 1)], sem).start()
        return c
    lax.fori_loop(0, MOE_BLOCK, body, 0, unroll=8)


MOE_CHUNK = 256


def _moe_kernel(nused_ref, bexp_ref, rtok_hbm, f_hbm, wg_ref, wu_ref, wd_ref, y_ref, xbuf, wgb, wub, wdb, rtok, sem,
                isem):
    i = pl.program_id(0)
    nused = nused_ref[0]
    nchunk = rtok_hbm.shape[0] // MOE_CHUNK

    def ids_copy(c):
        return pltpu.make_async_copy(rtok_hbm.at[pl.ds(c * MOE_CHUNK, MOE_CHUNK)], rtok.at[c % 2], isem.at[c % 2])

    def gather(blk):
        c = blk // MOE_CHUNK
        _gather_rows(rtok.at[c % 2], blk % MOE_CHUNK, f_hbm, xbuf.at[blk % 2], sem.at[blk % 2])

    @pl.when(i == 0)
    def _():
        ids_copy(0).start()
        ids_copy(0).wait()
        if nchunk > 1:
            ids_copy(1).start()

        @pl.when(nused > 0)
        def _():
            gather(0)

    @pl.when(((i + 1) % MOE_CHUNK == 0) & (i + 1 < pl.num_programs(0)))
    def _():
        c = (i + 1) // MOE_CHUNK
        ids_copy(c).wait()

        @pl.when(c + 1 < nchunk)
        def _():
            ids_copy(c + 1).start()

    @pl.when((i + 1 < nused) & (i + 1 < pl.num_programs(0)))
    def _():
        gather(i + 1)

    @pl.when(i < nused)
    def _():
        slot = i % 2
        pltpu.make_async_copy(f_hbm.at[pl.ds(0, MOE_BLOCK)], xbuf.at[slot], sem.at[slot]).wait()

        @pl.when((i == 0) | (bexp_ref[i] != bexp_ref[jnp.maximum(i - 1, 0)]))
        def _():
            wgb[...] = wg_ref[0].astype(BF16)
            wub[...] = wu_ref[0].astype(BF16)
            wdb[...] = wd_ref[0].astype(BF16)

        x = xbuf[slot].astype(BF16)
        g = _dot(x, wgb[...])
        u = _dot(x, wub[...])
        h = (g * (1.0 / (1.0 + jnp.exp(-g))) * u).astype(BF16)
        y_ref[...] = _dot(h, wdb[...])

    @pl.when(i >= nused)
    def _():
        y_ref[...] = jnp.zeros(y_ref.shape, F32)


def _moe_experts(nused, blk_exp, row_tok, f, w_gate_e, w_up_e, w_down_e):
    n_blocks = blk_exp.shape[0]
    nb_pad = -(-n_blocks // MOE_CHUNK) * MOE_CHUNK
    row_tok = jnp.pad(row_tok, (0, (nb_pad - n_blocks) * MOE_BLOCK)).reshape(nb_pad, MOE_BLOCK)
    D = f.shape[1]
    de = w_gate_e.shape[2]
    wspec = lambda shp: pl.BlockSpec((1,) + shp, lambda i, nu, be: (be[i], 0, 0))
    return pl.pallas_call(
        _moe_kernel,
        grid_spec=pltpu.PrefetchScalarGridSpec(
            num_scalar_prefetch=2, grid=(n_blocks,),
            in_specs=[pl.BlockSpec(memory_space=pl.ANY), pl.BlockSpec(memory_space=pl.ANY),
                      wspec((D, de)), wspec((D, de)), wspec((de, D))],
            out_specs=pl.BlockSpec((MOE_BLOCK, D), lambda i, nu, be: (i, 0)),
            scratch_shapes=[pltpu.VMEM((2, MOE_BLOCK, D), F32), pltpu.VMEM((D, de), BF16),
                            pltpu.VMEM((D, de), BF16), pltpu.VMEM((de, D), BF16),
                            pltpu.SMEM((2, MOE_CHUNK, MOE_BLOCK), jnp.int32),
                            pltpu.SemaphoreType.DMA((2,)), pltpu.SemaphoreType.DMA((2,))]),
        out_shape=jax.ShapeDtypeStruct((n_blocks * MOE_BLOCK, D), F32), name='moe_experts',
        compiler_params=pltpu.CompilerParams(dimension_semantics=('arbitrary',), vmem_limit_bytes=VMEM_LIMIT),
    )(nused, blk_exp, row_tok, f, w_gate_e, w_up_e, w_down_e)


def _combine_kernel(pos_ref, yb_hbm, xs_ref, gf_ref, w_ref, nf_ref, o_ref, buf, sem):
    tile = pl.program_id(0) * pl.num_programs(1) + pl.program_id(1)
    ntile = pl.num_programs(0) * pl.num_programs(1)

    def start(t, slot):
        for k in range(TOP_K):
            _gather_rows(pos_ref, t * TOP_K + k, yb_hbm, buf.at[slot, k], sem.at[slot])

    @pl.when(tile == 0)
    def _():
        start(0, 0)

    @pl.when(tile + 1 < ntile)
    def _():
        start(tile + 1, (tile + 1) % 2)

    slot = tile % 2
    w = w_ref[0]
    routed = jnp.zeros(xs_ref.shape[1:], F32)
    for k in range(TOP_K):
        pltpu.make_async_copy(yb_hbm.at[pl.ds(0, MOE_BLOCK)], buf.at[slot, k], sem.at[slot]).wait()
    for k in range(TOP_K):
        routed = routed + w[:, k:k + 1] * buf[slot, k]
    x2 = xs_ref[0] + gf_ref[0] * routed
    o_ref[0] = x2 * lax.rsqrt(jnp.mean(x2 * x2, axis=-1, keepdims=True) + EPS) * nf_ref[...]


def _combine(pos, yb, xs, gate_f, wts, norm_final):
    B, T, D = xs.shape
    tr = MOE_BLOCK
    gspec = (pl.BlockSpec((1, tr, D), lambda b, t, p: (b, t, 0)) if gate_f.shape[1] > 1
             else pl.BlockSpec((1, 1, D), lambda b, t, p: (b, 0, 0)))
    return pl.pallas_call(
        _combine_kernel,
        grid_spec=pltpu.PrefetchScalarGridSpec(
            num_scalar_prefetch=1, grid=(B, T // tr),
            in_specs=[pl.BlockSpec(memory_space=pl.ANY), pl.BlockSpec((1, tr, D), lambda b, t, p: (b, t, 0)), gspec,
                      pl.BlockSpec((1, tr, TOP_K), lambda b, t, p: (b, t, 0)),
                      pl.BlockSpec((1, D), lambda b, t, p: (0, 0))],
            out_specs=pl.BlockSpec((1, tr, D), lambda b, t, p: (b, t, 0)),
            scratch_shapes=[pltpu.VMEM((2, TOP_K, tr, D), F32), pltpu.SemaphoreType.DMA((2,))]),
        out_shape=jax.ShapeDtypeStruct((B, T, D), F32), name='combine',
        compiler_params=pltpu.CompilerParams(dimension_semantics=('arbitrary', 'arbitrary'),
                                             vmem_limit_bytes=VMEM_LIMIT),
    )(pos, yb, xs, gate_f, wts, norm_final.reshape(1, D))


def _route(scores, router_bias):
    n_tok = scores.shape[0]
    n_asg = n_tok * TOP_K
    idx, wts, rank, counts = _route_rank(scores, router_bias.astype(F32).reshape(1, N_EXPERTS))
    padded = (counts + MOE_BLOCK - 1) // MOE_BLOCK * MOE_BLOCK
    pad_end = jnp.cumsum(padded)
    pos = (pad_end - padded)[idx] + rank
    n_blocks = -(-(n_asg + N_EXPERTS * (MOE_BLOCK - 1)) // MOE_BLOCK)
    n_rows = n_blocks * MOE_BLOCK
    tok = jnp.broadcast_to(jnp.arange(n_tok, dtype=jnp.int32)[:, None], pos.shape)
    row_tok = jnp.zeros((n_rows,), jnp.int32).at[pos.reshape(-1)].set(tok.reshape(-1))
    blk_exp = jnp.minimum(jnp.searchsorted(pad_end, jnp.arange(n_blocks) * MOE_BLOCK, side='right'),
                          N_EXPERTS - 1).astype(jnp.int32)
    nused = (pad_end[-1] // MOE_BLOCK).astype(jnp.int32).reshape(1)
    return wts, pos, row_tok, blk_exp, nused


def _route_kernel(sc_ref, rb_ref, idx_ref, wts_ref, rank_ref, cnt_ref, carry):
    i = pl.program_id(0)

    @pl.when(i == 0)
    def _():
        carry[...] = jnp.zeros(carry.shape, F32)

    tr, ne = sc_ref.shape
    gsz = ne // N_GROUPS
    scores = sc_ref[...]
    biased = scores + rb_ref[...]
    lane = lax.broadcasted_iota(jnp.int32, (tr, ne), 1)
    lane_f = lane.astype(F32)
    lgrp = lane // gsz
    ninf = -jnp.inf

    def first_max(x):
        mx = jnp.max(x, axis=1, keepdims=True)
        return mx, jnp.min(jnp.where(x == mx, lane_f, float(ne)), axis=1, keepdims=True)

    gl = lax.broadcasted_iota(jnp.int32, (tr, LANE), 1)
    gscore = jnp.full((tr, LANE), ninf, F32)
    for g in range(N_GROUPS):
        xg = jnp.where(lgrp == g, biased, ninf)
        m1, f1 = first_max(xg)
        m2 = jnp.max(jnp.where(lane_f == f1, ninf, xg), axis=1, keepdims=True)
        gscore = jnp.where(gl == g, m1 + m2, gscore)
    gl_f = gl.astype(F32)
    keep = jnp.full((tr, ne), NEG, F32)
    for _ in range(TOP_GROUPS):
        mx = jnp.max(gscore, axis=1, keepdims=True)
        gf = jnp.min(jnp.where(gscore == mx, gl_f, float(LANE)), axis=1, keepdims=True)
        gscore = jnp.where(gl_f == gf, ninf, gscore)
        keep = jnp.where(lgrp.astype(F32) == gf, biased, keep)
    kcol = lax.broadcasted_iota(jnp.int32, (tr, TOP_K), 1)
    idx = jnp.zeros((tr, TOP_K), F32)
    wts = jnp.zeros((tr, TOP_K), F32)
    hot = []
    onehot = jnp.zeros((tr, ne), F32)
    for k in range(TOP_K):
        _, f = first_max(keep)
        hit = lane_f == f
        hot.append(hit)
        keep = jnp.where(hit, ninf, keep)
        onehot = jnp.where(hit, 1.0, onehot)
        idx = jnp.where(kcol == k, f, idx)
        wts = jnp.where(kcol == k, jnp.sum(jnp.where(hit, scores, 0.0), axis=1, keepdims=True), wts)
    idx_ref[...] = idx.astype(jnp.int32)
    wts_ref[...] = wts / jnp.sum(wts, axis=1, keepdims=True) * ROUTED_SCALE
    r_i = lax.broadcasted_iota(jnp.int32, (tr, tr), 0)
    c_i = lax.broadcasted_iota(jnp.int32, (tr, tr), 1)
    lower = jnp.where(c_i < r_i, 1.0, 0.0).astype(BF16)
    before = _dot(lower, onehot.astype(BF16)) + carry[0:1, :]
    rank = jnp.zeros((tr, TOP_K), F32)
    for k in range(TOP_K):
        rank = jnp.where(kcol == k, jnp.sum(jnp.where(hot[k], before, 0.0), axis=1, keepdims=True), rank)
    rank_ref[...] = rank.astype(jnp.int32)
    total = carry[0:1, :] + jnp.sum(onehot, axis=0, keepdims=True)
    carry[...] = jnp.broadcast_to(total, carry.shape)
    cnt_ref[...] = jnp.broadcast_to(total, cnt_ref.shape).astype(jnp.int32)


def _route_rank(scores, router_bias):
    n_tok, ne = scores.shape
    tr = MOE_BLOCK
    tk = lambda dt: jax.ShapeDtypeStruct((n_tok, TOP_K), dt)
    tspec = pl.BlockSpec((tr, TOP_K), lambda i: (i, 0))
    idx, wts, rank, cnt = pl.pallas_call(
        _route_kernel, grid=(n_tok // tr,),
        in_specs=[pl.BlockSpec((tr, ne), lambda i: (i, 0)), pl.BlockSpec((1, ne), lambda i: (0, 0))],
        out_specs=[tspec, tspec, tspec, pl.BlockSpec((8, ne), lambda i: (0, 0))],
        out_shape=[tk(jnp.int32), tk(F32), tk(jnp.int32), jax.ShapeDtypeStruct((8, ne), jnp.int32)],
        scratch_shapes=[pltpu.VMEM((8, ne), F32)], name='route',
        compiler_params=pltpu.CompilerParams(dimension_semantics=('arbitrary',), vmem_limit_bytes=VMEM_LIMIT),
    )(scores, router_bias)
    return idx, wts, rank, cnt[0]


PAGE = 128


def _softmax_with_new(s, s_new):
    m = jnp.maximum(jnp.max(s, axis=1, keepdims=True), s_new)
    p = jnp.exp(s - m)
    pn = jnp.exp(s_new - m)
    return p, pn, 1.0 / (jnp.sum(p, axis=1, keepdims=True) + pn)


def _samp_cmp_kernel(pt_ref, pool_hbm, q_ref, wbd_ref, pe_ref, w1c_ref, w2_ref, bc_ref, pf_ref,
                     oc_ref, idx_ref, buf_t, buf, peh_scr, sem, *, n_pages, n_cmp):
    s = pl.program_id(0)
    ns = pl.num_programs(0)
    nc = n_pages * PAGE // CMP_STRIDE

    def page_copy(page, slot, p):
        return pltpu.make_async_copy(pool_hbm.at[page], buf_t.at[slot, :, :, pl.ds(p * PAGE, PAGE)], sem.at[slot])

    def start(smp, slot):
        def body(p, c):
            page_copy(pt_ref[smp, p], slot, p).start()
            return c
        lax.fori_loop(0, n_pages, body, 0)

    @pl.when(s == 0)
    def _():
        start(0, 0)
        for j in range(2):
            pp = _dot(pe_ref[j], w1c_ref[j])
            peh_scr[j] = jnp.broadcast_to(pp[0:1, 0:CMP_HIDDEN] + pp[1:2, CMP_HIDDEN:], (8, CMP_HIDDEN))

    @pl.when(s + 1 < ns)
    def _():
        start(s + 1, (s + 1) % 2)

    slot = s % 2

    def wait_page(p, c):
        page_copy(0, slot, p).wait()
        return c
    lax.fori_loop(0, n_pages, wait_page, 0)

    tw = 4 * PAGE

    def to_rows(c, carry):
        l0 = pl.multiple_of(c * tw, tw)
        for j in range(2):
            xt = buf_t[slot, pl.ds(2 * j, 2), :, pl.ds(l0, tw)].reshape(2 * HEAD_DIM, tw)
            buf[j, pl.ds(l0, tw), :] = xt.T
        return carry
    lax.fori_loop(0, n_pages * PAGE // tw, to_rows, 0)

    rows = lax.broadcasted_iota(jnp.int32, (nc, LANE), 0)
    kvc = []
    for j in range(2):
        acc = jnp.zeros((nc, 4 * CMP_HIDDEN), F32)
        for r in range(CMP_STRIDE):
            x = buf[j, pl.ds(r, nc, stride=CMP_STRIDE), :].astype(BF16)
            acc = acc + _dot(x, wbd_ref[j, r])
        out = jnp.zeros((nc, LANE), F32)
        for g in range(N_KV_A):
            a = acc[:, g * 2 * CMP_HIDDEN:g * 2 * CMP_HIDDEN + CMP_HIDDEN]
            bm = acc[:, g * 2 * CMP_HIDDEN + CMP_HIDDEN:(g + 1) * 2 * CMP_HIDDEN]
            hid = a + pltpu.roll(bm, nc - 1, 0) + peh_scr[j, 0:1]
            act = hid * (1.0 / (1.0 + jnp.exp(-hid)))
            out = out + _dot(act.astype(BF16), w2_ref[j, g])
        kvc.append(jnp.where(rows < n_cmp, out, 0.0).astype(BF16))
    kc, vc = kvc

    q8 = q_ref[0]
    sc = _dot_t(q8, kc) + bc_ref[...]
    p = jnp.exp(sc - jnp.max(sc, axis=1, keepdims=True))
    p = p * (1.0 / jnp.sum(p, axis=1, keepdims=True))
    oc_ref[0] = _dot(p.astype(BF16), vc)

    pg = jnp.concatenate([jnp.sum(p[0:HPG], axis=0, keepdims=True), jnp.sum(p[HPG:2 * HPG], axis=0, keepdims=True),
                          jnp.zeros((8 - N_KV_A, nc), F32)], axis=0)
    imp = _hilo_dot(pg, pf_ref[...])
    lane = lax.broadcasted_iota(jnp.int32, (8, LANE), 1).astype(F32)
    score = jnp.where(lane == 0.0, FORCED, jnp.where(lane >= float(LANE - N_LOCAL_SLC + 1), FORCED, imp))
    picks = jnp.full((8, LANE), float(LANE), F32)
    for k in range(N_SLC - 1):
        mx = jnp.max(score, axis=1, keepdims=True)
        first = jnp.min(jnp.where(score == mx, lane, float(LANE)), axis=1, keepdims=True)
        picks = jnp.where(lane == float(k), first, picks)
        score = jnp.where(lane == first, -jnp.inf, score)
    idx_ref[0] = picks.astype(jnp.int32)


def _samp_cmp(page_table, pool2d, q3, scw, bc, n_cmp):
    Bd, n_pages = page_table.shape
    nc = n_pages * PAGE // CMP_STRIDE
    wbd, pe, w1c, w2g = scw
    m = jnp.arange(nc)
    pool_m = ((m[:, None] // SLC_PER_CMP == jnp.arange(LANE)[None, :]) & (m[:, None] < n_cmp)).astype(BF16)
    one = pl.Buffered(1)
    cst = lambda a: pl.BlockSpec(a.shape, lambda s, pt: (0,) * a.ndim, pipeline_mode=one)
    return pl.pallas_call(
        functools.partial(_samp_cmp_kernel, n_pages=n_pages, n_cmp=n_cmp),
        grid_spec=pltpu.PrefetchScalarGridSpec(
            num_scalar_prefetch=1, grid=(Bd,),
            in_specs=[pl.BlockSpec(memory_space=pl.ANY), pl.BlockSpec((1, 8, LANE), lambda s, pt: (s, 0, 0)),
                      cst(wbd), cst(pe), cst(w1c), cst(w2g), cst(bc), cst(pool_m)],
            out_specs=[pl.BlockSpec((1, 8, LANE), lambda s, pt: (s, 0, 0)),
                       pl.BlockSpec((1, 8, LANE), lambda s, pt: (s, 0, 0))],
            scratch_shapes=[pltpu.VMEM((2, 4, HEAD_DIM, n_pages * PAGE), F32),
                            pltpu.VMEM((2, n_pages * PAGE, LANE), F32), pltpu.VMEM((2, 8, CMP_HIDDEN), F32),
                            pltpu.SemaphoreType.DMA((2,))]),
        out_shape=[jax.ShapeDtypeStruct((Bd, 8, LANE), F32), jax.ShapeDtypeStruct((Bd, 8, LANE), jnp.int32)],
        name='sample_cmp',
        compiler_params=pltpu.CompilerParams(dimension_semantics=('arbitrary',), vmem_limit_bytes=VMEM_LIMIT),
    )(page_table, pool2d, q3, wbd, pe, w1c, w2g, bc, pool_m)


def _samp_cmp_weights(cw):
    w1cat, pe, w2a, w2b = cw
    w = w1cat.reshape(2, CMP_STRIDE, HEAD_DIM, 2 * CMP_HIDDEN)
    z = jnp.zeros_like(w)
    wbd = jnp.concatenate([jnp.concatenate([w, z], axis=-1), jnp.concatenate([z, w], axis=-1)], axis=2)
    return wbd, pe, w1cat, jnp.stack([w2a, w2b], axis=1)


def _t5_bias_rows(rel_bias, rel, valid):
    b = rel_bias.astype(F32)[_t5_bucket(rel)]
    return jnp.where(valid[None, :], b.T, NEG)


def _bucket_bias(rel, tbl_t):
    max_exact = N_BUCKETS // 2
    nf = jnp.maximum(rel, 1).astype(F32)
    large = max_exact + (jnp.log(nf / max_exact) / math.log(MAX_DISTANCE / max_exact)
                         * (N_BUCKETS - max_exact)).astype(jnp.int32)
    bucket = jnp.where(rel < max_exact, rel, jnp.minimum(large, N_BUCKETS - 1))
    bias = jnp.zeros(rel.shape, F32)
    for b in range(N_BUCKETS):
        bias = jnp.where(bucket == b, tbl_t[:, b:b + 1], bias)
    return bias


def _samp_sw_kernel(pt_ref, idx_ref, pool_hbm, q_ref, knew_ref, win_ref, wnew_ref, wcol_ref, oc_ref, gcol_ref,
                    tblt_ref, bw_ref, oa_ref, nwin_ref, kvbuf, sem, *, past_len):
    s = pl.program_id(0)
    ns = pl.num_programs(0)
    npb = past_len // SLC_BLOCK
    bpp = PAGE // SLC_BLOCK
    nk = N_SLC * PAGE

    def block_copy(page, slot, g, k, kv):
        return pltpu.make_async_copy(pool_hbm.at[page, kv * N_KV_A + g],
                                     kvbuf.at[slot, kv * N_KV_A + g, :, pl.ds(k * PAGE, PAGE)], sem.at[slot])

    def start(smp, slot):
        for g in range(N_KV_A):
            for k in range(N_SLC):
                j = jnp.minimum(idx_ref[(smp * N_KV_A + g) * N_SLC + k], npb - 1)
                page = pt_ref[smp, j // bpp]
                for kv in range(2):
                    block_copy(page, slot, g, k, kv).start()

    @pl.when(s == 0)
    def _():
        start(0, 0)

    @pl.when(s + 1 < ns)
    def _():
        start(s + 1, (s + 1) % 2)

    slot = s % 2
    q8 = q_ref[0]
    q32 = q8.astype(F32)
    qg = [q8[:, g * HEAD_DIM:(g + 1) * HEAD_DIM] for g in range(N_KV_A)]
    tbl_t = tblt_ref[...]
    row = lax.broadcasted_iota(jnp.int32, (8, LANE), 0)
    lane = lax.broadcasted_iota(jnp.int32, (8, LANE), 1)
    grp0 = row[:, 0:1] < HPG

    def new_token(kv_row):
        kn = kv_row[:, 0:LANE].astype(BF16).astype(F32)
        vn = kv_row[:, LANE:].astype(BF16).astype(F32)
        return jnp.sum(q32 * kn, axis=1, keepdims=True) + tbl_t[:, 0:1], vn

    def by_group(a0, a1):
        return jnp.concatenate([jnp.where(grp0, a0, 0.0), jnp.where(grp0, 0.0, a1)], axis=1)

    w = win_ref[0]
    wl = w.shape[-1]
    s_new, v_new = new_token(wnew_ref[0])
    sw = jnp.where(grp0, _dot(qg[0], w[0].astype(BF16)), _dot(qg[1], w[1].astype(BF16))) + bw_ref[...]
    p, pn, inv = _softmax_with_new(sw, s_new)
    pb = p.astype(BF16)
    o_w = (by_group(_dot_t(pb, w[2].astype(BF16)), _dot_t(pb, w[3].astype(BF16))) + pn * v_new) * inv
    wcol = lax.broadcasted_iota(jnp.int32, (HEAD_DIM, wl), 1)
    for c in range(2 * N_KV_A):
        nwin_ref[0, c] = jnp.where(wcol == wl - 1, wcol_ref[0, c], pltpu.roll(w[c], wl - 1, 1))

    for g in range(N_KV_A):
        for k in range(N_SLC):
            for kv in range(2):
                block_copy(0, slot, g, k, kv).wait()
    kl = lax.broadcasted_iota(jnp.int32, (8, nk), 1)
    kslot = kl >> 7
    kin = kl & (PAGE - 1)
    rowk = lax.broadcasted_iota(jnp.int32, (8, nk), 0) < HPG
    blk = jnp.zeros((8, nk), jnp.int32)
    for k in range(N_SLC):
        j0 = idx_ref[(s * N_KV_A) * N_SLC + k]
        j1 = idx_ref[(s * N_KV_A + 1) * N_SLC + k]
        blk = jnp.where(kslot == k, jnp.where(rowk, j0, j1), blk)
    rel = past_len - ((blk // bpp) * PAGE + kin)
    ok = (kin // SLC_BLOCK) == jnp.where(blk < npb, blk % bpp, -1)
    ss = jnp.where(rowk, _dot(qg[0], kvbuf[slot, 0].astype(BF16)), _dot(qg[1], kvbuf[slot, 1].astype(BF16)))
    ss = jnp.where(ok, ss + _bucket_bias(jnp.maximum(rel, 0), tbl_t), NEG)
    s_new, v_new = new_token(knew_ref[0])
    p, pn, inv = _softmax_with_new(ss, s_new)
    pb = p.astype(BF16)
    o_s = (by_group(_dot_t(pb, kvbuf[slot, 2].astype(BF16)), _dot_t(pb, kvbuf[slot, 3].astype(BF16)))
           + pn * v_new) * inv

    gc = gcol_ref[0]
    o = gc[:, 0:1] * oc_ref[0] + gc[:, 1:2] * o_s + gc[:, 2:3] * o_w
    oa_ref[0] = jnp.where((lane >= HEAD_DIM) == (row >= HPG), o, 0.0)


def _samp_sw(page_table, idx_flat, pool_t, q3, knew, win_t, wnew, wcol, o_c, gcol, tbl_t, bw, past_len):
    Bd = page_table.shape[0]
    wl = win_t.shape[-1]
    per = lambda shp: pl.BlockSpec((1,) + shp, lambda s, pt, ix: (s,) + (0,) * len(shp))
    cst = lambda a: pl.BlockSpec(a.shape, lambda s, pt, ix: (0,) * a.ndim)
    return pl.pallas_call(
        functools.partial(_samp_sw_kernel, past_len=past_len),
        grid_spec=pltpu.PrefetchScalarGridSpec(
            num_scalar_prefetch=2, grid=(Bd,),
            in_specs=[pl.BlockSpec(memory_space=pl.ANY), per((8, LANE)), per((1, 2 * LANE)),
                      per((2 * N_KV_A, HEAD_DIM, wl)), per((1, 2 * LANE)), per((2 * N_KV_A, HEAD_DIM, 1)),
                      per((8, LANE)), per((8, LANE)), cst(tbl_t), cst(bw)],
            out_specs=[per((8, LANE)), per((2 * N_KV_A, HEAD_DIM, wl))],
            scratch_shapes=[pltpu.VMEM((2, 2 * N_KV_A, HEAD_DIM, N_SLC * PAGE), F32),
                            pltpu.SemaphoreType.DMA((2,))]),
        out_shape=[jax.ShapeDtypeStruct((Bd, 8, LANE), F32), jax.ShapeDtypeStruct(win_t.shape, F32)],
        name='sample_slc_win',
        compiler_params=pltpu.CompilerParams(dimension_semantics=('arbitrary',), vmem_limit_bytes=VMEM_LIMIT),
    )(page_table, idx_flat, pool_t, q3, knew, win_t, wnew, wcol, o_c, gcol, tbl_t, bw)


def _samp_mla_kernel(pt_ref, pool_hbm, q_ref, lnew_ref, o_ref, buf, sem, *, n_pages):
    s = pl.program_id(0)
    ns = pl.num_programs(0)

    def page_copy(page, slot, p):
        return pltpu.make_async_copy(pool_hbm.at[page], buf.at[slot, :, pl.ds(p * PAGE, PAGE)], sem.at[slot])

    def start(smp, slot):
        def body(p, c):
            page_copy(pt_ref[smp, p], slot, p).start()
            return c
        lax.fori_loop(0, n_pages, body, 0)

    @pl.when(s == 0)
    def _():
        start(0, 0)

    @pl.when(s + 1 < ns)
    def _():
        start(s + 1, (s + 1) % 2)

    slot = s % 2

    def wait_page(p, c):
        page_copy(0, slot, p).wait()
        return c
    lax.fori_loop(0, n_pages, wait_page, 0)

    q8 = q_ref[0]
    ckv_t = buf[slot, 0:KV_LORA, :].astype(BF16)
    kr_t = buf[slot, KV_LORA:LATENT_DIM, :].astype(BF16)
    ln = lnew_ref[0].astype(F32)
    sc = _dot(q8[:, 0:KV_LORA], ckv_t) + _dot(q8[:, KV_LORA:LATENT_DIM], kr_t)
    s_new = jnp.sum(q8.astype(F32) * ln, axis=1, keepdims=True)
    p, pn, inv = _softmax_with_new(sc, s_new)
    o_ref[0] = (_dot_t(p.astype(BF16), ckv_t) + pn * ln[:, 0:KV_LORA]) * inv


def _samp_mla(page_table, pool2d, qm3, lnew):
    Bd, n_pages = page_table.shape
    per = lambda shp: pl.BlockSpec((1,) + shp, lambda s, pt: (s, 0, 0))
    return pl.pallas_call(
        functools.partial(_samp_mla_kernel, n_pages=n_pages),
        grid_spec=pltpu.PrefetchScalarGridSpec(
            num_scalar_prefetch=1, grid=(Bd,),
            in_specs=[pl.BlockSpec(memory_space=pl.ANY), per((8, 256)), per((1, 256))],
            out_specs=per((8, KV_LORA)),
            scratch_shapes=[pltpu.VMEM((2, LATENT_DIM, n_pages * PAGE), F32), pltpu.SemaphoreType.DMA((2,))]),
        out_shape=jax.ShapeDtypeStruct((Bd, 8, KV_LORA), F32), name='sample_mla',
        compiler_params=pltpu.CompilerParams(dimension_semantics=('arbitrary',), vmem_limit_bytes=VMEM_LIMIT),
    )(page_table, pool2d, qm3, lnew)


def _uv_kernel(o_ref, w_ref, y_ref):
    y_ref[...] = _dot(o_ref[...].astype(BF16), w_ref[0])


def _samp_uv(o_lat2d, w_uv):
    Bd = o_lat2d.shape[0]
    return pl.pallas_call(
        _uv_kernel, grid=(N_HEADS_B,),
        in_specs=[pl.BlockSpec((Bd, LANE), lambda h: (0, h)), pl.BlockSpec((1, KV_LORA, LANE), lambda h: (h, 0, 0))],
        out_specs=pl.BlockSpec((Bd, LANE), lambda h: (0, h)),
        out_shape=jax.ShapeDtypeStruct((Bd, N_HEADS_B * LANE), F32), name='sample_uv',
        compiler_params=pltpu.CompilerParams(dimension_semantics=('parallel',), vmem_limit_bytes=VMEM_LIMIT),
    )(o_lat2d, w_uv)


def _sample_mix(xs3, msm, pool_cmp, pool_slc, win_buf, pool_mla, page_table, gain, rel_bias, w, cw):
    Bd = xs3.shape[1]
    past_len = page_table.shape[1] * PAGE
    cos_s, sin_s = _rope_tables(jnp.full((Bd,), past_len, jnp.int32))
    qa_s, cmp_s, slc_s, win_s, _, gt_s, qm_s, lat_s, lat16_s = _inproj(xs3, msm[0], msm[1], gain, cos_s, sin_s, w, Bd)
    q3 = qa_s.reshape(Bd, N_HEADS_A, LANE)
    nc = past_len // CMP_STRIDE
    n_cmp = (past_len + 1 - CMP_BLOCK) // CMP_STRIDE + 1
    m = jnp.arange(nc)
    bc = _t5_bias_rows(rel_bias, past_len - (m * CMP_STRIDE + CMP_BLOCK - 1), m < n_cmp)
    fm = lambda a: a.transpose(0, 2, 3, 4, 1).reshape(a.shape[0], 2 * N_KV_A, HEAD_DIM, a.shape[1])
    o_c, idx = _samp_cmp(page_table, fm(pool_cmp), q3, _samp_cmp_weights(cw), bc, n_cmp)
    idx_flat = idx[:, :N_KV_A, :N_SLC].reshape(-1)
    wl = win_buf.shape[1]
    wi = jnp.arange(wl)
    bw = _t5_bias_rows(rel_bias, wl - wi, (wl - wi < WINDOW) & (past_len - wl + wi >= 0))
    tbl_t = jnp.pad(rel_bias.astype(F32).T, ((0, 0), (0, LANE - N_BUCKETS)))
    gcol = gt_s[0, :, :3 * N_HEADS_A].reshape(Bd, 3, N_HEADS_A).transpose(0, 2, 1)
    gcol = jnp.pad(gcol, ((0, 0), (0, 0), (0, LANE - 3)))
    oa_s, new_win = _samp_sw(page_table, idx_flat, fm(pool_slc), q3, slc_s.reshape(Bd, 1, 2 * LANE), fm(win_buf),
                             win_s.reshape(Bd, 1, 2 * LANE), win_s.reshape(Bd, 2 * N_KV_A, HEAD_DIM, 1), o_c, gcol,
                             tbl_t, bw, past_len)
    o_lat = _samp_mla(page_table, pool_mla.transpose(0, 2, 1), qm_s.reshape(Bd, N_HEADS_B, 256),
                      lat16_s.reshape(Bd, 1, 256))
    ob_s = _samp_uv(o_lat.reshape(Bd, N_HEADS_B * KV_LORA), w['w_uv'])
    new_win = new_win.reshape(Bd, 2, N_KV_A, HEAD_DIM, wl).transpose(0, 4, 1, 2, 3)
    return oa_s.reshape(1, Bd, -1), ob_s.reshape(1, Bd, -1), cmp_s, slc_s, new_win, lat_s


def kernel(x_prompt, x_sample, cache_nsa_cmp, cache_nsa_slc, cache_nsa_win, cache_mla, page_table, c_prompt, c_sample, rel_bias, w_ada, b_ada, norm_attn, norm_ffn, w_in, cmp_pe, cmp_w1, cmp_w2, q_norm, w_q_up, kv_norm, w_kv_up, out_norm_a, out_norm_b, w_out, w_router, router_bias, w_gate_e, w_up_e, w_down_e, w_gate_s, w_up_s, w_down_s, norm_final):
    B, S, D = x_prompt.shape
    Bd = x_sample.shape[0]
    l = 0
    n_mod = B + Bd
    c_all = jnp.pad(jnp.concatenate([c_prompt, c_sample], axis=0), ((0, -n_mod % 8), (0, 0)))
    mod = _adaln(c_all, w_ada[l], b_ada[l]).reshape(-1, 6, D)
    mp = [mod[:B, i][:, None, :] for i in range(6)]
    msm = [mod[B:n_mod, i][None] for i in range(6)]

    w = _inproj_weights(w_in[l], q_norm[l], w_q_up[l], kv_norm[l], w_kv_up[l])
    cw = _compress_weights(cmp_pe[l], cmp_w1[l], cmp_w2[l])
    mw = _merge_weights(out_norm_a[l], out_norm_b[l], w_out[l], norm_ffn[l], w_router[l], w_gate_s[l], w_up_s[l],
                        w_down_s[l])
    tb, lb = _bias_tables(rel_bias)

    cos, sin = _rope_tables(jnp.arange(S))
    qa, cmp32, slc32, win32, kv16, gt, qm, lat32, lat16 = _inproj(
        x_prompt, mp[0], mp[1], norm_attn[l], cos, sin, w, 256)
    nc = S // CMP_STRIDE
    n_cmp = (S - CMP_BLOCK) // CMP_STRIDE + 1
    xc = cmp32.reshape(B, nc, CMP_STRIDE, 4, HEAD_DIM).transpose(0, 3, 1, 2, 4).reshape(B, 4, nc, -1).astype(BF16)
    kcv = _compress(xc, cw, n_cmp)
    oa_p = _nsa_prompt(qa, gt, kv16, kcv, tb, lb)
    ob_p = _mla_prompt(qm, lat16, w['w_uv'])
    xs_p, f_p, sc_p = _merge(x_prompt, oa_p, ob_p, mp[2], mp[3], mp[4], mp[5], mw, 256)

    xs3 = x_sample.reshape(1, Bd, D)
    oa_s, ob_s, cmp_s, slc_s, new_win, lat_s = _sample_mix(
        xs3, msm, cache_nsa_cmp[l], cache_nsa_slc[l], cache_nsa_win[l], cache_mla[l], page_table, norm_attn[l],
        rel_bias, w, cw)
    xs_s, f_s, sc_s = _merge(xs3, oa_s, ob_s, msm[2], msm[3], msm[4], msm[5], mw, Bd)

    n_p = B * S
    f_all = jnp.concatenate([f_p.reshape(n_p, D), f_s.reshape(Bd, D)], axis=0)
    sc_all = jnp.concatenate([sc_p.reshape(n_p, N_EXPERTS), sc_s.reshape(Bd, N_EXPERTS)], axis=0)
    wts, pos, row_tok, blk_exp, nused = _route(sc_all, router_bias[l])
    yb = _moe_experts(nused, blk_exp, row_tok, f_all, w_gate_e[l], w_up_e[l], w_down_e[l])
    tile_pos = lambda p: p.reshape(-1, MOE_BLOCK, TOP_K).transpose(0, 2, 1).reshape(-1, MOE_BLOCK)
    y_p = _combine(tile_pos(pos[:n_p]), yb, xs_p, mp[5], wts[:n_p].reshape(B, S, TOP_K), norm_final)
    y_s = _combine(tile_pos(pos[n_p:]), yb, xs_s, msm[5], wts[n_p:].reshape(1, Bd, TOP_K), norm_final)

    sh6 = lambda a, b, t: a.reshape(1, b, t, 2, N_KV_A, HEAD_DIM)
    return (y_p, y_s.reshape(Bd, 1, D), sh6(cmp32, B, S), sh6(cmp_s, Bd, 1), sh6(slc32, B, S), sh6(slc_s, Bd, 1),
            sh6(win32[:, S - WINDOW:], B, WINDOW), new_win[None], lat32[None], lat_s.reshape(1, Bd, 1, LATENT_DIM))
```

```python
import functools
import math

import jax
import jax.numpy as jnp
from jax import lax
from jax.experimental import pallas as pl
from jax.experimental.pallas import tpu as pltpu

F32 = jnp.float32
BF16 = jnp.bfloat16

LANE = 128
VMEM_LIMIT = 56 * 1024 * 1024

HEAD_DIM = 64
N_HEADS_A = 8
N_KV_A = 2
HPG = N_HEADS_A // N_KV_A
CMP_BLOCK = 32
CMP_STRIDE = 16
CMP_HIDDEN = 128
SLC_BLOCK = 64
SLC_PER_CMP = SLC_BLOCK // CMP_STRIDE
N_SLC = 16
N_LOCAL_SLC = 2
WINDOW = 512
N_HEADS_B = 8
Q_LORA = 192
KV_LORA = 128
QK_NOPE = 64
QK_ROPE = 32
V_DIM = 64
LATENT_DIM = KV_LORA + QK_ROPE
ROPE_THETA = 10000.0
MLA_SCALE = (QK_NOPE + QK_ROPE) ** -0.5
N_BUCKETS = 32
MAX_DISTANCE = 128
N_EXPERTS = 256
TOP_K = 8
N_GROUPS = 8
TOP_GROUPS = 4
ROUTED_SCALE = 2.5
MOE_BLOCK = 128
EPS = 1e-6
NEG = -1e30
FORCED = 1e30

QB = 128
KT_NSA = 1024
KT_MLA = 2048
CMP_PAD = 16
LOC_W = 24


def _dot(a, b):
    return jnp.dot(a, b, preferred_element_type=F32)


def _dot_t(a, b):
    return lax.dot_general(a, b, (((1,), (1,)), ((), ())), preferred_element_type=F32)


def _lane_tiles(x):
    return [x[:, c * LANE:(c + 1) * LANE] for c in range(x.shape[1] // LANE)]


def _row_max(x):
    return jnp.max(functools.reduce(jnp.maximum, _lane_tiles(x)), axis=1, keepdims=True)


def _row_sum(x):
    return jnp.sum(functools.reduce(jnp.add, _lane_tiles(x)), axis=1, keepdims=True)


def _const_spec(shape):
    nd = len(shape)
    return pl.BlockSpec(shape, lambda *_: (0,) * nd)


def _inproj_kernel(x_ref, sh_ref, sc_ref, g_ref, cs_ref, sn_ref, wq_ref, wkv_ref, wg_ref, wqd_ref,
                   wkvd_ref, qn_ref, wqup_ref, bd_ref, plc_ref, kvn_ref,
                   qa_ref, cmp_ref, slc_ref, win_ref, kv16_ref, gt_ref, qm_ref, lat_ref, lat16_ref):
    x = x_ref[0]
    ms = jnp.mean(x * x, axis=-1, keepdims=True)
    xn = x * lax.rsqrt(ms + EPS) * g_ref[...]
    h = xn * (1.0 + sc_ref[0]) + sh_ref[0]
    hb = h.astype(BF16)
    qa_ref[0] = _dot(hb, wq_ref[...]).astype(BF16)
    kv = _dot(hb, wkv_ref[...])
    cmp_ref[0] = kv[:, 0:256]
    slc_ref[0] = kv[:, 256:512]
    win_ref[0] = kv[:, 512:768]
    kv16_ref[0] = kv.astype(BF16)
    gl = _dot(hb, wg_ref[...])
    gt_ref[0] = 1.0 / (1.0 + jnp.exp(-gl))
    qd = _dot(hb, wqd_ref[...])
    qn = qd * lax.rsqrt(jnp.sum(qd * qd, axis=-1, keepdims=True) * (1.0 / Q_LORA) + EPS) * qn_ref[...]
    qu = _dot(qn.astype(BF16), wqup_ref[...])
    cs = cs_ref[...]
    sn = sn_ref[...]
    qr = qu[:, 512:768] * cs + qu[:, 768:1024] * sn
    qm = _dot(qu[:, 0:512].astype(BF16), bd_ref[...]) + _dot(qr.astype(BF16), plc_ref[...])
    qm_ref[0] = (qm * MLA_SCALE).astype(BF16)
    kvd = _dot(hb, wkvd_ref[...])
    c = kvd[:, 0:128]
    ckv = c * lax.rsqrt(jnp.mean(c * c, axis=-1, keepdims=True) + EPS) * kvn_ref[...]
    kr = kvd[:, 128:256] * cs[:, 0:128] + kvd[:, 256:384] * sn[:, 0:128]
    lat_ref[0, :, 0:128] = ckv
    lat_ref[0, :, 128:160] = kr[:, 0:32]
    lat16_ref[0, :, 0:128] = ckv.astype(BF16)
    lat16_ref[0, :, 128:256] = kr.astype(BF16)


def _inproj_weights(w_in, q_norm, w_q_up, kv_norm, w_kv_up):
    D = w_in.shape[0]
    o1 = N_HEADS_A * HEAD_DIM
    o2 = o1 + 6 * N_KV_A * HEAD_DIM
    o3 = o2 + 3 * N_HEADS_A
    o4 = o3 + Q_LORA
    wq = w_in[:, :o1].reshape(D, N_HEADS_A, HEAD_DIM) * (HEAD_DIM ** -0.5)
    z = jnp.zeros_like(wq)
    grp = (jnp.arange(N_HEADS_A) // HPG)[None, :, None]
    wq_pad = jnp.concatenate([jnp.where(grp == 0, wq, z), jnp.where(grp == 1, wq, z)], axis=-1)
    wq_pad = wq_pad.reshape(D, N_HEADS_A * 2 * HEAD_DIM)
    wkv = w_in[:, o1:o2]
    wg = jnp.pad(w_in[:, o2:o3], ((0, 0), (0, LANE - 3 * N_HEADS_A)))
    wqd = jnp.pad(w_in[:, o3:o4], ((0, 0), (0, 256 - Q_LORA)))
    wkd = w_in[:, o4:]
    half = QK_ROPE // 2
    wc = wkd[:, :KV_LORA]
    wr = wkd[:, KV_LORA:]
    wrot = jnp.concatenate([-wr[:, half:], wr[:, :half]], axis=1)
    padr = ((0, 0), (0, LANE - QK_ROPE))
    wkvd = jnp.concatenate([wc, jnp.pad(wr, padr), jnp.pad(wrot, padr)], axis=1)
    qn = jnp.pad(q_norm, (0, 256 - Q_LORA)).reshape(1, 256)
    wu = jnp.pad(w_q_up, ((0, 256 - Q_LORA), (0, 0))).reshape(256, N_HEADS_B, QK_NOPE + QK_ROPE)
    wu_n = wu[:, :, :QK_NOPE].reshape(256, N_HEADS_B * QK_NOPE)
    wu_r = wu[:, :, QK_NOPE:]
    wu_rot = jnp.concatenate([-wu_r[:, :, half:], wu_r[:, :, :half]], axis=-1)
    wqup = jnp.concatenate([wu_n, wu_r.reshape(256, -1), wu_rot.reshape(256, -1)], axis=1)
    w_ukv = w_kv_up.reshape(KV_LORA, N_HEADS_B, QK_NOPE + V_DIM)
    w_uk = w_ukv[:, :, :QK_NOPE]
    eye = jnp.eye(N_HEADS_B, dtype=F32)
    bd = jnp.einsum('chn,hk->hnkc', w_uk, eye)
    bd = jnp.pad(bd, ((0, 0), (0, 0), (0, 0), (0, 256 - KV_LORA))).reshape(N_HEADS_B * QK_NOPE, N_HEADS_B * 256)
    plc = jnp.einsum('hk,rs->hrks', eye, jnp.eye(QK_ROPE, dtype=F32))
    plc = jnp.pad(plc, ((0, 0), (0, 0), (0, 0), (KV_LORA, 256 - KV_LORA - QK_ROPE)))
    plc = plc.reshape(N_HEADS_B * QK_ROPE, N_HEADS_B * 256)
    w_uv = jnp.pad(w_ukv[:, :, QK_NOPE:].transpose(1, 0, 2), ((0, 0), (0, 0), (0, LANE - V_DIM)))
    bf = lambda a: a.astype(BF16)
    return dict(wq=bf(wq_pad), wkv=bf(wkv), wg=bf(wg), wqd=bf(wqd), wkvd=bf(wkvd), qn=qn, wqup=bf(wqup),
                bd=bf(bd), plc=bf(plc), kvn=kv_norm.reshape(1, KV_LORA), w_uv=bf(w_uv))


def _rope_tables(pos):
    half = QK_ROPE // 2
    inv = ROPE_THETA ** (-jnp.arange(half, dtype=F32) / half)
    ang = pos.astype(F32)[:, None] * inv[None, :]
    cos = jnp.tile(jnp.cos(ang), (1, 2 * N_HEADS_B))
    sin = jnp.tile(jnp.sin(ang), (1, 2 * N_HEADS_B))
    return cos, sin


def _inproj(x, shift, scale, gain, cos, sin, w, tr):
    B, T, D = x.shape
    tm = shift.shape[1]
    mod_spec = pl.BlockSpec((1, tr if tm > 1 else 1, D), (lambda b, t: (b, t, 0)) if tm > 1 else (lambda b, t: (b, 0, 0)))
    row = lambda n: pl.BlockSpec((1, tr, n), lambda b, t: (b, t, 0))
    tab = pl.BlockSpec((tr, 256), lambda b, t: (t, 0))
    wnames = ['wq', 'wkv', 'wg', 'wqd', 'wkvd', 'qn', 'wqup', 'bd', 'plc', 'kvn']
    out_shape = [
        jax.ShapeDtypeStruct((B, T, 1024), BF16), jax.ShapeDtypeStruct((B, T, 256), F32),
        jax.ShapeDtypeStruct((B, T, 256), F32), jax.ShapeDtypeStruct((B, T, 256), F32),
        jax.ShapeDtypeStruct((B, T, 768), BF16), jax.ShapeDtypeStruct((B, T, LANE), F32),
        jax.ShapeDtypeStruct((B, T, 2048), BF16), jax.ShapeDtypeStruct((B, T, LATENT_DIM), F32),
        jax.ShapeDtypeStruct((B, T, 256), BF16)]
    in_specs = [row(D), mod_spec, mod_spec, _const_spec((1, D)), tab, tab]
    in_specs += [_const_spec(w[n].shape) for n in wnames[:5]]
    in_specs += [_const_spec(w['qn'].shape)] + [_const_spec(w[n].shape) for n in wnames[6:9]]
    in_specs += [_const_spec(w['kvn'].shape)]
    return pl.pallas_call(
        _inproj_kernel, grid=(B, T // tr), in_specs=in_specs,
        out_specs=[row(s.shape[-1]) for s in out_shape], out_shape=out_shape, name='inproj',
        compiler_params=pltpu.CompilerParams(dimension_semantics=('parallel', 'parallel'),
                                             vmem_limit_bytes=VMEM_LIMIT),
    )(x, shift, scale, gain.reshape(1, D), cos, sin, *[w[n] for n in wnames])


def _compress_kernel(x0_ref, x1_ref, w1_ref, pe_ref, w2a_ref, w2b_ref, o_ref, *, n_cmp):
    w1 = w1_ref[0]
    pp = _dot(pe_ref[0], w1)
    peh = pp[0:1, 0:CMP_HIDDEN] + pp[1:2, CMP_HIDDEN:]
    nc = x0_ref.shape[2]
    out = jnp.zeros((nc, LANE), F32)
    for x_ref, w2_ref in ((x0_ref, w2a_ref), (x1_ref, w2b_ref)):
        ab = _dot(x_ref[0, 0], w1)
        hid = ab[:, 0:CMP_HIDDEN] + pltpu.roll(ab[:, CMP_HIDDEN:], nc - 1, 0) + peh
        act = hid * (1.0 / (1.0 + jnp.exp(-hid)))
        out = out + _dot(act.astype(BF16), w2_ref[0])
    rows = lax.broadcasted_iota(jnp.int32, (nc, LANE), 0)
    out = jnp.where(rows < n_cmp, out, 0.0)
    o_ref[0, 0, 0:CMP_PAD] = jnp.zeros((CMP_PAD, LANE), F32)
    o_ref[0, 0, CMP_PAD:CMP_PAD + nc] = out
    o_ref[0, 0, CMP_PAD + nc:] = jnp.zeros((o_ref.shape[2] - CMP_PAD - nc, LANE), F32)


def _compress_weights(cmp_pe, cmp_w1, cmp_w2):
    kin = CMP_STRIDE * HEAD_DIM
    w1 = cmp_w1.reshape(2, 2, kin, CMP_HIDDEN)
    w1cat = jnp.concatenate([w1[:, 0], w1[:, 1]], axis=-1).astype(BF16)
    pe = jnp.pad(cmp_pe.reshape(2, 2, kin), ((0, 0), (0, 6), (0, 0))).astype(BF16)
    w2a = jnp.pad(cmp_w2, ((0, 0), (0, 0), (0, HEAD_DIM))).astype(BF16)
    w2b = jnp.pad(cmp_w2, ((0, 0), (0, 0), (HEAD_DIM, 0))).astype(BF16)
    return w1cat, pe, w2a, w2b


def _compress(xc, cw, n_cmp):
    B, _, nc, kin = xc.shape
    w1cat, pe, w2a, w2b = cw
    return pl.pallas_call(
        functools.partial(_compress_kernel, n_cmp=n_cmp), grid=(B, 2),
        in_specs=[pl.BlockSpec((1, 1, nc, kin), lambda b, j: (b, 2 * j, 0, 0)),
                  pl.BlockSpec((1, 1, nc, kin), lambda b, j: (b, 2 * j + 1, 0, 0)),
                  pl.BlockSpec((1, kin, 2 * CMP_HIDDEN), lambda b, j: (j, 0, 0)),
                  pl.BlockSpec((1, 8, kin), lambda b, j: (j, 0, 0)),
                  pl.BlockSpec((1, CMP_HIDDEN, LANE), lambda b, j: (j, 0, 0)),
                  pl.BlockSpec((1, CMP_HIDDEN, LANE), lambda b, j: (j, 0, 0))],
        out_specs=pl.BlockSpec((1, 1, nc + LANE, LANE), lambda b, j: (b, j, 0, 0)),
        out_shape=jax.ShapeDtypeStruct((B, 2, nc + LANE, LANE), F32), name='compress',
        compiler_params=pltpu.CompilerParams(dimension_semantics=('parallel', 'parallel'),
                                             vmem_limit_bytes=VMEM_LIMIT),
    )(xc, xc, w1cat, pe, w2a, w2b)


def _t5_bucket(rel):
    max_exact = N_BUCKETS // 2
    n = jnp.maximum(rel, 0)
    nf = jnp.maximum(n, 1).astype(F32)
    large = max_exact + (jnp.log(nf / max_exact) / math.log(MAX_DISTANCE / max_exact)
                         * (N_BUCKETS - max_exact)).astype(jnp.int32)
    large = jnp.minimum(large, N_BUCKETS - 1)
    return jnp.where(n < max_exact, n, large)


def _bias_tables(rel_bias):
    tbl = rel_bias.astype(F32)
    const = tbl[N_BUCKETS - 1]
    i = jnp.arange(QB)[:, None]
    j = jnp.arange(LANE)[None, :]

    def tab(rel):
        b = tbl[_t5_bucket(rel)]
        return jnp.moveaxis(b, -1, 0) - const[:, None, None]

    t0 = jnp.where((i - j >= 0)[None], tab(i - j), NEG)
    t1 = tab(QB + i - j)
    zero = jnp.zeros_like(t1)
    t4 = jnp.broadcast_to(jnp.where(j > i, 0.0, NEG)[None], t1.shape)
    tb = jnp.stack([t0, t1, zero, jnp.full_like(t1, NEG), t4]).reshape(5, N_HEADS_A * QB, LANE)
    rel_l = i - CMP_STRIDE * (j - CMP_PAD) - (CMP_BLOCK - 1)
    lb = jnp.where(((j < LOC_W) & (rel_l >= 0))[None], tab(rel_l), NEG).reshape(N_HEADS_A * QB, LANE)
    return tb, lb


def _hilo_dot(x, m):
    hi = x.astype(BF16)
    lo = (x - hi.astype(F32)).astype(BF16)
    return _dot(hi, m) + _dot(lo, m)


def _nsa_kernel(q_ref, gt_ref, ks_ref, vs_ref, kw_ref, vw_ref, kc_ref, vc_ref, tb_ref, lb_ref, pf_ref, e_ref,
                o_ref, m_scr, l_scr, acc_scr, *, n_cmp):
    qb = pl.program_id(1)
    nh = N_HEADS_A
    rows = nh * QB
    q = q_ref[0]
    q8 = jnp.concatenate([q[:, h * LANE:(h + 1) * LANE] for h in range(nh)], axis=0)
    nc = pf_ref.shape[0]

    kcf = kc_ref[0, 0, 0:nc].astype(BF16)
    vcf = vc_ref[0, 0, 0:nc].astype(BF16)
    l0 = pl.multiple_of(qb * (QB // CMP_STRIDE), 8)
    kcl = kc_ref[0, 0, pl.ds(l0, LANE)].astype(BF16)
    vcl = vc_ref[0, 0, pl.ds(l0, LANE)].astype(BF16)
    colf = lax.broadcasted_iota(jnp.int32, (1, nc), 1)
    far_ok = jnp.where(colf >= CMP_PAD, jnp.where(colf < l0, 0.0, NEG), NEG)
    coll = lax.broadcasted_iota(jnp.int32, (1, LANE), 1) + (l0 - CMP_PAD)
    loc_ok = jnp.where(coll >= 0, jnp.where(coll < n_cmp, 0.0, NEG), NEG)
    s_far = _dot_t(q8, kcf) + far_ok
    s_loc = _dot_t(q8, kcl) + lb_ref[...] + loc_ok
    mrow = jnp.maximum(jnp.max(s_far, axis=1, keepdims=True), jnp.max(s_loc, axis=1, keepdims=True))
    p_far = jnp.exp(s_far - mrow)
    p_loc = jnp.exp(s_loc - mrow)
    lsum = jnp.sum(p_far, axis=1, keepdims=True) + jnp.sum(p_loc, axis=1, keepdims=True)
    inv = jnp.where(mrow > 0.5 * NEG, 1.0 / lsum, 0.0)
    p_far = p_far * inv
    p_loc = p_loc * inv
    o_c = _dot(p_far.astype(BF16), vcf) + _dot(p_loc.astype(BF16), vcl)

    r_i = lax.broadcasted_iota(jnp.int32, (LANE, LANE), 0)
    c_i = lax.broadcasted_iota(jnp.int32, (LANE, LANE), 1)
    pool_loc = jnp.where(r_i < LOC_W,
                         jnp.where(c_i == (r_i >> 2) + (2 * qb - CMP_PAD // SLC_PER_CMP), 1.0, 0.0),
                         0.0).astype(BF16)
    tq = 2 * qb + jnp.where(r_i >= SLC_BLOCK, 1, 0)
    dist = tq - c_i
    c_f = c_i.astype(F32)
    sel = []
    for g in range(N_KV_A):
        pgf = p_far[(g * HPG) * QB:(g * HPG + 1) * QB]
        pgl = p_loc[(g * HPG) * QB:(g * HPG + 1) * QB]
        for hh in range(1, HPG):
            pgf = pgf + p_far[(g * HPG + hh) * QB:(g * HPG + hh + 1) * QB]
            pgl = pgl + p_loc[(g * HPG + hh) * QB:(g * HPG + hh + 1) * QB]
        imp = _hilo_dot(pgf, pf_ref[...]) + _hilo_dot(pgl, pool_loc)
        score = jnp.where(dist < 0, NEG, jnp.where(dist < N_LOCAL_SLC, FORCED, jnp.where(c_i == 0, FORCED, imp)))
        chosen = jnp.zeros((QB, LANE), F32)
        for _ in range(N_SLC):
            mx = jnp.max(score, axis=1, keepdims=True)
            first = jnp.min(jnp.where(score == mx, c_f, float(LANE)), axis=1, keepdims=True)
            hit = c_f == first
            chosen = jnp.where(hit, 1.0, chosen)
            score = jnp.where(hit, -jnp.inf, score)
        sel.append(chosen.astype(BF16))

    m_scr[...] = jnp.full(m_scr.shape, -jnp.inf, F32)
    l_scr[...] = jnp.zeros(l_scr.shape, F32)
    acc_scr[...] = jnp.zeros(acc_scr.shape, F32)
    KT = KT_NSA
    nsub = KT // LANE

    def slc_tile(kt, carry):
        k0 = pl.multiple_of(kt * KT, KT)
        s = _dot_t(q8, ks_ref[0, pl.ds(k0, KT), :])
        bias = []
        for c in range(nsub):
            d = qb - (kt * nsub + c)
            bias.append(tb_ref[jnp.where(d < 0, 3, jnp.minimum(d, 2))])
        e_t = e_ref[:, pl.ds(k0, KT)]
        madd = [(_dot(sel[g], e_t) - 1.0) * (-NEG) for g in range(N_KV_A)]
        madd = jnp.concatenate([madd[g] for g in range(N_KV_A) for _ in range(HPG)], axis=0)
        s = s + jnp.concatenate(bias, axis=1) + madd
        m_old = m_scr[...]
        m_new = jnp.maximum(m_old, jnp.max(s, axis=1, keepdims=True))
        alpha = jnp.exp(m_old - m_new)
        p = jnp.exp(s - m_new)
        l_scr[...] = alpha * l_scr[...] + jnp.sum(p, axis=1, keepdims=True)
        acc_scr[...] = alpha * acc_scr[...] + _dot(p.astype(BF16), vs_ref[0, pl.ds(k0, KT), :])
        m_scr[...] = m_new
        return carry

    lax.fori_loop(0, qb // nsub + 1, slc_tile, 0)
    o_s = acc_scr[...] * (1.0 / l_scr[...])

    nwin = WINDOW // QB + 1
    w0 = jnp.maximum(qb - (nwin - 1), 0)
    k0 = pl.multiple_of(w0 * QB, QB)
    s = _dot_t(q8, kw_ref[0, pl.ds(k0, nwin * QB), :])
    bias = []
    for c in range(nwin):
        d = qb - (w0 + c)
        bias.append(tb_ref[jnp.where(d < 0, 3, jnp.where(d >= nwin - 1, 4, jnp.minimum(d, 2)))])
    s = s + jnp.concatenate(bias, axis=1)
    p = jnp.exp(s - jnp.max(s, axis=1, keepdims=True))
    o_w = _dot(p.astype(BF16), vw_ref[0, pl.ds(k0, nwin * QB), :]) * (1.0 / jnp.sum(p, axis=1, keepdims=True))

    gt = gt_ref[0]
    for h in range(nh):
        r = slice(h * QB, (h + 1) * QB)
        comb = (gt[:, h:h + 1] * o_c[r] + gt[:, nh + h:nh + h + 1] * o_s[r]
                + gt[:, 2 * nh + h:2 * nh + h + 1] * o_w[r])
        keep = (c_i >= HEAD_DIM) if h // HPG else (c_i < HEAD_DIM)
        o_ref[0, :, h * LANE:(h + 1) * LANE] = jnp.where(keep, comb, 0.0)


def _nsa_prompt(qa, gates, kv16, kc, tb, lb):
    B, S, _ = qa.shape
    nc = S // CMP_STRIDE
    n_cmp = (S - CMP_BLOCK) // CMP_STRIDE + 1
    m = jnp.arange(nc)
    pool_far = ((m[:, None] // SLC_PER_CMP - CMP_PAD // SLC_PER_CMP == jnp.arange(LANE)[None, :])
                & (m[:, None] >= CMP_PAD)).astype(BF16)
    expand = (jnp.arange(S)[None, :] // SLC_BLOCK == jnp.arange(LANE)[:, None]).astype(BF16)
    rows = N_HEADS_A * QB
    one = pl.Buffered(1)
    kvs = lambda c: pl.BlockSpec((1, S, LANE), lambda b, t: (b, 0, c), pipeline_mode=one)
    cspec = pl.BlockSpec((1, 1, nc + LANE, LANE), lambda b, t: (b, 0, 0, 0), pipeline_mode=one)
    vspec = pl.BlockSpec((1, 1, nc + LANE, LANE), lambda b, t: (b, 1, 0, 0), pipeline_mode=one)
    cst = lambda shape: pl.BlockSpec(shape, lambda b, t: (0,) * len(shape), pipeline_mode=one)
    return pl.pallas_call(
        functools.partial(_nsa_kernel, n_cmp=n_cmp), grid=(B, S // QB),
        in_specs=[pl.BlockSpec((1, QB, 1024), lambda b, t: (b, t, 0)),
                  pl.BlockSpec((1, QB, LANE), lambda b, t: (b, t, 0)),
                  kvs(2), kvs(3), kvs(4), kvs(5), cspec, vspec,
                  cst(tb.shape), cst(lb.shape), cst(pool_far.shape), cst(expand.shape)],
        out_specs=pl.BlockSpec((1, QB, 1024), lambda b, t: (b, t, 0)),
        out_shape=jax.ShapeDtypeStruct((B, S, 1024), F32),
        scratch_shapes=[pltpu.VMEM((rows, 1), F32), pltpu.VMEM((rows, 1), F32), pltpu.VMEM((rows, LANE), F32)],
        name='nsa_prompt',
        compiler_params=pltpu.CompilerParams(dimension_semantics=('parallel', 'arbitrary'),
                                             vmem_limit_bytes=VMEM_LIMIT),
    )(qa, gates, kv16, kv16, kv16, kv16, kc, kc, tb, lb, pool_far, expand)


def _mla_kernel(q_ref, lat_ref, wuv_ref, o_ref, m_scr, l_scr, acc_scr):
    qb = pl.program_id(1)
    nh = N_HEADS_B
    rows = nh * QB
    q = q_ref[0]
    q8 = jnp.concatenate([q[:, h * 256:(h + 1) * 256] for h in range(nh)], axis=0)
    m_scr[...] = jnp.full(m_scr.shape, -jnp.inf, F32)
    l_scr[...] = jnp.zeros(l_scr.shape, F32)
    acc_scr[...] = jnp.zeros(acc_scr.shape, F32)
    KT = KT_MLA
    nsub = KT // QB

    def tile(kt, masked):
        k0 = pl.multiple_of(kt * KT, KT)
        lat = lat_ref[0, pl.ds(k0, KT), :]
        s = _dot_t(q8, lat)
        if masked:
            col = lax.broadcasted_iota(jnp.int32, (rows, KT), 1)
            row = lax.broadcasted_iota(jnp.int32, (rows, KT), 0) & (QB - 1)
            s = jnp.where(col - row <= qb * QB - kt * KT, s, NEG)
        m_old = m_scr[...]
        m_new = jnp.maximum(m_old, _row_max(s))
        alpha = jnp.exp(m_old - m_new)
        p = jnp.exp(s - m_new)
        l_scr[...] = alpha * l_scr[...] + _row_sum(p)
        acc_scr[...] = alpha * acc_scr[...] + _dot(p.astype(BF16), lat[:, 0:KV_LORA])
        m_scr[...] = m_new

    def full_tile(kt, carry):
        tile(kt, False)
        return carry

    lax.fori_loop(0, qb // nsub, full_tile, 0)
    tile(qb // nsub, True)
    o_lat = (acc_scr[...] * (1.0 / l_scr[...])).astype(BF16)
    for h in range(nh):
        o_ref[0, :, h * LANE:(h + 1) * LANE] = _dot(o_lat[h * QB:(h + 1) * QB], wuv_ref[h])


def _mla_prompt(qm, lat16, w_uv):
    B, S, _ = qm.shape
    rows = N_HEADS_B * QB
    one = pl.Buffered(1)
    return pl.pallas_call(
        _mla_kernel, grid=(B, S // QB),
        in_specs=[pl.BlockSpec((1, QB, 2048), lambda b, t: (b, t, 0)),
                  pl.BlockSpec((1, S, 256), lambda b, t: (b, 0, 0), pipeline_mode=one),
                  pl.BlockSpec(w_uv.shape, lambda b, t: (0, 0, 0), pipeline_mode=one)],
        out_specs=pl.BlockSpec((1, QB, 1024), lambda b, t: (b, t, 0)),
        out_shape=jax.ShapeDtypeStruct((B, S, 1024), F32),
        scratch_shapes=[pltpu.VMEM((rows, 1), F32), pltpu.VMEM((rows, 1), F32), pltpu.VMEM((rows, KV_LORA), F32)],
        name='mla_prompt',
        compiler_params=pltpu.CompilerParams(dimension_semantics=('parallel', 'arbitrary'),
                                             vmem_limit_bytes=VMEM_LIMIT),
    )(qm, lat16, w_uv)


def _adaln_kernel(c_ref, w_ref, b_ref, o_ref):
    c = c_ref[...]
    a = (c * (1.0 / (1.0 + jnp.exp(-c)))).astype(BF16)
    o_ref[...] = _dot(a, w_ref[...].astype(BF16)) + b_ref[...]


def _adaln(c, w_ada, b_ada, tn=512):
    R_, D = c.shape
    N = w_ada.shape[1]
    return pl.pallas_call(
        _adaln_kernel, grid=(N // tn,),
        in_specs=[pl.BlockSpec((R_, D), lambda j: (0, 0)), pl.BlockSpec((D, tn), lambda j: (0, j)),
                  pl.BlockSpec((1, tn), lambda j: (0, j))],
        out_specs=pl.BlockSpec((R_, tn), lambda j: (0, j)),
        out_shape=jax.ShapeDtypeStruct((R_, N), F32), name='adaln',
        compiler_params=pltpu.CompilerParams(dimension_semantics=('parallel',), vmem_limit_bytes=VMEM_LIMIT),
    )(c, w_ada, b_ada.reshape(1, N))


def _merge_kernel(x_ref, oa_ref, ob_ref, ga_ref, shf_ref, scf_ref, gf_ref, na_ref, nb_ref, nffn_ref,
                  wa_ref, wb_ref, wr_ref, wgs_ref, wus_ref, wds_ref, xs_ref, f_ref, sc_ref):
    n_real = N_HEADS_A * HEAD_DIM
    oa = oa_ref[0]
    ob = ob_ref[0]
    na = oa * lax.rsqrt(jnp.sum(oa * oa, axis=-1, keepdims=True) * (1.0 / n_real) + EPS) * na_ref[...]
    nb = ob * lax.rsqrt(jnp.sum(ob * ob, axis=-1, keepdims=True) * (1.0 / n_real) + EPS) * nb_ref[...]
    mix = _dot(na.astype(BF16), wa_ref[...]) + _dot(nb.astype(BF16), wb_ref[...])
    x1 = x_ref[0] + ga_ref[0] * mix
    f = x1 * lax.rsqrt(jnp.mean(x1 * x1, axis=-1, keepdims=True) + EPS) * nffn_ref[...]
    f = f * (1.0 + scf_ref[0]) + shf_ref[0]
    f_ref[0] = f
    fb = f.astype(BF16)
    sc_ref[0] = 1.0 / (1.0 + jnp.exp(-_dot(fb, wr_ref[...])))
    g = _dot(fb, wgs_ref[...])
    u = _dot(fb, wus_ref[...])
    hsh = (g * (1.0 / (1.0 + jnp.exp(-g))) * u).astype(BF16)
    xs_ref[0] = x1 + gf_ref[0] * _dot(hsh, wds_ref[...])


def _merge_weights(out_norm_a, out_norm_b, w_out, norm_ffn, w_router, w_gate_s, w_up_s, w_down_s):
    D = w_out.shape[1]
    na = out_norm_a.reshape(N_HEADS_A, 1, HEAD_DIM)
    grp = (jnp.arange(N_HEADS_A) // HPG)[:, None, None]
    half = jnp.arange(2)[None, :, None]
    na_pad = jnp.where(grp == half, na, 0.0).reshape(1, -1)
    nb_pad = jnp.pad(out_norm_b.reshape(N_HEADS_B, V_DIM), ((0, 0), (0, LANE - V_DIM))).reshape(1, -1)
    wa = w_out[:N_HEADS_A * HEAD_DIM].reshape(N_HEADS_A, 1, HEAD_DIM, D)
    wa_pad = jnp.where((grp == half)[..., None], wa, 0.0).reshape(-1, D)
    wb = w_out[N_HEADS_A * HEAD_DIM:].reshape(N_HEADS_B, V_DIM, D)
    wb_pad = jnp.pad(wb, ((0, 0), (0, LANE - V_DIM), (0, 0))).reshape(-1, D)
    bf = lambda a: a.astype(BF16)
    return [na_pad, nb_pad, norm_ffn.reshape(1, D), bf(wa_pad), bf(wb_pad), bf(w_router), bf(w_gate_s),
            bf(w_up_s), bf(w_down_s)]


def _mod_spec(a, tr):
    if a.shape[1] > 1:
        return pl.BlockSpec((1, tr, a.shape[2]), lambda b, t: (b, t, 0))
    return pl.BlockSpec((1, 1, a.shape[2]), lambda b, t: (b, 0, 0))


def _merge(x, oa, ob, gate_a, shift_f, scale_f, gate_f, mw, tr):
    B, T, D = x.shape
    row = lambda n: pl.BlockSpec((1, tr, n), lambda b, t: (b, t, 0))
    out_shape = [jax.ShapeDtypeStruct((B, T, D), F32), jax.ShapeDtypeStruct((B, T, D), F32),
                 jax.ShapeDtypeStruct((B, T, N_EXPERTS), F32)]
    return pl.pallas_call(
        _merge_kernel, grid=(B, T // tr),
        in_specs=[row(D), row(1024), row(1024)] + [_mod_spec(a, tr) for a in (gate_a, shift_f, scale_f, gate_f)]
        + [_const_spec(a.shape) for a in mw],
        out_specs=[row(D), row(D), row(N_EXPERTS)], out_shape=out_shape, name='merge',
        compiler_params=pltpu.CompilerParams(dimension_semantics=('parallel', 'parallel'),
                                             vmem_limit_bytes=VMEM_LIMIT),
    )(x, oa, ob, gate_a, shift_f, scale_f, gate_f, *mw)


def _gather_rows(idx_ref, idx_row, src_hbm, dst, sem):
    def body(r, c):
        pltpu.make_async_copy(src_hbm.at[pl.ds(idx_ref[idx_row, r], 1)], dst.at[pl.ds(r, 1)], sem).start()
        return c
    lax.fori_loop(0, MOE_BLOCK, body, 0, unroll=8)


MOE_CHUNK = 256


def _moe_kernel(nused_ref, bexp_ref, rtok_hbm, f_hbm, wg_ref, wu_ref, wd_ref, y_ref, xbuf, wgb, wub, wdb, rtok, sem,
                isem):
    i = pl.program_id(0)
    nused = nused_ref[0]
    nchunk = rtok_hbm.shape[0] // MOE_CHUNK

    def ids_copy(c):
        return pltpu.make_async_copy(rtok_hbm.at[pl.ds(c * MOE_CHUNK, MOE_CHUNK)], rtok.at[c % 2], isem.at[c % 2])

    def gather(blk):
        ids = rtok.at[(blk // MOE_CHUNK) % 2]
        row = blk % MOE_CHUNK
        for r in range(MOE_BLOCK):
            pltpu.make_async_copy(f_hbm.at[pl.ds(ids[row, r], 1)], xbuf.at[blk % 2, pl.ds(r, 1)],
                                  sem.at[blk % 2]).start()

    def wait_gather(blk):
        pltpu.make_async_copy(f_hbm.at[pl.ds(0, MOE_BLOCK)], xbuf.at[blk % 2], sem.at[blk % 2]).wait()

    @pl.when(i == 0)
    def _():
        ids_copy(0).start()
        ids_copy(0).wait()
        if nchunk > 1:
            ids_copy(1).start()
        gather(0)

    @pl.when(((i + 1) % MOE_CHUNK == 0) & (i + 1 < pl.num_programs(0)))
    def _():
        c = (i + 1) // MOE_CHUNK
        ids_copy(c).wait()

        @pl.when(c + 1 < nchunk)
        def _():
            ids_copy(c + 1).start()

    @pl.when(i < nused)
    def _():
        wait_gather(i)

        @pl.when((i == 0) | (bexp_ref[i] != bexp_ref[jnp.maximum(i - 1, 0)]))
        def _():
            wgb[...] = wg_ref[0].astype(BF16)
            wub[...] = wu_ref[0].astype(BF16)
            wdb[...] = wd_ref[0].astype(BF16)

        gather(i + 1)
        x = xbuf[i % 2].astype(BF16)
        g = _dot(x, wgb[...])
        u = _dot(x, wub[...])
        h = (g * (1.0 / (1.0 + jnp.exp(-g))) * u).astype(BF16)
        y_ref[...] = _dot(h, wdb[...])

    @pl.when(i >= nused)
    def _():
        @pl.when(i == nused)
        def _():
            wait_gather(i)
        y_ref[...] = jnp.zeros(y_ref.shape, F32)


def _moe_experts(nused, blk_exp, row_tok, f, w_gate_e, w_up_e, w_down_e):
    n_blocks = blk_exp.shape[0]
    n_steps = n_blocks + 1
    nb_pad = -(-n_steps // MOE_CHUNK) * MOE_CHUNK
    row_tok = jnp.pad(row_tok, (0, (nb_pad - n_blocks) * MOE_BLOCK)).reshape(nb_pad, MOE_BLOCK)
    blk_exp = jnp.pad(blk_exp, (0, 1), mode='edge')
    D = f.shape[1]
    de = w_gate_e.shape[2]
    wspec = lambda shp: pl.BlockSpec((1,) + shp, lambda i, nu, be: (be[i], 0, 0))
    return pl.pallas_call(
        _moe_kernel,
        grid_spec=pltpu.PrefetchScalarGridSpec(
            num_scalar_prefetch=2, grid=(n_steps,),
            in_specs=[pl.BlockSpec(memory_space=pl.ANY), pl.BlockSpec(memory_space=pl.ANY),
                      wspec((D, de)), wspec((D, de)), wspec((de, D))],
            out_specs=pl.BlockSpec((MOE_BLOCK, D), lambda i, nu, be: (i, 0)),
            scratch_shapes=[pltpu.VMEM((2, MOE_BLOCK, D), F32), pltpu.VMEM((D, de), BF16),
                            pltpu.VMEM((D, de), BF16), pltpu.VMEM((de, D), BF16),
                            pltpu.SMEM((2, MOE_CHUNK, MOE_BLOCK), jnp.int32),
                            pltpu.SemaphoreType.DMA((2,)), pltpu.SemaphoreType.DMA((2,))]),
        out_shape=jax.ShapeDtypeStruct((n_steps * MOE_BLOCK, D), F32), name='moe_experts',
        compiler_params=pltpu.CompilerParams(dimension_semantics=('arbitrary',), vmem_limit_bytes=VMEM_LIMIT),
    )(nused, blk_exp, row_tok, f, w_gate_e, w_up_e, w_down_e)


def _combine_kernel(pos_ref, yb_hbm, xs_ref, gf_ref, w_ref, nf_ref, o_ref, buf, sem):
    tile = pl.program_id(0) * pl.num_programs(1) + pl.program_id(1)
    ntile = pl.num_programs(0) * pl.num_programs(1)

    def start(t, slot):
        for k in range(TOP_K):
            _gather_rows(pos_ref, t * TOP_K + k, yb_hbm, buf.at[slot, k], sem.at[slot])

    @pl.when(tile == 0)
    def _():
        start(0, 0)

    @pl.when(tile + 1 < ntile)
    def _():
        start(tile + 1, (tile + 1) % 2)

    slot = tile % 2
    w = w_ref[0]
    routed = jnp.zeros(xs_ref.shape[1:], F32)
    for k in range(TOP_K):
        pltpu.make_async_copy(yb_hbm.at[pl.ds(0, MOE_BLOCK)], buf.at[slot, k], sem.at[slot]).wait()
    for k in range(TOP_K):
        routed = routed + w[:, k:k + 1] * buf[slot, k]
    x2 = xs_ref[0] + gf_ref[0] * routed
    o_ref[0] = x2 * lax.rsqrt(jnp.mean(x2 * x2, axis=-1, keepdims=True) + EPS) * nf_ref[...]


def _combine(pos, yb, xs, gate_f, wts, norm_final):
    B, T, D = xs.shape
    tr = MOE_BLOCK
    gspec = (pl.BlockSpec((1, tr, D), lambda b, t, p: (b, t, 0)) if gate_f.shape[1] > 1
             else pl.BlockSpec((1, 1, D), lambda b, t, p: (b, 0, 0)))
    return pl.pallas_call(
        _combine_kernel,
        grid_spec=pltpu.PrefetchScalarGridSpec(
            num_scalar_prefetch=1, grid=(B, T // tr),
            in_specs=[pl.BlockSpec(memory_space=pl.ANY), pl.BlockSpec((1, tr, D), lambda b, t, p: (b, t, 0)), gspec,
                      pl.BlockSpec((1, tr, TOP_K), lambda b, t, p: (b, t, 0)),
                      pl.BlockSpec((1, D), lambda b, t, p: (0, 0))],
            out_specs=pl.BlockSpec((1, tr, D), lambda b, t, p: (b, t, 0)),
            scratch_shapes=[pltpu.VMEM((2, TOP_K, tr, D), F32), pltpu.SemaphoreType.DMA((2,))]),
        out_shape=jax.ShapeDtypeStruct((B, T, D), F32), name='combine',
        compiler_params=pltpu.CompilerParams(dimension_semantics=('arbitrary', 'arbitrary'),
                                             vmem_limit_bytes=VMEM_LIMIT),
    )(pos, yb, xs, gate_f, wts, norm_final.reshape(1, D))


def _assign_rows_kernel(idx_ref, rank_ref, start_ref, pos_ref):
    idx = idx_ref[...]
    tr = idx.shape[0]
    start = start_ref[...]
    lane = lax.broadcasted_iota(jnp.int32, (tr, start.shape[1]), 1)
    kcol = lax.broadcasted_iota(jnp.int32, (tr, TOP_K), 1)
    base = jnp.zeros((tr, TOP_K), F32)
    for k in range(TOP_K):
        sk = jnp.sum(jnp.where(lane == idx[:, k:k + 1], start, 0.0), axis=1, keepdims=True)
        base = jnp.where(kcol == k, sk, base)
    pos_ref[...] = base.astype(jnp.int32) + rank_ref[...]


def _assign_rows(idx, rank, start):
    n_tok = idx.shape[0]
    tspec = pl.BlockSpec((MOE_BLOCK, TOP_K), lambda i: (i, 0))
    return pl.pallas_call(
        _assign_rows_kernel, grid=(n_tok // MOE_BLOCK,),
        in_specs=[tspec, tspec, pl.BlockSpec(start.shape, lambda i: (0, 0))], out_specs=tspec,
        out_shape=jax.ShapeDtypeStruct((n_tok, TOP_K), jnp.int32), name='assign_rows',
        compiler_params=pltpu.CompilerParams(dimension_semantics=('parallel',), vmem_limit_bytes=VMEM_LIMIT),
    )(idx, rank, start)


def _route(scores, router_bias):
    n_tok = scores.shape[0]
    n_asg = n_tok * TOP_K
    idx, wts, rank, counts = _route_rank(scores, router_bias.astype(F32).reshape(1, N_EXPERTS))
    padded = (counts + MOE_BLOCK - 1) // MOE_BLOCK * MOE_BLOCK
    pad_end = jnp.cumsum(padded)
    pos = _assign_rows(idx, rank, (pad_end - padded).astype(F32).reshape(1, N_EXPERTS))
    n_blocks = -(-(n_asg + N_EXPERTS * (MOE_BLOCK - 1)) // MOE_BLOCK)
    n_rows = n_blocks * MOE_BLOCK
    tok = jnp.broadcast_to(jnp.arange(n_tok, dtype=jnp.int32)[:, None], pos.shape)
    row_tok = jnp.zeros((n_rows,), jnp.int32).at[pos.reshape(-1)].set(tok.reshape(-1))
    blk_exp = jnp.minimum(jnp.searchsorted(pad_end, jnp.arange(n_blocks) * MOE_BLOCK, side='right'),
                          N_EXPERTS - 1).astype(jnp.int32)
    nused = (pad_end[-1] // MOE_BLOCK).astype(jnp.int32).reshape(1)
    return wts, pos, row_tok, blk_exp, nused


def _route_kernel(sc_ref, rb_ref, idx_ref, wts_ref, rank_ref, cnt_ref, carry):
    i = pl.program_id(0)

    @pl.when(i == 0)
    def _():
        carry[...] = jnp.zeros(carry.shape, F32)

    tr, ne = sc_ref.shape
    gsz = ne // N_GROUPS
    scores = sc_ref[...]
    biased = scores + rb_ref[...]
    lane = lax.broadcasted_iota(jnp.int32, (tr, ne), 1)
    lane_f = lane.astype(F32)
    lgrp = lane // gsz
    ninf = -jnp.inf

    def first_max(x):
        mx = jnp.max(x, axis=1, keepdims=True)
        return mx, jnp.min(jnp.where(x == mx, lane_f, float(ne)), axis=1, keepdims=True)

    gl = lax.broadcasted_iota(jnp.int32, (tr, LANE), 1)
    gscore = jnp.full((tr, LANE), ninf, F32)
    for g in range(N_GROUPS):
        xg = jnp.where(lgrp == g, biased, ninf)
        m1, f1 = first_max(xg)
        m2 = jnp.max(jnp.where(lane_f == f1, ninf, xg), axis=1, keepdims=True)
        gscore = jnp.where(gl == g, m1 + m2, gscore)
    gl_f = gl.astype(F32)
    keep = jnp.full((tr, ne), NEG, F32)
    for _ in range(TOP_GROUPS):
        mx = jnp.max(gscore, axis=1, keepdims=True)
        gf = jnp.min(jnp.where(gscore == mx, gl_f, float(LANE)), axis=1, keepdims=True)
        gscore = jnp.where(gl_f == gf, ninf, gscore)
        keep = jnp.where(lgrp.astype(F32) == gf, biased, keep)
    kcol = lax.broadcasted_iota(jnp.int32, (tr, TOP_K), 1)
    idx = jnp.zeros((tr, TOP_K), F32)
    wts = jnp.zeros((tr, TOP_K), F32)
    hot = []
    onehot = jnp.zeros((tr, ne), F32)
    for k in range(TOP_K):
        _, f = first_max(keep)
        hit = lane_f == f
        hot.append(hit)
        keep = jnp.where(hit, ninf, keep)
        onehot = jnp.where(hit, 1.0, onehot)
        idx = jnp.where(kcol == k, f, idx)
        wts = jnp.where(kcol == k, jnp.sum(jnp.where(hit, scores, 0.0), axis=1, keepdims=True), wts)
    idx_ref[...] = idx.astype(jnp.int32)
    wts_ref[...] = wts / jnp.sum(wts, axis=1, keepdims=True) * ROUTED_SCALE
    r_i = lax.broadcasted_iota(jnp.int32, (tr, tr), 0)
    c_i = lax.broadcasted_iota(jnp.int32, (tr, tr), 1)
    lower = jnp.where(c_i < r_i, 1.0, 0.0).astype(BF16)
    before = _dot(lower, onehot.astype(BF16)) + carry[0:1, :]
    rank = jnp.zeros((tr, TOP_K), F32)
    for k in range(TOP_K):
        rank = jnp.where(kcol == k, jnp.sum(jnp.where(hot[k], before, 0.0), axis=1, keepdims=True), rank)
    rank_ref[...] = rank.astype(jnp.int32)
    total = carry[0:1, :] + jnp.sum(onehot, axis=0, keepdims=True)
    carry[...] = jnp.broadcast_to(total, carry.shape)
    cnt_ref[...] = jnp.broadcast_to(total, cnt_ref.shape).astype(jnp.int32)


def _route_rank(scores, router_bias):
    n_tok, ne = scores.shape
    tr = MOE_BLOCK
    tk = lambda dt: jax.ShapeDtypeStruct((n_tok, TOP_K), dt)
    tspec = pl.BlockSpec((tr, TOP_K), lambda i: (i, 0))
    idx, wts, rank, cnt = pl.pallas_call(
        _route_kernel, grid=(n_tok // tr,),
        in_specs=[pl.BlockSpec((tr, ne), lambda i: (i, 0)), pl.BlockSpec((1, ne), lambda i: (0, 0))],
        out_specs=[tspec, tspec, tspec, pl.BlockSpec((8, ne), lambda i: (0, 0))],
        out_shape=[tk(jnp.int32), tk(F32), tk(jnp.int32), jax.ShapeDtypeStruct((8, ne), jnp.int32)],
        scratch_shapes=[pltpu.VMEM((8, ne), F32)], name='route',
        compiler_params=pltpu.CompilerParams(dimension_semantics=('arbitrary',), vmem_limit_bytes=VMEM_LIMIT),
    )(scores, router_bias)
    return idx, wts, rank, cnt[0]


PAGE = 128


def _softmax_with_new(s, s_new):
    m = jnp.maximum(jnp.max(s, axis=1, keepdims=True), s_new)
    p = jnp.exp(s - m)
    pn = jnp.exp(s_new - m)
    return p, pn, 1.0 / (jnp.sum(p, axis=1, keepdims=True) + pn)


def _samp_cmp_kernel(pt_ref, pool_hbm, q_ref, wbd_ref, pe_ref, w1c_ref, w2_ref, bc_ref, pf_ref,
                     oc_ref, idx_ref, buf_t, buf, peh_scr, sem, *, n_pages, n_cmp):
    s = pl.program_id(0)
    ns = pl.num_programs(0)
    nc = n_pages * PAGE // CMP_STRIDE

    def page_copy(page, slot, p):
        return pltpu.make_async_copy(pool_hbm.at[page], buf_t.at[slot, :, :, pl.ds(p * PAGE, PAGE)], sem.at[slot])

    def start(smp, slot):
        def body(p, c):
            page_copy(pt_ref[smp, p], slot, p).start()
            return c
        lax.fori_loop(0, n_pages, body, 0)

    @pl.when(s == 0)
    def _():
        start(0, 0)
        for j in range(2):
            pp = _dot(pe_ref[j], w1c_ref[j])
            peh_scr[j] = jnp.broadcast_to(pp[0:1, 0:CMP_HIDDEN] + pp[1:2, CMP_HIDDEN:], (8, CMP_HIDDEN))

    @pl.when(s + 1 < ns)
    def _():
        start(s + 1, (s + 1) % 2)

    slot = s % 2

    def wait_page(p, c):
        page_copy(0, slot, p).wait()
        return c
    lax.fori_loop(0, n_pages, wait_page, 0)

    tw = 4 * PAGE

    def to_rows(c, carry):
        l0 = pl.multiple_of(c * tw, tw)
        for j in range(2):
            xt = buf_t[slot, pl.ds(2 * j, 2), :, pl.ds(l0, tw)].reshape(2 * HEAD_DIM, tw)
            buf[j, pl.ds(l0, tw), :] = xt.T
        return carry
    lax.fori_loop(0, n_pages * PAGE // tw, to_rows, 0)

    rows = lax.broadcasted_iota(jnp.int32, (nc, LANE), 0)
    kvc = []
    for j in range(2):
        acc = jnp.zeros((nc, 4 * CMP_HIDDEN), F32)
        for r in range(0, CMP_STRIDE, 2):
            x = jnp.concatenate([buf[j, pl.ds(r, nc, stride=CMP_STRIDE), :],
                                 buf[j, pl.ds(r + 1, nc, stride=CMP_STRIDE), :]], axis=1).astype(BF16)
            acc = acc + _dot(x, wbd_ref[j, r // 2])
        out = jnp.zeros((nc, LANE), F32)
        for g in range(N_KV_A):
            a = acc[:, g * 2 * CMP_HIDDEN:g * 2 * CMP_HIDDEN + CMP_HIDDEN]
            bm = acc[:, g * 2 * CMP_HIDDEN + CMP_HIDDEN:(g + 1) * 2 * CMP_HIDDEN]
            hid = a + pltpu.roll(bm, nc - 1, 0) + peh_scr[j, 0:1]
            act = hid * (1.0 / (1.0 + jnp.exp(-hid)))
            out = out + _dot(act.astype(BF16), w2_ref[j, g])
        kvc.append(jnp.where(rows < n_cmp, out, 0.0).astype(BF16))
    kc, vc = kvc

    q8 = q_ref[0]
    sc = _dot_t(q8, kc) + bc_ref[...]
    p = jnp.exp(sc - jnp.max(sc, axis=1, keepdims=True))
    p = p * (1.0 / jnp.sum(p, axis=1, keepdims=True))
    oc_ref[0] = _dot(p.astype(BF16), vc)

    pg = jnp.concatenate([jnp.sum(p[0:HPG], axis=0, keepdims=True), jnp.sum(p[HPG:2 * HPG], axis=0, keepdims=True),
                          jnp.zeros((8 - N_KV_A, nc), F32)], axis=0)
    imp = _hilo_dot(pg, pf_ref[...])
    lane = lax.broadcasted_iota(jnp.int32, (8, LANE), 1).astype(F32)
    score = jnp.where(lane == 0.0, FORCED, jnp.where(lane >= float(LANE - N_LOCAL_SLC + 1), FORCED, imp))
    picks = jnp.full((8, LANE), float(LANE), F32)
    for k in range(N_SLC - 1):
        mx = jnp.max(score, axis=1, keepdims=True)
        first = jnp.min(jnp.where(score == mx, lane, float(LANE)), axis=1, keepdims=True)
        picks = jnp.where(lane == float(k), first, picks)
        score = jnp.where(lane == first, -jnp.inf, score)
    idx_ref[0] = picks.astype(jnp.int32)


def _samp_cmp(page_table, pool2d, q3, scw, bc, n_cmp):
    Bd, n_pages = page_table.shape
    nc = n_pages * PAGE // CMP_STRIDE
    wbd, pe, w1c, w2g = scw
    m = jnp.arange(nc)
    pool_m = ((m[:, None] // SLC_PER_CMP == jnp.arange(LANE)[None, :]) & (m[:, None] < n_cmp)).astype(BF16)
    one = pl.Buffered(1)
    cst = lambda a: pl.BlockSpec(a.shape, lambda s, pt: (0,) * a.ndim, pipeline_mode=one)
    return pl.pallas_call(
        functools.partial(_samp_cmp_kernel, n_pages=n_pages, n_cmp=n_cmp),
        grid_spec=pltpu.PrefetchScalarGridSpec(
            num_scalar_prefetch=1, grid=(Bd,),
            in_specs=[pl.BlockSpec(memory_space=pl.ANY), pl.BlockSpec((1, 8, LANE), lambda s, pt: (s, 0, 0)),
                      cst(wbd), cst(pe), cst(w1c), cst(w2g), cst(bc), cst(pool_m)],
            out_specs=[pl.BlockSpec((1, 8, LANE), lambda s, pt: (s, 0, 0)),
                       pl.BlockSpec((1, 8, LANE), lambda s, pt: (s, 0, 0))],
            scratch_shapes=[pltpu.VMEM((2, 4, HEAD_DIM, n_pages * PAGE), F32),
                            pltpu.VMEM((2, n_pages * PAGE, LANE), F32), pltpu.VMEM((2, 8, CMP_HIDDEN), F32),
                            pltpu.SemaphoreType.DMA((2,))]),
        out_shape=[jax.ShapeDtypeStruct((Bd, 8, LANE), F32), jax.ShapeDtypeStruct((Bd, 8, LANE), jnp.int32)],
        name='sample_cmp',
        compiler_params=pltpu.CompilerParams(dimension_semantics=('arbitrary',), vmem_limit_bytes=VMEM_LIMIT),
    )(page_table, pool2d, q3, wbd, pe, w1c, w2g, bc, pool_m)


def _samp_cmp_weights(cw):
    w1cat, pe, w2a, w2b = cw
    w = w1cat.reshape(2, CMP_STRIDE, HEAD_DIM, 2 * CMP_HIDDEN)
    z = jnp.zeros_like(w)
    wbd = jnp.concatenate([jnp.concatenate([w, z], axis=-1), jnp.concatenate([z, w], axis=-1)], axis=2)
    wbd = wbd.reshape(2, CMP_STRIDE // 2, 2 * LANE, 4 * CMP_HIDDEN)
    return wbd, pe, w1cat, jnp.stack([w2a, w2b], axis=1)


def _t5_bias_rows(rel_bias, rel, valid):
    b = rel_bias.astype(F32)[_t5_bucket(rel)]
    return jnp.where(valid[None, :], b.T, NEG)


def _bucket_bias(rel, tbl_t):
    max_exact = N_BUCKETS // 2
    nf = jnp.maximum(rel, 1).astype(F32)
    large = max_exact + (jnp.log(nf / max_exact) / math.log(MAX_DISTANCE / max_exact)
                         * (N_BUCKETS - max_exact)).astype(jnp.int32)
    bucket = jnp.where(rel < max_exact, rel, jnp.minimum(large, N_BUCKETS - 1))
    bias = jnp.zeros(rel.shape, F32)
    for b in range(N_BUCKETS):
        bias = jnp.where(bucket == b, tbl_t[:, b:b + 1], bias)
    return bias


def _samp_sw_kernel(pt_ref, idx_ref, pool_hbm, q_ref, knew_ref, win_ref, wnew_ref, wcol_ref, oc_ref, gcol_ref,
                    tblt_ref, bw_ref, oa_ref, nwin_ref, kvbuf, sem, *, past_len):
    s = pl.program_id(0)
    ns = pl.num_programs(0)
    npb = past_len // SLC_BLOCK
    bpp = PAGE // SLC_BLOCK
    nk = N_SLC * PAGE

    def block_copy(page, slot, g, k, kv):
        return pltpu.make_async_copy(pool_hbm.at[page, kv * N_KV_A + g],
                                     kvbuf.at[slot, kv * N_KV_A + g, :, pl.ds(k * PAGE, PAGE)], sem.at[slot])

    def start(smp, slot):
        for g in range(N_KV_A):
            for k in range(N_SLC):
                j = jnp.minimum(idx_ref[(smp * N_KV_A + g) * N_SLC + k], npb - 1)
                page = pt_ref[smp, j // bpp]
                for kv in range(2):
                    block_copy(page, slot, g, k, kv).start()

    @pl.when(s == 0)
    def _():
        start(0, 0)

    @pl.when(s + 1 < ns)
    def _():
        start(s + 1, (s + 1) % 2)

    slot = s % 2
    q8 = q_ref[0]
    q32 = q8.astype(F32)
    qg = [q8[:, g * HEAD_DIM:(g + 1) * HEAD_DIM] for g in range(N_KV_A)]
    tbl_t = tblt_ref[...]
    row = lax.broadcasted_iota(jnp.int32, (8, LANE), 0)
    lane = lax.broadcasted_iota(jnp.int32, (8, LANE), 1)
    grp0 = row[:, 0:1] < HPG

    def new_token(kv_row):
        kn = kv_row[:, 0:LANE].astype(BF16).astype(F32)
        vn = kv_row[:, LANE:].astype(BF16).astype(F32)
        return jnp.sum(q32 * kn, axis=1, keepdims=True) + tbl_t[:, 0:1], vn

    def by_group(a0, a1):
        return jnp.concatenate([jnp.where(grp0, a0, 0.0), jnp.where(grp0, 0.0, a1)], axis=1)

    w = win_ref[0]
    wl = w.shape[-1]
    s_new, v_new = new_token(wnew_ref[0])
    sw = jnp.where(grp0, _dot(qg[0], w[0].astype(BF16)), _dot(qg[1], w[1].astype(BF16))) + bw_ref[...]
    p, pn, inv = _softmax_with_new(sw, s_new)
    pb = p.astype(BF16)
    o_w = (by_group(_dot_t(pb, w[2].astype(BF16)), _dot_t(pb, w[3].astype(BF16))) + pn * v_new) * inv
    wcol = lax.broadcasted_iota(jnp.int32, (HEAD_DIM, wl), 1)
    for c in range(2 * N_KV_A):
        nwin_ref[0, c] = jnp.where(wcol == wl - 1, wcol_ref[0, c], pltpu.roll(w[c], wl - 1, 1))

    for g in range(N_KV_A):
        for k in range(N_SLC):
            for kv in range(2):
                block_copy(0, slot, g, k, kv).wait()
    kl = lax.broadcasted_iota(jnp.int32, (8, nk), 1)
    kslot = kl >> 7
    kin = kl & (PAGE - 1)
    rowk = lax.broadcasted_iota(jnp.int32, (8, nk), 0) < HPG
    blk = jnp.zeros((8, nk), jnp.int32)
    for k in range(N_SLC):
        j0 = idx_ref[(s * N_KV_A) * N_SLC + k]
        j1 = idx_ref[(s * N_KV_A + 1) * N_SLC + k]
        blk = jnp.where(kslot == k, jnp.where(rowk, j0, j1), blk)
    rel = past_len - ((blk // bpp) * PAGE + kin)
    ok = (kin // SLC_BLOCK) == jnp.where(blk < npb, blk % bpp, -1)
    ss = jnp.where(rowk, _dot(qg[0], kvbuf[slot, 0].astype(BF16)), _dot(qg[1], kvbuf[slot, 1].astype(BF16)))
    ss = jnp.where(ok, ss + _bucket_bias(jnp.maximum(rel, 0), tbl_t), NEG)
    s_new, v_new = new_token(knew_ref[0])
    p, pn, inv = _softmax_with_new(ss, s_new)
    pb = p.astype(BF16)
    o_s = (by_group(_dot_t(pb, kvbuf[slot, 2].astype(BF16)), _dot_t(pb, kvbuf[slot, 3].astype(BF16)))
           + pn * v_new) * inv

    gc = gcol_ref[0]
    o = gc[:, 0:1] * oc_ref[0] + gc[:, 1:2] * o_s + gc[:, 2:3] * o_w
    oa_ref[0] = jnp.where((lane >= HEAD_DIM) == (row >= HPG), o, 0.0)


def _samp_sw(page_table, idx_flat, pool_t, q3, knew, win_t, wnew, wcol, o_c, gcol, tbl_t, bw, past_len):
    Bd = page_table.shape[0]
    wl = win_t.shape[-1]
    per = lambda shp: pl.BlockSpec((1,) + shp, lambda s, pt, ix: (s,) + (0,) * len(shp))
    cst = lambda a: pl.BlockSpec(a.shape, lambda s, pt, ix: (0,) * a.ndim)
    return pl.pallas_call(
        functools.partial(_samp_sw_kernel, past_len=past_len),
        grid_spec=pltpu.PrefetchScalarGridSpec(
            num_scalar_prefetch=2, grid=(Bd,),
            in_specs=[pl.BlockSpec(memory_space=pl.ANY), per((8, LANE)), per((1, 2 * LANE)),
                      per((2 * N_KV_A, HEAD_DIM, wl)), per((1, 2 * LANE)), per((2 * N_KV_A, HEAD_DIM, 1)),
                      per((8, LANE)), per((8, LANE)), cst(tbl_t), cst(bw)],
            out_specs=[per((8, LANE)), per((2 * N_KV_A, HEAD_DIM, wl))],
            scratch_shapes=[pltpu.VMEM((2, 2 * N_KV_A, HEAD_DIM, N_SLC * PAGE), F32),
                            pltpu.SemaphoreType.DMA((2,))]),
        out_shape=[jax.ShapeDtypeStruct((Bd, 8, LANE), F32), jax.ShapeDtypeStruct(win_t.shape, F32)],
        name='sample_slc_win',
        compiler_params=pltpu.CompilerParams(dimension_semantics=('arbitrary',), vmem_limit_bytes=VMEM_LIMIT),
    )(page_table, idx_flat, pool_t, q3, knew, win_t, wnew, wcol, o_c, gcol, tbl_t, bw)


def _samp_mla_kernel(pt_ref, pool_hbm, q_ref, lnew_ref, o_ref, buf, sem, *, n_pages):
    s = pl.program_id(0)
    ns = pl.num_programs(0)

    def page_copy(page, slot, p):
        return pltpu.make_async_copy(pool_hbm.at[page], buf.at[slot, :, pl.ds(p * PAGE, PAGE)], sem.at[slot])

    def start(smp, slot):
        def body(p, c):
            page_copy(pt_ref[smp, p], slot, p).start()
            return c
        lax.fori_loop(0, n_pages, body, 0)

    @pl.when(s == 0)
    def _():
        start(0, 0)

    @pl.when(s + 1 < ns)
    def _():
        start(s + 1, (s + 1) % 2)

    slot = s % 2

    def wait_page(p, c):
        page_copy(0, slot, p).wait()
        return c
    lax.fori_loop(0, n_pages, wait_page, 0)

    q8 = q_ref[0]
    ckv_t = buf[slot, 0:KV_LORA, :].astype(BF16)
    kr_t = buf[slot, KV_LORA:LATENT_DIM, :].astype(BF16)
    ln = lnew_ref[0].astype(F32)
    sc = _dot(q8[:, 0:KV_LORA], ckv_t) + _dot(q8[:, KV_LORA:LATENT_DIM], kr_t)
    s_new = jnp.sum(q8.astype(F32) * ln, axis=1, keepdims=True)
    p, pn, inv = _softmax_with_new(sc, s_new)
    o_ref[0] = (_dot_t(p.astype(BF16), ckv_t) + pn * ln[:, 0:KV_LORA]) * inv


def _samp_mla(page_table, pool2d, qm3, lnew):
    Bd, n_pages = page_table.shape
    per = lambda shp: pl.BlockSpec((1,) + shp, lambda s, pt: (s, 0, 0))
    return pl.pallas_call(
        functools.partial(_samp_mla_kernel, n_pages=n_pages),
        grid_spec=pltpu.PrefetchScalarGridSpec(
            num_scalar_prefetch=1, grid=(Bd,),
            in_specs=[pl.BlockSpec(memory_space=pl.ANY), per((8, 256)), per((1, 256))],
            out_specs=per((8, KV_LORA)),
            scratch_shapes=[pltpu.VMEM((2, LATENT_DIM, n_pages * PAGE), F32), pltpu.SemaphoreType.DMA((2,))]),
        out_shape=jax.ShapeDtypeStruct((Bd, 8, KV_LORA), F32), name='sample_mla',
        compiler_params=pltpu.CompilerParams(dimension_semantics=('arbitrary',), vmem_limit_bytes=VMEM_LIMIT),
    )(page_table, pool2d, qm3, lnew)


def _uv_kernel(o_ref, w_ref, y_ref):
    y_ref[...] = _dot(o_ref[...].astype(BF16), w_ref[0])


def _samp_uv(o_lat2d, w_uv):
    Bd = o_lat2d.shape[0]
    return pl.pallas_call(
        _uv_kernel, grid=(N_HEADS_B,),
        in_specs=[pl.BlockSpec((Bd, LANE), lambda h: (0, h)), pl.BlockSpec((1, KV_LORA, LANE), lambda h: (h, 0, 0))],
        out_specs=pl.BlockSpec((Bd, LANE), lambda h: (0, h)),
        out_shape=jax.ShapeDtypeStruct((Bd, N_HEADS_B * LANE), F32), name='sample_uv',
        compiler_params=pltpu.CompilerParams(dimension_semantics=('parallel',), vmem_limit_bytes=VMEM_LIMIT),
    )(o_lat2d, w_uv)


def _sample_mix(xs3, msm, pool_cmp, pool_slc, win_buf, pool_mla, page_table, gain, rel_bias, w, cw):
    Bd = xs3.shape[1]
    past_len = page_table.shape[1] * PAGE
    cos_s, sin_s = _rope_tables(jnp.full((Bd,), past_len, jnp.int32))
    qa_s, cmp_s, slc_s, win_s, _, gt_s, qm_s, lat_s, lat16_s = _inproj(xs3, msm[0], msm[1], gain, cos_s, sin_s, w, Bd)
    q3 = qa_s.reshape(Bd, N_HEADS_A, LANE)
    nc = past_len // CMP_STRIDE
    n_cmp = (past_len + 1 - CMP_BLOCK) // CMP_STRIDE + 1
    m = jnp.arange(nc)
    bc = _t5_bias_rows(rel_bias, past_len - (m * CMP_STRIDE + CMP_BLOCK - 1), m < n_cmp)
    fm = lambda a: a.transpose(0, 2, 3, 4, 1).reshape(a.shape[0], 2 * N_KV_A, HEAD_DIM, a.shape[1])
    o_c, idx = _samp_cmp(page_table, fm(pool_cmp), q3, _samp_cmp_weights(cw), bc, n_cmp)
    idx_flat = idx[:, :N_KV_A, :N_SLC].reshape(-1)
    wl = win_buf.shape[1]
    wi = jnp.arange(wl)
    bw = _t5_bias_rows(rel_bias, wl - wi, (wl - wi < WINDOW) & (past_len - wl + wi >= 0))
    tbl_t = jnp.pad(rel_bias.astype(F32).T, ((0, 0), (0, LANE - N_BUCKETS)))
    gcol = gt_s[0, :, :3 * N_HEADS_A].reshape(Bd, 3, N_HEADS_A).transpose(0, 2, 1)
    gcol = jnp.pad(gcol, ((0, 0), (0, 0), (0, LANE - 3)))
    oa_s, new_win = _samp_sw(page_table, idx_flat, fm(pool_slc), q3, slc_s.reshape(Bd, 1, 2 * LANE), fm(win_buf),
                             win_s.reshape(Bd, 1, 2 * LANE), win_s.reshape(Bd, 2 * N_KV_A, HEAD_DIM, 1), o_c, gcol,
                             tbl_t, bw, past_len)
    o_lat = _samp_mla(page_table, pool_mla.transpose(0, 2, 1), qm_s.reshape(Bd, N_HEADS_B, 256),
                      lat16_s.reshape(Bd, 1, 256))
    ob_s = _samp_uv(o_lat.reshape(Bd, N_HEADS_B * KV_LORA), w['w_uv'])
    new_win = new_win.reshape(Bd, 2, N_KV_A, HEAD_DIM, wl).transpose(0, 4, 1, 2, 3)
    return oa_s.reshape(1, Bd, -1), ob_s.reshape(1, Bd, -1), cmp_s, slc_s, new_win, lat_s


def kernel(x_prompt, x_sample, cache_nsa_cmp, cache_nsa_slc, cache_nsa_win, cache_mla, page_table, c_prompt, c_sample, rel_bias, w_ada, b_ada, norm_attn, norm_ffn, w_in, cmp_pe, cmp_w1, cmp_w2, q_norm, w_q_up, kv_norm, w_kv_up, out_norm_a, out_norm_b, w_out, w_router, router_bias, w_gate_e, w_up_e, w_down_e, w_gate_s, w_up_s, w_down_s, norm_final):
    B, S, D = x_prompt.shape
    Bd = x_sample.shape[0]
    l = 0
    n_mod = B + Bd
    c_all = jnp.pad(jnp.concatenate([c_prompt, c_sample], axis=0), ((0, -n_mod % 8), (0, 0)))
    mod = _adaln(c_all, w_ada[l], b_ada[l]).reshape(-1, 6, D)
    mp = [mod[:B, i][:, None, :] for i in range(6)]
    msm = [mod[B:n_mod, i][None] for i in range(6)]

    w = _inproj_weights(w_in[l], q_norm[l], w_q_up[l], kv_norm[l], w_kv_up[l])
    cw = _compress_weights(cmp_pe[l], cmp_w1[l], cmp_w2[l])
    mw = _merge_weights(out_norm_a[l], out_norm_b[l], w_out[l], norm_ffn[l], w_router[l], w_gate_s[l], w_up_s[l],
                        w_down_s[l])
    tb, lb = _bias_tables(rel_bias)

    cos, sin = _rope_tables(jnp.arange(S))
    qa, cmp32, slc32, win32, kv16, gt, qm, lat32, lat16 = _inproj(
        x_prompt, mp[0], mp[1], norm_attn[l], cos, sin, w, 256)
    nc = S // CMP_STRIDE
    n_cmp = (S - CMP_BLOCK) // CMP_STRIDE + 1
    xc = cmp32.reshape(B, nc, CMP_STRIDE, 4, HEAD_DIM).transpose(0, 3, 1, 2, 4).reshape(B, 4, nc, -1).astype(BF16)
    kcv = _compress(xc, cw, n_cmp)
    oa_p = _nsa_prompt(qa, gt, kv16, kcv, tb, lb)
    ob_p = _mla_prompt(qm, lat16, w['w_uv'])
    xs_p, f_p, sc_p = _merge(x_prompt, oa_p, ob_p, mp[2], mp[3], mp[4], mp[5], mw, 256)

    xs3 = x_sample.reshape(1, Bd, D)
    oa_s, ob_s, cmp_s, slc_s, new_win, lat_s = _sample_mix(
        xs3, msm, cache_nsa_cmp[l], cache_nsa_slc[l], cache_nsa_win[l], cache_mla[l], page_table, norm_attn[l],
        rel_bias, w, cw)
    xs_s, f_s, sc_s = _merge(xs3, oa_s, ob_s, msm[2], msm[3], msm[4], msm[5], mw, Bd)

    n_p = B * S
    f_all = jnp.concatenate([f_p.reshape(n_p, D), f_s.reshape(Bd, D)], axis=0)
    sc_all = jnp.concatenate([sc_p.reshape(n_p, N_EXPERTS), sc_s.reshape(Bd, N_EXPERTS)], axis=0)
    wts, pos, row_tok, blk_exp, nused = _route(sc_all, router_bias[l])
    yb = _moe_experts(nused, blk_exp, row_tok, f_all, w_gate_e[l], w_up_e[l], w_down_e[l])
    tile_pos = lambda p: p.reshape(-1, MOE_BLOCK, TOP_K).transpose(0, 2, 1).reshape(-1, MOE_BLOCK)
    y_p = _combine(tile_pos(pos[:n_p]), yb, xs_p, mp[5], wts[:n_p].reshape(B, S, TOP_K), norm_final)
    y_s = _combine(tile_pos(pos[n_p:]), yb, xs_s, msm[5], wts[n_p:].reshape(1, Bd, TOP_K), norm_final)

    sh6 = lambda a, b, t: a.reshape(1, b, t, 2, N_KV_A, HEAD_DIM)
    return (y_p, y_s.reshape(Bd, 1, D), sh6(cmp32, B, S), sh6(cmp_s, Bd, 1), sh6(slc32, B, S), sh6(slc_s, Bd, 1),
            sh6(win32[:, S - WINDOW:], B, WINDOW), new_win[None], lat32[None], lat_s.reshape(1, Bd, 1, LATENT_DIM))
```

```python
import functools
import math

import jax
import jax.numpy as jnp
from jax import lax
from jax.experimental import pallas as pl
from jax.experimental.pallas import tpu as pltpu

F32 = jnp.float32
BF16 = jnp.bfloat16

LANE = 128
VMEM_LIMIT = 56 * 1024 * 1024

HEAD_DIM = 64
N_HEADS_A = 8
N_KV_A = 2
HPG = N_HEADS_A // N_KV_A
CMP_BLOCK = 32
CMP_STRIDE = 16
CMP_HIDDEN = 128
SLC_BLOCK = 64
SLC_PER_CMP = SLC_BLOCK // CMP_STRIDE
N_SLC = 16
N_LOCAL_SLC = 2
WINDOW = 512
N_HEADS_B = 8
Q_LORA = 192
KV_LORA = 128
QK_NOPE = 64
QK_ROPE = 32
V_DIM = 64
LATENT_DIM = KV_LORA + QK_ROPE
ROPE_THETA = 10000.0
MLA_SCALE = (QK_NOPE + QK_ROPE) ** -0.5
N_BUCKETS = 32
MAX_DISTANCE = 128
N_EXPERTS = 256
TOP_K = 8
N_GROUPS = 8
TOP_GROUPS = 4
ROUTED_SCALE = 2.5
MOE_BLOCK = 128
EPS = 1e-6
NEG = -1e30
FORCED = 1e30

QB = 128
KT_NSA = 1024
KT_MLA = 2048
CMP_PAD = 16
LOC_W = 24


def _dot(a, b):
    return jnp.dot(a, b, preferred_element_type=F32)


def _dot_t(a, b):
    return lax.dot_general(a, b, (((1,), (1,)), ((), ())), preferred_element_type=F32)


def _lane_tiles(x):
    return [x[:, c * LANE:(c + 1) * LANE] for c in range(x.shape[1] // LANE)]


def _row_max(x):
    return jnp.max(functools.reduce(jnp.maximum, _lane_tiles(x)), axis=1, keepdims=True)


def _row_sum(x):
    return jnp.sum(functools.reduce(jnp.add, _lane_tiles(x)), axis=1, keepdims=True)


def _const_spec(shape):
    nd = len(shape)
    return pl.BlockSpec(shape, lambda *_: (0,) * nd)


def _inproj_kernel(x_ref, sh_ref, sc_ref, g_ref, cs_ref, sn_ref, wq_ref, wkv_ref, wg_ref, wqd_ref,
                   wkvd_ref, qn_ref, wqup_ref, bd_ref, plc_ref, kvn_ref,
                   qa_ref, cmp_ref, slc_ref, win_ref, kv16_ref, gt_ref, qm_ref, lat_ref, lat16_ref):
    x = x_ref[0]
    ms = jnp.mean(x * x, axis=-1, keepdims=True)
    xn = x * lax.rsqrt(ms + EPS) * g_ref[...]
    h = xn * (1.0 + sc_ref[0]) + sh_ref[0]
    hb = h.astype(BF16)
    qa_ref[0] = _dot(hb, wq_ref[...]).astype(BF16)
    kv = _dot(hb, wkv_ref[...])
    cmp_ref[0] = kv[:, 0:256]
    slc_ref[0] = kv[:, 256:512]
    win_ref[0] = kv[:, 512:768]
    kv16_ref[0] = kv.astype(BF16)
    gl = _dot(hb, wg_ref[...])
    gt_ref[0] = 1.0 / (1.0 + jnp.exp(-gl))
    qd = _dot(hb, wqd_ref[...])
    qn = qd * lax.rsqrt(jnp.sum(qd * qd, axis=-1, keepdims=True) * (1.0 / Q_LORA) + EPS) * qn_ref[...]
    qu = _dot(qn.astype(BF16), wqup_ref[...])
    cs = cs_ref[...]
    sn = sn_ref[...]
    qr = qu[:, 512:768] * cs + qu[:, 768:1024] * sn
    qm = _dot(qu[:, 0:512].astype(BF16), bd_ref[...]) + _dot(qr.astype(BF16), plc_ref[...])
    qm_ref[0] = (qm * MLA_SCALE).astype(BF16)
    kvd = _dot(hb, wkvd_ref[...])
    c = kvd[:, 0:128]
    ckv = c * lax.rsqrt(jnp.mean(c * c, axis=-1, keepdims=True) + EPS) * kvn_ref[...]
    kr = kvd[:, 128:256] * cs[:, 0:128] + kvd[:, 256:384] * sn[:, 0:128]
    lat_ref[0, :, 0:128] = ckv
    lat_ref[0, :, 128:160] = kr[:, 0:32]
    lat16_ref[0, :, 0:128] = ckv.astype(BF16)
    lat16_ref[0, :, 128:256] = kr.astype(BF16)


def _inproj_weights(w_in, q_norm, w_q_up, kv_norm, w_kv_up):
    D = w_in.shape[0]
    o1 = N_HEADS_A * HEAD_DIM
    o2 = o1 + 6 * N_KV_A * HEAD_DIM
    o3 = o2 + 3 * N_HEADS_A
    o4 = o3 + Q_LORA
    wq = w_in[:, :o1].reshape(D, N_HEADS_A, HEAD_DIM) * (HEAD_DIM ** -0.5)
    z = jnp.zeros_like(wq)
    grp = (jnp.arange(N_HEADS_A) // HPG)[None, :, None]
    wq_pad = jnp.concatenate([jnp.where(grp == 0, wq, z), jnp.where(grp == 1, wq, z)], axis=-1)
    wq_pad = wq_pad.reshape(D, N_HEADS_A * 2 * HEAD_DIM)
    wkv = w_in[:, o1:o2]
    wg = jnp.pad(w_in[:, o2:o3], ((0, 0), (0, LANE - 3 * N_HEADS_A)))
    wqd = jnp.pad(w_in[:, o3:o4], ((0, 0), (0, 256 - Q_LORA)))
    wkd = w_in[:, o4:]
    half = QK_ROPE // 2
    wc = wkd[:, :KV_LORA]
    wr = wkd[:, KV_LORA:]
    wrot = jnp.concatenate([-wr[:, half:], wr[:, :half]], axis=1)
    padr = ((0, 0), (0, LANE - QK_ROPE))
    wkvd = jnp.concatenate([wc, jnp.pad(wr, padr), jnp.pad(wrot, padr)], axis=1)
    qn = jnp.pad(q_norm, (0, 256 - Q_LORA)).reshape(1, 256)
    wu = jnp.pad(w_q_up, ((0, 256 - Q_LORA), (0, 0))).reshape(256, N_HEADS_B, QK_NOPE + QK_ROPE)
    wu_n = wu[:, :, :QK_NOPE].reshape(256, N_HEADS_B * QK_NOPE)
    wu_r = wu[:, :, QK_NOPE:]
    wu_rot = jnp.concatenate([-wu_r[:, :, half:], wu_r[:, :, :half]], axis=-1)
    wqup = jnp.concatenate([wu_n, wu_r.reshape(256, -1), wu_rot.reshape(256, -1)], axis=1)
    w_ukv = w_kv_up.reshape(KV_LORA, N_HEADS_B, QK_NOPE + V_DIM)
    w_uk = w_ukv[:, :, :QK_NOPE]
    eye = jnp.eye(N_HEADS_B, dtype=F32)
    bd = jnp.einsum('chn,hk->hnkc', w_uk, eye)
    bd = jnp.pad(bd, ((0, 0), (0, 0), (0, 0), (0, 256 - KV_LORA))).reshape(N_HEADS_B * QK_NOPE, N_HEADS_B * 256)
    plc = jnp.einsum('hk,rs->hrks', eye, jnp.eye(QK_ROPE, dtype=F32))
    plc = jnp.pad(plc, ((0, 0), (0, 0), (0, 0), (KV_LORA, 256 - KV_LORA - QK_ROPE)))
    plc = plc.reshape(N_HEADS_B * QK_ROPE, N_HEADS_B * 256)
    w_uv = jnp.pad(w_ukv[:, :, QK_NOPE:].transpose(1, 0, 2), ((0, 0), (0, 0), (0, LANE - V_DIM)))
    bf = lambda a: a.astype(BF16)
    return dict(wq=bf(wq_pad), wkv=bf(wkv), wg=bf(wg), wqd=bf(wqd), wkvd=bf(wkvd), qn=qn, wqup=bf(wqup),
                bd=bf(bd), plc=bf(plc), kvn=kv_norm.reshape(1, KV_LORA), w_uv=bf(w_uv))


def _rope_tables(pos):
    half = QK_ROPE // 2
    inv = ROPE_THETA ** (-jnp.arange(half, dtype=F32) / half)
    ang = pos.astype(F32)[:, None] * inv[None, :]
    cos = jnp.tile(jnp.cos(ang), (1, 2 * N_HEADS_B))
    sin = jnp.tile(jnp.sin(ang), (1, 2 * N_HEADS_B))
    return cos, sin


def _inproj(x, shift, scale, gain, cos, sin, w, tr):
    B, T, D = x.shape
    tm = shift.shape[1]
    mod_spec = pl.BlockSpec((1, tr if tm > 1 else 1, D), (lambda b, t: (b, t, 0)) if tm > 1 else (lambda b, t: (b, 0, 0)))
    row = lambda n: pl.BlockSpec((1, tr, n), lambda b, t: (b, t, 0))
    tab = pl.BlockSpec((tr, 256), lambda b, t: (t, 0))
    wnames = ['wq', 'wkv', 'wg', 'wqd', 'wkvd', 'qn', 'wqup', 'bd', 'plc', 'kvn']
    out_shape = [
        jax.ShapeDtypeStruct((B, T, 1024), BF16), jax.ShapeDtypeStruct((B, T, 256), F32),
        jax.ShapeDtypeStruct((B, T, 256), F32), jax.ShapeDtypeStruct((B, T, 256), F32),
        jax.ShapeDtypeStruct((B, T, 768), BF16), jax.ShapeDtypeStruct((B, T, LANE), F32),
        jax.ShapeDtypeStruct((B, T, 2048), BF16), jax.ShapeDtypeStruct((B, T, LATENT_DIM), F32),
        jax.ShapeDtypeStruct((B, T, 256), BF16)]
    in_specs = [row(D), mod_spec, mod_spec, _const_spec((1, D)), tab, tab]
    in_specs += [_const_spec(w[n].shape) for n in wnames[:5]]
    in_specs += [_const_spec(w['qn'].shape)] + [_const_spec(w[n].shape) for n in wnames[6:9]]
    in_specs += [_const_spec(w['kvn'].shape)]
    return pl.pallas_call(
        _inproj_kernel, grid=(B, T // tr), in_specs=in_specs,
        out_specs=[row(s.shape[-1]) for s in out_shape], out_shape=out_shape, name='inproj',
        compiler_params=pltpu.CompilerParams(dimension_semantics=('parallel', 'parallel'),
                                             vmem_limit_bytes=VMEM_LIMIT),
    )(x, shift, scale, gain.reshape(1, D), cos, sin, *[w[n] for n in wnames])


def _compress_kernel(x0_ref, x1_ref, w1_ref, pe_ref, w2a_ref, w2b_ref, o_ref, *, n_cmp):
    w1 = w1_ref[0]
    pp = _dot(pe_ref[0], w1)
    peh = pp[0:1, 0:CMP_HIDDEN] + pp[1:2, CMP_HIDDEN:]
    nc = x0_ref.shape[2]
    out = jnp.zeros((nc, LANE), F32)
    for x_ref, w2_ref in ((x0_ref, w2a_ref), (x1_ref, w2b_ref)):
        ab = _dot(x_ref[0, 0], w1)
        hid = ab[:, 0:CMP_HIDDEN] + pltpu.roll(ab[:, CMP_HIDDEN:], nc - 1, 0) + peh
        act = hid * (1.0 / (1.0 + jnp.exp(-hid)))
        out = out + _dot(act.astype(BF16), w2_ref[0])
    rows = lax.broadcasted_iota(jnp.int32, (nc, LANE), 0)
    out = jnp.where(rows < n_cmp, out, 0.0)
    o_ref[0, 0, 0:CMP_PAD] = jnp.zeros((CMP_PAD, LANE), F32)
    o_ref[0, 0, CMP_PAD:CMP_PAD + nc] = out
    o_ref[0, 0, CMP_PAD + nc:] = jnp.zeros((o_ref.shape[2] - CMP_PAD - nc, LANE), F32)


def _compress_weights(cmp_pe, cmp_w1, cmp_w2):
    kin = CMP_STRIDE * HEAD_DIM
    w1 = cmp_w1.reshape(2, 2, kin, CMP_HIDDEN)
    w1cat = jnp.concatenate([w1[:, 0], w1[:, 1]], axis=-1).astype(BF16)
    pe = jnp.pad(cmp_pe.reshape(2, 2, kin), ((0, 0), (0, 6), (0, 0))).astype(BF16)
    w2a = jnp.pad(cmp_w2, ((0, 0), (0, 0), (0, HEAD_DIM))).astype(BF16)
    w2b = jnp.pad(cmp_w2, ((0, 0), (0, 0), (HEAD_DIM, 0))).astype(BF16)
    return w1cat, pe, w2a, w2b


def _compress(xc, cw, n_cmp):
    B, _, nc, kin = xc.shape
    w1cat, pe, w2a, w2b = cw
    return pl.pallas_call(
        functools.partial(_compress_kernel, n_cmp=n_cmp), grid=(B, 2),
        in_specs=[pl.BlockSpec((1, 1, nc, kin), lambda b, j: (b, 2 * j, 0, 0)),
                  pl.BlockSpec((1, 1, nc, kin), lambda b, j: (b, 2 * j + 1, 0, 0)),
                  pl.BlockSpec((1, kin, 2 * CMP_HIDDEN), lambda b, j: (j, 0, 0)),
                  pl.BlockSpec((1, 8, kin), lambda b, j: (j, 0, 0)),
                  pl.BlockSpec((1, CMP_HIDDEN, LANE), lambda b, j: (j, 0, 0)),
                  pl.BlockSpec((1, CMP_HIDDEN, LANE), lambda b, j: (j, 0, 0))],
        out_specs=pl.BlockSpec((1, 1, nc + LANE, LANE), lambda b, j: (b, j, 0, 0)),
        out_shape=jax.ShapeDtypeStruct((B, 2, nc + LANE, LANE), F32), name='compress',
        compiler_params=pltpu.CompilerParams(dimension_semantics=('parallel', 'parallel'),
                                             vmem_limit_bytes=VMEM_LIMIT),
    )(xc, xc, w1cat, pe, w2a, w2b)


def _t5_bucket(rel):
    max_exact = N_BUCKETS // 2
    n = jnp.maximum(rel, 0)
    nf = jnp.maximum(n, 1).astype(F32)
    large = max_exact + (jnp.log(nf / max_exact) / math.log(MAX_DISTANCE / max_exact)
                         * (N_BUCKETS - max_exact)).astype(jnp.int32)
    large = jnp.minimum(large, N_BUCKETS - 1)
    return jnp.where(n < max_exact, n, large)


def _bias_tables(rel_bias):
    tbl = rel_bias.astype(F32)
    const = tbl[N_BUCKETS - 1]
    i = jnp.arange(QB)[:, None]
    j = jnp.arange(LANE)[None, :]

    def tab(rel):
        b = tbl[_t5_bucket(rel)]
        return jnp.moveaxis(b, -1, 0) - const[:, None, None]

    t0 = jnp.where((i - j >= 0)[None], tab(i - j), NEG)
    t1 = tab(QB + i - j)
    zero = jnp.zeros_like(t1)
    t4 = jnp.broadcast_to(jnp.where(j > i, 0.0, NEG)[None], t1.shape)
    tb = jnp.stack([t0, t1, zero, jnp.full_like(t1, NEG), t4]).reshape(5, N_HEADS_A * QB, LANE)
    rel_l = i - CMP_STRIDE * (j - CMP_PAD) - (CMP_BLOCK - 1)
    lb = jnp.where(((j < LOC_W) & (rel_l >= 0))[None], tab(rel_l), NEG).reshape(N_HEADS_A * QB, LANE)
    return tb, lb


def _hilo_dot(x, m):
    hi = x.astype(BF16)
    lo = (x - hi.astype(F32)).astype(BF16)
    return _dot(hi, m) + _dot(lo, m)


def _nsa_kernel(q_ref, gt_ref, ks_ref, vs_ref, kw_ref, vw_ref, kc_ref, vc_ref, tb_ref, lb_ref, pf_ref, e_ref,
                o_ref, m_scr, l_scr, acc_scr, *, n_cmp):
    qb = pl.program_id(1)
    nh = N_HEADS_A
    rows = nh * QB
    q = q_ref[0]
    q8 = jnp.concatenate([q[:, h * LANE:(h + 1) * LANE] for h in range(nh)], axis=0)
    nc = pf_ref.shape[0]

    kcf = kc_ref[0, 0, 0:nc].astype(BF16)
    vcf = vc_ref[0, 0, 0:nc].astype(BF16)
    l0 = pl.multiple_of(qb * (QB // CMP_STRIDE), 8)
    kcl = kc_ref[0, 0, pl.ds(l0, LANE)].astype(BF16)
    vcl = vc_ref[0, 0, pl.ds(l0, LANE)].astype(BF16)
    colf = lax.broadcasted_iota(jnp.int32, (1, nc), 1)
    far_ok = jnp.where(colf >= CMP_PAD, jnp.where(colf < l0, 0.0, NEG), NEG)
    coll = lax.broadcasted_iota(jnp.int32, (1, LANE), 1) + (l0 - CMP_PAD)
    loc_ok = jnp.where(coll >= 0, jnp.where(coll < n_cmp, 0.0, NEG), NEG)
    s_far = _dot_t(q8, kcf) + far_ok
    s_loc = _dot_t(q8, kcl) + lb_ref[...] + loc_ok
    mrow = jnp.maximum(jnp.max(s_far, axis=1, keepdims=True), jnp.max(s_loc, axis=1, keepdims=True))
    p_far = jnp.exp(s_far - mrow)
    p_loc = jnp.exp(s_loc - mrow)
    lsum = jnp.sum(p_far, axis=1, keepdims=True) + jnp.sum(p_loc, axis=1, keepdims=True)
    inv = jnp.where(mrow > 0.5 * NEG, 1.0 / lsum, 0.0)
    p_far = p_far * inv
    p_loc = p_loc * inv
    o_c = _dot(p_far.astype(BF16), vcf) + _dot(p_loc.astype(BF16), vcl)

    r_i = lax.broadcasted_iota(jnp.int32, (LANE, LANE), 0)
    c_i = lax.broadcasted_iota(jnp.int32, (LANE, LANE), 1)
    pool_loc = jnp.where(r_i < LOC_W,
                         jnp.where(c_i == (r_i >> 2) + (2 * qb - CMP_PAD // SLC_PER_CMP), 1.0, 0.0),
                         0.0).astype(BF16)
    tq = 2 * qb + jnp.where(r_i >= SLC_BLOCK, 1, 0)
    dist = tq - c_i
    c_f = c_i.astype(F32)
    sel = []
    for g in range(N_KV_A):
        pgf = p_far[(g * HPG) * QB:(g * HPG + 1) * QB]
        pgl = p_loc[(g * HPG) * QB:(g * HPG + 1) * QB]
        for hh in range(1, HPG):
            pgf = pgf + p_far[(g * HPG + hh) * QB:(g * HPG + hh + 1) * QB]
            pgl = pgl + p_loc[(g * HPG + hh) * QB:(g * HPG + hh + 1) * QB]
        imp = _hilo_dot(pgf, pf_ref[...]) + _hilo_dot(pgl, pool_loc)
        score = jnp.where(dist < 0, NEG, jnp.where(dist < N_LOCAL_SLC, FORCED, jnp.where(c_i == 0, FORCED, imp)))
        chosen = jnp.zeros((QB, LANE), F32)
        for _ in range(N_SLC):
            mx = jnp.max(score, axis=1, keepdims=True)
            first = jnp.min(jnp.where(score == mx, c_f, float(LANE)), axis=1, keepdims=True)
            hit = c_f == first
            chosen = jnp.where(hit, 1.0, chosen)
            score = jnp.where(hit, -jnp.inf, score)
        sel.append(chosen.astype(BF16))

    m_scr[...] = jnp.full(m_scr.shape, -jnp.inf, F32)
    l_scr[...] = jnp.zeros(l_scr.shape, F32)
    acc_scr[...] = jnp.zeros(acc_scr.shape, F32)
    KT = KT_NSA
    nsub = KT // LANE

    def slc_tile(kt, carry):
        k0 = pl.multiple_of(kt * KT, KT)
        s = _dot_t(q8, ks_ref[0, pl.ds(k0, KT), :])
        bias = []
        for c in range(nsub):
            d = qb - (kt * nsub + c)
            bias.append(tb_ref[jnp.where(d < 0, 3, jnp.minimum(d, 2))])
        e_t = e_ref[:, pl.ds(k0, KT)]
        madd = [(_dot(sel[g], e_t) - 1.0) * (-NEG) for g in range(N_KV_A)]
        madd = jnp.concatenate([madd[g] for g in range(N_KV_A) for _ in range(HPG)], axis=0)
        s = s + jnp.concatenate(bias, axis=1) + madd
        m_old = m_scr[...]
        m_new = jnp.maximum(m_old, jnp.max(s, axis=1, keepdims=True))
        alpha = jnp.exp(m_old - m_new)
        p = jnp.exp(s - m_new)
        l_scr[...] = alpha * l_scr[...] + jnp.sum(p, axis=1, keepdims=True)
        acc_scr[...] = alpha * acc_scr[...] + _dot(p.astype(BF16), vs_ref[0, pl.ds(k0, KT), :])
        m_scr[...] = m_new
        return carry

    lax.fori_loop(0, qb // nsub + 1, slc_tile, 0)
    o_s = acc_scr[...] * (1.0 / l_scr[...])

    nwin = WINDOW // QB + 1
    w0 = jnp.maximum(qb - (nwin - 1), 0)
    k0 = pl.multiple_of(w0 * QB, QB)
    s = _dot_t(q8, kw_ref[0, pl.ds(k0, nwin * QB), :])
    bias = []
    for c in range(nwin):
        d = qb - (w0 + c)
        bias.append(tb_ref[jnp.where(d < 0, 3, jnp.where(d >= nwin - 1, 4, jnp.minimum(d, 2)))])
    s = s + jnp.concatenate(bias, axis=1)
    p = jnp.exp(s - jnp.max(s, axis=1, keepdims=True))
    o_w = _dot(p.astype(BF16), vw_ref[0, pl.ds(k0, nwin * QB), :]) * (1.0 / jnp.sum(p, axis=1, keepdims=True))

    gt = gt_ref[0]
    for h in range(nh):
        r = slice(h * QB, (h + 1) * QB)
        comb = (gt[:, h:h + 1] * o_c[r] + gt[:, nh + h:nh + h + 1] * o_s[r]
                + gt[:, 2 * nh + h:2 * nh + h + 1] * o_w[r])
        keep = (c_i >= HEAD_DIM) if h // HPG else (c_i < HEAD_DIM)
        o_ref[0, :, h * LANE:(h + 1) * LANE] = jnp.where(keep, comb, 0.0)


def _nsa_prompt(qa, gates, kv16, kc, tb, lb):
    B, S, _ = qa.shape
    nc = S // CMP_STRIDE
    n_cmp = (S - CMP_BLOCK) // CMP_STRIDE + 1
    m = jnp.arange(nc)
    pool_far = ((m[:, None] // SLC_PER_CMP - CMP_PAD // SLC_PER_CMP == jnp.arange(LANE)[None, :])
                & (m[:, None] >= CMP_PAD)).astype(BF16)
    expand = (jnp.arange(S)[None, :] // SLC_BLOCK == jnp.arange(LANE)[:, None]).astype(BF16)
    rows = N_HEADS_A * QB
    one = pl.Buffered(1)
    kvs = lambda c: pl.BlockSpec((1, S, LANE), lambda b, t: (b, 0, c), pipeline_mode=one)
    cspec = pl.BlockSpec((1, 1, nc + LANE, LANE), lambda b, t: (b, 0, 0, 0), pipeline_mode=one)
    vspec = pl.BlockSpec((1, 1, nc + LANE, LANE), lambda b, t: (b, 1, 0, 0), pipeline_mode=one)
    cst = lambda shape: pl.BlockSpec(shape, lambda b, t: (0,) * len(shape), pipeline_mode=one)
    return pl.pallas_call(
        functools.partial(_nsa_kernel, n_cmp=n_cmp), grid=(B, S // QB),
        in_specs=[pl.BlockSpec((1, QB, 1024), lambda b, t: (b, t, 0)),
                  pl.BlockSpec((1, QB, LANE), lambda b, t: (b, t, 0)),
                  kvs(2), kvs(3), kvs(4), kvs(5), cspec, vspec,
                  cst(tb.shape), cst(lb.shape), cst(pool_far.shape), cst(expand.shape)],
        out_specs=pl.BlockSpec((1, QB, 1024), lambda b, t: (b, t, 0)),
        out_shape=jax.ShapeDtypeStruct((B, S, 1024), F32),
        scratch_shapes=[pltpu.VMEM((rows, 1), F32), pltpu.VMEM((rows, 1), F32), pltpu.VMEM((rows, LANE), F32)],
        name='nsa_prompt',
        compiler_params=pltpu.CompilerParams(dimension_semantics=('parallel', 'arbitrary'),
                                             vmem_limit_bytes=VMEM_LIMIT),
    )(qa, gates, kv16, kv16, kv16, kv16, kc, kc, tb, lb, pool_far, expand)


def _mla_kernel(q_ref, lat_ref, wuv_ref, o_ref, m_scr, l_scr, acc_scr):
    qb = pl.program_id(1)
    nh = N_HEADS_B
    rows = nh * QB
    q = q_ref[0]
    q8 = jnp.concatenate([q[:, h * 256:(h + 1) * 256] for h in range(nh)], axis=0)
    m_scr[...] = jnp.full(m_scr.shape, -jnp.inf, F32)
    l_scr[...] = jnp.zeros(l_scr.shape, F32)
    acc_scr[...] = jnp.zeros(acc_scr.shape, F32)
    KT = KT_MLA
    nsub = KT // QB

    def tile(kt, masked):
        k0 = pl.multiple_of(kt * KT, KT)
        lat = lat_ref[0, pl.ds(k0, KT), :]
        s = _dot_t(q8, lat)
        if masked:
            col = lax.broadcasted_iota(jnp.int32, (rows, KT), 1)
            row = lax.broadcasted_iota(jnp.int32, (rows, KT), 0) & (QB - 1)
            s = jnp.where(col - row <= qb * QB - kt * KT, s, NEG)
        m_old = m_scr[...]
        m_new = jnp.maximum(m_old, _row_max(s))
        alpha = jnp.exp(m_old - m_new)
        p = jnp.exp(s - m_new)
        l_scr[...] = alpha * l_scr[...] + _row_sum(p)
        acc_scr[...] = alpha * acc_scr[...] + _dot(p.astype(BF16), lat[:, 0:KV_LORA])
        m_scr[...] = m_new

    def full_tile(kt, carry):
        tile(kt, False)
        return carry

    lax.fori_loop(0, qb // nsub, full_tile, 0)
    tile(qb // nsub, True)
    o_lat = (acc_scr[...] * (1.0 / l_scr[...])).astype(BF16)
    for h in range(nh):
        o_ref[0, :, h * LANE:(h + 1) * LANE] = _dot(o_lat[h * QB:(h + 1) * QB], wuv_ref[h])


def _mla_prompt(qm, lat16, w_uv):
    B, S, _ = qm.shape
    rows = N_HEADS_B * QB
    one = pl.Buffered(1)
    return pl.pallas_call(
        _mla_kernel, grid=(B, S // QB),
        in_specs=[pl.BlockSpec((1, QB, 2048), lambda b, t: (b, t, 0)),
                  pl.BlockSpec((1, S, 256), lambda b, t: (b, 0, 0), pipeline_mode=one),
                  pl.BlockSpec(w_uv.shape, lambda b, t: (0, 0, 0), pipeline_mode=one)],
        out_specs=pl.BlockSpec((1, QB, 1024), lambda b, t: (b, t, 0)),
        out_shape=jax.ShapeDtypeStruct((B, S, 1024), F32),
        scratch_shapes=[pltpu.VMEM((rows, 1), F32), pltpu.VMEM((rows, 1), F32), pltpu.VMEM((rows, KV_LORA), F32)],
        name='mla_prompt',
        compiler_params=pltpu.CompilerParams(dimension_semantics=('parallel', 'arbitrary'),
                                             vmem_limit_bytes=VMEM_LIMIT),
    )(qm, lat16, w_uv)


def _adaln_kernel(c_ref, w_ref, b_ref, o_ref):
    c = c_ref[...]
    a = (c * (1.0 / (1.0 + jnp.exp(-c)))).astype(BF16)
    o_ref[...] = _dot(a, w_ref[...].astype(BF16)) + b_ref[...]


def _adaln(c, w_ada, b_ada, tn=512):
    R_, D = c.shape
    N = w_ada.shape[1]
    return pl.pallas_call(
        _adaln_kernel, grid=(N // tn,),
        in_specs=[pl.BlockSpec((R_, D), lambda j: (0, 0)), pl.BlockSpec((D, tn), lambda j: (0, j)),
                  pl.BlockSpec((1, tn), lambda j: (0, j))],
        out_specs=pl.BlockSpec((R_, tn), lambda j: (0, j)),
        out_shape=jax.ShapeDtypeStruct((R_, N), F32), name='adaln',
        compiler_params=pltpu.CompilerParams(dimension_semantics=('parallel',), vmem_limit_bytes=VMEM_LIMIT),
    )(c, w_ada, b_ada.reshape(1, N))


def _merge_kernel(x_ref, oa_ref, ob_ref, ga_ref, shf_ref, scf_ref, gf_ref, na_ref, nb_ref, nffn_ref,
                  wa_ref, wb_ref, wr_ref, wgs_ref, wus_ref, wds_ref, xs_ref, f_ref, sc_ref):
    n_real = N_HEADS_A * HEAD_DIM
    oa = oa_ref[0]
    ob = ob_ref[0]
    na = oa * lax.rsqrt(jnp.sum(oa * oa, axis=-1, keepdims=True) * (1.0 / n_real) + EPS) * na_ref[...]
    nb = ob * lax.rsqrt(jnp.sum(ob * ob, axis=-1, keepdims=True) * (1.0 / n_real) + EPS) * nb_ref[...]
    mix = _dot(na.astype(BF16), wa_ref[...]) + _dot(nb.astype(BF16), wb_ref[...])
    x1 = x_ref[0] + ga_ref[0] * mix
    f = x1 * lax.rsqrt(jnp.mean(x1 * x1, axis=-1, keepdims=True) + EPS) * nffn_ref[...]
    f = f * (1.0 + scf_ref[0]) + shf_ref[0]
    for c in range(f.shape[1] // LANE):
        f_ref[0, :, c, :] = f[:, c * LANE:(c + 1) * LANE]
    fb = f.astype(BF16)
    sc_ref[0] = 1.0 / (1.0 + jnp.exp(-_dot(fb, wr_ref[...])))
    g = _dot(fb, wgs_ref[...])
    u = _dot(fb, wus_ref[...])
    hsh = (g * (1.0 / (1.0 + jnp.exp(-g))) * u).astype(BF16)
    xs_ref[0] = x1 + gf_ref[0] * _dot(hsh, wds_ref[...])


def _merge_weights(out_norm_a, out_norm_b, w_out, norm_ffn, w_router, w_gate_s, w_up_s, w_down_s):
    D = w_out.shape[1]
    na = out_norm_a.reshape(N_HEADS_A, 1, HEAD_DIM)
    grp = (jnp.arange(N_HEADS_A) // HPG)[:, None, None]
    half = jnp.arange(2)[None, :, None]
    na_pad = jnp.where(grp == half, na, 0.0).reshape(1, -1)
    nb_pad = jnp.pad(out_norm_b.reshape(N_HEADS_B, V_DIM), ((0, 0), (0, LANE - V_DIM))).reshape(1, -1)
    wa = w_out[:N_HEADS_A * HEAD_DIM].reshape(N_HEADS_A, 1, HEAD_DIM, D)
    wa_pad = jnp.where((grp == half)[..., None], wa, 0.0).reshape(-1, D)
    wb = w_out[N_HEADS_A * HEAD_DIM:].reshape(N_HEADS_B, V_DIM, D)
    wb_pad = jnp.pad(wb, ((0, 0), (0, LANE - V_DIM), (0, 0))).reshape(-1, D)
    bf = lambda a: a.astype(BF16)
    return [na_pad, nb_pad, norm_ffn.reshape(1, D), bf(wa_pad), bf(wb_pad), bf(w_router), bf(w_gate_s),
            bf(w_up_s), bf(w_down_s)]


def _mod_spec(a, tr):
    if a.shape[1] > 1:
        return pl.BlockSpec((1, tr, a.shape[2]), lambda b, t: (b, t, 0))
    return pl.BlockSpec((1, 1, a.shape[2]), lambda b, t: (b, 0, 0))


def _merge(x, oa, ob, gate_a, shift_f, scale_f, gate_f, mw, tr):
    B, T, D = x.shape
    row = lambda n: pl.BlockSpec((1, tr, n), lambda b, t: (b, t, 0))
    out_shape = [jax.ShapeDtypeStruct((B, T, D), F32), jax.ShapeDtypeStruct((B, T, D // LANE, LANE), F32),
                 jax.ShapeDtypeStruct((B, T, N_EXPERTS), F32)]
    return pl.pallas_call(
        _merge_kernel, grid=(B, T // tr),
        in_specs=[row(D), row(1024), row(1024)] + [_mod_spec(a, tr) for a in (gate_a, shift_f, scale_f, gate_f)]
        + [_const_spec(a.shape) for a in mw],
        out_specs=[row(D), pl.BlockSpec((1, tr, D // LANE, LANE), lambda b, t: (b, t, 0, 0)), row(N_EXPERTS)],
        out_shape=out_shape, name='merge',
        compiler_params=pltpu.CompilerParams(dimension_semantics=('parallel', 'parallel'),
                                             vmem_limit_bytes=VMEM_LIMIT),
    )(x, oa, ob, gate_a, shift_f, scale_f, gate_f, *mw)


MOE_CHUNK = 256
MOE_LOOK = 2


def _moe_kernel(nused_ref, bexp_ref, rtok_hbm, f_hbm, wg_ref, wu_ref, wd_ref, y_ref, xbuf, wgb, wub, wdb, rtok, sem,
                isem):
    i = pl.program_id(0)
    nused = nused_ref[0]
    nchunk = rtok_hbm.shape[0] // MOE_CHUNK

    def ids_copy(c):
        return pltpu.make_async_copy(rtok_hbm.at[pl.ds(c * MOE_CHUNK, MOE_CHUNK)], rtok.at[c % 2], isem.at[c % 2])

    nbuf = MOE_LOOK + 1
    nct = f_hbm.shape[1]

    def gather(blk):
        ids = rtok.at[(blk // MOE_CHUNK) % 2]
        row = blk % MOE_CHUNK
        for r in range(MOE_BLOCK):
            pltpu.make_async_copy(f_hbm.at[pl.ds(ids[row, r], 1)], xbuf.at[blk % nbuf, pl.ds(r, 1)],
                                  sem.at[blk % nbuf]).start()

    def wait_gather(blk):
        pltpu.make_async_copy(f_hbm.at[pl.ds(0, MOE_BLOCK)], xbuf.at[blk % nbuf], sem.at[blk % nbuf]).wait()

    @pl.when(i == 0)
    def _():
        ids_copy(0).start()
        ids_copy(0).wait()
        if nchunk > 1:
            ids_copy(1).start()
        for b in range(MOE_LOOK):
            gather(b)

    @pl.when(((i + MOE_LOOK) % MOE_CHUNK == 0) & ((i + MOE_LOOK) // MOE_CHUNK < nchunk))
    def _():
        c = (i + MOE_LOOK) // MOE_CHUNK
        ids_copy(c).wait()

        @pl.when(c + 1 < nchunk)
        def _():
            ids_copy(c + 1).start()

    @pl.when(i < nused)
    def _():
        wait_gather(i)

        @pl.when((i == 0) | (bexp_ref[i] != bexp_ref[jnp.maximum(i - 1, 0)]))
        def _():
            wgb[...] = wg_ref[0].astype(BF16)
            wub[...] = wu_ref[0].astype(BF16)
            wdb[...] = wd_ref[0].astype(BF16)

        gather(i + MOE_LOOK)
        slot = i % nbuf
        g = jnp.zeros((MOE_BLOCK, wgb.shape[1]), F32)
        u = jnp.zeros((MOE_BLOCK, wub.shape[1]), F32)
        for c in range(0, nct, 2):
            x = jnp.concatenate([xbuf[slot, :, c, :], xbuf[slot, :, c + 1, :]], axis=1).astype(BF16)
            g = g + _dot(x, wgb[c * LANE:(c + 2) * LANE, :])
            u = u + _dot(x, wub[c * LANE:(c + 2) * LANE, :])
        h = (g * (1.0 / (1.0 + jnp.exp(-g))) * u).astype(BF16)
        y = _dot(h, wdb[...])
        for c in range(nct):
            y_ref[:, c, :] = y[:, c * LANE:(c + 1) * LANE]

    @pl.when(i >= nused)
    def _():
        @pl.when(i < nused + MOE_LOOK)
        def _():
            wait_gather(i)
        y_ref[...] = jnp.zeros(y_ref.shape, F32)


def _moe_experts(nused, blk_exp, row_tok, f, w_gate_e, w_up_e, w_down_e):
    n_blocks = blk_exp.shape[0]
    n_steps = n_blocks + MOE_LOOK
    nb_pad = -(-n_steps // MOE_CHUNK) * MOE_CHUNK
    row_tok = jnp.pad(row_tok, (0, (nb_pad - n_blocks) * MOE_BLOCK)).reshape(nb_pad, MOE_BLOCK)
    blk_exp = jnp.pad(blk_exp, (0, MOE_LOOK), mode='edge')
    nct = f.shape[1]
    D = nct * LANE
    de = w_gate_e.shape[2]
    wspec = lambda shp: pl.BlockSpec((1,) + shp, lambda i, nu, be: (be[i], 0, 0))
    return pl.pallas_call(
        _moe_kernel,
        grid_spec=pltpu.PrefetchScalarGridSpec(
            num_scalar_prefetch=2, grid=(n_steps,),
            in_specs=[pl.BlockSpec(memory_space=pl.ANY), pl.BlockSpec(memory_space=pl.ANY),
                      wspec((D, de)), wspec((D, de)), wspec((de, D))],
            out_specs=pl.BlockSpec((MOE_BLOCK, nct, LANE), lambda i, nu, be: (i, 0, 0)),
            scratch_shapes=[pltpu.VMEM((MOE_LOOK + 1, MOE_BLOCK, nct, LANE), F32), pltpu.VMEM((D, de), BF16),
                            pltpu.VMEM((D, de), BF16), pltpu.VMEM((de, D), BF16),
                            pltpu.SMEM((2, MOE_CHUNK, MOE_BLOCK), jnp.int32),
                            pltpu.SemaphoreType.DMA((MOE_LOOK + 1,)), pltpu.SemaphoreType.DMA((2,))]),
        out_shape=jax.ShapeDtypeStruct((n_steps * MOE_BLOCK, nct, LANE), F32), name='moe_experts',
        compiler_params=pltpu.CompilerParams(dimension_semantics=('arbitrary',), vmem_limit_bytes=VMEM_LIMIT),
    )(nused, blk_exp, row_tok, f, w_gate_e, w_up_e, w_down_e)


def _combine_kernel(pos_ref, yb_hbm, xs_ref, gf_ref, w_ref, nf_ref, o_ref, buf, sem):
    tile = pl.program_id(0) * pl.num_programs(1) + pl.program_id(1)
    ntile = pl.num_programs(0) * pl.num_programs(1)

    nct = yb_hbm.shape[1]

    def start(t, slot):
        def per_k(k, c):
            for r in range(MOE_BLOCK):
                pltpu.make_async_copy(yb_hbm.at[pl.ds(pos_ref[t * TOP_K + k, r], 1)],
                                      buf.at[slot, k, pl.ds(r, 1)], sem.at[slot]).start()
            return c
        lax.fori_loop(0, TOP_K, per_k, 0)

    @pl.when(tile == 0)
    def _():
        start(0, 0)

    @pl.when(tile + 1 < ntile)
    def _():
        start(tile + 1, (tile + 1) % 2)

    slot = tile % 2
    w = w_ref[0]
    for k in range(TOP_K):
        pltpu.make_async_copy(yb_hbm.at[pl.ds(0, MOE_BLOCK)], buf.at[slot, k], sem.at[slot]).wait()
    x2 = []
    ss = jnp.zeros((MOE_BLOCK, 1), F32)
    for c in range(nct):
        routed = jnp.zeros((MOE_BLOCK, LANE), F32)
        for k in range(TOP_K):
            routed = routed + w[:, k:k + 1] * buf[slot, k, :, c, :]
        cols = slice(c * LANE, (c + 1) * LANE)
        xc = xs_ref[0, :, cols] + gf_ref[0, :, cols] * routed
        ss = ss + jnp.sum(xc * xc, axis=-1, keepdims=True)
        x2.append(xc)
    inv = lax.rsqrt(ss * (1.0 / (nct * LANE)) + EPS)
    for c in range(nct):
        cols = slice(c * LANE, (c + 1) * LANE)
        o_ref[0, :, cols] = x2[c] * inv * nf_ref[:, cols]


def _combine(pos, yb, xs, gate_f, wts, norm_final):
    B, T, D = xs.shape
    tr = MOE_BLOCK
    gspec = (pl.BlockSpec((1, tr, D), lambda b, t, p: (b, t, 0)) if gate_f.shape[1] > 1
             else pl.BlockSpec((1, 1, D), lambda b, t, p: (b, 0, 0)))
    return pl.pallas_call(
        _combine_kernel,
        grid_spec=pltpu.PrefetchScalarGridSpec(
            num_scalar_prefetch=1, grid=(B, T // tr),
            in_specs=[pl.BlockSpec(memory_space=pl.ANY), pl.BlockSpec((1, tr, D), lambda b, t, p: (b, t, 0)), gspec,
                      pl.BlockSpec((1, tr, TOP_K), lambda b, t, p: (b, t, 0)),
                      pl.BlockSpec((1, D), lambda b, t, p: (0, 0))],
            out_specs=pl.BlockSpec((1, tr, D), lambda b, t, p: (b, t, 0)),
            scratch_shapes=[pltpu.VMEM((2, TOP_K, tr, D // LANE, LANE), F32), pltpu.SemaphoreType.DMA((2,))]),
        out_shape=jax.ShapeDtypeStruct((B, T, D), F32), name='combine',
        compiler_params=pltpu.CompilerParams(dimension_semantics=('arbitrary', 'arbitrary'),
                                             vmem_limit_bytes=VMEM_LIMIT),
    )(pos, yb, xs, gate_f, wts, norm_final.reshape(1, D))


def _assign_rows_kernel(idx_ref, rank_ref, start_ref, pos_ref):
    idx = idx_ref[...]
    tr = idx.shape[0]
    start = start_ref[...]
    lane = lax.broadcasted_iota(jnp.int32, (tr, start.shape[1]), 1)
    kcol = lax.broadcasted_iota(jnp.int32, (tr, TOP_K), 1)
    base = jnp.zeros((tr, TOP_K), F32)
    for k in range(TOP_K):
        sk = jnp.sum(jnp.where(lane == idx[:, k:k + 1], start, 0.0), axis=1, keepdims=True)
        base = jnp.where(kcol == k, sk, base)
    pos_ref[...] = base.astype(jnp.int32) + rank_ref[...]


def _assign_rows(idx, rank, start):
    n_tok = idx.shape[0]
    tspec = pl.BlockSpec((MOE_BLOCK, TOP_K), lambda i: (i, 0))
    return pl.pallas_call(
        _assign_rows_kernel, grid=(n_tok // MOE_BLOCK,),
        in_specs=[tspec, tspec, pl.BlockSpec(start.shape, lambda i: (0, 0))], out_specs=tspec,
        out_shape=jax.ShapeDtypeStruct((n_tok, TOP_K), jnp.int32), name='assign_rows',
        compiler_params=pltpu.CompilerParams(dimension_semantics=('parallel',), vmem_limit_bytes=VMEM_LIMIT),
    )(idx, rank, start)


def _route(scores, router_bias):
    n_tok = scores.shape[0]
    n_asg = n_tok * TOP_K
    idx, wts, rank, counts = _route_rank(scores, router_bias.astype(F32).reshape(1, N_EXPERTS))
    padded = (counts + MOE_BLOCK - 1) // MOE_BLOCK * MOE_BLOCK
    pad_end = jnp.cumsum(padded)
    pos = _assign_rows(idx, rank, (pad_end - padded).astype(F32).reshape(1, N_EXPERTS))
    n_blocks = -(-(n_asg + N_EXPERTS * (MOE_BLOCK - 1)) // MOE_BLOCK)
    n_rows = n_blocks * MOE_BLOCK
    tok = jnp.broadcast_to(jnp.arange(n_tok, dtype=jnp.int32)[:, None], pos.shape)
    row_tok = jnp.zeros((n_rows,), jnp.int32).at[pos.reshape(-1)].set(tok.reshape(-1))
    blk_exp = jnp.minimum(jnp.searchsorted(pad_end, jnp.arange(n_blocks) * MOE_BLOCK, side='right'),
                          N_EXPERTS - 1).astype(jnp.int32)
    nused = (pad_end[-1] // MOE_BLOCK).astype(jnp.int32).reshape(1)
    return wts, pos, row_tok, blk_exp, nused


def _route_kernel(sc_ref, rb_ref, idx_ref, wts_ref, rank_ref, cnt_ref, carry):
    i = pl.program_id(0)

    @pl.when(i == 0)
    def _():
        carry[...] = jnp.zeros(carry.shape, F32)

    tr, ne = sc_ref.shape
    gsz = ne // N_GROUPS
    scores = sc_ref[...]
    biased = scores + rb_ref[...]
    lane = lax.broadcasted_iota(jnp.int32, (tr, ne), 1)
    lane_f = lane.astype(F32)
    lgrp = lane // gsz
    ninf = -jnp.inf

    def first_max(x):
        mx = jnp.max(x, axis=1, keepdims=True)
        return mx, jnp.min(jnp.where(x == mx, lane_f, float(ne)), axis=1, keepdims=True)

    gl = lax.broadcasted_iota(jnp.int32, (tr, LANE), 1)
    gscore = jnp.full((tr, LANE), ninf, F32)
    for g in range(N_GROUPS):
        xg = jnp.where(lgrp == g, biased, ninf)
        m1, f1 = first_max(xg)
        m2 = jnp.max(jnp.where(lane_f == f1, ninf, xg), axis=1, keepdims=True)
        gscore = jnp.where(gl == g, m1 + m2, gscore)
    gl_f = gl.astype(F32)
    keep = jnp.full((tr, ne), NEG, F32)
    for _ in range(TOP_GROUPS):
        mx = jnp.max(gscore, axis=1, keepdims=True)
        gf = jnp.min(jnp.where(gscore == mx, gl_f, float(LANE)), axis=1, keepdims=True)
        gscore = jnp.where(gl_f == gf, ninf, gscore)
        keep = jnp.where(lgrp.astype(F32) == gf, biased, keep)
    kcol = lax.broadcasted_iota(jnp.int32, (tr, TOP_K), 1)
    idx = jnp.zeros((tr, TOP_K), F32)
    wts = jnp.zeros((tr, TOP_K), F32)
    hot = []
    onehot = jnp.zeros((tr, ne), F32)
    for k in range(TOP_K):
        _, f = first_max(keep)
        hit = lane_f == f
        hot.append(hit)
        keep = jnp.where(hit, ninf, keep)
        onehot = jnp.where(hit, 1.0, onehot)
        idx = jnp.where(kcol == k, f, idx)
        wts = jnp.where(kcol == k, jnp.sum(jnp.where(hit, scores, 0.0), axis=1, keepdims=True), wts)
    idx_ref[...] = idx.astype(jnp.int32)
    wts_ref[...] = wts / jnp.sum(wts, axis=1, keepdims=True) * ROUTED_SCALE
    r_i = lax.broadcasted_iota(jnp.int32, (tr, tr), 0)
    c_i = lax.broadcasted_iota(jnp.int32, (tr, tr), 1)
    lower = jnp.where(c_i < r_i, 1.0, 0.0).astype(BF16)
    before = _dot(lower, onehot.astype(BF16)) + carry[0:1, :]
    rank = jnp.zeros((tr, TOP_K), F32)
    for k in range(TOP_K):
        rank = jnp.where(kcol == k, jnp.sum(jnp.where(hot[k], before, 0.0), axis=1, keepdims=True), rank)
    rank_ref[...] = rank.astype(jnp.int32)
    total = carry[0:1, :] + jnp.sum(onehot, axis=0, keepdims=True)
    carry[...] = jnp.broadcast_to(total, carry.shape)
    cnt_ref[...] = jnp.broadcast_to(total, cnt_ref.shape).astype(jnp.int32)


def _route_rank(scores, router_bias):
    n_tok, ne = scores.shape
    tr = MOE_BLOCK
    tk = lambda dt: jax.ShapeDtypeStruct((n_tok, TOP_K), dt)
    tspec = pl.BlockSpec((tr, TOP_K), lambda i: (i, 0))
    idx, wts, rank, cnt = pl.pallas_call(
        _route_kernel, grid=(n_tok // tr,),
        in_specs=[pl.BlockSpec((tr, ne), lambda i: (i, 0)), pl.BlockSpec((1, ne), lambda i: (0, 0))],
        out_specs=[tspec, tspec, tspec, pl.BlockSpec((8, ne), lambda i: (0, 0))],
        out_shape=[tk(jnp.int32), tk(F32), tk(jnp.int32), jax.ShapeDtypeStruct((8, ne), jnp.int32)],
        scratch_shapes=[pltpu.VMEM((8, ne), F32)], name='route',
        compiler_params=pltpu.CompilerParams(dimension_semantics=('arbitrary',), vmem_limit_bytes=VMEM_LIMIT),
    )(scores, router_bias)
    return idx, wts, rank, cnt[0]


PAGE = 128


def _softmax_with_new(s, s_new):
    m = jnp.maximum(jnp.max(s, axis=1, keepdims=True), s_new)
    p = jnp.exp(s - m)
    pn = jnp.exp(s_new - m)
    return p, pn, 1.0 / (jnp.sum(p, axis=1, keepdims=True) + pn)


def _samp_cmp_kernel(pt_ref, pool_hbm, q_ref, wbd_ref, pe_ref, w1c_ref, w2_ref, bc_ref, pf_ref,
                     oc_ref, idx_ref, buf_t, buf, peh_scr, sem, *, n_pages, n_cmp):
    s = pl.program_id(0)
    ns = pl.num_programs(0)
    nc = n_pages * PAGE // CMP_STRIDE

    def page_copy(page, slot, p):
        return pltpu.make_async_copy(pool_hbm.at[page], buf_t.at[slot, :, :, pl.ds(p * PAGE, PAGE)], sem.at[slot])

    def start(smp, slot):
        def body(p, c):
            page_copy(pt_ref[smp, p], slot, p).start()
            return c
        lax.fori_loop(0, n_pages, body, 0)

    @pl.when(s == 0)
    def _():
        start(0, 0)
        for j in range(2):
            pp = _dot(pe_ref[j], w1c_ref[j])
            peh_scr[j] = jnp.broadcast_to(pp[0:1, 0:CMP_HIDDEN] + pp[1:2, CMP_HIDDEN:], (8, CMP_HIDDEN))

    @pl.when(s + 1 < ns)
    def _():
        start(s + 1, (s + 1) % 2)

    slot = s % 2

    def wait_page(p, c):
        page_copy(0, slot, p).wait()
        return c
    lax.fori_loop(0, n_pages, wait_page, 0)

    tw = 4 * PAGE

    def to_rows(c, carry):
        l0 = pl.multiple_of(c * tw, tw)
        for j in range(2):
            xt = buf_t[slot, pl.ds(2 * j, 2), :, pl.ds(l0, tw)].reshape(2 * HEAD_DIM, tw)
            buf[j, pl.ds(l0, tw), :] = xt.T
        return carry
    lax.fori_loop(0, n_pages * PAGE // tw, to_rows, 0)

    rows = lax.broadcasted_iota(jnp.int32, (nc, LANE), 0)
    kvc = []
    for j in range(2):
        acc = jnp.zeros((nc, 4 * CMP_HIDDEN), F32)
        for r in range(0, CMP_STRIDE, 2):
            x = jnp.concatenate([buf[j, pl.ds(r, nc, stride=CMP_STRIDE), :],
                                 buf[j, pl.ds(r + 1, nc, stride=CMP_STRIDE), :]], axis=1).astype(BF16)
            acc = acc + _dot(x, wbd_ref[j, r // 2])
        out = jnp.zeros((nc, LANE), F32)
        for g in range(N_KV_A):
            a = acc[:, g * 2 * CMP_HIDDEN:g * 2 * CMP_HIDDEN + CMP_HIDDEN]
            bm = acc[:, g * 2 * CMP_HIDDEN + CMP_HIDDEN:(g + 1) * 2 * CMP_HIDDEN]
            hid = a + pltpu.roll(bm, nc - 1, 0) + peh_scr[j, 0:1]
            act = hid * (1.0 / (1.0 + jnp.exp(-hid)))
            out = out + _dot(act.astype(BF16), w2_ref[j, g])
        kvc.append(jnp.where(rows < n_cmp, out, 0.0).astype(BF16))
    kc, vc = kvc

    q8 = q_ref[0]
    sc = _dot_t(q8, kc) + bc_ref[...]
    p = jnp.exp(sc - jnp.max(sc, axis=1, keepdims=True))
    p = p * (1.0 / jnp.sum(p, axis=1, keepdims=True))
    oc_ref[0] = _dot(p.astype(BF16), vc)

    pg = jnp.concatenate([jnp.sum(p[0:HPG], axis=0, keepdims=True), jnp.sum(p[HPG:2 * HPG], axis=0, keepdims=True),
                          jnp.zeros((8 - N_KV_A, nc), F32)], axis=0)
    imp = _hilo_dot(pg, pf_ref[...])
    lane = lax.broadcasted_iota(jnp.int32, (8, LANE), 1).astype(F32)
    score = jnp.where(lane == 0.0, FORCED, jnp.where(lane >= float(LANE - N_LOCAL_SLC + 1), FORCED, imp))
    picks = jnp.full((8, LANE), float(LANE), F32)
    for k in range(N_SLC - 1):
        mx = jnp.max(score, axis=1, keepdims=True)
        first = jnp.min(jnp.where(score == mx, lane, float(LANE)), axis=1, keepdims=True)
        picks = jnp.where(lane == float(k), first, picks)
        score = jnp.where(lane == first, -jnp.inf, score)
    idx_ref[0] = picks.astype(jnp.int32)


def _samp_cmp(page_table, pool2d, q3, scw, bc, n_cmp):
    Bd, n_pages = page_table.shape
    nc = n_pages * PAGE // CMP_STRIDE
    wbd, pe, w1c, w2g = scw
    m = jnp.arange(nc)
    pool_m = ((m[:, None] // SLC_PER_CMP == jnp.arange(LANE)[None, :]) & (m[:, None] < n_cmp)).astype(BF16)
    one = pl.Buffered(1)
    cst = lambda a: pl.BlockSpec(a.shape, lambda s, pt: (0,) * a.ndim, pipeline_mode=one)
    return pl.pallas_call(
        functools.partial(_samp_cmp_kernel, n_pages=n_pages, n_cmp=n_cmp),
        grid_spec=pltpu.PrefetchScalarGridSpec(
            num_scalar_prefetch=1, grid=(Bd,),
            in_specs=[pl.BlockSpec(memory_space=pl.ANY), pl.BlockSpec((1, 8, LANE), lambda s, pt: (s, 0, 0)),
                      cst(wbd), cst(pe), cst(w1c), cst(w2g), cst(bc), cst(pool_m)],
            out_specs=[pl.BlockSpec((1, 8, LANE), lambda s, pt: (s, 0, 0)),
                       pl.BlockSpec((1, 8, LANE), lambda s, pt: (s, 0, 0))],
            scratch_shapes=[pltpu.VMEM((2, 4, HEAD_DIM, n_pages * PAGE), F32),
                            pltpu.VMEM((2, n_pages * PAGE, LANE), F32), pltpu.VMEM((2, 8, CMP_HIDDEN), F32),
                            pltpu.SemaphoreType.DMA((2,))]),
        out_shape=[jax.ShapeDtypeStruct((Bd, 8, LANE), F32), jax.ShapeDtypeStruct((Bd, 8, LANE), jnp.int32)],
        name='sample_cmp',
        compiler_params=pltpu.CompilerParams(dimension_semantics=('arbitrary',), vmem_limit_bytes=VMEM_LIMIT),
    )(page_table, pool2d, q3, wbd, pe, w1c, w2g, bc, pool_m)


def _samp_cmp_weights(cw):
    w1cat, pe, w2a, w2b = cw
    w = w1cat.reshape(2, CMP_STRIDE, HEAD_DIM, 2 * CMP_HIDDEN)
    z = jnp.zeros_like(w)
    wbd = jnp.concatenate([jnp.concatenate([w, z], axis=-1), jnp.concatenate([z, w], axis=-1)], axis=2)
    wbd = wbd.reshape(2, CMP_STRIDE // 2, 2 * LANE, 4 * CMP_HIDDEN)
    return wbd, pe, w1cat, jnp.stack([w2a, w2b], axis=1)


def _t5_bias_rows(rel_bias, rel, valid):
    b = rel_bias.astype(F32)[_t5_bucket(rel)]
    return jnp.where(valid[None, :], b.T, NEG)


def _bucket_bias(rel, tbl_t):
    max_exact = N_BUCKETS // 2
    nf = jnp.maximum(rel, 1).astype(F32)
    large = max_exact + (jnp.log(nf / max_exact) / math.log(MAX_DISTANCE / max_exact)
                         * (N_BUCKETS - max_exact)).astype(jnp.int32)
    bucket = jnp.where(rel < max_exact, rel, jnp.minimum(large, N_BUCKETS - 1))
    bias = jnp.zeros(rel.shape, F32)
    for b in range(N_BUCKETS):
        bias = jnp.where(bucket == b, tbl_t[:, b:b + 1], bias)
    return bias


def _samp_sw_kernel(pt_ref, idx_ref, pool_hbm, q_ref, knew_ref, win_ref, wnew_ref, wcol_ref, oc_ref, gcol_ref,
                    tblt_ref, bw_ref, oa_ref, nwin_ref, kvbuf, sem, *, past_len):
    s = pl.program_id(0)
    ns = pl.num_programs(0)
    npb = past_len // SLC_BLOCK
    bpp = PAGE // SLC_BLOCK
    nk = N_SLC * PAGE

    def block_copy(page, slot, g, k, kv):
        return pltpu.make_async_copy(pool_hbm.at[page, kv * N_KV_A + g],
                                     kvbuf.at[slot, kv * N_KV_A + g, :, pl.ds(k * PAGE, PAGE)], sem.at[slot])

    def start(smp, slot):
        for g in range(N_KV_A):
            for k in range(N_SLC):
                j = jnp.minimum(idx_ref[(smp * N_KV_A + g) * N_SLC + k], npb - 1)
                page = pt_ref[smp, j // bpp]
                for kv in range(2):
                    block_copy(page, slot, g, k, kv).start()

    @pl.when(s == 0)
    def _():
        start(0, 0)

    @pl.when(s + 1 < ns)
    def _():
        start(s + 1, (s + 1) % 2)

    slot = s % 2
    q8 = q_ref[0]
    q32 = q8.astype(F32)
    qg = [q8[:, g * HEAD_DIM:(g + 1) * HEAD_DIM] for g in range(N_KV_A)]
    tbl_t = tblt_ref[...]
    row = lax.broadcasted_iota(jnp.int32, (8, LANE), 0)
    lane = lax.broadcasted_iota(jnp.int32, (8, LANE), 1)
    grp0 = row[:, 0:1] < HPG

    def new_token(kv_row):
        kn = kv_row[:, 0:LANE].astype(BF16).astype(F32)
        vn = kv_row[:, LANE:].astype(BF16).astype(F32)
        return jnp.sum(q32 * kn, axis=1, keepdims=True) + tbl_t[:, 0:1], vn

    def by_group(a0, a1):
        return jnp.concatenate([jnp.where(grp0, a0, 0.0), jnp.where(grp0, 0.0, a1)], axis=1)

    w = win_ref[0]
    wl = w.shape[-1]
    s_new, v_new = new_token(wnew_ref[0])
    sw = jnp.where(grp0, _dot(qg[0], w[0].astype(BF16)), _dot(qg[1], w[1].astype(BF16))) + bw_ref[...]
    p, pn, inv = _softmax_with_new(sw, s_new)
    pb = p.astype(BF16)
    o_w = (by_group(_dot_t(pb, w[2].astype(BF16)), _dot_t(pb, w[3].astype(BF16))) + pn * v_new) * inv
    wcol = lax.broadcasted_iota(jnp.int32, (HEAD_DIM, wl), 1)
    for c in range(2 * N_KV_A):
        nwin_ref[0, c] = jnp.where(wcol == wl - 1, wcol_ref[0, c], pltpu.roll(w[c], wl - 1, 1))

    for g in range(N_KV_A):
        for k in range(N_SLC):
            for kv in range(2):
                block_copy(0, slot, g, k, kv).wait()
    kl = lax.broadcasted_iota(jnp.int32, (8, nk), 1)
    kslot = kl >> 7
    kin = kl & (PAGE - 1)
    rowk = lax.broadcasted_iota(jnp.int32, (8, nk), 0) < HPG
    blk = jnp.zeros((8, nk), jnp.int32)
    for k in range(N_SLC):
        j0 = idx_ref[(s * N_KV_A) * N_SLC + k]
        j1 = idx_ref[(s * N_KV_A + 1) * N_SLC + k]
        blk = jnp.where(kslot == k, jnp.where(rowk, j0, j1), blk)
    rel = past_len - ((blk // bpp) * PAGE + kin)
    ok = (kin // SLC_BLOCK) == jnp.where(blk < npb, blk % bpp, -1)
    ss = jnp.where(rowk, _dot(qg[0], kvbuf[slot, 0].astype(BF16)), _dot(qg[1], kvbuf[slot, 1].astype(BF16)))
    ss = jnp.where(ok, ss + _bucket_bias(jnp.maximum(rel, 0), tbl_t), NEG)
    s_new, v_new = new_token(knew_ref[0])
    p, pn, inv = _softmax_with_new(ss, s_new)
    pb = p.astype(BF16)
    o_s = (by_group(_dot_t(pb, kvbuf[slot, 2].astype(BF16)), _dot_t(pb, kvbuf[slot, 3].astype(BF16)))
           + pn * v_new) * inv

    gc = gcol_ref[0]
    o = gc[:, 0:1] * oc_ref[0] + gc[:, 1:2] * o_s + gc[:, 2:3] * o_w
    oa_ref[0] = jnp.where((lane >= HEAD_DIM) == (row >= HPG), o, 0.0)


def _samp_sw(page_table, idx_flat, pool_t, q3, knew, win_t, wnew, wcol, o_c, gcol, tbl_t, bw, past_len):
    Bd = page_table.shape[0]
    wl = win_t.shape[-1]
    per = lambda shp: pl.BlockSpec((1,) + shp, lambda s, pt, ix: (s,) + (0,) * len(shp))
    cst = lambda a: pl.BlockSpec(a.shape, lambda s, pt, ix: (0,) * a.ndim)
    return pl.pallas_call(
        functools.partial(_samp_sw_kernel, past_len=past_len),
        grid_spec=pltpu.PrefetchScalarGridSpec(
            num_scalar_prefetch=2, grid=(Bd,),
            in_specs=[pl.BlockSpec(memory_space=pl.ANY), per((8, LANE)), per((1, 2 * LANE)),
                      per((2 * N_KV_A, HEAD_DIM, wl)), per((1, 2 * LANE)), per((2 * N_KV_A, HEAD_DIM, 1)),
                      per((8, LANE)), per((8, LANE)), cst(tbl_t), cst(bw)],
            out_specs=[per((8, LANE)), per((2 * N_KV_A, HEAD_DIM, wl))],
            scratch_shapes=[pltpu.VMEM((2, 2 * N_KV_A, HEAD_DIM, N_SLC * PAGE), F32),
                            pltpu.SemaphoreType.DMA((2,))]),
        out_shape=[jax.ShapeDtypeStruct((Bd, 8, LANE), F32), jax.ShapeDtypeStruct(win_t.shape, F32)],
        name='sample_slc_win',
        compiler_params=pltpu.CompilerParams(dimension_semantics=('arbitrary',), vmem_limit_bytes=VMEM_LIMIT),
    )(page_table, idx_flat, pool_t, q3, knew, win_t, wnew, wcol, o_c, gcol, tbl_t, bw)


def _samp_mla_kernel(pt_ref, pool_hbm, q_ref, lnew_ref, o_ref, buf, sem, *, n_pages):
    s = pl.program_id(0)
    ns = pl.num_programs(0)

    def page_copy(page, slot, p):
        return pltpu.make_async_copy(pool_hbm.at[page], buf.at[slot, :, pl.ds(p * PAGE, PAGE)], sem.at[slot])

    def start(smp, slot):
        def body(p, c):
            page_copy(pt_ref[smp, p], slot, p).start()
            return c
        lax.fori_loop(0, n_pages, body, 0)

    @pl.when(s == 0)
    def _():
        start(0, 0)

    @pl.when(s + 1 < ns)
    def _():
        start(s + 1, (s + 1) % 2)

    slot = s % 2

    def wait_page(p, c):
        page_copy(0, slot, p).wait()
        return c
    lax.fori_loop(0, n_pages, wait_page, 0)

    q8 = q_ref[0]
    ckv_t = buf[slot, 0:KV_LORA, :].astype(BF16)
    kr_t = buf[slot, KV_LORA:LATENT_DIM, :].astype(BF16)
    ln = lnew_ref[0].astype(F32)
    sc = _dot(q8[:, 0:KV_LORA], ckv_t) + _dot(q8[:, KV_LORA:LATENT_DIM], kr_t)
    s_new = jnp.sum(q8.astype(F32) * ln, axis=1, keepdims=True)
    p, pn, inv = _softmax_with_new(sc, s_new)
    o_ref[0] = (_dot_t(p.astype(BF16), ckv_t) + pn * ln[:, 0:KV_LORA]) * inv


def _samp_mla(page_table, pool2d, qm3, lnew):
    Bd, n_pages = page_table.shape
    per = lambda shp: pl.BlockSpec((1,) + shp, lambda s, pt: (s, 0, 0))
    return pl.pallas_call(
        functools.partial(_samp_mla_kernel, n_pages=n_pages),
        grid_spec=pltpu.PrefetchScalarGridSpec(
            num_scalar_prefetch=1, grid=(Bd,),
            in_specs=[pl.BlockSpec(memory_space=pl.ANY), per((8, 256)), per((1, 256))],
            out_specs=per((8, KV_LORA)),
            scratch_shapes=[pltpu.VMEM((2, LATENT_DIM, n_pages * PAGE), F32), pltpu.SemaphoreType.DMA((2,))]),
        out_shape=jax.ShapeDtypeStruct((Bd, 8, KV_LORA), F32), name='sample_mla',
        compiler_params=pltpu.CompilerParams(dimension_semantics=('arbitrary',), vmem_limit_bytes=VMEM_LIMIT),
    )(page_table, pool2d, qm3, lnew)


def _uv_kernel(o_ref, w_ref, y_ref):
    y_ref[...] = _dot(o_ref[...].astype(BF16), w_ref[0])


def _samp_uv(o_lat2d, w_uv):
    Bd = o_lat2d.shape[0]
    return pl.pallas_call(
        _uv_kernel, grid=(N_HEADS_B,),
        in_specs=[pl.BlockSpec((Bd, LANE), lambda h: (0, h)), pl.BlockSpec((1, KV_LORA, LANE), lambda h: (h, 0, 0))],
        out_specs=pl.BlockSpec((Bd, LANE), lambda h: (0, h)),
        out_shape=jax.ShapeDtypeStruct((Bd, N_HEADS_B * LANE), F32), name='sample_uv',
        compiler_params=pltpu.CompilerParams(dimension_semantics=('parallel',), vmem_limit_bytes=VMEM_LIMIT),
    )(o_lat2d, w_uv)


def _sample_mix(xs3, msm, pool_cmp, pool_slc, win_buf, pool_mla, page_table, gain, rel_bias, w, cw):
    Bd = xs3.shape[1]
    past_len = page_table.shape[1] * PAGE
    cos_s, sin_s = _rope_tables(jnp.full((Bd,), past_len, jnp.int32))
    qa_s, cmp_s, slc_s, win_s, _, gt_s, qm_s, lat_s, lat16_s = _inproj(xs3, msm[0], msm[1], gain, cos_s, sin_s, w, Bd)
    q3 = qa_s.reshape(Bd, N_HEADS_A, LANE)
    nc = past_len // CMP_STRIDE
    n_cmp = (past_len + 1 - CMP_BLOCK) // CMP_STRIDE + 1
    m = jnp.arange(nc)
    bc = _t5_bias_rows(rel_bias, past_len - (m * CMP_STRIDE + CMP_BLOCK - 1), m < n_cmp)
    fm = lambda a: a.transpose(0, 2, 3, 4, 1).reshape(a.shape[0], 2 * N_KV_A, HEAD_DIM, a.shape[1])
    o_c, idx = _samp_cmp(page_table, fm(pool_cmp), q3, _samp_cmp_weights(cw), bc, n_cmp)
    idx_flat = idx[:, :N_KV_A, :N_SLC].reshape(-1)
    wl = win_buf.shape[1]
    wi = jnp.arange(wl)
    bw = _t5_bias_rows(rel_bias, wl - wi, (wl - wi < WINDOW) & (past_len - wl + wi >= 0))
    tbl_t = jnp.pad(rel_bias.astype(F32).T, ((0, 0), (0, LANE - N_BUCKETS)))
    gcol = gt_s[0, :, :3 * N_HEADS_A].reshape(Bd, 3, N_HEADS_A).transpose(0, 2, 1)
    gcol = jnp.pad(gcol, ((0, 0), (0, 0), (0, LANE - 3)))
    oa_s, new_win = _samp_sw(page_table, idx_flat, fm(pool_slc), q3, slc_s.reshape(Bd, 1, 2 * LANE), fm(win_buf),
                             win_s.reshape(Bd, 1, 2 * LANE), win_s.reshape(Bd, 2 * N_KV_A, HEAD_DIM, 1), o_c, gcol,
                             tbl_t, bw, past_len)
    o_lat = _samp_mla(page_table, pool_mla.transpose(0, 2, 1), qm_s.reshape(Bd, N_HEADS_B, 256),
                      lat16_s.reshape(Bd, 1, 256))
    ob_s = _samp_uv(o_lat.reshape(Bd, N_HEADS_B * KV_LORA), w['w_uv'])
    new_win = new_win.reshape(Bd, 2, N_KV_A, HEAD_DIM, wl).transpose(0, 4, 1, 2, 3)
    return oa_s.reshape(1, Bd, -1), ob_s.reshape(1, Bd, -1), cmp_s, slc_s, new_win, lat_s


def kernel(x_prompt, x_sample, cache_nsa_cmp, cache_nsa_slc, cache_nsa_win, cache_mla, page_table, c_prompt, c_sample, rel_bias, w_ada, b_ada, norm_attn, norm_ffn, w_in, cmp_pe, cmp_w1, cmp_w2, q_norm, w_q_up, kv_norm, w_kv_up, out_norm_a, out_norm_b, w_out, w_router, router_bias, w_gate_e, w_up_e, w_down_e, w_gate_s, w_up_s, w_down_s, norm_final):
    B, S, D = x_prompt.shape
    Bd = x_sample.shape[0]
    l = 0
    n_mod = B + Bd
    c_all = jnp.pad(jnp.concatenate([c_prompt, c_sample], axis=0), ((0, -n_mod % 8), (0, 0)))
    mod = _adaln(c_all, w_ada[l], b_ada[l]).reshape(-1, 6, D)
    mp = [mod[:B, i][:, None, :] for i in range(6)]
    msm = [mod[B:n_mod, i][None] for i in range(6)]

    w = _inproj_weights(w_in[l], q_norm[l], w_q_up[l], kv_norm[l], w_kv_up[l])
    cw = _compress_weights(cmp_pe[l], cmp_w1[l], cmp_w2[l])
    mw = _merge_weights(out_norm_a[l], out_norm_b[l], w_out[l], norm_ffn[l], w_router[l], w_gate_s[l], w_up_s[l],
                        w_down_s[l])
    tb, lb = _bias_tables(rel_bias)

    cos, sin = _rope_tables(jnp.arange(S))
    qa, cmp32, slc32, win32, kv16, gt, qm, lat32, lat16 = _inproj(
        x_prompt, mp[0], mp[1], norm_attn[l], cos, sin, w, 256)
    nc = S // CMP_STRIDE
    n_cmp = (S - CMP_BLOCK) // CMP_STRIDE + 1
    xc = cmp32.reshape(B, nc, CMP_STRIDE, 4, HEAD_DIM).transpose(0, 3, 1, 2, 4).reshape(B, 4, nc, -1).astype(BF16)
    kcv = _compress(xc, cw, n_cmp)
    oa_p = _nsa_prompt(qa, gt, kv16, kcv, tb, lb)
    ob_p = _mla_prompt(qm, lat16, w['w_uv'])
    xs_p, f_p, sc_p = _merge(x_prompt, oa_p, ob_p, mp[2], mp[3], mp[4], mp[5], mw, 256)

    xs3 = x_sample.reshape(1, Bd, D)
    oa_s, ob_s, cmp_s, slc_s, new_win, lat_s = _sample_mix(
        xs3, msm, cache_nsa_cmp[l], cache_nsa_slc[l], cache_nsa_win[l], cache_mla[l], page_table, norm_attn[l],
        rel_bias, w, cw)
    xs_s, f_s, sc_s = _merge(xs3, oa_s, ob_s, msm[2], msm[3], msm[4], msm[5], mw, Bd)

    n_p = B * S
    f_all = jnp.concatenate([f_p.reshape(n_p, D // LANE, LANE), f_s.reshape(Bd, D // LANE, LANE)], axis=0)
    sc_all = jnp.concatenate([sc_p.reshape(n_p, N_EXPERTS), sc_s.reshape(Bd, N_EXPERTS)], axis=0)
    wts, pos, row_tok, blk_exp, nused = _route(sc_all, router_bias[l])
    yb = _moe_experts(nused, blk_exp, row_tok, f_all, w_gate_e[l], w_up_e[l], w_down_e[l])
    tile_pos = lambda p: p.reshape(-1, MOE_BLOCK, TOP_K).transpose(0, 2, 1).reshape(-1, MOE_BLOCK)
    y_p = _combine(tile_pos(pos[:n_p]), yb, xs_p, mp[5], wts[:n_p].reshape(B, S, TOP_K), norm_final)
    y_s = _combine(tile_pos(pos[n_p:]), yb, xs_s, msm[5], wts[n_p:].reshape(1, Bd, TOP_K), norm_final)

    sh6 = lambda a, b, t: a.reshape(1, b, t, 2, N_KV_A, HEAD_DIM)
    return (y_p, y_s.reshape(Bd, 1, D), sh6(cmp32, B, S), sh6(cmp_s, Bd, 1), sh6(slc32, B, S), sh6(slc_s, Bd, 1),
            sh6(win32[:, S - WINDOW:], B, WINDOW), new_win[None], lat32[None], lat_s.reshape(1, Bd, 1, LATENT_DIM))
```

```python
import functools
import math

import jax
import jax.numpy as jnp
from jax import lax
from jax.experimental import pallas as pl
from jax.experimental.pallas import tpu as pltpu

F32 = jnp.float32
BF16 = jnp.bfloat16

LANE = 128
VMEM_LIMIT = 56 * 1024 * 1024

HEAD_DIM = 64
N_HEADS_A = 8
N_KV_A = 2
HPG = N_HEADS_A // N_KV_A
CMP_BLOCK = 32
CMP_STRIDE = 16
CMP_HIDDEN = 128
SLC_BLOCK = 64
SLC_PER_CMP = SLC_BLOCK // CMP_STRIDE
N_SLC = 16
N_LOCAL_SLC = 2
WINDOW = 512
N_HEADS_B = 8
Q_LORA = 192
KV_LORA = 128
QK_NOPE = 64
QK_ROPE = 32
V_DIM = 64
LATENT_DIM = KV_LORA + QK_ROPE
ROPE_THETA = 10000.0
MLA_SCALE = (QK_NOPE + QK_ROPE) ** -0.5
N_BUCKETS = 32
MAX_DISTANCE = 128
N_EXPERTS = 256
TOP_K = 8
N_GROUPS = 8
TOP_GROUPS = 4
ROUTED_SCALE = 2.5
MOE_BLOCK = 128
EPS = 1e-6
NEG = -1e30
FORCED = 1e30

QB = 128
KT_NSA = 1024
KT_MLA = 2048
CMP_PAD = 16
LOC_W = 24


def _dot(a, b):
    return jnp.dot(a, b, preferred_element_type=F32)


def _dot_t(a, b):
    return lax.dot_general(a, b, (((1,), (1,)), ((), ())), preferred_element_type=F32)


def _lane_tiles(x):
    return [x[:, c * LANE:(c + 1) * LANE] for c in range(x.shape[1] // LANE)]


def _row_max(x):
    return jnp.max(functools.reduce(jnp.maximum, _lane_tiles(x)), axis=1, keepdims=True)


def _row_sum(x):
    return jnp.sum(functools.reduce(jnp.add, _lane_tiles(x)), axis=1, keepdims=True)


HALF_ROWS = 64
WORD_TILES = 4


def _pack_rows(x, ref, lead):
    xb = pltpu.bitcast(x.astype(BF16).astype(F32), jnp.uint32)
    for c in range(WORD_TILES):
        w = (xb[:, c * LANE:(c + 1) * LANE] >> 16) | (xb[:, (WORD_TILES + c) * LANE:(WORD_TILES + c + 1) * LANE]
                                                       & jnp.uint32(0xFFFF0000))
        for g in range(x.shape[0] // (2 * HALF_ROWS)):
            for b in range(2):
                r0 = g * 2 * HALF_ROWS + b * HALF_ROWS
                ref[lead + (slice(g * HALF_ROWS, (g + 1) * HALF_ROWS), WORD_TILES * b + c, slice(None))] = \
                    w[r0:r0 + HALF_ROWS]


def _unpack_rows(ref, lead):
    lo, hi = [], []
    for c in range(WORD_TILES):
        w = jnp.concatenate([ref[lead + (slice(None), c, slice(None))],
                             ref[lead + (slice(None), WORD_TILES + c, slice(None))]], axis=0)
        lo.append(pltpu.bitcast(w << 16, F32))
        hi.append(pltpu.bitcast(w & jnp.uint32(0xFFFF0000), F32))
    return lo, hi


def _packed_row_addr(r):
    return ((r >> 7) << 6) + (r & (HALF_ROWS - 1)), ((r >> 6) & 1) * WORD_TILES


def _const_spec(shape):
    nd = len(shape)
    return pl.BlockSpec(shape, lambda *_: (0,) * nd)


def _inproj_kernel(x_ref, sh_ref, sc_ref, g_ref, cs_ref, sn_ref, wq_ref, wkv_ref, wg_ref, wqd_ref,
                   wkvd_ref, qn_ref, wqup_ref, bd_ref, plc_ref, kvn_ref,
                   qa_ref, cmp_ref, slc_ref, win_ref, kv16_ref, gt_ref, qm_ref, lat_ref, lat16_ref):
    x = x_ref[0]
    ms = jnp.mean(x * x, axis=-1, keepdims=True)
    xn = x * lax.rsqrt(ms + EPS) * g_ref[...]
    h = xn * (1.0 + sc_ref[0]) + sh_ref[0]
    hb = h.astype(BF16)
    qa_ref[0] = _dot(hb, wq_ref[...]).astype(BF16)
    kv = _dot(hb, wkv_ref[...])
    cmp_ref[0] = kv[:, 0:256]
    slc_ref[0] = kv[:, 256:512]
    win_ref[0] = kv[:, 512:768]
    kv16_ref[0] = kv.astype(BF16)
    gl = _dot(hb, wg_ref[...])
    gt_ref[0] = 1.0 / (1.0 + jnp.exp(-gl))
    qd = _dot(hb, wqd_ref[...])
    qn = qd * lax.rsqrt(jnp.sum(qd * qd, axis=-1, keepdims=True) * (1.0 / Q_LORA) + EPS) * qn_ref[...]
    qu = _dot(qn.astype(BF16), wqup_ref[...])
    cs = cs_ref[...]
    sn = sn_ref[...]
    qr = qu[:, 512:768] * cs + qu[:, 768:1024] * sn
    qm = _dot(qu[:, 0:512].astype(BF16), bd_ref[...]) + _dot(qr.astype(BF16), plc_ref[...])
    qm_ref[0] = (qm * MLA_SCALE).astype(BF16)
    kvd = _dot(hb, wkvd_ref[...])
    c = kvd[:, 0:128]
    ckv = c * lax.rsqrt(jnp.mean(c * c, axis=-1, keepdims=True) + EPS) * kvn_ref[...]
    kr = kvd[:, 128:256] * cs[:, 0:128] + kvd[:, 256:384] * sn[:, 0:128]
    lat_ref[0, :, 0:128] = ckv
    lat_ref[0, :, 128:160] = kr[:, 0:32]
    lat16_ref[0, :, 0:128] = ckv.astype(BF16)
    lat16_ref[0, :, 128:256] = kr.astype(BF16)


def _inproj_weights(w_in, q_norm, w_q_up, kv_norm, w_kv_up):
    D = w_in.shape[0]
    o1 = N_HEADS_A * HEAD_DIM
    o2 = o1 + 6 * N_KV_A * HEAD_DIM
    o3 = o2 + 3 * N_HEADS_A
    o4 = o3 + Q_LORA
    wq = w_in[:, :o1].reshape(D, N_HEADS_A, HEAD_DIM) * (HEAD_DIM ** -0.5)
    z = jnp.zeros_like(wq)
    grp = (jnp.arange(N_HEADS_A) // HPG)[None, :, None]
    wq_pad = jnp.concatenate([jnp.where(grp == 0, wq, z), jnp.where(grp == 1, wq, z)], axis=-1)
    wq_pad = wq_pad.reshape(D, N_HEADS_A * 2 * HEAD_DIM)
    wkv = w_in[:, o1:o2]
    wg = jnp.pad(w_in[:, o2:o3], ((0, 0), (0, LANE - 3 * N_HEADS_A)))
    wqd = jnp.pad(w_in[:, o3:o4], ((0, 0), (0, 256 - Q_LORA)))
    wkd = w_in[:, o4:]
    half = QK_ROPE // 2
    wc = wkd[:, :KV_LORA]
    wr = wkd[:, KV_LORA:]
    wrot = jnp.concatenate([-wr[:, half:], wr[:, :half]], axis=1)
    padr = ((0, 0), (0, LANE - QK_ROPE))
    wkvd = jnp.concatenate([wc, jnp.pad(wr, padr), jnp.pad(wrot, padr)], axis=1)
    qn = jnp.pad(q_norm, (0, 256 - Q_LORA)).reshape(1, 256)
    wu = jnp.pad(w_q_up, ((0, 256 - Q_LORA), (0, 0))).reshape(256, N_HEADS_B, QK_NOPE + QK_ROPE)
    wu_n = wu[:, :, :QK_NOPE].reshape(256, N_HEADS_B * QK_NOPE)
    wu_r = wu[:, :, QK_NOPE:]
    wu_rot = jnp.concatenate([-wu_r[:, :, half:], wu_r[:, :, :half]], axis=-1)
    wqup = jnp.concatenate([wu_n, wu_r.reshape(256, -1), wu_rot.reshape(256, -1)], axis=1)
    w_ukv = w_kv_up.reshape(KV_LORA, N_HEADS_B, QK_NOPE + V_DIM)
    w_uk = w_ukv[:, :, :QK_NOPE]
    eye = jnp.eye(N_HEADS_B, dtype=F32)
    bd = jnp.einsum('chn,hk->hnkc', w_uk, eye)
    bd = jnp.pad(bd, ((0, 0), (0, 0), (0, 0), (0, 256 - KV_LORA))).reshape(N_HEADS_B * QK_NOPE, N_HEADS_B * 256)
    plc = jnp.einsum('hk,rs->hrks', eye, jnp.eye(QK_ROPE, dtype=F32))
    plc = jnp.pad(plc, ((0, 0), (0, 0), (0, 0), (KV_LORA, 256 - KV_LORA - QK_ROPE)))
    plc = plc.reshape(N_HEADS_B * QK_ROPE, N_HEADS_B * 256)
    w_uv = jnp.pad(w_ukv[:, :, QK_NOPE:].transpose(1, 0, 2), ((0, 0), (0, 0), (0, LANE - V_DIM)))
    bf = lambda a: a.astype(BF16)
    return dict(wq=bf(wq_pad), wkv=bf(wkv), wg=bf(wg), wqd=bf(wqd), wkvd=bf(wkvd), qn=qn, wqup=bf(wqup),
                bd=bf(bd), plc=bf(plc), kvn=kv_norm.reshape(1, KV_LORA), w_uv=bf(w_uv))


def _rope_tables(pos):
    half = QK_ROPE // 2
    inv = ROPE_THETA ** (-jnp.arange(half, dtype=F32) / half)
    ang = pos.astype(F32)[:, None] * inv[None, :]
    cos = jnp.tile(jnp.cos(ang), (1, 2 * N_HEADS_B))
    sin = jnp.tile(jnp.sin(ang), (1, 2 * N_HEADS_B))
    return cos, sin


def _inproj(x, shift, scale, gain, cos, sin, w, tr):
    B, T, D = x.shape
    tm = shift.shape[1]
    mod_spec = pl.BlockSpec((1, tr if tm > 1 else 1, D), (lambda b, t: (b, t, 0)) if tm > 1 else (lambda b, t: (b, 0, 0)))
    row = lambda n: pl.BlockSpec((1, tr, n), lambda b, t: (b, t, 0))
    tab = pl.BlockSpec((tr, 256), lambda b, t: (t, 0))
    wnames = ['wq', 'wkv', 'wg', 'wqd', 'wkvd', 'qn', 'wqup', 'bd', 'plc', 'kvn']
    out_shape = [
        jax.ShapeDtypeStruct((B, T, 1024), BF16), jax.ShapeDtypeStruct((B, T, 256), F32),
        jax.ShapeDtypeStruct((B, T, 256), F32), jax.ShapeDtypeStruct((B, T, 256), F32),
        jax.ShapeDtypeStruct((B, T, 768), BF16), jax.ShapeDtypeStruct((B, T, LANE), F32),
        jax.ShapeDtypeStruct((B, T, 2048), BF16), jax.ShapeDtypeStruct((B, T, LATENT_DIM), F32),
        jax.ShapeDtypeStruct((B, T, 256), BF16)]
    in_specs = [row(D), mod_spec, mod_spec, _const_spec((1, D)), tab, tab]
    in_specs += [_const_spec(w[n].shape) for n in wnames[:5]]
    in_specs += [_const_spec(w['qn'].shape)] + [_const_spec(w[n].shape) for n in wnames[6:9]]
    in_specs += [_const_spec(w['kvn'].shape)]
    return pl.pallas_call(
        _inproj_kernel, grid=(B, T // tr), in_specs=in_specs,
        out_specs=[row(s.shape[-1]) for s in out_shape], out_shape=out_shape, name='inproj',
        compiler_params=pltpu.CompilerParams(dimension_semantics=('parallel', 'parallel'),
                                             vmem_limit_bytes=VMEM_LIMIT),
    )(x, shift, scale, gain.reshape(1, D), cos, sin, *[w[n] for n in wnames])


def _compress_kernel(x0_ref, x1_ref, w1_ref, pe_ref, w2a_ref, w2b_ref, o_ref, *, n_cmp):
    w1 = w1_ref[0]
    pp = _dot(pe_ref[0], w1)
    peh = pp[0:1, 0:CMP_HIDDEN] + pp[1:2, CMP_HIDDEN:]
    nc = x0_ref.shape[2]
    out = jnp.zeros((nc, LANE), F32)
    for x_ref, w2_ref in ((x0_ref, w2a_ref), (x1_ref, w2b_ref)):
        ab = _dot(x_ref[0, 0], w1)
        hid = ab[:, 0:CMP_HIDDEN] + pltpu.roll(ab[:, CMP_HIDDEN:], nc - 1, 0) + peh
        act = hid * (1.0 / (1.0 + jnp.exp(-hid)))
        out = out + _dot(act.astype(BF16), w2_ref[0])
    rows = lax.broadcasted_iota(jnp.int32, (nc, LANE), 0)
    out = jnp.where(rows < n_cmp, out, 0.0)
    o_ref[0, 0, 0:CMP_PAD] = jnp.zeros((CMP_PAD, LANE), F32)
    o_ref[0, 0, CMP_PAD:CMP_PAD + nc] = out
    o_ref[0, 0, CMP_PAD + nc:] = jnp.zeros((o_ref.shape[2] - CMP_PAD - nc, LANE), F32)


def _compress_weights(cmp_pe, cmp_w1, cmp_w2):
    kin = CMP_STRIDE * HEAD_DIM
    w1 = cmp_w1.reshape(2, 2, kin, CMP_HIDDEN)
    w1cat = jnp.concatenate([w1[:, 0], w1[:, 1]], axis=-1).astype(BF16)
    pe = jnp.pad(cmp_pe.reshape(2, 2, kin), ((0, 0), (0, 6), (0, 0))).astype(BF16)
    w2a = jnp.pad(cmp_w2, ((0, 0), (0, 0), (0, HEAD_DIM))).astype(BF16)
    w2b = jnp.pad(cmp_w2, ((0, 0), (0, 0), (HEAD_DIM, 0))).astype(BF16)
    return w1cat, pe, w2a, w2b


def _compress(xc, cw, n_cmp):
    B, _, nc, kin = xc.shape
    w1cat, pe, w2a, w2b = cw
    return pl.pallas_call(
        functools.partial(_compress_kernel, n_cmp=n_cmp), grid=(B, 2),
        in_specs=[pl.BlockSpec((1, 1, nc, kin), lambda b, j: (b, 2 * j, 0, 0)),
                  pl.BlockSpec((1, 1, nc, kin), lambda b, j: (b, 2 * j + 1, 0, 0)),
                  pl.BlockSpec((1, kin, 2 * CMP_HIDDEN), lambda b, j: (j, 0, 0)),
                  pl.BlockSpec((1, 8, kin), lambda b, j: (j, 0, 0)),
                  pl.BlockSpec((1, CMP_HIDDEN, LANE), lambda b, j: (j, 0, 0)),
                  pl.BlockSpec((1, CMP_HIDDEN, LANE), lambda b, j: (j, 0, 0))],
        out_specs=pl.BlockSpec((1, 1, nc + LANE, LANE), lambda b, j: (b, j, 0, 0)),
        out_shape=jax.ShapeDtypeStruct((B, 2, nc + LANE, LANE), F32), name='compress',
        compiler_params=pltpu.CompilerParams(dimension_semantics=('parallel', 'parallel'),
                                             vmem_limit_bytes=VMEM_LIMIT),
    )(xc, xc, w1cat, pe, w2a, w2b)


def _t5_bucket(rel):
    max_exact = N_BUCKETS // 2
    n = jnp.maximum(rel, 0)
    nf = jnp.maximum(n, 1).astype(F32)
    large = max_exact + (jnp.log(nf / max_exact) / math.log(MAX_DISTANCE / max_exact)
                         * (N_BUCKETS - max_exact)).astype(jnp.int32)
    large = jnp.minimum(large, N_BUCKETS - 1)
    return jnp.where(n < max_exact, n, large)


def _bias_tables(rel_bias):
    tbl = rel_bias.astype(F32)
    const = tbl[N_BUCKETS - 1]
    i = jnp.arange(QB)[:, None]
    j = jnp.arange(LANE)[None, :]

    def tab(rel):
        b = tbl[_t5_bucket(rel)]
        return jnp.moveaxis(b, -1, 0) - const[:, None, None]

    t0 = jnp.where((i - j >= 0)[None], tab(i - j), NEG)
    t1 = tab(QB + i - j)
    zero = jnp.zeros_like(t1)
    t4 = jnp.broadcast_to(jnp.where(j > i, 0.0, NEG)[None], t1.shape)
    tb = jnp.stack([t0, t1, zero, jnp.full_like(t1, NEG), t4]).reshape(5, N_HEADS_A * QB, LANE)
    rel_l = i - CMP_STRIDE * (j - CMP_PAD) - (CMP_BLOCK - 1)
    lb = jnp.where(((j < LOC_W) & (rel_l >= 0))[None], tab(rel_l), NEG).reshape(N_HEADS_A * QB, LANE)
    return tb, lb


def _hilo_dot(x, m):
    hi = x.astype(BF16)
    lo = (x - hi.astype(F32)).astype(BF16)
    return _dot(hi, m) + _dot(lo, m)


def _nsa_kernel(q_ref, gt_ref, ks_ref, vs_ref, kw_ref, vw_ref, kc_ref, vc_ref, tb_ref, lb_ref, pf_ref,
                o_ref, m_scr, l_scr, acc_scr, *, n_cmp):
    qb = pl.program_id(1)
    nh = N_HEADS_A
    rows = nh * QB
    q = q_ref[0]
    q8 = jnp.concatenate([q[:, h * LANE:(h + 1) * LANE] for h in range(nh)], axis=0)
    nc = pf_ref.shape[0]

    kcf = kc_ref[0, 0, 0:nc].astype(BF16)
    vcf = vc_ref[0, 0, 0:nc].astype(BF16)
    l0 = pl.multiple_of(qb * (QB // CMP_STRIDE), 8)
    kcl = kc_ref[0, 0, pl.ds(l0, LANE)].astype(BF16)
    vcl = vc_ref[0, 0, pl.ds(l0, LANE)].astype(BF16)
    colf = lax.broadcasted_iota(jnp.int32, (1, nc), 1)
    far_ok = jnp.where(colf >= CMP_PAD, jnp.where(colf < l0, 0.0, NEG), NEG)
    coll = lax.broadcasted_iota(jnp.int32, (1, LANE), 1) + (l0 - CMP_PAD)
    loc_ok = jnp.where(coll >= 0, jnp.where(coll < n_cmp, 0.0, NEG), NEG)
    s_far = _dot_t(q8, kcf) + far_ok
    s_loc = _dot_t(q8, kcl) + lb_ref[...] + loc_ok
    mrow = jnp.maximum(jnp.max(s_far, axis=1, keepdims=True), jnp.max(s_loc, axis=1, keepdims=True))
    p_far = jnp.exp(s_far - mrow)
    p_loc = jnp.exp(s_loc - mrow)
    lsum = jnp.sum(p_far, axis=1, keepdims=True) + jnp.sum(p_loc, axis=1, keepdims=True)
    inv = jnp.where(mrow > 0.5 * NEG, 1.0 / lsum, 0.0)
    p_far = p_far * inv
    p_loc = p_loc * inv
    o_c = _dot(p_far.astype(BF16), vcf) + _dot(p_loc.astype(BF16), vcl)

    r_i = lax.broadcasted_iota(jnp.int32, (LANE, LANE), 0)
    c_i = lax.broadcasted_iota(jnp.int32, (LANE, LANE), 1)
    pool_loc = jnp.where(r_i < LOC_W,
                         jnp.where(c_i == (r_i >> 2) + (2 * qb - CMP_PAD // SLC_PER_CMP), 1.0, 0.0),
                         0.0).astype(BF16)
    tq = 2 * qb + jnp.where(r_i >= SLC_BLOCK, 1, 0)
    dist = tq - c_i
    c_f = c_i.astype(F32)
    sel = []
    for g in range(N_KV_A):
        pgf = p_far[(g * HPG) * QB:(g * HPG + 1) * QB]
        pgl = p_loc[(g * HPG) * QB:(g * HPG + 1) * QB]
        for hh in range(1, HPG):
            pgf = pgf + p_far[(g * HPG + hh) * QB:(g * HPG + hh + 1) * QB]
            pgl = pgl + p_loc[(g * HPG + hh) * QB:(g * HPG + hh + 1) * QB]
        imp = _hilo_dot(pgf, pf_ref[...]) + _hilo_dot(pgl, pool_loc)
        score = jnp.where(dist < 0, NEG, jnp.where(dist < N_LOCAL_SLC, FORCED, jnp.where(c_i == 0, FORCED, imp)))
        chosen = jnp.zeros((QB, LANE), F32)
        for _ in range(N_SLC):
            mx = jnp.max(score, axis=1, keepdims=True)
            first = jnp.min(jnp.where(score == mx, c_f, float(LANE)), axis=1, keepdims=True)
            hit = c_f == first
            chosen = jnp.where(hit, 1.0, chosen)
            score = jnp.where(hit, -jnp.inf, score)
        sel.append(jnp.where(chosen > 0.5, 0.0, NEG).astype(BF16))

    m_scr[...] = jnp.full(m_scr.shape, -jnp.inf, F32)
    l_scr[...] = jnp.zeros(l_scr.shape, F32)
    acc_scr[...] = jnp.zeros(acc_scr.shape, F32)
    KT = KT_NSA
    nsub = KT // LANE
    qa8 = jnp.concatenate([q8, jnp.concatenate([sel[g] for g in range(N_KV_A) for _ in range(HPG)], axis=0)], axis=1)

    def slc_tile(kt, near):
        k0 = pl.multiple_of(kt * KT, KT)
        s = _dot_t(qa8, ks_ref[0, pl.ds(k0, KT), :])
        if near:
            bias = []
            for c in range(nsub):
                d = qb - (kt * nsub + c)
                bias.append(tb_ref[jnp.where(d < 0, 3, jnp.minimum(d, 2))])
            s = s + jnp.concatenate(bias, axis=1)
        m_old = m_scr[...]
        m_new = jnp.maximum(m_old, jnp.max(s, axis=1, keepdims=True))
        alpha = jnp.exp(m_old - m_new)
        p = jnp.exp(s - m_new)
        l_scr[...] = alpha * l_scr[...] + jnp.sum(p, axis=1, keepdims=True)
        acc_scr[...] = alpha * acc_scr[...] + _dot(p.astype(BF16), vs_ref[0, pl.ds(k0, KT), :])
        m_scr[...] = m_new

    n_tiles = qb // nsub + 1
    n_far = jnp.maximum(n_tiles - 2, 0)

    def far_tile(kt, carry):
        slc_tile(kt, False)
        return carry

    def near_tile(kt, carry):
        slc_tile(kt, True)
        return carry

    lax.fori_loop(0, n_far, far_tile, 0)
    lax.fori_loop(n_far, n_tiles, near_tile, 0)
    o_s = acc_scr[...] * (1.0 / l_scr[...])

    nwin = WINDOW // QB + 1
    w0 = jnp.maximum(qb - (nwin - 1), 0)
    k0 = pl.multiple_of(w0 * QB, QB)
    s = _dot_t(q8, kw_ref[0, pl.ds(k0, nwin * QB), :])
    bias = []
    for c in range(nwin):
        d = qb - (w0 + c)
        bias.append(tb_ref[jnp.where(d < 0, 3, jnp.where(d >= nwin - 1, 4, jnp.minimum(d, 2)))])
    s = s + jnp.concatenate(bias, axis=1)
    p = jnp.exp(s - jnp.max(s, axis=1, keepdims=True))
    o_w = _dot(p.astype(BF16), vw_ref[0, pl.ds(k0, nwin * QB), :]) * (1.0 / jnp.sum(p, axis=1, keepdims=True))

    gt = gt_ref[0]
    for h in range(nh):
        r = slice(h * QB, (h + 1) * QB)
        comb = (gt[:, h:h + 1] * o_c[r] + gt[:, nh + h:nh + h + 1] * o_s[r]
                + gt[:, 2 * nh + h:2 * nh + h + 1] * o_w[r])
        keep = (c_i >= HEAD_DIM) if h // HPG else (c_i < HEAD_DIM)
        o_ref[0, :, h * LANE:(h + 1) * LANE] = jnp.where(keep, comb, 0.0)


def _nsa_prompt(qa, gates, kv16, kc, tb, lb):
    B, S, _ = qa.shape
    nc = S // CMP_STRIDE
    n_cmp = (S - CMP_BLOCK) // CMP_STRIDE + 1
    m = jnp.arange(nc)
    pool_far = ((m[:, None] // SLC_PER_CMP - CMP_PAD // SLC_PER_CMP == jnp.arange(LANE)[None, :])
                & (m[:, None] >= CMP_PAD)).astype(BF16)
    blk_hot = (jnp.arange(S)[:, None] // SLC_BLOCK == jnp.arange(LANE)[None, :]).astype(BF16)
    ks_aug = jnp.concatenate([kv16[:, :, 2 * LANE:3 * LANE], jnp.broadcast_to(blk_hot[None], (B, S, LANE))], axis=-1)
    rows = N_HEADS_A * QB
    one = pl.Buffered(1)
    kvs = lambda c: pl.BlockSpec((1, S, LANE), lambda b, t: (b, 0, c), pipeline_mode=one)
    cspec = pl.BlockSpec((1, 1, nc + LANE, LANE), lambda b, t: (b, 0, 0, 0), pipeline_mode=one)
    vspec = pl.BlockSpec((1, 1, nc + LANE, LANE), lambda b, t: (b, 1, 0, 0), pipeline_mode=one)
    cst = lambda shape: pl.BlockSpec(shape, lambda b, t: (0,) * len(shape), pipeline_mode=one)
    return pl.pallas_call(
        functools.partial(_nsa_kernel, n_cmp=n_cmp), grid=(B, S // QB),
        in_specs=[pl.BlockSpec((1, QB, 1024), lambda b, t: (b, t, 0)),
                  pl.BlockSpec((1, QB, LANE), lambda b, t: (b, t, 0)),
                  pl.BlockSpec((1, S, 2 * LANE), lambda b, t: (b, 0, 0), pipeline_mode=one),
                  kvs(3), kvs(4), kvs(5), cspec, vspec, cst(tb.shape), cst(lb.shape), cst(pool_far.shape)],
        out_specs=pl.BlockSpec((1, QB, 1024), lambda b, t: (b, t, 0)),
        out_shape=jax.ShapeDtypeStruct((B, S, 1024), F32),
        scratch_shapes=[pltpu.VMEM((rows, 1), F32), pltpu.VMEM((rows, 1), F32), pltpu.VMEM((rows, LANE), F32)],
        name='nsa_prompt',
        compiler_params=pltpu.CompilerParams(dimension_semantics=('parallel', 'arbitrary'),
                                             vmem_limit_bytes=VMEM_LIMIT),
    )(qa, gates, ks_aug, kv16, kv16, kv16, kc, kc, tb, lb, pool_far)


def _mla_kernel(q_ref, lat_ref, wuv_ref, o_ref, m_scr, l_scr, acc_scr):
    qb = pl.program_id(1)
    nh = N_HEADS_B
    rows = nh * QB
    q = q_ref[0]
    q8 = jnp.concatenate([q[:, h * 256:(h + 1) * 256] for h in range(nh)], axis=0)
    m_scr[...] = jnp.full(m_scr.shape, -jnp.inf, F32)
    l_scr[...] = jnp.zeros(l_scr.shape, F32)
    acc_scr[...] = jnp.zeros(acc_scr.shape, F32)
    KT = KT_MLA
    nsub = KT // QB

    def tile(kt, masked):
        k0 = pl.multiple_of(kt * KT, KT)
        lat = lat_ref[0, pl.ds(k0, KT), :]
        s = _dot_t(q8, lat)
        if masked:
            col = lax.broadcasted_iota(jnp.int32, (rows, KT), 1)
            row = lax.broadcasted_iota(jnp.int32, (rows, KT), 0) & (QB - 1)
            s = jnp.where(col - row <= qb * QB - kt * KT, s, NEG)
        m_old = m_scr[...]
        m_new = jnp.maximum(m_old, _row_max(s))
        alpha = jnp.exp(m_old - m_new)
        p = jnp.exp(s - m_new)
        l_scr[...] = alpha * l_scr[...] + _row_sum(p)
        acc_scr[...] = alpha * acc_scr[...] + _dot(p.astype(BF16), lat[:, 0:KV_LORA])
        m_scr[...] = m_new

    def full_tile(kt, carry):
        tile(kt, False)
        return carry

    lax.fori_loop(0, qb // nsub, full_tile, 0)
    tile(qb // nsub, True)
    o_lat = (acc_scr[...] * (1.0 / l_scr[...])).astype(BF16)
    for h in range(nh):
        o_ref[0, :, h * LANE:(h + 1) * LANE] = _dot(o_lat[h * QB:(h + 1) * QB], wuv_ref[h])


def _mla_prompt(qm, lat16, w_uv):
    B, S, _ = qm.shape
    rows = N_HEADS_B * QB
    one = pl.Buffered(1)
    return pl.pallas_call(
        _mla_kernel, grid=(B, S // QB),
        in_specs=[pl.BlockSpec((1, QB, 2048), lambda b, t: (b, t, 0)),
                  pl.BlockSpec((1, S, 256), lambda b, t: (b, 0, 0), pipeline_mode=one),
                  pl.BlockSpec(w_uv.shape, lambda b, t: (0, 0, 0), pipeline_mode=one)],
        out_specs=pl.BlockSpec((1, QB, 1024), lambda b, t: (b, t, 0)),
        out_shape=jax.ShapeDtypeStruct((B, S, 1024), F32),
        scratch_shapes=[pltpu.VMEM((rows, 1), F32), pltpu.VMEM((rows, 1), F32), pltpu.VMEM((rows, KV_LORA), F32)],
        name='mla_prompt',
        compiler_params=pltpu.CompilerParams(dimension_semantics=('parallel', 'arbitrary'),
                                             vmem_limit_bytes=VMEM_LIMIT),
    )(qm, lat16, w_uv)


def _adaln_kernel(c_ref, w_ref, b_ref, o_ref):
    c = c_ref[...]
    a = (c * (1.0 / (1.0 + jnp.exp(-c)))).astype(BF16)
    o_ref[...] = _dot(a, w_ref[...].astype(BF16)) + b_ref[...]


def _adaln(c, w_ada, b_ada, tn=512):
    R_, D = c.shape
    N = w_ada.shape[1]
    return pl.pallas_call(
        _adaln_kernel, grid=(N // tn,),
        in_specs=[pl.BlockSpec((R_, D), lambda j: (0, 0)), pl.BlockSpec((D, tn), lambda j: (0, j)),
                  pl.BlockSpec((1, tn), lambda j: (0, j))],
        out_specs=pl.BlockSpec((R_, tn), lambda j: (0, j)),
        out_shape=jax.ShapeDtypeStruct((R_, N), F32), name='adaln',
        compiler_params=pltpu.CompilerParams(dimension_semantics=('parallel',), vmem_limit_bytes=VMEM_LIMIT),
    )(c, w_ada, b_ada.reshape(1, N))


def _merge_kernel(x_ref, oa_ref, ob_ref, ga_ref, shf_ref, scf_ref, gf_ref, na_ref, nb_ref, nffn_ref,
                  wa_ref, wb_ref, wr_ref, wgs_ref, wus_ref, wds_ref, xs_ref, f_ref, sc_ref):
    n_real = N_HEADS_A * HEAD_DIM
    oa = oa_ref[0]
    ob = ob_ref[0]
    na = oa * lax.rsqrt(jnp.sum(oa * oa, axis=-1, keepdims=True) * (1.0 / n_real) + EPS) * na_ref[...]
    nb = ob * lax.rsqrt(jnp.sum(ob * ob, axis=-1, keepdims=True) * (1.0 / n_real) + EPS) * nb_ref[...]
    mix = _dot(na.astype(BF16), wa_ref[...]) + _dot(nb.astype(BF16), wb_ref[...])
    x1 = x_ref[0] + ga_ref[0] * mix
    f = x1 * lax.rsqrt(jnp.mean(x1 * x1, axis=-1, keepdims=True) + EPS) * nffn_ref[...]
    f = f * (1.0 + scf_ref[0]) + shf_ref[0]
    _pack_rows(f, f_ref, (0,))
    fb = f.astype(BF16)
    sc_ref[0] = 1.0 / (1.0 + jnp.exp(-_dot(fb, wr_ref[...])))
    g = _dot(fb, wgs_ref[...])
    u = _dot(fb, wus_ref[...])
    hsh = (g * (1.0 / (1.0 + jnp.exp(-g))) * u).astype(BF16)
    xs_ref[0] = x1 + gf_ref[0] * _dot(hsh, wds_ref[...])


def _merge_weights(out_norm_a, out_norm_b, w_out, norm_ffn, w_router, w_gate_s, w_up_s, w_down_s):
    D = w_out.shape[1]
    na = out_norm_a.reshape(N_HEADS_A, 1, HEAD_DIM)
    grp = (jnp.arange(N_HEADS_A) // HPG)[:, None, None]
    half = jnp.arange(2)[None, :, None]
    na_pad = jnp.where(grp == half, na, 0.0).reshape(1, -1)
    nb_pad = jnp.pad(out_norm_b.reshape(N_HEADS_B, V_DIM), ((0, 0), (0, LANE - V_DIM))).reshape(1, -1)
    wa = w_out[:N_HEADS_A * HEAD_DIM].reshape(N_HEADS_A, 1, HEAD_DIM, D)
    wa_pad = jnp.where((grp == half)[..., None], wa, 0.0).reshape(-1, D)
    wb = w_out[N_HEADS_A * HEAD_DIM:].reshape(N_HEADS_B, V_DIM, D)
    wb_pad = jnp.pad(wb, ((0, 0), (0, LANE - V_DIM), (0, 0))).reshape(-1, D)
    bf = lambda a: a.astype(BF16)
    return [na_pad, nb_pad, norm_ffn.reshape(1, D), bf(wa_pad), bf(wb_pad), bf(w_router), bf(w_gate_s),
            bf(w_up_s), bf(w_down_s)]


def _mod_spec(a, tr):
    if a.shape[1] > 1:
        return pl.BlockSpec((1, tr, a.shape[2]), lambda b, t: (b, t, 0))
    return pl.BlockSpec((1, 1, a.shape[2]), lambda b, t: (b, 0, 0))


def _merge(x, oa, ob, gate_a, shift_f, scale_f, gate_f, mw, tr):
    B, T, D = x.shape
    row = lambda n: pl.BlockSpec((1, tr, n), lambda b, t: (b, t, 0))
    out_shape = [jax.ShapeDtypeStruct((B, T, D), F32), jax.ShapeDtypeStruct((B, T // 2, 8, LANE), jnp.uint32),
                 jax.ShapeDtypeStruct((B, T, N_EXPERTS), F32)]
    return pl.pallas_call(
        _merge_kernel, grid=(B, T // tr),
        in_specs=[row(D), row(1024), row(1024)] + [_mod_spec(a, tr) for a in (gate_a, shift_f, scale_f, gate_f)]
        + [_const_spec(a.shape) for a in mw],
        out_specs=[row(D), pl.BlockSpec((1, tr // 2, 8, LANE), lambda b, t: (b, t, 0, 0)), row(N_EXPERTS)],
        out_shape=out_shape, name='merge',
        compiler_params=pltpu.CompilerParams(dimension_semantics=('parallel', 'parallel'),
                                             vmem_limit_bytes=VMEM_LIMIT),
    )(x, oa, ob, gate_a, shift_f, scale_f, gate_f, *mw)


MOE_CHUNK = 256
MOE_LOOK = 2


def _moe_kernel(nused_ref, bexp_ref, rtok_hbm, f_hbm, wg_ref, wu_ref, wd_ref, y_ref, xbuf, wgb, wub, wdb, rtok, sem,
                isem):
    i = pl.program_id(0)
    nused = nused_ref[0]
    nchunk = rtok_hbm.shape[0] // MOE_CHUNK

    def ids_copy(c):
        return pltpu.make_async_copy(rtok_hbm.at[pl.ds(c * MOE_CHUNK, MOE_CHUNK)], rtok.at[c % 2], isem.at[c % 2])

    nbuf = MOE_LOOK + 1

    def gather(blk):
        ids = rtok.at[(blk // MOE_CHUNK) % 2]
        row = blk % MOE_CHUNK
        for r in range(MOE_BLOCK):
            tile, sub = _packed_row_addr(ids[row, r])
            pltpu.make_async_copy(
                f_hbm.at[pl.ds(tile, 1), pl.ds(sub, WORD_TILES)],
                xbuf.at[blk % nbuf, pl.ds(r % HALF_ROWS, 1), pl.ds((r // HALF_ROWS) * WORD_TILES, WORD_TILES)],
                sem.at[blk % nbuf]).start()

    def wait_gather(blk):
        pltpu.make_async_copy(f_hbm.at[pl.ds(0, HALF_ROWS)], xbuf.at[blk % nbuf], sem.at[blk % nbuf]).wait()

    @pl.when(i == 0)
    def _():
        ids_copy(0).start()
        ids_copy(0).wait()
        if nchunk > 1:
            ids_copy(1).start()
        for b in range(MOE_LOOK):
            gather(b)

    @pl.when(((i + MOE_LOOK) % MOE_CHUNK == 0) & ((i + MOE_LOOK) // MOE_CHUNK < nchunk))
    def _():
        c = (i + MOE_LOOK) // MOE_CHUNK
        ids_copy(c).wait()

        @pl.when(c + 1 < nchunk)
        def _():
            ids_copy(c + 1).start()

    @pl.when(i < nused)
    def _():
        wait_gather(i)

        @pl.when((i == 0) | (bexp_ref[i] != bexp_ref[jnp.maximum(i - 1, 0)]))
        def _():
            wgb[...] = wg_ref[0].astype(BF16)
            wub[...] = wu_ref[0].astype(BF16)
            wdb[...] = wd_ref[0].astype(BF16)

        gather(i + MOE_LOOK)
        lo, hi = _unpack_rows(xbuf, (i % nbuf,))
        g = jnp.zeros((MOE_BLOCK, wgb.shape[1]), F32)
        u = jnp.zeros((MOE_BLOCK, wub.shape[1]), F32)
        for half, tiles in enumerate((lo, hi)):
            for c in range(0, WORD_TILES, 2):
                x = jnp.concatenate([tiles[c], tiles[c + 1]], axis=1).astype(BF16)
                k0 = (half * WORD_TILES + c) * LANE
                g = g + _dot(x, wgb[k0:k0 + 2 * LANE, :])
                u = u + _dot(x, wub[k0:k0 + 2 * LANE, :])
        h = (g * (1.0 / (1.0 + jnp.exp(-g))) * u).astype(BF16)
        _pack_rows(_dot(h, wdb[...]), y_ref, ())

    @pl.when(i >= nused)
    def _():
        @pl.when(i < nused + MOE_LOOK)
        def _():
            wait_gather(i)
        y_ref[...] = jnp.zeros(y_ref.shape, jnp.uint32)


def _moe_experts(nused, blk_exp, row_tok, f, w_gate_e, w_up_e, w_down_e):
    n_blocks = blk_exp.shape[0]
    n_steps = n_blocks + MOE_LOOK
    nb_pad = -(-n_steps // MOE_CHUNK) * MOE_CHUNK
    row_tok = jnp.pad(row_tok, (0, (nb_pad - n_blocks) * MOE_BLOCK)).reshape(nb_pad, MOE_BLOCK)
    blk_exp = jnp.pad(blk_exp, (0, MOE_LOOK), mode='edge')
    D, de = w_gate_e.shape[1:]
    wspec = lambda shp: pl.BlockSpec((1,) + shp, lambda i, nu, be: (be[i], 0, 0))
    return pl.pallas_call(
        _moe_kernel,
        grid_spec=pltpu.PrefetchScalarGridSpec(
            num_scalar_prefetch=2, grid=(n_steps,),
            in_specs=[pl.BlockSpec(memory_space=pl.ANY), pl.BlockSpec(memory_space=pl.ANY),
                      wspec((D, de)), wspec((D, de)), wspec((de, D))],
            out_specs=pl.BlockSpec((HALF_ROWS, 8, LANE), lambda i, nu, be: (i, 0, 0)),
            scratch_shapes=[pltpu.VMEM((MOE_LOOK + 1, HALF_ROWS, 8, LANE), jnp.uint32), pltpu.VMEM((D, de), BF16),
                            pltpu.VMEM((D, de), BF16), pltpu.VMEM((de, D), BF16),
                            pltpu.SMEM((2, MOE_CHUNK, MOE_BLOCK), jnp.int32),
                            pltpu.SemaphoreType.DMA((MOE_LOOK + 1,)), pltpu.SemaphoreType.DMA((2,))]),
        out_shape=jax.ShapeDtypeStruct((n_steps * HALF_ROWS, 8, LANE), jnp.uint32), name='moe_experts',
        compiler_params=pltpu.CompilerParams(dimension_semantics=('arbitrary',), vmem_limit_bytes=VMEM_LIMIT),
    )(nused, blk_exp, row_tok, f, w_gate_e, w_up_e, w_down_e)


def _combine_kernel(pos_ref, yb_hbm, xs_ref, gf_ref, w_ref, nf_ref, o_ref, buf, sem):
    tile = pl.program_id(0) * pl.num_programs(1) + pl.program_id(1)
    ntile = pl.num_programs(0) * pl.num_programs(1)

    nct = 2 * WORD_TILES

    def start(t, slot):
        def per_k(k, c):
            for r in range(MOE_BLOCK):
                src_tile, sub = _packed_row_addr(pos_ref[t * TOP_K + k, r])
                pltpu.make_async_copy(
                    yb_hbm.at[pl.ds(src_tile, 1), pl.ds(sub, WORD_TILES)],
                    buf.at[slot, k, pl.ds(r % HALF_ROWS, 1), pl.ds((r // HALF_ROWS) * WORD_TILES, WORD_TILES)],
                    sem.at[slot]).start()
            return c
        lax.fori_loop(0, TOP_K, per_k, 0)

    @pl.when(tile == 0)
    def _():
        start(0, 0)

    @pl.when(tile + 1 < ntile)
    def _():
        start(tile + 1, (tile + 1) % 2)

    slot = tile % 2
    w = w_ref[0]
    for k in range(TOP_K):
        pltpu.make_async_copy(yb_hbm.at[pl.ds(0, HALF_ROWS)], buf.at[slot, k], sem.at[slot]).wait()
    routed = [jnp.zeros((MOE_BLOCK, LANE), F32) for _ in range(nct)]
    for k in range(TOP_K):
        lo, hi = _unpack_rows(buf, (slot, k))
        wk = w[:, k:k + 1]
        for c in range(WORD_TILES):
            routed[c] = routed[c] + wk * lo[c]
            routed[WORD_TILES + c] = routed[WORD_TILES + c] + wk * hi[c]
    x2 = []
    ss = jnp.zeros((MOE_BLOCK, 1), F32)
    for c in range(nct):
        cols = slice(c * LANE, (c + 1) * LANE)
        xc = xs_ref[0, :, cols] + gf_ref[0, :, cols] * routed[c]
        ss = ss + jnp.sum(xc * xc, axis=-1, keepdims=True)
        x2.append(xc)
    inv = lax.rsqrt(ss * (1.0 / (nct * LANE)) + EPS)
    for c in range(nct):
        cols = slice(c * LANE, (c + 1) * LANE)
        o_ref[0, :, cols] = x2[c] * inv * nf_ref[:, cols]


def _combine(pos, yb, xs, gate_f, wts, norm_final):
    B, T, D = xs.shape
    tr = MOE_BLOCK
    gspec = (pl.BlockSpec((1, tr, D), lambda b, t, p: (b, t, 0)) if gate_f.shape[1] > 1
             else pl.BlockSpec((1, 1, D), lambda b, t, p: (b, 0, 0)))
    return pl.pallas_call(
        _combine_kernel,
        grid_spec=pltpu.PrefetchScalarGridSpec(
            num_scalar_prefetch=1, grid=(B, T // tr),
            in_specs=[pl.BlockSpec(memory_space=pl.ANY), pl.BlockSpec((1, tr, D), lambda b, t, p: (b, t, 0)), gspec,
                      pl.BlockSpec((1, tr, TOP_K), lambda b, t, p: (b, t, 0)),
                      pl.BlockSpec((1, D), lambda b, t, p: (0, 0))],
            out_specs=pl.BlockSpec((1, tr, D), lambda b, t, p: (b, t, 0)),
            scratch_shapes=[pltpu.VMEM((2, TOP_K, HALF_ROWS, 8, LANE), jnp.uint32), pltpu.SemaphoreType.DMA((2,))]),
        out_shape=jax.ShapeDtypeStruct((B, T, D), F32), name='combine',
        compiler_params=pltpu.CompilerParams(dimension_semantics=('arbitrary', 'arbitrary'),
                                             vmem_limit_bytes=VMEM_LIMIT),
    )(pos, yb, xs, gate_f, wts, norm_final.reshape(1, D))


def _assign_rows_kernel(idx_ref, rank_ref, start_ref, pos_ref):
    idx = idx_ref[...]
    tr = idx.shape[0]
    start = start_ref[...]
    lane = lax.broadcasted_iota(jnp.int32, (tr, start.shape[1]), 1)
    kcol = lax.broadcasted_iota(jnp.int32, (tr, TOP_K), 1)
    base = jnp.zeros((tr, TOP_K), F32)
    for k in range(TOP_K):
        sk = jnp.sum(jnp.where(lane == idx[:, k:k + 1], start, 0.0), axis=1, keepdims=True)
        base = jnp.where(kcol == k, sk, base)
    pos_ref[...] = base.astype(jnp.int32) + rank_ref[...]


def _assign_rows(idx, rank, start):
    n_tok = idx.shape[0]
    tspec = pl.BlockSpec((MOE_BLOCK, TOP_K), lambda i: (i, 0))
    return pl.pallas_call(
        _assign_rows_kernel, grid=(n_tok // MOE_BLOCK,),
        in_specs=[tspec, tspec, pl.BlockSpec(start.shape, lambda i: (0, 0))], out_specs=tspec,
        out_shape=jax.ShapeDtypeStruct((n_tok, TOP_K), jnp.int32), name='assign_rows',
        compiler_params=pltpu.CompilerParams(dimension_semantics=('parallel',), vmem_limit_bytes=VMEM_LIMIT),
    )(idx, rank, start)


def _route(scores, router_bias):
    n_tok = scores.shape[0]
    n_asg = n_tok * TOP_K
    idx, wts, rank, counts = _route_rank(scores, router_bias.astype(F32).reshape(1, N_EXPERTS))
    padded = (counts + MOE_BLOCK - 1) // MOE_BLOCK * MOE_BLOCK
    pad_end = jnp.cumsum(padded)
    pos = _assign_rows(idx, rank, (pad_end - padded).astype(F32).reshape(1, N_EXPERTS))
    n_blocks = -(-(n_asg + N_EXPERTS * (MOE_BLOCK - 1)) // MOE_BLOCK)
    n_rows = n_blocks * MOE_BLOCK
    tok = jnp.broadcast_to(jnp.arange(n_tok, dtype=jnp.int32)[:, None], pos.shape)
    row_tok = jnp.zeros((n_rows,), jnp.int32).at[pos.reshape(-1)].set(tok.reshape(-1))
    blk_start = jnp.arange(n_blocks, dtype=jnp.int32)[:, None] * MOE_BLOCK
    blk_exp = jnp.minimum(jnp.sum((pad_end[None, :] <= blk_start).astype(jnp.int32), axis=1), N_EXPERTS - 1)
    nused = (pad_end[-1] // MOE_BLOCK).astype(jnp.int32).reshape(1)
    return wts, pos, row_tok, blk_exp, nused


def _route_kernel(sc_ref, rb_ref, idx_ref, wts_ref, rank_ref, cnt_ref, carry):
    i = pl.program_id(0)

    @pl.when(i == 0)
    def _():
        carry[...] = jnp.zeros(carry.shape, F32)

    tr, ne = sc_ref.shape
    gsz = ne // N_GROUPS
    scores = sc_ref[...]
    biased = scores + rb_ref[...]
    lane = lax.broadcasted_iota(jnp.int32, (tr, ne), 1)
    lane_f = lane.astype(F32)
    lgrp = lane // gsz
    ninf = -jnp.inf

    def first_max(x):
        mx = jnp.max(x, axis=1, keepdims=True)
        return mx, jnp.min(jnp.where(x == mx, lane_f, float(ne)), axis=1, keepdims=True)

    gl = lax.broadcasted_iota(jnp.int32, (tr, LANE), 1)
    gscore = jnp.full((tr, LANE), ninf, F32)
    for g in range(N_GROUPS):
        xg = jnp.where(lgrp == g, biased, ninf)
        m1, f1 = first_max(xg)
        m2 = jnp.max(jnp.where(lane_f == f1, ninf, xg), axis=1, keepdims=True)
        gscore = jnp.where(gl == g, m1 + m2, gscore)
    gl_f = gl.astype(F32)
    keep = jnp.full((tr, ne), NEG, F32)
    for _ in range(TOP_GROUPS):
        mx = jnp.max(gscore, axis=1, keepdims=True)
        gf = jnp.min(jnp.where(gscore == mx, gl_f, float(LANE)), axis=1, keepdims=True)
        gscore = jnp.where(gl_f == gf, ninf, gscore)
        keep = jnp.where(lgrp.astype(F32) == gf, biased, keep)
    kcol = lax.broadcasted_iota(jnp.int32, (tr, TOP_K), 1)
    idx = jnp.zeros((tr, TOP_K), F32)
    wts = jnp.zeros((tr, TOP_K), F32)
    hot = []
    onehot = jnp.zeros((tr, ne), F32)
    for k in range(TOP_K):
        _, f = first_max(keep)
        hit = lane_f == f
        hot.append(hit)
        keep = jnp.where(hit, ninf, keep)
        onehot = jnp.where(hit, 1.0, onehot)
        idx = jnp.where(kcol == k, f, idx)
        wts = jnp.where(kcol == k, jnp.sum(jnp.where(hit, scores, 0.0), axis=1, keepdims=True), wts)
    idx_ref[...] = idx.astype(jnp.int32)
    wts_ref[...] = wts / jnp.sum(wts, axis=1, keepdims=True) * ROUTED_SCALE
    r_i = lax.broadcasted_iota(jnp.int32, (tr, tr), 0)
    c_i = lax.broadcasted_iota(jnp.int32, (tr, tr), 1)
    lower = jnp.where(c_i < r_i, 1.0, 0.0).astype(BF16)
    before = _dot(lower, onehot.astype(BF16)) + carry[0:1, :]
    rank = jnp.zeros((tr, TOP_K), F32)
    for k in range(TOP_K):
        rank = jnp.where(kcol == k, jnp.sum(jnp.where(hot[k], before, 0.0), axis=1, keepdims=True), rank)
    rank_ref[...] = rank.astype(jnp.int32)
    total = carry[0:1, :] + jnp.sum(onehot, axis=0, keepdims=True)
    carry[...] = jnp.broadcast_to(total, carry.shape)
    cnt_ref[...] = jnp.broadcast_to(total, cnt_ref.shape).astype(jnp.int32)


def _route_rank(scores, router_bias):
    n_tok, ne = scores.shape
    tr = MOE_BLOCK
    tk = lambda dt: jax.ShapeDtypeStruct((n_tok, TOP_K), dt)
    tspec = pl.BlockSpec((tr, TOP_K), lambda i: (i, 0))
    idx, wts, rank, cnt = pl.pallas_call(
        _route_kernel, grid=(n_tok // tr,),
        in_specs=[pl.BlockSpec((tr, ne), lambda i: (i, 0)), pl.BlockSpec((1, ne), lambda i: (0, 0))],
        out_specs=[tspec, tspec, tspec, pl.BlockSpec((8, ne), lambda i: (0, 0))],
        out_shape=[tk(jnp.int32), tk(F32), tk(jnp.int32), jax.ShapeDtypeStruct((8, ne), jnp.int32)],
        scratch_shapes=[pltpu.VMEM((8, ne), F32)], name='route',
        compiler_params=pltpu.CompilerParams(dimension_semantics=('arbitrary',), vmem_limit_bytes=VMEM_LIMIT),
    )(scores, router_bias)
    return idx, wts, rank, cnt[0]


PAGE = 128


def _softmax_with_new(s, s_new):
    m = jnp.maximum(jnp.max(s, axis=1, keepdims=True), s_new)
    p = jnp.exp(s - m)
    pn = jnp.exp(s_new - m)
    return p, pn, 1.0 / (jnp.sum(p, axis=1, keepdims=True) + pn)


def _samp_cmp_kernel(pt_ref, pool_hbm, q_ref, wbd_ref, pe_ref, w1c_ref, w2_ref, bc_ref, pf_ref,
                     oc_ref, idx_ref, buf_t, buf, peh_scr, sem, *, n_pages, n_cmp):
    s = pl.program_id(0)
    ns = pl.num_programs(0)
    nc = n_pages * PAGE // CMP_STRIDE

    def page_copy(page, slot, p):
        return pltpu.make_async_copy(pool_hbm.at[page], buf_t.at[slot, :, :, pl.ds(p * PAGE, PAGE)], sem.at[slot])

    def start(smp, slot):
        def body(p, c):
            page_copy(pt_ref[smp, p], slot, p).start()
            return c
        lax.fori_loop(0, n_pages, body, 0)

    @pl.when(s == 0)
    def _():
        start(0, 0)
        for j in range(2):
            pp = _dot(pe_ref[j], w1c_ref[j])
            peh_scr[j] = jnp.broadcast_to(pp[0:1, 0:CMP_HIDDEN] + pp[1:2, CMP_HIDDEN:], (8, CMP_HIDDEN))

    @pl.when(s + 1 < ns)
    def _():
        start(s + 1, (s + 1) % 2)

    slot = s % 2

    def wait_page(p, c):
        page_copy(0, slot, p).wait()
        return c
    lax.fori_loop(0, n_pages, wait_page, 0)

    tw = 4 * PAGE

    def to_rows(c, carry):
        l0 = pl.multiple_of(c * tw, tw)
        for j in range(2):
            xt = buf_t[slot, pl.ds(2 * j, 2), :, pl.ds(l0, tw)].reshape(2 * HEAD_DIM, tw)
            buf[j, pl.ds(l0, tw), :] = xt.T
        return carry
    lax.fori_loop(0, n_pages * PAGE // tw, to_rows, 0)

    rows = lax.broadcasted_iota(jnp.int32, (nc, LANE), 0)
    kvc = []
    for j in range(2):
        acc = jnp.zeros((nc, 4 * CMP_HIDDEN), F32)
        for r in range(0, CMP_STRIDE, 2):
            x = jnp.concatenate([buf[j, pl.ds(r, nc, stride=CMP_STRIDE), :],
                                 buf[j, pl.ds(r + 1, nc, stride=CMP_STRIDE), :]], axis=1).astype(BF16)
            acc = acc + _dot(x, wbd_ref[j, r // 2])
        out = jnp.zeros((nc, LANE), F32)
        for g in range(N_KV_A):
            a = acc[:, g * 2 * CMP_HIDDEN:g * 2 * CMP_HIDDEN + CMP_HIDDEN]
            bm = acc[:, g * 2 * CMP_HIDDEN + CMP_HIDDEN:(g + 1) * 2 * CMP_HIDDEN]
            hid = a + pltpu.roll(bm, nc - 1, 0) + peh_scr[j, 0:1]
            act = hid * (1.0 / (1.0 + jnp.exp(-hid)))
            out = out + _dot(act.astype(BF16), w2_ref[j, g])
        kvc.append(jnp.where(rows < n_cmp, out, 0.0).astype(BF16))
    kc, vc = kvc

    q8 = q_ref[0]
    sc = _dot_t(q8, kc) + bc_ref[...]
    p = jnp.exp(sc - jnp.max(sc, axis=1, keepdims=True))
    p = p * (1.0 / jnp.sum(p, axis=1, keepdims=True))
    oc_ref[0] = _dot(p.astype(BF16), vc)

    pg = jnp.concatenate([jnp.sum(p[0:HPG], axis=0, keepdims=True), jnp.sum(p[HPG:2 * HPG], axis=0, keepdims=True),
                          jnp.zeros((8 - N_KV_A, nc), F32)], axis=0)
    imp = _hilo_dot(pg, pf_ref[...])
    lane = lax.broadcasted_iota(jnp.int32, (8, LANE), 1).astype(F32)
    score = jnp.where(lane == 0.0, FORCED, jnp.where(lane >= float(LANE - N_LOCAL_SLC + 1), FORCED, imp))
    picks = jnp.full((8, LANE), float(LANE), F32)
    for k in range(N_SLC - 1):
        mx = jnp.max(score, axis=1, keepdims=True)
        first = jnp.min(jnp.where(score == mx, lane, float(LANE)), axis=1, keepdims=True)
        picks = jnp.where(lane == float(k), first, picks)
        score = jnp.where(lane == first, -jnp.inf, score)
    idx_ref[0] = picks.astype(jnp.int32)


def _samp_cmp(page_table, pool2d, q3, scw, bc, n_cmp):
    Bd, n_pages = page_table.shape
    nc = n_pages * PAGE // CMP_STRIDE
    wbd, pe, w1c, w2g = scw
    m = jnp.arange(nc)
    pool_m = ((m[:, None] // SLC_PER_CMP == jnp.arange(LANE)[None, :]) & (m[:, None] < n_cmp)).astype(BF16)
    one = pl.Buffered(1)
    cst = lambda a: pl.BlockSpec(a.shape, lambda s, pt: (0,) * a.ndim, pipeline_mode=one)
    return pl.pallas_call(
        functools.partial(_samp_cmp_kernel, n_pages=n_pages, n_cmp=n_cmp),
        grid_spec=pltpu.PrefetchScalarGridSpec(
            num_scalar_prefetch=1, grid=(Bd,),
            in_specs=[pl.BlockSpec(memory_space=pl.ANY), pl.BlockSpec((1, 8, LANE), lambda s, pt: (s, 0, 0)),
                      cst(wbd), cst(pe), cst(w1c), cst(w2g), cst(bc), cst(pool_m)],
            out_specs=[pl.BlockSpec((1, 8, LANE), lambda s, pt: (s, 0, 0)),
                       pl.BlockSpec((1, 8, LANE), lambda s, pt: (s, 0, 0))],
            scratch_shapes=[pltpu.VMEM((2, 4, HEAD_DIM, n_pages * PAGE), F32),
                            pltpu.VMEM((2, n_pages * PAGE, LANE), F32), pltpu.VMEM((2, 8, CMP_HIDDEN), F32),
                            pltpu.SemaphoreType.DMA((2,))]),
        out_shape=[jax.ShapeDtypeStruct((Bd, 8, LANE), F32), jax.ShapeDtypeStruct((Bd, 8, LANE), jnp.int32)],
        name='sample_cmp',
        compiler_params=pltpu.CompilerParams(dimension_semantics=('arbitrary',), vmem_limit_bytes=VMEM_LIMIT),
    )(page_table, pool2d, q3, wbd, pe, w1c, w2g, bc, pool_m)


def _samp_cmp_weights(cw):
    w1cat, pe, w2a, w2b = cw
    w = w1cat.reshape(2, CMP_STRIDE, HEAD_DIM, 2 * CMP_HIDDEN)
    z = jnp.zeros_like(w)
    wbd = jnp.concatenate([jnp.concatenate([w, z], axis=-1), jnp.concatenate([z, w], axis=-1)], axis=2)
    wbd = wbd.reshape(2, CMP_STRIDE // 2, 2 * LANE, 4 * CMP_HIDDEN)
    return wbd, pe, w1cat, jnp.stack([w2a, w2b], axis=1)


def _t5_bias_rows(rel_bias, rel, valid):
    b = rel_bias.astype(F32)[_t5_bucket(rel)]
    return jnp.where(valid[None, :], b.T, NEG)


def _bucket_bias(rel, tbl_t):
    max_exact = N_BUCKETS // 2
    nf = jnp.maximum(rel, 1).astype(F32)
    large = max_exact + (jnp.log(nf / max_exact) / math.log(MAX_DISTANCE / max_exact)
                         * (N_BUCKETS - max_exact)).astype(jnp.int32)
    bucket = jnp.where(rel < max_exact, rel, jnp.minimum(large, N_BUCKETS - 1))
    bias = jnp.zeros(rel.shape, F32)
    for b in range(N_BUCKETS):
        bias = jnp.where(bucket == b, tbl_t[:, b:b + 1], bias)
    return bias


def _samp_sw_kernel(pt_ref, idx_ref, pool_hbm, q_ref, knew_ref, win_ref, wnew_ref, wcol_ref, oc_ref, gcol_ref,
                    tblt_ref, bw_ref, oa_ref, nwin_ref, kvbuf, sem, *, past_len):
    s = pl.program_id(0)
    ns = pl.num_programs(0)
    npb = past_len // SLC_BLOCK
    bpp = PAGE // SLC_BLOCK
    nk = N_SLC * PAGE

    def block_copy(page, slot, g, k, kv):
        return pltpu.make_async_copy(pool_hbm.at[page, kv * N_KV_A + g],
                                     kvbuf.at[slot, kv * N_KV_A + g, :, pl.ds(k * PAGE, PAGE)], sem.at[slot])

    def start(smp, slot):
        for g in range(N_KV_A):
            for k in range(N_SLC):
                j = jnp.minimum(idx_ref[(smp * N_KV_A + g) * N_SLC + k], npb - 1)
                page = pt_ref[smp, j // bpp]
                for kv in range(2):
                    block_copy(page, slot, g, k, kv).start()

    @pl.when(s == 0)
    def _():
        start(0, 0)

    @pl.when(s + 1 < ns)
    def _():
        start(s + 1, (s + 1) % 2)

    slot = s % 2
    q8 = q_ref[0]
    q32 = q8.astype(F32)
    qg = [q8[:, g * HEAD_DIM:(g + 1) * HEAD_DIM] for g in range(N_KV_A)]
    tbl_t = tblt_ref[...]
    row = lax.broadcasted_iota(jnp.int32, (8, LANE), 0)
    lane = lax.broadcasted_iota(jnp.int32, (8, LANE), 1)
    grp0 = row[:, 0:1] < HPG

    def new_token(kv_row):
        kn = kv_row[:, 0:LANE].astype(BF16).astype(F32)
        vn = kv_row[:, LANE:].astype(BF16).astype(F32)
        return jnp.sum(q32 * kn, axis=1, keepdims=True) + tbl_t[:, 0:1], vn

    def by_group(a0, a1):
        return jnp.concatenate([jnp.where(grp0, a0, 0.0), jnp.where(grp0, 0.0, a1)], axis=1)

    w = win_ref[0]
    wl = w.shape[-1]
    s_new, v_new = new_token(wnew_ref[0])
    sw = jnp.where(grp0, _dot(qg[0], w[0].astype(BF16)), _dot(qg[1], w[1].astype(BF16))) + bw_ref[...]
    p, pn, inv = _softmax_with_new(sw, s_new)
    pb = p.astype(BF16)
    o_w = (by_group(_dot_t(pb, w[2].astype(BF16)), _dot_t(pb, w[3].astype(BF16))) + pn * v_new) * inv
    wcol = lax.broadcasted_iota(jnp.int32, (HEAD_DIM, wl), 1)
    for c in range(2 * N_KV_A):
        nwin_ref[0, c] = jnp.where(wcol == wl - 1, wcol_ref[0, c], pltpu.roll(w[c], wl - 1, 1))

    for g in range(N_KV_A):
        for k in range(N_SLC):
            for kv in range(2):
                block_copy(0, slot, g, k, kv).wait()
    kl = lax.broadcasted_iota(jnp.int32, (8, nk), 1)
    kslot = kl >> 7
    kin = kl & (PAGE - 1)
    rowk = lax.broadcasted_iota(jnp.int32, (8, nk), 0) < HPG
    blk = jnp.zeros((8, nk), jnp.int32)
    for k in range(N_SLC):
        j0 = idx_ref[(s * N_KV_A) * N_SLC + k]
        j1 = idx_ref[(s * N_KV_A + 1) * N_SLC + k]
        blk = jnp.where(kslot == k, jnp.where(rowk, j0, j1), blk)
    rel = past_len - ((blk // bpp) * PAGE + kin)
    ok = (kin // SLC_BLOCK) == jnp.where(blk < npb, blk % bpp, -1)
    ss = jnp.where(rowk, _dot(qg[0], kvbuf[slot, 0].astype(BF16)), _dot(qg[1], kvbuf[slot, 1].astype(BF16)))
    ss = jnp.where(ok, ss + _bucket_bias(jnp.maximum(rel, 0), tbl_t), NEG)
    s_new, v_new = new_token(knew_ref[0])
    p, pn, inv = _softmax_with_new(ss, s_new)
    pb = p.astype(BF16)
    o_s = (by_group(_dot_t(pb, kvbuf[slot, 2].astype(BF16)), _dot_t(pb, kvbuf[slot, 3].astype(BF16)))
           + pn * v_new) * inv

    gc = gcol_ref[0]
    o = gc[:, 0:1] * oc_ref[0] + gc[:, 1:2] * o_s + gc[:, 2:3] * o_w
    oa_ref[0] = jnp.where((lane >= HEAD_DIM) == (row >= HPG), o, 0.0)


def _samp_sw(page_table, idx_flat, pool_t, q3, knew, win_t, wnew, wcol, o_c, gcol, tbl_t, bw, past_len):
    Bd = page_table.shape[0]
    wl = win_t.shape[-1]
    per = lambda shp: pl.BlockSpec((1,) + shp, lambda s, pt, ix: (s,) + (0,) * len(shp))
    cst = lambda a: pl.BlockSpec(a.shape, lambda s, pt, ix: (0,) * a.ndim)
    return pl.pallas_call(
        functools.partial(_samp_sw_kernel, past_len=past_len),
        grid_spec=pltpu.PrefetchScalarGridSpec(
            num_scalar_prefetch=2, grid=(Bd,),
            in_specs=[pl.BlockSpec(memory_space=pl.ANY), per((8, LANE)), per((1, 2 * LANE)),
                      per((2 * N_KV_A, HEAD_DIM, wl)), per((1, 2 * LANE)), per((2 * N_KV_A, HEAD_DIM, 1)),
                      per((8, LANE)), per((8, LANE)), cst(tbl_t), cst(bw)],
            out_specs=[per((8, LANE)), per((2 * N_KV_A, HEAD_DIM, wl))],
            scratch_shapes=[pltpu.VMEM((2, 2 * N_KV_A, HEAD_DIM, N_SLC * PAGE), F32),
                            pltpu.SemaphoreType.DMA((2,))]),
        out_shape=[jax.ShapeDtypeStruct((Bd, 8, LANE), F32), jax.ShapeDtypeStruct(win_t.shape, F32)],
        name='sample_slc_win',
        compiler_params=pltpu.CompilerParams(dimension_semantics=('arbitrary',), vmem_limit_bytes=VMEM_LIMIT),
    )(page_table, idx_flat, pool_t, q3, knew, win_t, wnew, wcol, o_c, gcol, tbl_t, bw)


def _samp_mla_kernel(pt_ref, pool_hbm, q_ref, lnew_ref, o_ref, buf, sem, *, n_pages):
    s = pl.program_id(0)
    ns = pl.num_programs(0)

    def page_copy(page, slot, p):
        return pltpu.make_async_copy(pool_hbm.at[page], buf.at[slot, :, pl.ds(p * PAGE, PAGE)], sem.at[slot])

    def start(smp, slot):
        def body(p, c):
            page_copy(pt_ref[smp, p], slot, p).start()
            return c
        lax.fori_loop(0, n_pages, body, 0)

    @pl.when(s == 0)
    def _():
        start(0, 0)

    @pl.when(s + 1 < ns)
    def _():
        start(s + 1, (s + 1) % 2)

    slot = s % 2

    def wait_page(p, c):
        page_copy(0, slot, p).wait()
        return c
    lax.fori_loop(0, n_pages, wait_page, 0)

    q8 = q_ref[0]
    ckv_t = buf[slot, 0:KV_LORA, :].astype(BF16)
    kr_t = buf[slot, KV_LORA:LATENT_DIM, :].astype(BF16)
    ln = lnew_ref[0].astype(F32)
    sc = _dot(q8[:, 0:KV_LORA], ckv_t) + _dot(q8[:, KV_LORA:LATENT_DIM], kr_t)
    s_new = jnp.sum(q8.astype(F32) * ln, axis=1, keepdims=True)
    p, pn, inv = _softmax_with_new(sc, s_new)
    o_ref[0] = (_dot_t(p.astype(BF16), ckv_t) + pn * ln[:, 0:KV_LORA]) * inv


def _samp_mla(page_table, pool2d, qm3, lnew):
    Bd, n_pages = page_table.shape
    per = lambda shp: pl.BlockSpec((1,) + shp, lambda s, pt: (s, 0, 0))
    return pl.pallas_call(
        functools.partial(_samp_mla_kernel, n_pages=n_pages),
        grid_spec=pltpu.PrefetchScalarGridSpec(
            num_scalar_prefetch=1, grid=(Bd,),
            in_specs=[pl.BlockSpec(memory_space=pl.ANY), per((8, 256)), per((1, 256))],
            out_specs=per((8, KV_LORA)),
            scratch_shapes=[pltpu.VMEM((2, LATENT_DIM, n_pages * PAGE), F32), pltpu.SemaphoreType.DMA((2,))]),
        out_shape=jax.ShapeDtypeStruct((Bd, 8, KV_LORA), F32), name='sample_mla',
        compiler_params=pltpu.CompilerParams(dimension_semantics=('arbitrary',), vmem_limit_bytes=VMEM_LIMIT),
    )(page_table, pool2d, qm3, lnew)


def _uv_kernel(o_ref, w_ref, y_ref):
    y_ref[...] = _dot(o_ref[...].astype(BF16), w_ref[0])


def _samp_uv(o_lat2d, w_uv):
    Bd = o_lat2d.shape[0]
    return pl.pallas_call(
        _uv_kernel, grid=(N_HEADS_B,),
        in_specs=[pl.BlockSpec((Bd, LANE), lambda h: (0, h)), pl.BlockSpec((1, KV_LORA, LANE), lambda h: (h, 0, 0))],
        out_specs=pl.BlockSpec((Bd, LANE), lambda h: (0, h)),
        out_shape=jax.ShapeDtypeStruct((Bd, N_HEADS_B * LANE), F32), name='sample_uv',
        compiler_params=pltpu.CompilerParams(dimension_semantics=('parallel',), vmem_limit_bytes=VMEM_LIMIT),
    )(o_lat2d, w_uv)


def _sample_mix(xs3, msm, pool_cmp, pool_slc, win_buf, pool_mla, page_table, gain, rel_bias, w, cw):
    Bd = xs3.shape[1]
    past_len = page_table.shape[1] * PAGE
    cos_s, sin_s = _rope_tables(jnp.full((Bd,), past_len, jnp.int32))
    qa_s, cmp_s, slc_s, win_s, _, gt_s, qm_s, lat_s, lat16_s = _inproj(xs3, msm[0], msm[1], gain, cos_s, sin_s, w, Bd)
    q3 = qa_s.reshape(Bd, N_HEADS_A, LANE)
    nc = past_len // CMP_STRIDE
    n_cmp = (past_len + 1 - CMP_BLOCK) // CMP_STRIDE + 1
    m = jnp.arange(nc)
    bc = _t5_bias_rows(rel_bias, past_len - (m * CMP_STRIDE + CMP_BLOCK - 1), m < n_cmp)
    fm = lambda a: a.transpose(0, 2, 3, 4, 1).reshape(a.shape[0], 2 * N_KV_A, HEAD_DIM, a.shape[1])
    o_c, idx = _samp_cmp(page_table, fm(pool_cmp), q3, _samp_cmp_weights(cw), bc, n_cmp)
    idx_flat = idx[:, :N_KV_A, :N_SLC].reshape(-1)
    wl = win_buf.shape[1]
    wi = jnp.arange(wl)
    bw = _t5_bias_rows(rel_bias, wl - wi, (wl - wi < WINDOW) & (past_len - wl + wi >= 0))
    tbl_t = jnp.pad(rel_bias.astype(F32).T, ((0, 0), (0, LANE - N_BUCKETS)))
    gcol = gt_s[0, :, :3 * N_HEADS_A].reshape(Bd, 3, N_HEADS_A).transpose(0, 2, 1)
    gcol = jnp.pad(gcol, ((0, 0), (0, 0), (0, LANE - 3)))
    oa_s, new_win = _samp_sw(page_table, idx_flat, fm(pool_slc), q3, slc_s.reshape(Bd, 1, 2 * LANE), fm(win_buf),
                             win_s.reshape(Bd, 1, 2 * LANE), win_s.reshape(Bd, 2 * N_KV_A, HEAD_DIM, 1), o_c, gcol,
                             tbl_t, bw, past_len)
    o_lat = _samp_mla(page_table, pool_mla.transpose(0, 2, 1), qm_s.reshape(Bd, N_HEADS_B, 256),
                      lat16_s.reshape(Bd, 1, 256))
    ob_s = _samp_uv(o_lat.reshape(Bd, N_HEADS_B * KV_LORA), w['w_uv'])
    new_win = new_win.reshape(Bd, 2, N_KV_A, HEAD_DIM, wl).transpose(0, 4, 1, 2, 3)
    return oa_s.reshape(1, Bd, -1), ob_s.reshape(1, Bd, -1), cmp_s, slc_s, new_win, lat_s


def kernel(x_prompt, x_sample, cache_nsa_cmp, cache_nsa_slc, cache_nsa_win, cache_mla, page_table, c_prompt, c_sample, rel_bias, w_ada, b_ada, norm_attn, norm_ffn, w_in, cmp_pe, cmp_w1, cmp_w2, q_norm, w_q_up, kv_norm, w_kv_up, out_norm_a, out_norm_b, w_out, w_router, router_bias, w_gate_e, w_up_e, w_down_e, w_gate_s, w_up_s, w_down_s, norm_final):
    B, S, D = x_prompt.shape
    Bd = x_sample.shape[0]
    l = 0
    n_mod = B + Bd
    c_all = jnp.pad(jnp.concatenate([c_prompt, c_sample], axis=0), ((0, -n_mod % 8), (0, 0)))
    mod = _adaln(c_all, w_ada[l], b_ada[l]).reshape(-1, 6, D)
    mp = [mod[:B, i][:, None, :] for i in range(6)]
    msm = [mod[B:n_mod, i][None] for i in range(6)]

    w = _inproj_weights(w_in[l], q_norm[l], w_q_up[l], kv_norm[l], w_kv_up[l])
    cw = _compress_weights(cmp_pe[l], cmp_w1[l], cmp_w2[l])
    mw = _merge_weights(out_norm_a[l], out_norm_b[l], w_out[l], norm_ffn[l], w_router[l], w_gate_s[l], w_up_s[l],
                        w_down_s[l])
    tb, lb = _bias_tables(rel_bias)

    cos, sin = _rope_tables(jnp.arange(S))
    qa, cmp32, slc32, win32, kv16, gt, qm, lat32, lat16 = _inproj(
        x_prompt, mp[0], mp[1], norm_attn[l], cos, sin, w, 256)
    nc = S // CMP_STRIDE
    n_cmp = (S - CMP_BLOCK) // CMP_STRIDE + 1
    xc = cmp32.reshape(B, nc, CMP_STRIDE, 4, HEAD_DIM).transpose(0, 3, 1, 2, 4).reshape(B, 4, nc, -1).astype(BF16)
    kcv = _compress(xc, cw, n_cmp)
    oa_p = _nsa_prompt(qa, gt, kv16, kcv, tb, lb)
    ob_p = _mla_prompt(qm, lat16, w['w_uv'])
    xs_p, f_p, sc_p = _merge(x_prompt, oa_p, ob_p, mp[2], mp[3], mp[4], mp[5], mw, 256)

    xs3 = x_sample.reshape(1, Bd, D)
    oa_s, ob_s, cmp_s, slc_s, new_win, lat_s = _sample_mix(
        xs3, msm, cache_nsa_cmp[l], cache_nsa_slc[l], cache_nsa_win[l], cache_mla[l], page_table, norm_attn[l],
        rel_bias, w, cw)
    xs_s, f_s, sc_s = _merge(xs3, oa_s, ob_s, msm[2], msm[3], msm[4], msm[5], mw, Bd)

    n_p = B * S
    f_all = jnp.concatenate([f_p.reshape(n_p // 2, 8, LANE), f_s.reshape(Bd // 2, 8, LANE)], axis=0)
    sc_all = jnp.concatenate([sc_p.reshape(n_p, N_EXPERTS), sc_s.reshape(Bd, N_EXPERTS)], axis=0)
    wts, pos, row_tok, blk_exp, nused = _route(sc_all, router_bias[l])
    yb = _moe_experts(nused, blk_exp, row_tok, f_all, w_gate_e[l], w_up_e[l], w_down_e[l])
    tile_pos = lambda p: p.reshape(-1, MOE_BLOCK, TOP_K).transpose(0, 2, 1).reshape(-1, MOE_BLOCK)
    y_p = _combine(tile_pos(pos[:n_p]), yb, xs_p, mp[5], wts[:n_p].reshape(B, S, TOP_K), norm_final)
    y_s = _combine(tile_pos(pos[n_p:]), yb, xs_s, msm[5], wts[n_p:].reshape(1, Bd, TOP_K), norm_final)

    sh6 = lambda a, b, t: a.reshape(1, b, t, 2, N_KV_A, HEAD_DIM)
    return (y_p, y_s.reshape(Bd, 1, D), sh6(cmp32, B, S), sh6(cmp_s, Bd, 1), sh6(slc32, B, S), sh6(slc_s, Bd, 1),
            sh6(win32[:, S - WINDOW:], B, WINDOW), new_win[None], lat32[None], lat_s.reshape(1, Bd, 1, LATENT_DIM))
```

```python
import functools
import math

import jax
import jax.numpy as jnp
from jax import lax
from jax.experimental import pallas as pl
from jax.experimental.pallas import tpu as pltpu

F32 = jnp.float32
BF16 = jnp.bfloat16

LANE = 128
VMEM_LIMIT = 56 * 1024 * 1024

HEAD_DIM = 64
N_HEADS_A = 8
N_KV_A = 2
HPG = N_HEADS_A // N_KV_A
CMP_BLOCK = 32
CMP_STRIDE = 16
CMP_HIDDEN = 128
SLC_BLOCK = 64
SLC_PER_CMP = SLC_BLOCK // CMP_STRIDE
N_SLC = 16
N_LOCAL_SLC = 2
WINDOW = 512
N_HEADS_B = 8
Q_LORA = 192
KV_LORA = 128
QK_NOPE = 64
QK_ROPE = 32
V_DIM = 64
LATENT_DIM = KV_LORA + QK_ROPE
ROPE_THETA = 10000.0
MLA_SCALE = (QK_NOPE + QK_ROPE) ** -0.5
N_BUCKETS = 32
MAX_DISTANCE = 128
N_EXPERTS = 256
TOP_K = 8
N_GROUPS = 8
TOP_GROUPS = 4
ROUTED_SCALE = 2.5
MOE_BLOCK = 128
EPS = 1e-6
NEG = -1e30
FORCED = 1e30

QB = 128
KT_NSA = 1024
KT_MLA = 2048
CMP_PAD = 16
LOC_W = 24


def _dot(a, b):
    return jnp.dot(a, b, preferred_element_type=F32)


def _dot_t(a, b):
    return lax.dot_general(a, b, (((1,), (1,)), ((), ())), preferred_element_type=F32)


def _lane_tiles(x):
    return [x[:, c * LANE:(c + 1) * LANE] for c in range(x.shape[1] // LANE)]


def _row_max(x):
    return jnp.max(functools.reduce(jnp.maximum, _lane_tiles(x)), axis=1, keepdims=True)


def _row_sum(x):
    return jnp.sum(functools.reduce(jnp.add, _lane_tiles(x)), axis=1, keepdims=True)


def _const_spec(shape):
    nd = len(shape)
    return pl.BlockSpec(shape, lambda *_: (0,) * nd)


def _inproj_kernel(x_ref, sh_ref, sc_ref, g_ref, cs_ref, sn_ref, wq_ref, wkv_ref, wg_ref, wqd_ref,
                   wkvd_ref, qn_ref, wqup_ref, bd_ref, plc_ref, kvn_ref,
                   qa_ref, cmp_ref, slc_ref, win_ref, kv16_ref, gt_ref, qm_ref, lat_ref, lat16_ref):
    x = x_ref[0]
    ms = jnp.mean(x * x, axis=-1, keepdims=True)
    xn = x * lax.rsqrt(ms + EPS) * g_ref[...]
    h = xn * (1.0 + sc_ref[0]) + sh_ref[0]
    hb = h.astype(BF16)
    qa_ref[0] = _dot(hb, wq_ref[...]).astype(BF16)
    kv = _dot(hb, wkv_ref[...])
    cmp_ref[0] = kv[:, 0:256]
    slc_ref[0] = kv[:, 256:512]
    win_ref[0] = kv[:, 512:768]
    kv16_ref[0] = kv.astype(BF16)
    gl = _dot(hb, wg_ref[...])
    gt_ref[0] = 1.0 / (1.0 + jnp.exp(-gl))
    qd = _dot(hb, wqd_ref[...])
    qn = qd * lax.rsqrt(jnp.sum(qd * qd, axis=-1, keepdims=True) * (1.0 / Q_LORA) + EPS) * qn_ref[...]
    qu = _dot(qn.astype(BF16), wqup_ref[...])
    cs = cs_ref[...]
    sn = sn_ref[...]
    qr = qu[:, 512:768] * cs + qu[:, 768:1024] * sn
    qm = _dot(qu[:, 0:512].astype(BF16), bd_ref[...]) + _dot(qr.astype(BF16), plc_ref[...])
    qm_ref[0] = (qm * MLA_SCALE).astype(BF16)
    kvd = _dot(hb, wkvd_ref[...])
    c = kvd[:, 0:128]
    ckv = c * lax.rsqrt(jnp.mean(c * c, axis=-1, keepdims=True) + EPS) * kvn_ref[...]
    kr = kvd[:, 128:256] * cs[:, 0:128] + kvd[:, 256:384] * sn[:, 0:128]
    lat_ref[0, :, 0:128] = ckv
    lat_ref[0, :, 128:160] = kr[:, 0:32]
    lat16_ref[0, :, 0:128] = ckv.astype(BF16)
    lat16_ref[0, :, 128:256] = kr.astype(BF16)


def _inproj_weights(w_in, q_norm, w_q_up, kv_norm, w_kv_up):
    D = w_in.shape[0]
    o1 = N_HEADS_A * HEAD_DIM
    o2 = o1 + 6 * N_KV_A * HEAD_DIM
    o3 = o2 + 3 * N_HEADS_A
    o4 = o3 + Q_LORA
    wq = w_in[:, :o1].reshape(D, N_HEADS_A, HEAD_DIM) * (HEAD_DIM ** -0.5)
    z = jnp.zeros_like(wq)
    grp = (jnp.arange(N_HEADS_A) // HPG)[None, :, None]
    wq_pad = jnp.concatenate([jnp.where(grp == 0, wq, z), jnp.where(grp == 1, wq, z)], axis=-1)
    wq_pad = wq_pad.reshape(D, N_HEADS_A * 2 * HEAD_DIM)
    wkv = w_in[:, o1:o2]
    wg = jnp.pad(w_in[:, o2:o3], ((0, 0), (0, LANE - 3 * N_HEADS_A)))
    wqd = jnp.pad(w_in[:, o3:o4], ((0, 0), (0, 256 - Q_LORA)))
    wkd = w_in[:, o4:]
    half = QK_ROPE // 2
    wc = wkd[:, :KV_LORA]
    wr = wkd[:, KV_LORA:]
    wrot = jnp.concatenate([-wr[:, half:], wr[:, :half]], axis=1)
    padr = ((0, 0), (0, LANE - QK_ROPE))
    wkvd = jnp.concatenate([wc, jnp.pad(wr, padr), jnp.pad(wrot, padr)], axis=1)
    qn = jnp.pad(q_norm, (0, 256 - Q_LORA)).reshape(1, 256)
    wu = jnp.pad(w_q_up, ((0, 256 - Q_LORA), (0, 0))).reshape(256, N_HEADS_B, QK_NOPE + QK_ROPE)
    wu_n = wu[:, :, :QK_NOPE].reshape(256, N_HEADS_B * QK_NOPE)
    wu_r = wu[:, :, QK_NOPE:]
    wu_rot = jnp.concatenate([-wu_r[:, :, half:], wu_r[:, :, :half]], axis=-1)
    wqup = jnp.concatenate([wu_n, wu_r.reshape(256, -1), wu_rot.reshape(256, -1)], axis=1)
    w_ukv = w_kv_up.reshape(KV_LORA, N_HEADS_B, QK_NOPE + V_DIM)
    w_uk = w_ukv[:, :, :QK_NOPE]
    eye = jnp.eye(N_HEADS_B, dtype=F32)
    bd = jnp.einsum('chn,hk->hnkc', w_uk, eye)
    bd = jnp.pad(bd, ((0, 0), (0, 0), (0, 0), (0, 256 - KV_LORA))).reshape(N_HEADS_B * QK_NOPE, N_HEADS_B * 256)
    plc = jnp.einsum('hk,rs->hrks', eye, jnp.eye(QK_ROPE, dtype=F32))
    plc = jnp.pad(plc, ((0, 0), (0, 0), (0, 0), (KV_LORA, 256 - KV_LORA - QK_ROPE)))
    plc = plc.reshape(N_HEADS_B * QK_ROPE, N_HEADS_B * 256)
    w_uv = jnp.pad(w_ukv[:, :, QK_NOPE:].transpose(1, 0, 2), ((0, 0), (0, 0), (0, LANE - V_DIM)))
    bf = lambda a: a.astype(BF16)
    return dict(wq=bf(wq_pad), wkv=bf(wkv), wg=bf(wg), wqd=bf(wqd), wkvd=bf(wkvd), qn=qn, wqup=bf(wqup),
                bd=bf(bd), plc=bf(plc), kvn=kv_norm.reshape(1, KV_LORA), w_uv=bf(w_uv))


def _rope_tables(pos):
    half = QK_ROPE // 2
    inv = ROPE_THETA ** (-jnp.arange(half, dtype=F32) / half)
    ang = pos.astype(F32)[:, None] * inv[None, :]
    cos = jnp.tile(jnp.cos(ang), (1, 2 * N_HEADS_B))
    sin = jnp.tile(jnp.sin(ang), (1, 2 * N_HEADS_B))
    return cos, sin


def _inproj(x, shift, scale, gain, cos, sin, w, tr):
    B, T, D = x.shape
    tm = shift.shape[1]
    mod_spec = pl.BlockSpec((1, tr if tm > 1 else 1, D), (lambda b, t: (b, t, 0)) if tm > 1 else (lambda b, t: (b, 0, 0)))
    row = lambda n: pl.BlockSpec((1, tr, n), lambda b, t: (b, t, 0))
    tab = pl.BlockSpec((tr, 256), lambda b, t: (t, 0))
    wnames = ['wq', 'wkv', 'wg', 'wqd', 'wkvd', 'qn', 'wqup', 'bd', 'plc', 'kvn']
    out_shape = [
        jax.ShapeDtypeStruct((B, T, 1024), BF16), jax.ShapeDtypeStruct((B, T, 256), F32),
        jax.ShapeDtypeStruct((B, T, 256), F32), jax.ShapeDtypeStruct((B, T, 256), F32),
        jax.ShapeDtypeStruct((B, T, 768), BF16), jax.ShapeDtypeStruct((B, T, LANE), F32),
        jax.ShapeDtypeStruct((B, T, 2048), BF16), jax.ShapeDtypeStruct((B, T, LATENT_DIM), F32),
        jax.ShapeDtypeStruct((B, T, 256), BF16)]
    in_specs = [row(D), mod_spec, mod_spec, _const_spec((1, D)), tab, tab]
    in_specs += [_const_spec(w[n].shape) for n in wnames[:5]]
    in_specs += [_const_spec(w['qn'].shape)] + [_const_spec(w[n].shape) for n in wnames[6:9]]
    in_specs += [_const_spec(w['kvn'].shape)]
    return pl.pallas_call(
        _inproj_kernel, grid=(B, T // tr), in_specs=in_specs,
        out_specs=[row(s.shape[-1]) for s in out_shape], out_shape=out_shape, name='inproj',
        compiler_params=pltpu.CompilerParams(dimension_semantics=('parallel', 'parallel'),
                                             vmem_limit_bytes=VMEM_LIMIT),
    )(x, shift, scale, gain.reshape(1, D), cos, sin, *[w[n] for n in wnames])


def _compress_kernel(x0_ref, x1_ref, w1_ref, pe_ref, w2a_ref, w2b_ref, o_ref, *, n_cmp):
    w1 = w1_ref[0]
    pp = _dot(pe_ref[0], w1)
    peh = pp[0:1, 0:CMP_HIDDEN] + pp[1:2, CMP_HIDDEN:]
    nc = x0_ref.shape[2]
    out = jnp.zeros((nc, LANE), F32)
    for x_ref, w2_ref in ((x0_ref, w2a_ref), (x1_ref, w2b_ref)):
        ab = _dot(x_ref[0, 0], w1)
        hid = ab[:, 0:CMP_HIDDEN] + pltpu.roll(ab[:, CMP_HIDDEN:], nc - 1, 0) + peh
        act = hid * (1.0 / (1.0 + jnp.exp(-hid)))
        out = out + _dot(act.astype(BF16), w2_ref[0])
    rows = lax.broadcasted_iota(jnp.int32, (nc, LANE), 0)
    out = jnp.where(rows < n_cmp, out, 0.0)
    o_ref[0, 0, 0:CMP_PAD] = jnp.zeros((CMP_PAD, LANE), F32)
    o_ref[0, 0, CMP_PAD:CMP_PAD + nc] = out
    o_ref[0, 0, CMP_PAD + nc:] = jnp.zeros((o_ref.shape[2] - CMP_PAD - nc, LANE), F32)


def _compress_weights(cmp_pe, cmp_w1, cmp_w2):
    kin = CMP_STRIDE * HEAD_DIM
    w1 = cmp_w1.reshape(2, 2, kin, CMP_HIDDEN)
    w1cat = jnp.concatenate([w1[:, 0], w1[:, 1]], axis=-1).astype(BF16)
    pe = jnp.pad(cmp_pe.reshape(2, 2, kin), ((0, 0), (0, 6), (0, 0))).astype(BF16)
    w2a = jnp.pad(cmp_w2, ((0, 0), (0, 0), (0, HEAD_DIM))).astype(BF16)
    w2b = jnp.pad(cmp_w2, ((0, 0), (0, 0), (HEAD_DIM, 0))).astype(BF16)
    return w1cat, pe, w2a, w2b


def _compress(xc, cw, n_cmp):
    B, _, nc, kin = xc.shape
    w1cat, pe, w2a, w2b = cw
    return pl.pallas_call(
        functools.partial(_compress_kernel, n_cmp=n_cmp), grid=(B, 2),
        in_specs=[pl.BlockSpec((1, 1, nc, kin), lambda b, j: (b, 2 * j, 0, 0)),
                  pl.BlockSpec((1, 1, nc, kin), lambda b, j: (b, 2 * j + 1, 0, 0)),
                  pl.BlockSpec((1, kin, 2 * CMP_HIDDEN), lambda b, j: (j, 0, 0)),
                  pl.BlockSpec((1, 8, kin), lambda b, j: (j, 0, 0)),
                  pl.BlockSpec((1, CMP_HIDDEN, LANE), lambda b, j: (j, 0, 0)),
                  pl.BlockSpec((1, CMP_HIDDEN, LANE), lambda b, j: (j, 0, 0))],
        out_specs=pl.BlockSpec((1, 1, nc + LANE, LANE), lambda b, j: (b, j, 0, 0)),
        out_shape=jax.ShapeDtypeStruct((B, 2, nc + LANE, LANE), F32), name='compress',
        compiler_params=pltpu.CompilerParams(dimension_semantics=('parallel', 'parallel'),
                                             vmem_limit_bytes=VMEM_LIMIT),
    )(xc, xc, w1cat, pe, w2a, w2b)


def _t5_bucket(rel):
    max_exact = N_BUCKETS // 2
    n = jnp.maximum(rel, 0)
    nf = jnp.maximum(n, 1).astype(F32)
    large = max_exact + (jnp.log(nf / max_exact) / math.log(MAX_DISTANCE / max_exact)
                         * (N_BUCKETS - max_exact)).astype(jnp.int32)
    large = jnp.minimum(large, N_BUCKETS - 1)
    return jnp.where(n < max_exact, n, large)


def _bias_tables(rel_bias):
    tbl = rel_bias.astype(F32)
    const = tbl[N_BUCKETS - 1]
    i = jnp.arange(QB)[:, None]
    j = jnp.arange(LANE)[None, :]

    def tab(rel):
        b = tbl[_t5_bucket(rel)]
        return jnp.moveaxis(b, -1, 0) - const[:, None, None]

    t0 = jnp.where((i - j >= 0)[None], tab(i - j), NEG)
    t1 = tab(QB + i - j)
    zero = jnp.zeros_like(t1)
    t4 = jnp.broadcast_to(jnp.where(j > i, 0.0, NEG)[None], t1.shape)
    tb = jnp.stack([t0, t1, zero, jnp.full_like(t1, NEG), t4]).reshape(5, N_HEADS_A * QB, LANE)
    rel_l = i - CMP_STRIDE * (j - CMP_PAD) - (CMP_BLOCK - 1)
    lb = jnp.where(((j < LOC_W) & (rel_l >= 0))[None], tab(rel_l), NEG).reshape(N_HEADS_A * QB, LANE)
    return tb, lb


def _hilo_dot(x, m):
    hi = x.astype(BF16)
    lo = (x - hi.astype(F32)).astype(BF16)
    return _dot(hi, m) + _dot(lo, m)


def _nsa_kernel(q_ref, gt_ref, ks_ref, vs_ref, kw_ref, vw_ref, kc_ref, vc_ref, tb_ref, lb_ref, pf_ref,
                o_ref, m_scr, l_scr, acc_scr, *, n_cmp):
    qb = pl.program_id(1)
    nh = N_HEADS_A
    rows = nh * QB
    q = q_ref[0]
    q8 = jnp.concatenate([q[:, h * LANE:(h + 1) * LANE] for h in range(nh)], axis=0)
    nc = pf_ref.shape[0]

    kcf = kc_ref[0, 0, 0:nc].astype(BF16)
    vcf = vc_ref[0, 0, 0:nc].astype(BF16)
    l0 = pl.multiple_of(qb * (QB // CMP_STRIDE), 8)
    kcl = kc_ref[0, 0, pl.ds(l0, LANE)].astype(BF16)
    vcl = vc_ref[0, 0, pl.ds(l0, LANE)].astype(BF16)
    colf = lax.broadcasted_iota(jnp.int32, (1, nc), 1)
    far_ok = jnp.where(colf >= CMP_PAD, jnp.where(colf < l0, 0.0, NEG), NEG)
    coll = lax.broadcasted_iota(jnp.int32, (1, LANE), 1) + (l0 - CMP_PAD)
    loc_ok = jnp.where(coll >= 0, jnp.where(coll < n_cmp, 0.0, NEG), NEG)
    s_far = _dot_t(q8, kcf) + far_ok
    s_loc = _dot_t(q8, kcl) + lb_ref[...] + loc_ok
    mrow = jnp.maximum(jnp.max(s_far, axis=1, keepdims=True), jnp.max(s_loc, axis=1, keepdims=True))
    p_far = jnp.exp(s_far - mrow)
    p_loc = jnp.exp(s_loc - mrow)
    lsum = jnp.sum(p_far, axis=1, keepdims=True) + jnp.sum(p_loc, axis=1, keepdims=True)
    inv = jnp.where(mrow > 0.5 * NEG, 1.0 / lsum, 0.0)
    p_far = p_far * inv
    p_loc = p_loc * inv
    o_c = _dot(p_far.astype(BF16), vcf) + _dot(p_loc.astype(BF16), vcl)

    r_i = lax.broadcasted_iota(jnp.int32, (LANE, LANE), 0)
    c_i = lax.broadcasted_iota(jnp.int32, (LANE, LANE), 1)
    pool_loc = jnp.where(r_i < LOC_W,
                         jnp.where(c_i == (r_i >> 2) + (2 * qb - CMP_PAD // SLC_PER_CMP), 1.0, 0.0),
                         0.0).astype(BF16)
    tq = 2 * qb + jnp.where(r_i >= SLC_BLOCK, 1, 0)
    dist = tq - c_i
    c_f = c_i.astype(F32)
    sel = []
    for g in range(N_KV_A):
        pgf = p_far[(g * HPG) * QB:(g * HPG + 1) * QB]
        pgl = p_loc[(g * HPG) * QB:(g * HPG + 1) * QB]
        for hh in range(1, HPG):
            pgf = pgf + p_far[(g * HPG + hh) * QB:(g * HPG + hh + 1) * QB]
            pgl = pgl + p_loc[(g * HPG + hh) * QB:(g * HPG + hh + 1) * QB]
        imp = _hilo_dot(pgf, pf_ref[...]) + _hilo_dot(pgl, pool_loc)
        score = jnp.where(dist < 0, NEG, jnp.where(dist < N_LOCAL_SLC, FORCED, jnp.where(c_i == 0, FORCED, imp)))
        chosen = jnp.zeros((QB, LANE), F32)
        for _ in range(N_SLC):
            mx = jnp.max(score, axis=1, keepdims=True)
            first = jnp.min(jnp.where(score == mx, c_f, float(LANE)), axis=1, keepdims=True)
            hit = c_f == first
            chosen = jnp.where(hit, 1.0, chosen)
            score = jnp.where(hit, -jnp.inf, score)
        sel.append(jnp.where(chosen > 0.5, 0.0, NEG).astype(BF16))

    m_scr[...] = jnp.full(m_scr.shape, -jnp.inf, F32)
    l_scr[...] = jnp.zeros(l_scr.shape, F32)
    acc_scr[...] = jnp.zeros(acc_scr.shape, F32)
    KT = KT_NSA
    nsub = KT // LANE
    qa8 = jnp.concatenate([q8, jnp.concatenate([sel[g] for g in range(N_KV_A) for _ in range(HPG)], axis=0)], axis=1)

    def slc_tile(kt, near):
        k0 = pl.multiple_of(kt * KT, KT)
        s = _dot_t(qa8, ks_ref[0, pl.ds(k0, KT), :])
        if near:
            bias = []
            for c in range(nsub):
                d = qb - (kt * nsub + c)
                bias.append(tb_ref[jnp.where(d < 0, 3, jnp.minimum(d, 2))])
            s = s + jnp.concatenate(bias, axis=1)
        m_old = m_scr[...]
        m_new = jnp.maximum(m_old, jnp.max(s, axis=1, keepdims=True))
        alpha = jnp.exp(m_old - m_new)
        p = jnp.exp(s - m_new)
        l_scr[...] = alpha * l_scr[...] + jnp.sum(p, axis=1, keepdims=True)
        acc_scr[...] = alpha * acc_scr[...] + _dot(p.astype(BF16), vs_ref[0, pl.ds(k0, KT), :])
        m_scr[...] = m_new

    n_tiles = qb // nsub + 1
    n_far = jnp.maximum(n_tiles - 2, 0)

    def far_tile(kt, carry):
        slc_tile(kt, False)
        return carry

    def near_tile(kt, carry):
        slc_tile(kt, True)
        return carry

    lax.fori_loop(0, n_far, far_tile, 0)
    lax.fori_loop(n_far, n_tiles, near_tile, 0)
    o_s = acc_scr[...] * (1.0 / l_scr[...])

    nwin = WINDOW // QB + 1
    w0 = jnp.maximum(qb - (nwin - 1), 0)
    k0 = pl.multiple_of(w0 * QB, QB)
    s = _dot_t(q8, kw_ref[0, pl.ds(k0, nwin * QB), :])
    bias = []
    for c in range(nwin):
        d = qb - (w0 + c)
        bias.append(tb_ref[jnp.where(d < 0, 3, jnp.where(d >= nwin - 1, 4, jnp.minimum(d, 2)))])
    s = s + jnp.concatenate(bias, axis=1)
    p = jnp.exp(s - jnp.max(s, axis=1, keepdims=True))
    o_w = _dot(p.astype(BF16), vw_ref[0, pl.ds(k0, nwin * QB), :]) * (1.0 / jnp.sum(p, axis=1, keepdims=True))

    gt = gt_ref[0]
    for h in range(nh):
        r = slice(h * QB, (h + 1) * QB)
        comb = (gt[:, h:h + 1] * o_c[r] + gt[:, nh + h:nh + h + 1] * o_s[r]
                + gt[:, 2 * nh + h:2 * nh + h + 1] * o_w[r])
        keep = (c_i >= HEAD_DIM) if h // HPG else (c_i < HEAD_DIM)
        o_ref[0, :, h * LANE:(h + 1) * LANE] = jnp.where(keep, comb, 0.0)


def _nsa_prompt(qa, gates, kv16, kc, tb, lb):
    B, S, _ = qa.shape
    nc = S // CMP_STRIDE
    n_cmp = (S - CMP_BLOCK) // CMP_STRIDE + 1
    m = jnp.arange(nc)
    pool_far = ((m[:, None] // SLC_PER_CMP - CMP_PAD // SLC_PER_CMP == jnp.arange(LANE)[None, :])
                & (m[:, None] >= CMP_PAD)).astype(BF16)
    blk_hot = (jnp.arange(S)[:, None] // SLC_BLOCK == jnp.arange(LANE)[None, :]).astype(BF16)
    ks_aug = jnp.concatenate([kv16[:, :, 2 * LANE:3 * LANE], jnp.broadcast_to(blk_hot[None], (B, S, LANE))], axis=-1)
    rows = N_HEADS_A * QB
    one = pl.Buffered(1)
    kvs = lambda c: pl.BlockSpec((1, S, LANE), lambda b, t: (b, 0, c), pipeline_mode=one)
    cspec = pl.BlockSpec((1, 1, nc + LANE, LANE), lambda b, t: (b, 0, 0, 0), pipeline_mode=one)
    vspec = pl.BlockSpec((1, 1, nc + LANE, LANE), lambda b, t: (b, 1, 0, 0), pipeline_mode=one)
    cst = lambda shape: pl.BlockSpec(shape, lambda b, t: (0,) * len(shape), pipeline_mode=one)
    return pl.pallas_call(
        functools.partial(_nsa_kernel, n_cmp=n_cmp), grid=(B, S // QB),
        in_specs=[pl.BlockSpec((1, QB, 1024), lambda b, t: (b, t, 0)),
                  pl.BlockSpec((1, QB, LANE), lambda b, t: (b, t, 0)),
                  pl.BlockSpec((1, S, 2 * LANE), lambda b, t: (b, 0, 0), pipeline_mode=one),
                  kvs(3), kvs(4), kvs(5), cspec, vspec, cst(tb.shape), cst(lb.shape), cst(pool_far.shape)],
        out_specs=pl.BlockSpec((1, QB, 1024), lambda b, t: (b, t, 0)),
        out_shape=jax.ShapeDtypeStruct((B, S, 1024), F32),
        scratch_shapes=[pltpu.VMEM((rows, 1), F32), pltpu.VMEM((rows, 1), F32), pltpu.VMEM((rows, LANE), F32)],
        name='nsa_prompt',
        compiler_params=pltpu.CompilerParams(dimension_semantics=('parallel', 'arbitrary'),
                                             vmem_limit_bytes=VMEM_LIMIT),
    )(qa, gates, ks_aug, kv16, kv16, kv16, kc, kc, tb, lb, pool_far)


def _mla_kernel(q_ref, lat_ref, wuv_ref, o_ref, m_scr, l_scr, acc_scr):
    qb = pl.program_id(1)
    nh = N_HEADS_B
    rows = nh * QB
    q = q_ref[0]
    q8 = jnp.concatenate([q[:, h * 256:(h + 1) * 256] for h in range(nh)], axis=0)
    m_scr[...] = jnp.full(m_scr.shape, -jnp.inf, F32)
    l_scr[...] = jnp.zeros(l_scr.shape, F32)
    acc_scr[...] = jnp.zeros(acc_scr.shape, F32)
    KT = KT_MLA
    nsub = KT // QB

    def tile(kt, masked):
        k0 = pl.multiple_of(kt * KT, KT)
        lat = lat_ref[0, pl.ds(k0, KT), :]
        s = _dot_t(q8, lat)
        if masked:
            col = lax.broadcasted_iota(jnp.int32, (rows, KT), 1)
            row = lax.broadcasted_iota(jnp.int32, (rows, KT), 0) & (QB - 1)
            s = jnp.where(col - row <= qb * QB - kt * KT, s, NEG)
        m_old = m_scr[...]
        m_new = jnp.maximum(m_old, _row_max(s))
        alpha = jnp.exp(m_old - m_new)
        p = jnp.exp(s - m_new)
        l_scr[...] = alpha * l_scr[...] + _row_sum(p)
        acc_scr[...] = alpha * acc_scr[...] + _dot(p.astype(BF16), lat[:, 0:KV_LORA])
        m_scr[...] = m_new

    def full_tile(kt, carry):
        tile(kt, False)
        return carry

    lax.fori_loop(0, qb // nsub, full_tile, 0)
    tile(qb // nsub, True)
    o_lat = (acc_scr[...] * (1.0 / l_scr[...])).astype(BF16)
    for h in range(nh):
        o_ref[0, :, h * LANE:(h + 1) * LANE] = _dot(o_lat[h * QB:(h + 1) * QB], wuv_ref[h])


def _mla_prompt(qm, lat16, w_uv):
    B, S, _ = qm.shape
    rows = N_HEADS_B * QB
    one = pl.Buffered(1)
    return pl.pallas_call(
        _mla_kernel, grid=(B, S // QB),
        in_specs=[pl.BlockSpec((1, QB, 2048), lambda b, t: (b, t, 0)),
                  pl.BlockSpec((1, S, 256), lambda b, t: (b, 0, 0), pipeline_mode=one),
                  pl.BlockSpec(w_uv.shape, lambda b, t: (0, 0, 0), pipeline_mode=one)],
        out_specs=pl.BlockSpec((1, QB, 1024), lambda b, t: (b, t, 0)),
        out_shape=jax.ShapeDtypeStruct((B, S, 1024), F32),
        scratch_shapes=[pltpu.VMEM((rows, 1), F32), pltpu.VMEM((rows, 1), F32), pltpu.VMEM((rows, KV_LORA), F32)],
        name='mla_prompt',
        compiler_params=pltpu.CompilerParams(dimension_semantics=('parallel', 'arbitrary'),
                                             vmem_limit_bytes=VMEM_LIMIT),
    )(qm, lat16, w_uv)


def _adaln_kernel(c_ref, w_ref, b_ref, o_ref):
    c = c_ref[...]
    a = (c * (1.0 / (1.0 + jnp.exp(-c)))).astype(BF16)
    o_ref[...] = _dot(a, w_ref[...].astype(BF16)) + b_ref[...]


def _adaln(c, w_ada, b_ada, tn=512):
    R_, D = c.shape
    N = w_ada.shape[1]
    return pl.pallas_call(
        _adaln_kernel, grid=(N // tn,),
        in_specs=[pl.BlockSpec((R_, D), lambda j: (0, 0)), pl.BlockSpec((D, tn), lambda j: (0, j)),
                  pl.BlockSpec((1, tn), lambda j: (0, j))],
        out_specs=pl.BlockSpec((R_, tn), lambda j: (0, j)),
        out_shape=jax.ShapeDtypeStruct((R_, N), F32), name='adaln',
        compiler_params=pltpu.CompilerParams(dimension_semantics=('parallel',), vmem_limit_bytes=VMEM_LIMIT),
    )(c, w_ada, b_ada.reshape(1, N))


def _merge_kernel(x_ref, oa_ref, ob_ref, ga_ref, shf_ref, scf_ref, gf_ref, na_ref, nb_ref, nffn_ref,
                  wa_ref, wb_ref, wr_ref, wgs_ref, wus_ref, wds_ref, xs_ref, f_ref, sc_ref):
    n_real = N_HEADS_A * HEAD_DIM
    oa = oa_ref[0]
    ob = ob_ref[0]
    na = oa * lax.rsqrt(jnp.sum(oa * oa, axis=-1, keepdims=True) * (1.0 / n_real) + EPS) * na_ref[...]
    nb = ob * lax.rsqrt(jnp.sum(ob * ob, axis=-1, keepdims=True) * (1.0 / n_real) + EPS) * nb_ref[...]
    mix = _dot(na.astype(BF16), wa_ref[...]) + _dot(nb.astype(BF16), wb_ref[...])
    x1 = x_ref[0] + ga_ref[0] * mix
    f = x1 * lax.rsqrt(jnp.mean(x1 * x1, axis=-1, keepdims=True) + EPS) * nffn_ref[...]
    f = f * (1.0 + scf_ref[0]) + shf_ref[0]
    for c in range(f.shape[1] // LANE):
        f_ref[0, :, c, :] = f[:, c * LANE:(c + 1) * LANE]
    fb = f.astype(BF16)
    sc_ref[0] = 1.0 / (1.0 + jnp.exp(-_dot(fb, wr_ref[...])))
    g = _dot(fb, wgs_ref[...])
    u = _dot(fb, wus_ref[...])
    hsh = (g * (1.0 / (1.0 + jnp.exp(-g))) * u).astype(BF16)
    xs_ref[0] = x1 + gf_ref[0] * _dot(hsh, wds_ref[...])


def _merge_weights(out_norm_a, out_norm_b, w_out, norm_ffn, w_router, w_gate_s, w_up_s, w_down_s):
    D = w_out.shape[1]
    na = out_norm_a.reshape(N_HEADS_A, 1, HEAD_DIM)
    grp = (jnp.arange(N_HEADS_A) // HPG)[:, None, None]
    half = jnp.arange(2)[None, :, None]
    na_pad = jnp.where(grp == half, na, 0.0).reshape(1, -1)
    nb_pad = jnp.pad(out_norm_b.reshape(N_HEADS_B, V_DIM), ((0, 0), (0, LANE - V_DIM))).reshape(1, -1)
    wa = w_out[:N_HEADS_A * HEAD_DIM].reshape(N_HEADS_A, 1, HEAD_DIM, D)
    wa_pad = jnp.where((grp == half)[..., None], wa, 0.0).reshape(-1, D)
    wb = w_out[N_HEADS_A * HEAD_DIM:].reshape(N_HEADS_B, V_DIM, D)
    wb_pad = jnp.pad(wb, ((0, 0), (0, LANE - V_DIM), (0, 0))).reshape(-1, D)
    bf = lambda a: a.astype(BF16)
    return [na_pad, nb_pad, norm_ffn.reshape(1, D), bf(wa_pad), bf(wb_pad), bf(w_router), bf(w_gate_s),
            bf(w_up_s), bf(w_down_s)]


def _mod_spec(a, tr):
    if a.shape[1] > 1:
        return pl.BlockSpec((1, tr, a.shape[2]), lambda b, t: (b, t, 0))
    return pl.BlockSpec((1, 1, a.shape[2]), lambda b, t: (b, 0, 0))


def _merge(x, oa, ob, gate_a, shift_f, scale_f, gate_f, mw, tr):
    B, T, D = x.shape
    row = lambda n: pl.BlockSpec((1, tr, n), lambda b, t: (b, t, 0))
    out_shape = [jax.ShapeDtypeStruct((B, T, D), F32), jax.ShapeDtypeStruct((B, T, D // LANE, LANE), F32),
                 jax.ShapeDtypeStruct((B, T, N_EXPERTS), F32)]
    return pl.pallas_call(
        _merge_kernel, grid=(B, T // tr),
        in_specs=[row(D), row(1024), row(1024)] + [_mod_spec(a, tr) for a in (gate_a, shift_f, scale_f, gate_f)]
        + [_const_spec(a.shape) for a in mw],
        out_specs=[row(D), pl.BlockSpec((1, tr, D // LANE, LANE), lambda b, t: (b, t, 0, 0)), row(N_EXPERTS)],
        out_shape=out_shape, name='merge',
        compiler_params=pltpu.CompilerParams(dimension_semantics=('parallel', 'parallel'),
                                             vmem_limit_bytes=VMEM_LIMIT),
    )(x, oa, ob, gate_a, shift_f, scale_f, gate_f, *mw)


MOE_CHUNK = 256
MOE_LOOK = 2


def _moe_kernel(nused_ref, bexp_ref, seg_ref, segexp_ref, nseg_ref, rtok_hbm, f_hbm, wg_hbm, wu_hbm, wd_hbm, y_ref,
                xbuf, wgf, wuf, wdf, wgb, wub, wdb, rtok, sem, isem, wsem):
    i = pl.program_id(0)
    nused = nused_ref[0]
    nchunk = rtok_hbm.shape[0] // MOE_CHUNK

    def weight_copies(j):
        e = segexp_ref[j]
        return [pltpu.make_async_copy(src.at[e], dst.at[j % 2], wsem.at[j % 2])
                for src, dst in ((wg_hbm, wgf), (wu_hbm, wuf), (wd_hbm, wdf))]

    def ids_copy(c):
        return pltpu.make_async_copy(rtok_hbm.at[pl.ds(c * MOE_CHUNK, MOE_CHUNK)], rtok.at[c % 2], isem.at[c % 2])

    nbuf = MOE_LOOK + 1
    nct = f_hbm.shape[1]

    def gather(blk):
        ids = rtok.at[(blk // MOE_CHUNK) % 2]
        row = blk % MOE_CHUNK
        for r in range(MOE_BLOCK):
            pltpu.make_async_copy(f_hbm.at[pl.ds(ids[row, r], 1)], xbuf.at[blk % nbuf, pl.ds(r, 1)],
                                  sem.at[blk % nbuf]).start()

    def wait_gather(blk):
        pltpu.make_async_copy(f_hbm.at[pl.ds(0, MOE_BLOCK)], xbuf.at[blk % nbuf], sem.at[blk % nbuf]).wait()

    @pl.when(i == 0)
    def _():
        ids_copy(0).start()
        ids_copy(0).wait()
        if nchunk > 1:
            ids_copy(1).start()
        for b in range(MOE_LOOK):
            gather(b)
        for cp in weight_copies(0):
            cp.start()

    @pl.when((i == 0) | (bexp_ref[i] != bexp_ref[jnp.maximum(i - 1, 0)]))
    def _():
        j = seg_ref[i]
        for cp in weight_copies(j):
            cp.wait()

        @pl.when(j + 1 < nseg_ref[0])
        def _():
            for cp in weight_copies(j + 1):
                cp.start()
        wgb[...] = wgf[j % 2].astype(BF16)
        wub[...] = wuf[j % 2].astype(BF16)
        wdb[...] = wdf[j % 2].astype(BF16)

    @pl.when(((i + MOE_LOOK) % MOE_CHUNK == 0) & ((i + MOE_LOOK) // MOE_CHUNK < nchunk))
    def _():
        c = (i + MOE_LOOK) // MOE_CHUNK
        ids_copy(c).wait()

        @pl.when(c + 1 < nchunk)
        def _():
            ids_copy(c + 1).start()

    @pl.when(i < nused)
    def _():
        wait_gather(i)
        gather(i + MOE_LOOK)
        slot = i % nbuf
        g = jnp.zeros((MOE_BLOCK, wgb.shape[1]), F32)
        u = jnp.zeros((MOE_BLOCK, wub.shape[1]), F32)
        for c in range(0, nct, 2):
            x = jnp.concatenate([xbuf[slot, :, c, :], xbuf[slot, :, c + 1, :]], axis=1).astype(BF16)
            g = g + _dot(x, wgb[c * LANE:(c + 2) * LANE, :])
            u = u + _dot(x, wub[c * LANE:(c + 2) * LANE, :])
        h = (g * (1.0 / (1.0 + jnp.exp(-g))) * u).astype(BF16)
        y = _dot(h, wdb[...])
        for c in range(nct):
            y_ref[:, c, :] = y[:, c * LANE:(c + 1) * LANE]

    @pl.when(i >= nused)
    def _():
        @pl.when(i < nused + MOE_LOOK)
        def _():
            wait_gather(i)
        y_ref[...] = jnp.zeros(y_ref.shape, F32)


def _moe_experts(nused, blk_exp, row_tok, f, w_gate_e, w_up_e, w_down_e):
    n_blocks = blk_exp.shape[0]
    n_steps = n_blocks + MOE_LOOK
    nb_pad = -(-n_steps // MOE_CHUNK) * MOE_CHUNK
    row_tok = jnp.pad(row_tok, (0, (nb_pad - n_blocks) * MOE_BLOCK)).reshape(nb_pad, MOE_BLOCK)
    blk_exp = jnp.pad(blk_exp, (0, MOE_LOOK), mode='edge')
    nct = f.shape[1]
    D = nct * LANE
    ne, _, de = w_gate_e.shape
    new_run = jnp.concatenate([jnp.ones((1,), jnp.int32), (blk_exp[1:] != blk_exp[:-1]).astype(jnp.int32)])
    seg = jnp.cumsum(new_run) - 1
    seg_exp = jnp.zeros((ne + 1,), jnp.int32).at[seg].set(blk_exp)
    nseg = (seg[-1] + 1).reshape(1)
    hbm = pl.BlockSpec(memory_space=pl.ANY)
    return pl.pallas_call(
        _moe_kernel,
        grid_spec=pltpu.PrefetchScalarGridSpec(
            num_scalar_prefetch=5, grid=(n_steps,),
            in_specs=[hbm, hbm, hbm, hbm, hbm],
            out_specs=pl.BlockSpec((MOE_BLOCK, nct, LANE), lambda i, *_: (i, 0, 0)),
            scratch_shapes=[pltpu.VMEM((MOE_LOOK + 1, MOE_BLOCK, nct, LANE), F32),
                            pltpu.VMEM((2, D, de), F32), pltpu.VMEM((2, D, de), F32), pltpu.VMEM((2, de, D), F32),
                            pltpu.VMEM((D, de), BF16), pltpu.VMEM((D, de), BF16), pltpu.VMEM((de, D), BF16),
                            pltpu.SMEM((2, MOE_CHUNK, MOE_BLOCK), jnp.int32),
                            pltpu.SemaphoreType.DMA((MOE_LOOK + 1,)), pltpu.SemaphoreType.DMA((2,)),
                            pltpu.SemaphoreType.DMA((2,))]),
        out_shape=jax.ShapeDtypeStruct((n_steps * MOE_BLOCK, nct, LANE), F32), name='moe_experts',
        compiler_params=pltpu.CompilerParams(dimension_semantics=('arbitrary',), vmem_limit_bytes=VMEM_LIMIT),
    )(nused, blk_exp, seg, seg_exp, nseg, row_tok, f, w_gate_e, w_up_e, w_down_e)


def _combine_kernel(pos_ref, yb_hbm, xs_ref, gf_ref, w_ref, nf_ref, o_ref, buf, sem):
    tile = pl.program_id(0) * pl.num_programs(1) + pl.program_id(1)
    ntile = pl.num_programs(0) * pl.num_programs(1)

    nct = yb_hbm.shape[1]

    def start(t, slot):
        def per_k(k, c):
            for r in range(MOE_BLOCK):
                pltpu.make_async_copy(yb_hbm.at[pl.ds(pos_ref[t * TOP_K + k, r], 1)],
                                      buf.at[slot, k, pl.ds(r, 1)], sem.at[slot]).start()
            return c
        lax.fori_loop(0, TOP_K, per_k, 0)

    @pl.when(tile == 0)
    def _():
        start(0, 0)

    @pl.when(tile + 1 < ntile)
    def _():
        start(tile + 1, (tile + 1) % 2)

    slot = tile % 2
    w = w_ref[0]
    for k in range(TOP_K):
        pltpu.make_async_copy(yb_hbm.at[pl.ds(0, MOE_BLOCK)], buf.at[slot, k], sem.at[slot]).wait()
    x2 = []
    ss = jnp.zeros((MOE_BLOCK, 1), F32)
    for c in range(nct):
        routed = jnp.zeros((MOE_BLOCK, LANE), F32)
        for k in range(TOP_K):
            routed = routed + w[:, k:k + 1] * buf[slot, k, :, c, :]
        cols = slice(c * LANE, (c + 1) * LANE)
        xc = xs_ref[0, :, cols] + gf_ref[0, :, cols] * routed
        ss = ss + jnp.sum(xc * xc, axis=-1, keepdims=True)
        x2.append(xc)
    inv = lax.rsqrt(ss * (1.0 / (nct * LANE)) + EPS)
    for c in range(nct):
        cols = slice(c * LANE, (c + 1) * LANE)
        o_ref[0, :, cols] = x2[c] * inv * nf_ref[:, cols]


def _combine(pos, yb, xs, gate_f, wts, norm_final):
    B, T, D = xs.shape
    tr = MOE_BLOCK
    gspec = (pl.BlockSpec((1, tr, D), lambda b, t, p: (b, t, 0)) if gate_f.shape[1] > 1
             else pl.BlockSpec((1, 1, D), lambda b, t, p: (b, 0, 0)))
    return pl.pallas_call(
        _combine_kernel,
        grid_spec=pltpu.PrefetchScalarGridSpec(
            num_scalar_prefetch=1, grid=(B, T // tr),
            in_specs=[pl.BlockSpec(memory_space=pl.ANY), pl.BlockSpec((1, tr, D), lambda b, t, p: (b, t, 0)), gspec,
                      pl.BlockSpec((1, tr, TOP_K), lambda b, t, p: (b, t, 0)),
                      pl.BlockSpec((1, D), lambda b, t, p: (0, 0))],
            out_specs=pl.BlockSpec((1, tr, D), lambda b, t, p: (b, t, 0)),
            scratch_shapes=[pltpu.VMEM((2, TOP_K, tr, D // LANE, LANE), F32), pltpu.SemaphoreType.DMA((2,))]),
        out_shape=jax.ShapeDtypeStruct((B, T, D), F32), name='combine',
        compiler_params=pltpu.CompilerParams(dimension_semantics=('arbitrary', 'arbitrary'),
                                             vmem_limit_bytes=VMEM_LIMIT),
    )(pos, yb, xs, gate_f, wts, norm_final.reshape(1, D))


def _assign_rows_kernel(idx_ref, rank_ref, start_ref, pos_ref):
    idx = idx_ref[...]
    tr = idx.shape[0]
    start = start_ref[...]
    lane = lax.broadcasted_iota(jnp.int32, (tr, start.shape[1]), 1)
    kcol = lax.broadcasted_iota(jnp.int32, (tr, TOP_K), 1)
    base = jnp.zeros((tr, TOP_K), F32)
    for k in range(TOP_K):
        sk = jnp.sum(jnp.where(lane == idx[:, k:k + 1], start, 0.0), axis=1, keepdims=True)
        base = jnp.where(kcol == k, sk, base)
    pos_ref[...] = base.astype(jnp.int32) + rank_ref[...]


def _assign_rows(idx, rank, start):
    n_tok = idx.shape[0]
    tspec = pl.BlockSpec((MOE_BLOCK, TOP_K), lambda i: (i, 0))
    return pl.pallas_call(
        _assign_rows_kernel, grid=(n_tok // MOE_BLOCK,),
        in_specs=[tspec, tspec, pl.BlockSpec(start.shape, lambda i: (0, 0))], out_specs=tspec,
        out_shape=jax.ShapeDtypeStruct((n_tok, TOP_K), jnp.int32), name='assign_rows',
        compiler_params=pltpu.CompilerParams(dimension_semantics=('parallel',), vmem_limit_bytes=VMEM_LIMIT),
    )(idx, rank, start)


def _route(scores, router_bias):
    n_tok = scores.shape[0]
    n_asg = n_tok * TOP_K
    idx, wts, rank, counts = _route_rank(scores, router_bias.astype(F32).reshape(1, N_EXPERTS))
    padded = (counts + MOE_BLOCK - 1) // MOE_BLOCK * MOE_BLOCK
    pad_end = jnp.cumsum(padded)
    pos = _assign_rows(idx, rank, (pad_end - padded).astype(F32).reshape(1, N_EXPERTS))
    n_blocks = -(-(n_asg + N_EXPERTS * (MOE_BLOCK - 1)) // MOE_BLOCK)
    n_rows = n_blocks * MOE_BLOCK
    tok = jnp.broadcast_to(jnp.arange(n_tok, dtype=jnp.int32)[:, None], pos.shape)
    row_tok = jnp.zeros((n_rows,), jnp.int32).at[pos.reshape(-1)].set(tok.reshape(-1))
    blk_start = jnp.arange(n_blocks, dtype=jnp.int32)[:, None] * MOE_BLOCK
    blk_exp = jnp.minimum(jnp.sum((pad_end[None, :] <= blk_start).astype(jnp.int32), axis=1), N_EXPERTS - 1)
    nused = (pad_end[-1] // MOE_BLOCK).astype(jnp.int32).reshape(1)
    return wts, pos, row_tok, blk_exp, nused


def _route_kernel(sc_ref, rb_ref, idx_ref, wts_ref, rank_ref, cnt_ref, carry):
    i = pl.program_id(0)

    @pl.when(i == 0)
    def _():
        carry[...] = jnp.zeros(carry.shape, F32)

    tr, ne = sc_ref.shape
    gsz = ne // N_GROUPS
    scores = sc_ref[...]
    biased = scores + rb_ref[...]
    lane = lax.broadcasted_iota(jnp.int32, (tr, ne), 1)
    lane_f = lane.astype(F32)
    lgrp = lane // gsz
    ninf = -jnp.inf

    def first_max(x):
        mx = jnp.max(x, axis=1, keepdims=True)
        return mx, jnp.min(jnp.where(x == mx, lane_f, float(ne)), axis=1, keepdims=True)

    gl = lax.broadcasted_iota(jnp.int32, (tr, LANE), 1)
    gscore = jnp.full((tr, LANE), ninf, F32)
    for g in range(N_GROUPS):
        xg = jnp.where(lgrp == g, biased, ninf)
        m1, f1 = first_max(xg)
        m2 = jnp.max(jnp.where(lane_f == f1, ninf, xg), axis=1, keepdims=True)
        gscore = jnp.where(gl == g, m1 + m2, gscore)
    gl_f = gl.astype(F32)
    keep = jnp.full((tr, ne), NEG, F32)
    for _ in range(TOP_GROUPS):
        mx = jnp.max(gscore, axis=1, keepdims=True)
        gf = jnp.min(jnp.where(gscore == mx, gl_f, float(LANE)), axis=1, keepdims=True)
        gscore = jnp.where(gl_f == gf, ninf, gscore)
        keep = jnp.where(lgrp.astype(F32) == gf, biased, keep)
    kcol = lax.broadcasted_iota(jnp.int32, (tr, TOP_K), 1)
    idx = jnp.zeros((tr, TOP_K), F32)
    wts = jnp.zeros((tr, TOP_K), F32)
    hot = []
    onehot = jnp.zeros((tr, ne), F32)
    for k in range(TOP_K):
        _, f = first_max(keep)
        hit = lane_f == f
        hot.append(hit)
        keep = jnp.where(hit, ninf, keep)
        onehot = jnp.where(hit, 1.0, onehot)
        idx = jnp.where(kcol == k, f, idx)
        wts = jnp.where(kcol == k, jnp.sum(jnp.where(hit, scores, 0.0), axis=1, keepdims=True), wts)
    idx_ref[...] = idx.astype(jnp.int32)
    wts_ref[...] = wts / jnp.sum(wts, axis=1, keepdims=True) * ROUTED_SCALE
    r_i = lax.broadcasted_iota(jnp.int32, (tr, tr), 0)
    c_i = lax.broadcasted_iota(jnp.int32, (tr, tr), 1)
    lower = jnp.where(c_i < r_i, 1.0, 0.0).astype(BF16)
    before = _dot(lower, onehot.astype(BF16)) + carry[0:1, :]
    rank = jnp.zeros((tr, TOP_K), F32)
    for k in range(TOP_K):
        rank = jnp.where(kcol == k, jnp.sum(jnp.where(hot[k], before, 0.0), axis=1, keepdims=True), rank)
    rank_ref[...] = rank.astype(jnp.int32)
    total = carry[0:1, :] + jnp.sum(onehot, axis=0, keepdims=True)
    carry[...] = jnp.broadcast_to(total, carry.shape)
    cnt_ref[...] = jnp.broadcast_to(total, cnt_ref.shape).astype(jnp.int32)


def _route_rank(scores, router_bias):
    n_tok, ne = scores.shape
    tr = MOE_BLOCK
    tk = lambda dt: jax.ShapeDtypeStruct((n_tok, TOP_K), dt)
    tspec = pl.BlockSpec((tr, TOP_K), lambda i: (i, 0))
    idx, wts, rank, cnt = pl.pallas_call(
        _route_kernel, grid=(n_tok // tr,),
        in_specs=[pl.BlockSpec((tr, ne), lambda i: (i, 0)), pl.BlockSpec((1, ne), lambda i: (0, 0))],
        out_specs=[tspec, tspec, tspec, pl.BlockSpec((8, ne), lambda i: (0, 0))],
        out_shape=[tk(jnp.int32), tk(F32), tk(jnp.int32), jax.ShapeDtypeStruct((8, ne), jnp.int32)],
        scratch_shapes=[pltpu.VMEM((8, ne), F32)], name='route',
        compiler_params=pltpu.CompilerParams(dimension_semantics=('arbitrary',), vmem_limit_bytes=VMEM_LIMIT),
    )(scores, router_bias)
    return idx, wts, rank, cnt[0]


PAGE = 128


def _softmax_with_new(s, s_new):
    m = jnp.maximum(jnp.max(s, axis=1, keepdims=True), s_new)
    p = jnp.exp(s - m)
    pn = jnp.exp(s_new - m)
    return p, pn, 1.0 / (jnp.sum(p, axis=1, keepdims=True) + pn)


def _samp_cmp_kernel(pt_ref, pool_hbm, q_ref, wbd_ref, pe_ref, w1c_ref, w2_ref, bc_ref, pf_ref,
                     oc_ref, idx_ref, buf_t, buf, peh_scr, sem, *, n_pages, n_cmp):
    s = pl.program_id(0)
    ns = pl.num_programs(0)
    nc = n_pages * PAGE // CMP_STRIDE

    def page_copy(page, slot, p):
        return pltpu.make_async_copy(pool_hbm.at[page], buf_t.at[slot, :, :, pl.ds(p * PAGE, PAGE)], sem.at[slot])

    def start(smp, slot):
        def body(p, c):
            page_copy(pt_ref[smp, p], slot, p).start()
            return c
        lax.fori_loop(0, n_pages, body, 0)

    @pl.when(s == 0)
    def _():
        start(0, 0)
        for j in range(2):
            pp = _dot(pe_ref[j], w1c_ref[j])
            peh_scr[j] = jnp.broadcast_to(pp[0:1, 0:CMP_HIDDEN] + pp[1:2, CMP_HIDDEN:], (8, CMP_HIDDEN))

    @pl.when(s + 1 < ns)
    def _():
        start(s + 1, (s + 1) % 2)

    slot = s % 2

    def wait_page(p, c):
        page_copy(0, slot, p).wait()
        return c
    lax.fori_loop(0, n_pages, wait_page, 0)

    tw = 4 * PAGE

    def to_rows(c, carry):
        l0 = pl.multiple_of(c * tw, tw)
        for j in range(2):
            xt = buf_t[slot, pl.ds(2 * j, 2), :, pl.ds(l0, tw)].reshape(2 * HEAD_DIM, tw)
            buf[j, pl.ds(l0, tw), :] = xt.T
        return carry
    lax.fori_loop(0, n_pages * PAGE // tw, to_rows, 0)

    rows = lax.broadcasted_iota(jnp.int32, (nc, LANE), 0)
    kvc = []
    for j in range(2):
        acc = jnp.zeros((nc, 4 * CMP_HIDDEN), F32)
        for r in range(0, CMP_STRIDE, 2):
            x = jnp.concatenate([buf[j, pl.ds(r, nc, stride=CMP_STRIDE), :],
                                 buf[j, pl.ds(r + 1, nc, stride=CMP_STRIDE), :]], axis=1).astype(BF16)
            acc = acc + _dot(x, wbd_ref[j, r // 2])
        out = jnp.zeros((nc, LANE), F32)
        for g in range(N_KV_A):
            a = acc[:, g * 2 * CMP_HIDDEN:g * 2 * CMP_HIDDEN + CMP_HIDDEN]
            bm = acc[:, g * 2 * CMP_HIDDEN + CMP_HIDDEN:(g + 1) * 2 * CMP_HIDDEN]
            hid = a + pltpu.roll(bm, nc - 1, 0) + peh_scr[j, 0:1]
            act = hid * (1.0 / (1.0 + jnp.exp(-hid)))
            out = out + _dot(act.astype(BF16), w2_ref[j, g])
        kvc.append(jnp.where(rows < n_cmp, out, 0.0).astype(BF16))
    kc, vc = kvc

    q8 = q_ref[0]
    sc = _dot_t(q8, kc) + bc_ref[...]
    p = jnp.exp(sc - jnp.max(sc, axis=1, keepdims=True))
    p = p * (1.0 / jnp.sum(p, axis=1, keepdims=True))
    oc_ref[0] = _dot(p.astype(BF16), vc)

    pg = jnp.concatenate([jnp.sum(p[0:HPG], axis=0, keepdims=True), jnp.sum(p[HPG:2 * HPG], axis=0, keepdims=True),
                          jnp.zeros((8 - N_KV_A, nc), F32)], axis=0)
    imp = _hilo_dot(pg, pf_ref[...])
    lane = lax.broadcasted_iota(jnp.int32, (8, LANE), 1).astype(F32)
    score = jnp.where(lane == 0.0, FORCED, jnp.where(lane >= float(LANE - N_LOCAL_SLC + 1), FORCED, imp))
    picks = jnp.full((8, LANE), float(LANE), F32)
    for k in range(N_SLC - 1):
        mx = jnp.max(score, axis=1, keepdims=True)
        first = jnp.min(jnp.where(score == mx, lane, float(LANE)), axis=1, keepdims=True)
        picks = jnp.where(lane == float(k), first, picks)
        score = jnp.where(lane == first, -jnp.inf, score)
    idx_ref[0] = picks.astype(jnp.int32)


def _samp_cmp(page_table, pool2d, q3, scw, bc, n_cmp):
    Bd, n_pages = page_table.shape
    nc = n_pages * PAGE // CMP_STRIDE
    wbd, pe, w1c, w2g = scw
    m = jnp.arange(nc)
    pool_m = ((m[:, None] // SLC_PER_CMP == jnp.arange(LANE)[None, :]) & (m[:, None] < n_cmp)).astype(BF16)
    one = pl.Buffered(1)
    cst = lambda a: pl.BlockSpec(a.shape, lambda s, pt: (0,) * a.ndim, pipeline_mode=one)
    return pl.pallas_call(
        functools.partial(_samp_cmp_kernel, n_pages=n_pages, n_cmp=n_cmp),
        grid_spec=pltpu.PrefetchScalarGridSpec(
            num_scalar_prefetch=1, grid=(Bd,),
            in_specs=[pl.BlockSpec(memory_space=pl.ANY), pl.BlockSpec((1, 8, LANE), lambda s, pt: (s, 0, 0)),
                      cst(wbd), cst(pe), cst(w1c), cst(w2g), cst(bc), cst(pool_m)],
            out_specs=[pl.BlockSpec((1, 8, LANE), lambda s, pt: (s, 0, 0)),
                       pl.BlockSpec((1, 8, LANE), lambda s, pt: (s, 0, 0))],
            scratch_shapes=[pltpu.VMEM((2, 4, HEAD_DIM, n_pages * PAGE), F32),
                            pltpu.VMEM((2, n_pages * PAGE, LANE), F32), pltpu.VMEM((2, 8, CMP_HIDDEN), F32),
                            pltpu.SemaphoreType.DMA((2,))]),
        out_shape=[jax.ShapeDtypeStruct((Bd, 8, LANE), F32), jax.ShapeDtypeStruct((Bd, 8, LANE), jnp.int32)],
        name='sample_cmp',
        compiler_params=pltpu.CompilerParams(dimension_semantics=('arbitrary',), vmem_limit_bytes=VMEM_LIMIT),
    )(page_table, pool2d, q3, wbd, pe, w1c, w2g, bc, pool_m)


def _samp_cmp_weights(cw):
    w1cat, pe, w2a, w2b = cw
    w = w1cat.reshape(2, CMP_STRIDE, HEAD_DIM, 2 * CMP_HIDDEN)
    z = jnp.zeros_like(w)
    wbd = jnp.concatenate([jnp.concatenate([w, z], axis=-1), jnp.concatenate([z, w], axis=-1)], axis=2)
    wbd = wbd.reshape(2, CMP_STRIDE // 2, 2 * LANE, 4 * CMP_HIDDEN)
    return wbd, pe, w1cat, jnp.stack([w2a, w2b], axis=1)


def _t5_bias_rows(rel_bias, rel, valid):
    b = rel_bias.astype(F32)[_t5_bucket(rel)]
    return jnp.where(valid[None, :], b.T, NEG)


def _bucket_bias(rel, tbl_t):
    max_exact = N_BUCKETS // 2
    nf = jnp.maximum(rel, 1).astype(F32)
    large = max_exact + (jnp.log(nf / max_exact) / math.log(MAX_DISTANCE / max_exact)
                         * (N_BUCKETS - max_exact)).astype(jnp.int32)
    bucket = jnp.where(rel < max_exact, rel, jnp.minimum(large, N_BUCKETS - 1))
    bias = jnp.zeros(rel.shape, F32)
    for b in range(N_BUCKETS):
        bias = jnp.where(bucket == b, tbl_t[:, b:b + 1], bias)
    return bias


def _samp_sw_kernel(pt_ref, idx_ref, pool_hbm, q_ref, knew_ref, win_ref, wnew_ref, wcol_ref, oc_ref, gcol_ref,
                    tblt_ref, bw_ref, oa_ref, nwin_ref, kvbuf, sem, *, past_len):
    s = pl.program_id(0)
    ns = pl.num_programs(0)
    npb = past_len // SLC_BLOCK
    bpp = PAGE // SLC_BLOCK
    nk = N_SLC * PAGE

    def block_copy(page, slot, g, k, kv):
        return pltpu.make_async_copy(pool_hbm.at[page, kv * N_KV_A + g],
                                     kvbuf.at[slot, kv * N_KV_A + g, :, pl.ds(k * PAGE, PAGE)], sem.at[slot])

    def start(smp, slot):
        for g in range(N_KV_A):
            for k in range(N_SLC):
                j = jnp.minimum(idx_ref[(smp * N_KV_A + g) * N_SLC + k], npb - 1)
                page = pt_ref[smp, j // bpp]
                for kv in range(2):
                    block_copy(page, slot, g, k, kv).start()

    @pl.when(s == 0)
    def _():
        start(0, 0)

    @pl.when(s + 1 < ns)
    def _():
        start(s + 1, (s + 1) % 2)

    slot = s % 2
    q8 = q_ref[0]
    q32 = q8.astype(F32)
    qg = [q8[:, g * HEAD_DIM:(g + 1) * HEAD_DIM] for g in range(N_KV_A)]
    tbl_t = tblt_ref[...]
    row = lax.broadcasted_iota(jnp.int32, (8, LANE), 0)
    lane = lax.broadcasted_iota(jnp.int32, (8, LANE), 1)
    grp0 = row[:, 0:1] < HPG

    def new_token(kv_row):
        kn = kv_row[:, 0:LANE].astype(BF16).astype(F32)
        vn = kv_row[:, LANE:].astype(BF16).astype(F32)
        return jnp.sum(q32 * kn, axis=1, keepdims=True) + tbl_t[:, 0:1], vn

    def by_group(a0, a1):
        return jnp.concatenate([jnp.where(grp0, a0, 0.0), jnp.where(grp0, 0.0, a1)], axis=1)

    w = win_ref[0]
    wl = w.shape[-1]
    s_new, v_new = new_token(wnew_ref[0])
    sw = jnp.where(grp0, _dot(qg[0], w[0].astype(BF16)), _dot(qg[1], w[1].astype(BF16))) + bw_ref[...]
    p, pn, inv = _softmax_with_new(sw, s_new)
    pb = p.astype(BF16)
    o_w = (by_group(_dot_t(pb, w[2].astype(BF16)), _dot_t(pb, w[3].astype(BF16))) + pn * v_new) * inv
    wcol = lax.broadcasted_iota(jnp.int32, (HEAD_DIM, wl), 1)
    for c in range(2 * N_KV_A):
        nwin_ref[0, c] = jnp.where(wcol == wl - 1, wcol_ref[0, c], pltpu.roll(w[c], wl - 1, 1))

    for g in range(N_KV_A):
        for k in range(N_SLC):
            for kv in range(2):
                block_copy(0, slot, g, k, kv).wait()
    kl = lax.broadcasted_iota(jnp.int32, (8, nk), 1)
    kslot = kl >> 7
    kin = kl & (PAGE - 1)
    rowk = lax.broadcasted_iota(jnp.int32, (8, nk), 0) < HPG
    blk = jnp.zeros((8, nk), jnp.int32)
    for k in range(N_SLC):
        j0 = idx_ref[(s * N_KV_A) * N_SLC + k]
        j1 = idx_ref[(s * N_KV_A + 1) * N_SLC + k]
        blk = jnp.where(kslot == k, jnp.where(rowk, j0, j1), blk)
    rel = past_len - ((blk // bpp) * PAGE + kin)
    ok = (kin // SLC_BLOCK) == jnp.where(blk < npb, blk % bpp, -1)
    ss = jnp.where(rowk, _dot(qg[0], kvbuf[slot, 0].astype(BF16)), _dot(qg[1], kvbuf[slot, 1].astype(BF16)))
    ss = jnp.where(ok, ss + _bucket_bias(jnp.maximum(rel, 0), tbl_t), NEG)
    s_new, v_new = new_token(knew_ref[0])
    p, pn, inv = _softmax_with_new(ss, s_new)
    pb = p.astype(BF16)
    o_s = (by_group(_dot_t(pb, kvbuf[slot, 2].astype(BF16)), _dot_t(pb, kvbuf[slot, 3].astype(BF16)))
           + pn * v_new) * inv

    gc = gcol_ref[0]
    o = gc[:, 0:1] * oc_ref[0] + gc[:, 1:2] * o_s + gc[:, 2:3] * o_w
    oa_ref[0] = jnp.where((lane >= HEAD_DIM) == (row >= HPG), o, 0.0)


def _samp_sw(page_table, idx_flat, pool_t, q3, knew, win_t, wnew, wcol, o_c, gcol, tbl_t, bw, past_len):
    Bd = page_table.shape[0]
    wl = win_t.shape[-1]
    per = lambda shp: pl.BlockSpec((1,) + shp, lambda s, pt, ix: (s,) + (0,) * len(shp))
    cst = lambda a: pl.BlockSpec(a.shape, lambda s, pt, ix: (0,) * a.ndim)
    return pl.pallas_call(
        functools.partial(_samp_sw_kernel, past_len=past_len),
        grid_spec=pltpu.PrefetchScalarGridSpec(
            num_scalar_prefetch=2, grid=(Bd,),
            in_specs=[pl.BlockSpec(memory_space=pl.ANY), per((8, LANE)), per((1, 2 * LANE)),
                      per((2 * N_KV_A, HEAD_DIM, wl)), per((1, 2 * LANE)), per((2 * N_KV_A, HEAD_DIM, 1)),
                      per((8, LANE)), per((8, LANE)), cst(tbl_t), cst(bw)],
            out_specs=[per((8, LANE)), per((2 * N_KV_A, HEAD_DIM, wl))],
            scratch_shapes=[pltpu.VMEM((2, 2 * N_KV_A, HEAD_DIM, N_SLC * PAGE), F32),
                            pltpu.SemaphoreType.DMA((2,))]),
        out_shape=[jax.ShapeDtypeStruct((Bd, 8, LANE), F32), jax.ShapeDtypeStruct(win_t.shape, F32)],
        name='sample_slc_win',
        compiler_params=pltpu.CompilerParams(dimension_semantics=('arbitrary',), vmem_limit_bytes=VMEM_LIMIT),
    )(page_table, idx_flat, pool_t, q3, knew, win_t, wnew, wcol, o_c, gcol, tbl_t, bw)


def _samp_mla_kernel(pt_ref, pool_hbm, q_ref, lnew_ref, o_ref, buf, sem, *, n_pages):
    s = pl.program_id(0)
    ns = pl.num_programs(0)

    def page_copy(page, slot, p):
        return pltpu.make_async_copy(pool_hbm.at[page], buf.at[slot, :, pl.ds(p * PAGE, PAGE)], sem.at[slot])

    def start(smp, slot):
        def body(p, c):
            page_copy(pt_ref[smp, p], slot, p).start()
            return c
        lax.fori_loop(0, n_pages, body, 0)

    @pl.when(s == 0)
    def _():
        start(0, 0)

    @pl.when(s + 1 < ns)
    def _():
        start(s + 1, (s + 1) % 2)

    slot = s % 2

    def wait_page(p, c):
        page_copy(0, slot, p).wait()
        return c
    lax.fori_loop(0, n_pages, wait_page, 0)

    q8 = q_ref[0]
    ckv_t = buf[slot, 0:KV_LORA, :].astype(BF16)
    kr_t = buf[slot, KV_LORA:LATENT_DIM, :].astype(BF16)
    ln = lnew_ref[0].astype(F32)
    sc = _dot(q8[:, 0:KV_LORA], ckv_t) + _dot(q8[:, KV_LORA:LATENT_DIM], kr_t)
    s_new = jnp.sum(q8.astype(F32) * ln, axis=1, keepdims=True)
    p, pn, inv = _softmax_with_new(sc, s_new)
    o_ref[0] = (_dot_t(p.astype(BF16), ckv_t) + pn * ln[:, 0:KV_LORA]) * inv


def _samp_mla(page_table, pool2d, qm3, lnew):
    Bd, n_pages = page_table.shape
    per = lambda shp: pl.BlockSpec((1,) + shp, lambda s, pt: (s, 0, 0))
    return pl.pallas_call(
        functools.partial(_samp_mla_kernel, n_pages=n_pages),
        grid_spec=pltpu.PrefetchScalarGridSpec(
            num_scalar_prefetch=1, grid=(Bd,),
            in_specs=[pl.BlockSpec(memory_space=pl.ANY), per((8, 256)), per((1, 256))],
            out_specs=per((8, KV_LORA)),
            scratch_shapes=[pltpu.VMEM((2, LATENT_DIM, n_pages * PAGE), F32), pltpu.SemaphoreType.DMA((2,))]),
        out_shape=jax.ShapeDtypeStruct((Bd, 8, KV_LORA), F32), name='sample_mla',
        compiler_params=pltpu.CompilerParams(dimension_semantics=('arbitrary',), vmem_limit_bytes=VMEM_LIMIT),
    )(page_table, pool2d, qm3, lnew)


def _uv_kernel(o_ref, w_ref, y_ref):
    y_ref[...] = _dot(o_ref[...].astype(BF16), w_ref[0])


def _samp_uv(o_lat2d, w_uv):
    Bd = o_lat2d.shape[0]
    return pl.pallas_call(
        _uv_kernel, grid=(N_HEADS_B,),
        in_specs=[pl.BlockSpec((Bd, LANE), lambda h: (0, h)), pl.BlockSpec((1, KV_LORA, LANE), lambda h: (h, 0, 0))],
        out_specs=pl.BlockSpec((Bd, LANE), lambda h: (0, h)),
        out_shape=jax.ShapeDtypeStruct((Bd, N_HEADS_B * LANE), F32), name='sample_uv',
        compiler_params=pltpu.CompilerParams(dimension_semantics=('parallel',), vmem_limit_bytes=VMEM_LIMIT),
    )(o_lat2d, w_uv)


def _sample_mix(xs3, msm, pool_cmp, pool_slc, win_buf, pool_mla, page_table, gain, rel_bias, w, cw):
    Bd = xs3.shape[1]
    past_len = page_table.shape[1] * PAGE
    cos_s, sin_s = _rope_tables(jnp.full((Bd,), past_len, jnp.int32))
    qa_s, cmp_s, slc_s, win_s, _, gt_s, qm_s, lat_s, lat16_s = _inproj(xs3, msm[0], msm[1], gain, cos_s, sin_s, w, Bd)
    q3 = qa_s.reshape(Bd, N_HEADS_A, LANE)
    nc = past_len // CMP_STRIDE
    n_cmp = (past_len + 1 - CMP_BLOCK) // CMP_STRIDE + 1
    m = jnp.arange(nc)
    bc = _t5_bias_rows(rel_bias, past_len - (m * CMP_STRIDE + CMP_BLOCK - 1), m < n_cmp)
    fm = lambda a: a.transpose(0, 2, 3, 4, 1).reshape(a.shape[0], 2 * N_KV_A, HEAD_DIM, a.shape[1])
    o_c, idx = _samp_cmp(page_table, fm(pool_cmp), q3, _samp_cmp_weights(cw), bc, n_cmp)
    idx_flat = idx[:, :N_KV_A, :N_SLC].reshape(-1)
    wl = win_buf.shape[1]
    wi = jnp.arange(wl)
    bw = _t5_bias_rows(rel_bias, wl - wi, (wl - wi < WINDOW) & (past_len - wl + wi >= 0))
    tbl_t = jnp.pad(rel_bias.astype(F32).T, ((0, 0), (0, LANE - N_BUCKETS)))
    gcol = gt_s[0, :, :3 * N_HEADS_A].reshape(Bd, 3, N_HEADS_A).transpose(0, 2, 1)
    gcol = jnp.pad(gcol, ((0, 0), (0, 0), (0, LANE - 3)))
    oa_s, new_win = _samp_sw(page_table, idx_flat, fm(pool_slc), q3, slc_s.reshape(Bd, 1, 2 * LANE), fm(win_buf),
                             win_s.reshape(Bd, 1, 2 * LANE), win_s.reshape(Bd, 2 * N_KV_A, HEAD_DIM, 1), o_c, gcol,
                             tbl_t, bw, past_len)
    o_lat = _samp_mla(page_table, pool_mla.transpose(0, 2, 1), qm_s.reshape(Bd, N_HEADS_B, 256),
                      lat16_s.reshape(Bd, 1, 256))
    ob_s = _samp_uv(o_lat.reshape(Bd, N_HEADS_B * KV_LORA), w['w_uv'])
    new_win = new_win.reshape(Bd, 2, N_KV_A, HEAD_DIM, wl).transpose(0, 4, 1, 2, 3)
    return oa_s.reshape(1, Bd, -1), ob_s.reshape(1, Bd, -1), cmp_s, slc_s, new_win, lat_s


def kernel(x_prompt, x_sample, cache_nsa_cmp, cache_nsa_slc, cache_nsa_win, cache_mla, page_table, c_prompt, c_sample, rel_bias, w_ada, b_ada, norm_attn, norm_ffn, w_in, cmp_pe, cmp_w1, cmp_w2, q_norm, w_q_up, kv_norm, w_kv_up, out_norm_a, out_norm_b, w_out, w_router, router_bias, w_gate_e, w_up_e, w_down_e, w_gate_s, w_up_s, w_down_s, norm_final):
    B, S, D = x_prompt.shape
    Bd = x_sample.shape[0]
    l = 0
    n_mod = B + Bd
    c_all = jnp.pad(jnp.concatenate([c_prompt, c_sample], axis=0), ((0, -n_mod % 8), (0, 0)))
    mod = _adaln(c_all, w_ada[l], b_ada[l]).reshape(-1, 6, D)
    mp = [mod[:B, i][:, None, :] for i in range(6)]
    msm = [mod[B:n_mod, i][None] for i in range(6)]

    w = _inproj_weights(w_in[l], q_norm[l], w_q_up[l], kv_norm[l], w_kv_up[l])
    cw = _compress_weights(cmp_pe[l], cmp_w1[l], cmp_w2[l])
    mw = _merge_weights(out_norm_a[l], out_norm_b[l], w_out[l], norm_ffn[l], w_router[l], w_gate_s[l], w_up_s[l],
                        w_down_s[l])
    tb, lb = _bias_tables(rel_bias)

    cos, sin = _rope_tables(jnp.arange(S))
    qa, cmp32, slc32, win32, kv16, gt, qm, lat32, lat16 = _inproj(
        x_prompt, mp[0], mp[1], norm_attn[l], cos, sin, w, 256)
    nc = S // CMP_STRIDE
    n_cmp = (S - CMP_BLOCK) // CMP_STRIDE + 1
    xc = cmp32.reshape(B, nc, CMP_STRIDE, 4, HEAD_DIM).transpose(0, 3, 1, 2, 4).reshape(B, 4, nc, -1).astype(BF16)
    kcv = _compress(xc, cw, n_cmp)
    oa_p = _nsa_prompt(qa, gt, kv16, kcv, tb, lb)
    ob_p = _mla_prompt(qm, lat16, w['w_uv'])
    xs_p, f_p, sc_p = _merge(x_prompt, oa_p, ob_p, mp[2], mp[3], mp[4], mp[5], mw, 256)

    xs3 = x_sample.reshape(1, Bd, D)
    oa_s, ob_s, cmp_s, slc_s, new_win, lat_s = _sample_mix(
        xs3, msm, cache_nsa_cmp[l], cache_nsa_slc[l], cache_nsa_win[l], cache_mla[l], page_table, norm_attn[l],
        rel_bias, w, cw)
    xs_s, f_s, sc_s = _merge(xs3, oa_s, ob_s, msm[2], msm[3], msm[4], msm[5], mw, Bd)

    n_p = B * S
    f_all = jnp.concatenate([f_p.reshape(n_p, D // LANE, LANE), f_s.reshape(Bd, D // LANE, LANE)], axis=0)
    sc_all = jnp.concatenate([sc_p.reshape(n_p, N_EXPERTS), sc_s.reshape(Bd, N_EXPERTS)], axis=0)
    wts, pos, row_tok, blk_exp, nused = _route(sc_all, router_bias[l])
    yb = _moe_experts(nused, blk_exp, row_tok, f_all, w_gate_e[l], w_up_e[l], w_down_e[l])
    tile_pos = lambda p: p.reshape(-1, MOE_BLOCK, TOP_K).transpose(0, 2, 1).reshape(-1, MOE_BLOCK)
    y_p = _combine(tile_pos(pos[:n_p]), yb, xs_p, mp[5], wts[:n_p].reshape(B, S, TOP_K), norm_final)
    y_s = _combine(tile_pos(pos[n_p:]), yb, xs_s, msm[5], wts[n_p:].reshape(1, Bd, TOP_K), norm_final)

    sh6 = lambda a, b, t: a.reshape(1, b, t, 2, N_KV_A, HEAD_DIM)
    return (y_p, y_s.reshape(Bd, 1, D), sh6(cmp32, B, S), sh6(cmp_s, Bd, 1), sh6(slc32, B, S), sh6(slc_s, Bd, 1),
            sh6(win32[:, S - WINDOW:], B, WINDOW), new_win[None], lat32[None], lat_s.reshape(1, Bd, 1, LATENT_DIM))
```

```python
import functools
import math

import jax
import jax.numpy as jnp
from jax import lax
from jax.experimental import pallas as pl
from jax.experimental.pallas import tpu as pltpu

F32 = jnp.float32
BF16 = jnp.bfloat16

LANE = 128
VMEM_LIMIT = 56 * 1024 * 1024

HEAD_DIM = 64
N_HEADS_A = 8
N_KV_A = 2
HPG = N_HEADS_A // N_KV_A
CMP_BLOCK = 32
CMP_STRIDE = 16
CMP_HIDDEN = 128
SLC_BLOCK = 64
SLC_PER_CMP = SLC_BLOCK // CMP_STRIDE
N_SLC = 16
N_LOCAL_SLC = 2
WINDOW = 512
N_HEADS_B = 8
Q_LORA = 192
KV_LORA = 128
QK_NOPE = 64
QK_ROPE = 32
V_DIM = 64
LATENT_DIM = KV_LORA + QK_ROPE
ROPE_THETA = 10000.0
MLA_SCALE = (QK_NOPE + QK_ROPE) ** -0.5
N_BUCKETS = 32
MAX_DISTANCE = 128
N_EXPERTS = 256
TOP_K = 8
N_GROUPS = 8
TOP_GROUPS = 4
ROUTED_SCALE = 2.5
MOE_BLOCK = 128
EPS = 1e-6
NEG = -1e30
FORCED = 1e30

QB = 128
KT_NSA = 1024
KT_MLA = 2048
CMP_PAD = 16
LOC_W = 24


def _dot(a, b):
    return jnp.dot(a, b, preferred_element_type=F32)


def _dot_t(a, b):
    return lax.dot_general(a, b, (((1,), (1,)), ((), ())), preferred_element_type=F32)


def _lane_tiles(x):
    return [x[:, c * LANE:(c + 1) * LANE] for c in range(x.shape[1] // LANE)]


def _row_max(x):
    return jnp.max(functools.reduce(jnp.maximum, _lane_tiles(x)), axis=1, keepdims=True)


def _row_sum(x):
    return jnp.sum(functools.reduce(jnp.add, _lane_tiles(x)), axis=1, keepdims=True)


def _const_spec(shape):
    nd = len(shape)
    return pl.BlockSpec(shape, lambda *_: (0,) * nd)


def _inproj_kernel(x_ref, sh_ref, sc_ref, g_ref, cs_ref, sn_ref, wq_ref, wkv_ref, wg_ref, wqd_ref,
                   wkvd_ref, qn_ref, wqup_ref, bd_ref, plc_ref, kvn_ref,
                   qa_ref, cmp_ref, slc_ref, win_ref, kv16_ref, gt_ref, qm_ref, lat_ref, lat16_ref):
    x = x_ref[0]
    ms = jnp.mean(x * x, axis=-1, keepdims=True)
    xn = x * lax.rsqrt(ms + EPS) * g_ref[...]
    h = xn * (1.0 + sc_ref[0]) + sh_ref[0]
    hb = h.astype(BF16)
    qa_ref[0] = _dot(hb, wq_ref[...]).astype(BF16)
    kv = _dot(hb, wkv_ref[...])
    cmp_ref[0] = kv[:, 0:256]
    slc_ref[0] = kv[:, 256:512]
    win_ref[0] = kv[:, 512:768]
    kv16_ref[0] = kv.astype(BF16)
    gl = _dot(hb, wg_ref[...])
    gt_ref[0] = 1.0 / (1.0 + jnp.exp(-gl))
    qd = _dot(hb, wqd_ref[...])
    qn = qd * lax.rsqrt(jnp.sum(qd * qd, axis=-1, keepdims=True) * (1.0 / Q_LORA) + EPS) * qn_ref[...]
    qu = _dot(qn.astype(BF16), wqup_ref[...])
    cs = cs_ref[...]
    sn = sn_ref[...]
    qr = qu[:, 512:768] * cs + qu[:, 768:1024] * sn
    qm = _dot(qu[:, 0:512].astype(BF16), bd_ref[...]) + _dot(qr.astype(BF16), plc_ref[...])
    qm_ref[0] = (qm * MLA_SCALE).astype(BF16)
    kvd = _dot(hb, wkvd_ref[...])
    c = kvd[:, 0:128]
    ckv = c * lax.rsqrt(jnp.mean(c * c, axis=-1, keepdims=True) + EPS) * kvn_ref[...]
    kr = kvd[:, 128:256] * cs[:, 0:128] + kvd[:, 256:384] * sn[:, 0:128]
    lat_ref[0, :, 0:128] = ckv
    lat_ref[0, :, 128:160] = kr[:, 0:32]
    lat16_ref[0, :, 0:128] = ckv.astype(BF16)
    lat16_ref[0, :, 128:256] = kr.astype(BF16)


def _inproj_weights(w_in, q_norm, w_q_up, kv_norm, w_kv_up):
    D = w_in.shape[0]
    o1 = N_HEADS_A * HEAD_DIM
    o2 = o1 + 6 * N_KV_A * HEAD_DIM
    o3 = o2 + 3 * N_HEADS_A
    o4 = o3 + Q_LORA
    wq = w_in[:, :o1].reshape(D, N_HEADS_A, HEAD_DIM) * (HEAD_DIM ** -0.5)
    z = jnp.zeros_like(wq)
    grp = (jnp.arange(N_HEADS_A) // HPG)[None, :, None]
    wq_pad = jnp.concatenate([jnp.where(grp == 0, wq, z), jnp.where(grp == 1, wq, z)], axis=-1)
    wq_pad = wq_pad.reshape(D, N_HEADS_A * 2 * HEAD_DIM)
    wkv = w_in[:, o1:o2]
    wg = jnp.pad(w_in[:, o2:o3], ((0, 0), (0, LANE - 3 * N_HEADS_A)))
    wqd = jnp.pad(w_in[:, o3:o4], ((0, 0), (0, 256 - Q_LORA)))
    wkd = w_in[:, o4:]
    half = QK_ROPE // 2
    wc = wkd[:, :KV_LORA]
    wr = wkd[:, KV_LORA:]
    wrot = jnp.concatenate([-wr[:, half:], wr[:, :half]], axis=1)
    padr = ((0, 0), (0, LANE - QK_ROPE))
    wkvd = jnp.concatenate([wc, jnp.pad(wr, padr), jnp.pad(wrot, padr)], axis=1)
    qn = jnp.pad(q_norm, (0, 256 - Q_LORA)).reshape(1, 256)
    wu = jnp.pad(w_q_up, ((0, 256 - Q_LORA), (0, 0))).reshape(256, N_HEADS_B, QK_NOPE + QK_ROPE)
    wu_n = wu[:, :, :QK_NOPE].reshape(256, N_HEADS_B * QK_NOPE)
    wu_r = wu[:, :, QK_NOPE:]
    wu_rot = jnp.concatenate([-wu_r[:, :, half:], wu_r[:, :, :half]], axis=-1)
    wqup = jnp.concatenate([wu_n, wu_r.reshape(256, -1), wu_rot.reshape(256, -1)], axis=1)
    w_ukv = w_kv_up.reshape(KV_LORA, N_HEADS_B, QK_NOPE + V_DIM)
    w_uk = w_ukv[:, :, :QK_NOPE]
    eye = jnp.eye(N_HEADS_B, dtype=F32)
    bd = jnp.einsum('chn,hk->hnkc', w_uk, eye)
    bd = jnp.pad(bd, ((0, 0), (0, 0), (0, 0), (0, 256 - KV_LORA))).reshape(N_HEADS_B * QK_NOPE, N_HEADS_B * 256)
    plc = jnp.einsum('hk,rs->hrks', eye, jnp.eye(QK_ROPE, dtype=F32))
    plc = jnp.pad(plc, ((0, 0), (0, 0), (0, 0), (KV_LORA, 256 - KV_LORA - QK_ROPE)))
    plc = plc.reshape(N_HEADS_B * QK_ROPE, N_HEADS_B * 256)
    w_uv = jnp.pad(w_ukv[:, :, QK_NOPE:].transpose(1, 0, 2), ((0, 0), (0, 0), (0, LANE - V_DIM)))
    bf = lambda a: a.astype(BF16)
    return dict(wq=bf(wq_pad), wkv=bf(wkv), wg=bf(wg), wqd=bf(wqd), wkvd=bf(wkvd), qn=qn, wqup=bf(wqup),
                bd=bf(bd), plc=bf(plc), kvn=kv_norm.reshape(1, KV_LORA), w_uv=bf(w_uv))


def _rope_tables(pos):
    half = QK_ROPE // 2
    inv = ROPE_THETA ** (-jnp.arange(half, dtype=F32) / half)
    ang = pos.astype(F32)[:, None] * inv[None, :]
    cos = jnp.tile(jnp.cos(ang), (1, 2 * N_HEADS_B))
    sin = jnp.tile(jnp.sin(ang), (1, 2 * N_HEADS_B))
    return cos, sin


def _inproj(x, shift, scale, gain, cos, sin, w, tr):
    B, T, D = x.shape
    tm = shift.shape[1]
    mod_spec = pl.BlockSpec((1, tr if tm > 1 else 1, D), (lambda b, t: (b, t, 0)) if tm > 1 else (lambda b, t: (b, 0, 0)))
    row = lambda n: pl.BlockSpec((1, tr, n), lambda b, t: (b, t, 0))
    tab = pl.BlockSpec((tr, 256), lambda b, t: (t, 0))
    wnames = ['wq', 'wkv', 'wg', 'wqd', 'wkvd', 'qn', 'wqup', 'bd', 'plc', 'kvn']
    out_shape = [
        jax.ShapeDtypeStruct((B, T, 1024), BF16), jax.ShapeDtypeStruct((B, T, 256), F32),
        jax.ShapeDtypeStruct((B, T, 256), F32), jax.ShapeDtypeStruct((B, T, 256), F32),
        jax.ShapeDtypeStruct((B, T, 768), BF16), jax.ShapeDtypeStruct((B, T, LANE), F32),
        jax.ShapeDtypeStruct((B, T, 2048), BF16), jax.ShapeDtypeStruct((B, T, LATENT_DIM), F32),
        jax.ShapeDtypeStruct((B, T, 256), BF16)]
    in_specs = [row(D), mod_spec, mod_spec, _const_spec((1, D)), tab, tab]
    in_specs += [_const_spec(w[n].shape) for n in wnames[:5]]
    in_specs += [_const_spec(w['qn'].shape)] + [_const_spec(w[n].shape) for n in wnames[6:9]]
    in_specs += [_const_spec(w['kvn'].shape)]
    return pl.pallas_call(
        _inproj_kernel, grid=(B, T // tr), in_specs=in_specs,
        out_specs=[row(s.shape[-1]) for s in out_shape], out_shape=out_shape, name='inproj',
        compiler_params=pltpu.CompilerParams(dimension_semantics=('parallel', 'parallel'),
                                             vmem_limit_bytes=VMEM_LIMIT),
    )(x, shift, scale, gain.reshape(1, D), cos, sin, *[w[n] for n in wnames])


def _compress_kernel(x0_ref, x1_ref, w1_ref, pe_ref, w2a_ref, w2b_ref, o_ref, *, n_cmp):
    w1 = w1_ref[0]
    pp = _dot(pe_ref[0], w1)
    peh = pp[0:1, 0:CMP_HIDDEN] + pp[1:2, CMP_HIDDEN:]
    nc = x0_ref.shape[2]
    out = jnp.zeros((nc, LANE), F32)
    for x_ref, w2_ref in ((x0_ref, w2a_ref), (x1_ref, w2b_ref)):
        ab = _dot(x_ref[0, 0], w1)
        hid = ab[:, 0:CMP_HIDDEN] + pltpu.roll(ab[:, CMP_HIDDEN:], nc - 1, 0) + peh
        act = hid * (1.0 / (1.0 + jnp.exp(-hid)))
        out = out + _dot(act.astype(BF16), w2_ref[0])
    rows = lax.broadcasted_iota(jnp.int32, (nc, LANE), 0)
    out = jnp.where(rows < n_cmp, out, 0.0)
    o_ref[0, 0, 0:CMP_PAD] = jnp.zeros((CMP_PAD, LANE), F32)
    o_ref[0, 0, CMP_PAD:CMP_PAD + nc] = out
    o_ref[0, 0, CMP_PAD + nc:] = jnp.zeros((o_ref.shape[2] - CMP_PAD - nc, LANE), F32)


def _compress_weights(cmp_pe, cmp_w1, cmp_w2):
    kin = CMP_STRIDE * HEAD_DIM
    w1 = cmp_w1.reshape(2, 2, kin, CMP_HIDDEN)
    w1cat = jnp.concatenate([w1[:, 0], w1[:, 1]], axis=-1).astype(BF16)
    pe = jnp.pad(cmp_pe.reshape(2, 2, kin), ((0, 0), (0, 6), (0, 0))).astype(BF16)
    w2a = jnp.pad(cmp_w2, ((0, 0), (0, 0), (0, HEAD_DIM))).astype(BF16)
    w2b = jnp.pad(cmp_w2, ((0, 0), (0, 0), (HEAD_DIM, 0))).astype(BF16)
    return w1cat, pe, w2a, w2b


def _compress(xc, cw, n_cmp):
    B, _, nc, kin = xc.shape
    w1cat, pe, w2a, w2b = cw
    return pl.pallas_call(
        functools.partial(_compress_kernel, n_cmp=n_cmp), grid=(B, 2),
        in_specs=[pl.BlockSpec((1, 1, nc, kin), lambda b, j: (b, 2 * j, 0, 0)),
                  pl.BlockSpec((1, 1, nc, kin), lambda b, j: (b, 2 * j + 1, 0, 0)),
                  pl.BlockSpec((1, kin, 2 * CMP_HIDDEN), lambda b, j: (j, 0, 0)),
                  pl.BlockSpec((1, 8, kin), lambda b, j: (j, 0, 0)),
                  pl.BlockSpec((1, CMP_HIDDEN, LANE), lambda b, j: (j, 0, 0)),
                  pl.BlockSpec((1, CMP_HIDDEN, LANE), lambda b, j: (j, 0, 0))],
        out_specs=pl.BlockSpec((1, 1, nc + LANE, LANE), lambda b, j: (b, j, 0, 0)),
        out_shape=jax.ShapeDtypeStruct((B, 2, nc + LANE, LANE), F32), name='compress',
        compiler_params=pltpu.CompilerParams(dimension_semantics=('parallel', 'parallel'),
                                             vmem_limit_bytes=VMEM_LIMIT),
    )(xc, xc, w1cat, pe, w2a, w2b)


def _t5_bucket(rel):
    max_exact = N_BUCKETS // 2
    n = jnp.maximum(rel, 0)
    nf = jnp.maximum(n, 1).astype(F32)
    large = max_exact + (jnp.log(nf / max_exact) / math.log(MAX_DISTANCE / max_exact)
                         * (N_BUCKETS - max_exact)).astype(jnp.int32)
    large = jnp.minimum(large, N_BUCKETS - 1)
    return jnp.where(n < max_exact, n, large)


def _bucket_lookup(tbl, bucket):
    hot = (bucket[..., None] == jnp.arange(N_BUCKETS)).astype(F32)
    return jnp.einsum('...b,bh->...h', hot, tbl, precision=lax.Precision.HIGHEST)


def _bias_tables(rel_bias):
    tbl = rel_bias.astype(F32)
    const = tbl[N_BUCKETS - 1]
    i = jnp.arange(QB)[:, None]
    j = jnp.arange(LANE)[None, :]

    def tab(rel):
        b = _bucket_lookup(tbl, _t5_bucket(rel))
        return jnp.moveaxis(b, -1, 0) - const[:, None, None]

    t0 = jnp.where((i - j >= 0)[None], tab(i - j), NEG)
    t1 = tab(QB + i - j)
    zero = jnp.zeros_like(t1)
    t4 = jnp.broadcast_to(jnp.where(j > i, 0.0, NEG)[None], t1.shape)
    tb = jnp.stack([t0, t1, zero, jnp.full_like(t1, NEG), t4]).reshape(5, N_HEADS_A * QB, LANE)
    rel_l = i - CMP_STRIDE * (j - CMP_PAD) - (CMP_BLOCK - 1)
    lb = jnp.where(((j < LOC_W) & (rel_l >= 0))[None], tab(rel_l), NEG).reshape(N_HEADS_A * QB, LANE)
    return tb, lb


def _hilo_dot(x, m):
    hi = x.astype(BF16)
    lo = (x - hi.astype(F32)).astype(BF16)
    return _dot(hi, m) + _dot(lo, m)


def _nsa_kernel(q_ref, gt_ref, ks_ref, vs_ref, kw_ref, vw_ref, kc_ref, vc_ref, tb_ref, lb_ref, pf_ref,
                o_ref, m_scr, l_scr, acc_scr, *, n_cmp):
    qb = pl.program_id(1)
    nh = N_HEADS_A
    rows = nh * QB
    q = q_ref[0]
    q8 = jnp.concatenate([q[:, h * LANE:(h + 1) * LANE] for h in range(nh)], axis=0)
    nc = pf_ref.shape[0]

    kcf = kc_ref[0, 0, 0:nc].astype(BF16)
    vcf = vc_ref[0, 0, 0:nc].astype(BF16)
    l0 = pl.multiple_of(qb * (QB // CMP_STRIDE), 8)
    kcl = kc_ref[0, 0, pl.ds(l0, LANE)].astype(BF16)
    vcl = vc_ref[0, 0, pl.ds(l0, LANE)].astype(BF16)
    colf = lax.broadcasted_iota(jnp.int32, (1, nc), 1)
    far_ok = jnp.where(colf >= CMP_PAD, jnp.where(colf < l0, 0.0, NEG), NEG)
    coll = lax.broadcasted_iota(jnp.int32, (1, LANE), 1) + (l0 - CMP_PAD)
    loc_ok = jnp.where(coll >= 0, jnp.where(coll < n_cmp, 0.0, NEG), NEG)
    s_far = _dot_t(q8, kcf) + far_ok
    s_loc = _dot_t(q8, kcl) + lb_ref[...] + loc_ok
    mrow = jnp.maximum(jnp.max(s_far, axis=1, keepdims=True), jnp.max(s_loc, axis=1, keepdims=True))
    p_far = jnp.exp(s_far - mrow)
    p_loc = jnp.exp(s_loc - mrow)
    lsum = jnp.sum(p_far, axis=1, keepdims=True) + jnp.sum(p_loc, axis=1, keepdims=True)
    inv = jnp.where(mrow > 0.5 * NEG, 1.0 / lsum, 0.0)
    p_far = p_far * inv
    p_loc = p_loc * inv
    o_c = _dot(p_far.astype(BF16), vcf) + _dot(p_loc.astype(BF16), vcl)

    r_i = lax.broadcasted_iota(jnp.int32, (LANE, LANE), 0)
    c_i = lax.broadcasted_iota(jnp.int32, (LANE, LANE), 1)
    pool_loc = jnp.where(r_i < LOC_W,
                         jnp.where(c_i == (r_i >> 2) + (2 * qb - CMP_PAD // SLC_PER_CMP), 1.0, 0.0),
                         0.0).astype(BF16)
    tq = 2 * qb + jnp.where(r_i >= SLC_BLOCK, 1, 0)
    dist = tq - c_i
    c_f = c_i.astype(F32)
    sel = []
    for g in range(N_KV_A):
        pgf = p_far[(g * HPG) * QB:(g * HPG + 1) * QB]
        pgl = p_loc[(g * HPG) * QB:(g * HPG + 1) * QB]
        for hh in range(1, HPG):
            pgf = pgf + p_far[(g * HPG + hh) * QB:(g * HPG + hh + 1) * QB]
            pgl = pgl + p_loc[(g * HPG + hh) * QB:(g * HPG + hh + 1) * QB]
        imp = _hilo_dot(pgf, pf_ref[...]) + _hilo_dot(pgl, pool_loc)
        score = jnp.where(dist < 0, NEG, jnp.where(dist < N_LOCAL_SLC, FORCED, jnp.where(c_i == 0, FORCED, imp)))
        chosen = jnp.zeros((QB, LANE), F32)
        for _ in range(N_SLC):
            mx = jnp.max(score, axis=1, keepdims=True)
            first = jnp.min(jnp.where(score == mx, c_f, float(LANE)), axis=1, keepdims=True)
            hit = c_f == first
            chosen = jnp.where(hit, 1.0, chosen)
            score = jnp.where(hit, -jnp.inf, score)
        sel.append(jnp.where(chosen > 0.5, 0.0, NEG).astype(BF16))

    m_scr[...] = jnp.full(m_scr.shape, -jnp.inf, F32)
    l_scr[...] = jnp.zeros(l_scr.shape, F32)
    acc_scr[...] = jnp.zeros(acc_scr.shape, F32)
    KT = KT_NSA
    nsub = KT // LANE
    qa8 = jnp.concatenate([q8, jnp.concatenate([sel[g] for g in range(N_KV_A) for _ in range(HPG)], axis=0)], axis=1)

    def slc_tile(kt, near):
        k0 = pl.multiple_of(kt * KT, KT)
        s = _dot_t(qa8, ks_ref[0, pl.ds(k0, KT), :])
        if near:
            bias = []
            for c in range(nsub):
                d = qb - (kt * nsub + c)
                bias.append(tb_ref[jnp.where(d < 0, 3, jnp.minimum(d, 2))])
            s = s + jnp.concatenate(bias, axis=1)
        m_old = m_scr[...]
        m_new = jnp.maximum(m_old, jnp.max(s, axis=1, keepdims=True))
        alpha = jnp.exp(m_old - m_new)
        p = jnp.exp(s - m_new)
        l_scr[...] = alpha * l_scr[...] + jnp.sum(p, axis=1, keepdims=True)
        acc_scr[...] = alpha * acc_scr[...] + _dot(p.astype(BF16), vs_ref[0, pl.ds(k0, KT), :])
        m_scr[...] = m_new

    n_tiles = qb // nsub + 1
    n_far = jnp.maximum(n_tiles - 2, 0)

    def far_tile(kt, carry):
        slc_tile(kt, False)
        return carry

    def near_tile(kt, carry):
        slc_tile(kt, True)
        return carry

    lax.fori_loop(0, n_far, far_tile, 0)
    lax.fori_loop(n_far, n_tiles, near_tile, 0)
    o_s = acc_scr[...] * (1.0 / l_scr[...])

    nwin = WINDOW // QB + 1
    w0 = jnp.maximum(qb - (nwin - 1), 0)
    k0 = pl.multiple_of(w0 * QB, QB)
    s = _dot_t(q8, kw_ref[0, pl.ds(k0, nwin * QB), :])
    bias = []
    for c in range(nwin):
        d = qb - (w0 + c)
        bias.append(tb_ref[jnp.where(d < 0, 3, jnp.where(d >= nwin - 1, 4, jnp.minimum(d, 2)))])
    s = s + jnp.concatenate(bias, axis=1)
    p = jnp.exp(s - jnp.max(s, axis=1, keepdims=True))
    o_w = _dot(p.astype(BF16), vw_ref[0, pl.ds(k0, nwin * QB), :]) * (1.0 / jnp.sum(p, axis=1, keepdims=True))

    gt = gt_ref[0]
    for h in range(nh):
        r = slice(h * QB, (h + 1) * QB)
        comb = (gt[:, h:h + 1] * o_c[r] + gt[:, nh + h:nh + h + 1] * o_s[r]
                + gt[:, 2 * nh + h:2 * nh + h + 1] * o_w[r])
        keep = (c_i >= HEAD_DIM) if h // HPG else (c_i < HEAD_DIM)
        o_ref[0, :, h * LANE:(h + 1) * LANE] = jnp.where(keep, comb, 0.0)


def _nsa_prompt(qa, gates, kv16, kc, tb, lb):
    B, S, _ = qa.shape
    nc = S // CMP_STRIDE
    n_cmp = (S - CMP_BLOCK) // CMP_STRIDE + 1
    m = jnp.arange(nc)
    pool_far = ((m[:, None] // SLC_PER_CMP - CMP_PAD // SLC_PER_CMP == jnp.arange(LANE)[None, :])
                & (m[:, None] >= CMP_PAD)).astype(BF16)
    blk_hot = (jnp.arange(S)[:, None] // SLC_BLOCK == jnp.arange(LANE)[None, :]).astype(BF16)
    ks_aug = jnp.concatenate([kv16[:, :, 2 * LANE:3 * LANE], jnp.broadcast_to(blk_hot[None], (B, S, LANE))], axis=-1)
    rows = N_HEADS_A * QB
    one = pl.Buffered(1)
    kvs = lambda c: pl.BlockSpec((1, S, LANE), lambda b, t: (b, 0, c), pipeline_mode=one)
    cspec = pl.BlockSpec((1, 1, nc + LANE, LANE), lambda b, t: (b, 0, 0, 0), pipeline_mode=one)
    vspec = pl.BlockSpec((1, 1, nc + LANE, LANE), lambda b, t: (b, 1, 0, 0), pipeline_mode=one)
    cst = lambda shape: pl.BlockSpec(shape, lambda b, t: (0,) * len(shape), pipeline_mode=one)
    return pl.pallas_call(
        functools.partial(_nsa_kernel, n_cmp=n_cmp), grid=(B, S // QB),
        in_specs=[pl.BlockSpec((1, QB, 1024), lambda b, t: (b, t, 0)),
                  pl.BlockSpec((1, QB, LANE), lambda b, t: (b, t, 0)),
                  pl.BlockSpec((1, S, 2 * LANE), lambda b, t: (b, 0, 0), pipeline_mode=one),
                  kvs(3), kvs(4), kvs(5), cspec, vspec, cst(tb.shape), cst(lb.shape), cst(pool_far.shape)],
        out_specs=pl.BlockSpec((1, QB, 1024), lambda b, t: (b, t, 0)),
        out_shape=jax.ShapeDtypeStruct((B, S, 1024), F32),
        scratch_shapes=[pltpu.VMEM((rows, 1), F32), pltpu.VMEM((rows, 1), F32), pltpu.VMEM((rows, LANE), F32)],
        name='nsa_prompt',
        compiler_params=pltpu.CompilerParams(dimension_semantics=('parallel', 'arbitrary'),
                                             vmem_limit_bytes=VMEM_LIMIT),
    )(qa, gates, ks_aug, kv16, kv16, kv16, kc, kc, tb, lb, pool_far)


def _mla_kernel(q_ref, lat_ref, wuv_ref, o_ref, m_scr, l_scr, acc_scr):
    qb = pl.program_id(1)
    nh = N_HEADS_B
    rows = nh * QB
    q = q_ref[0]
    q8 = jnp.concatenate([q[:, h * 256:(h + 1) * 256] for h in range(nh)], axis=0)
    m_scr[...] = jnp.full(m_scr.shape, -jnp.inf, F32)
    l_scr[...] = jnp.zeros(l_scr.shape, F32)
    acc_scr[...] = jnp.zeros(acc_scr.shape, F32)
    KT = KT_MLA
    nsub = KT // QB

    def tile(kt, masked):
        k0 = pl.multiple_of(kt * KT, KT)
        lat = lat_ref[0, pl.ds(k0, KT), :]
        s = _dot_t(q8, lat)
        if masked:
            col = lax.broadcasted_iota(jnp.int32, (rows, KT), 1)
            row = lax.broadcasted_iota(jnp.int32, (rows, KT), 0) & (QB - 1)
            s = jnp.where(col - row <= qb * QB - kt * KT, s, NEG)
        m_old = m_scr[...]
        m_new = jnp.maximum(m_old, _row_max(s))
        alpha = jnp.exp(m_old - m_new)
        p = jnp.exp(s - m_new)
        l_scr[...] = alpha * l_scr[...] + _row_sum(p)
        acc_scr[...] = alpha * acc_scr[...] + _dot(p.astype(BF16), lat[:, 0:KV_LORA])
        m_scr[...] = m_new

    def full_tile(kt, carry):
        tile(kt, False)
        return carry

    lax.fori_loop(0, qb // nsub, full_tile, 0)
    tile(qb // nsub, True)
    o_lat = (acc_scr[...] * (1.0 / l_scr[...])).astype(BF16)
    for h in range(nh):
        o_ref[0, :, h * LANE:(h + 1) * LANE] = _dot(o_lat[h * QB:(h + 1) * QB], wuv_ref[h])


def _mla_prompt(qm, lat16, w_uv):
    B, S, _ = qm.shape
    rows = N_HEADS_B * QB
    one = pl.Buffered(1)
    return pl.pallas_call(
        _mla_kernel, grid=(B, S // QB),
        in_specs=[pl.BlockSpec((1, QB, 2048), lambda b, t: (b, t, 0)),
                  pl.BlockSpec((1, S, 256), lambda b, t: (b, 0, 0), pipeline_mode=one),
                  pl.BlockSpec(w_uv.shape, lambda b, t: (0, 0, 0), pipeline_mode=one)],
        out_specs=pl.BlockSpec((1, QB, 1024), lambda b, t: (b, t, 0)),
        out_shape=jax.ShapeDtypeStruct((B, S, 1024), F32),
        scratch_shapes=[pltpu.VMEM((rows, 1), F32), pltpu.VMEM((rows, 1), F32), pltpu.VMEM((rows, KV_LORA), F32)],
        name='mla_prompt',
        compiler_params=pltpu.CompilerParams(dimension_semantics=('parallel', 'arbitrary'),
                                             vmem_limit_bytes=VMEM_LIMIT),
    )(qm, lat16, w_uv)


def _adaln_kernel(c_ref, w_ref, b_ref, o_ref):
    c = c_ref[...]
    a = (c * (1.0 / (1.0 + jnp.exp(-c)))).astype(BF16)
    o_ref[...] = _dot(a, w_ref[...].astype(BF16)) + b_ref[...]


def _adaln(c, w_ada, b_ada, tn=512):
    R_, D = c.shape
    N = w_ada.shape[1]
    return pl.pallas_call(
        _adaln_kernel, grid=(N // tn,),
        in_specs=[pl.BlockSpec((R_, D), lambda j: (0, 0)), pl.BlockSpec((D, tn), lambda j: (0, j)),
                  pl.BlockSpec((1, tn), lambda j: (0, j))],
        out_specs=pl.BlockSpec((R_, tn), lambda j: (0, j)),
        out_shape=jax.ShapeDtypeStruct((R_, N), F32), name='adaln',
        compiler_params=pltpu.CompilerParams(dimension_semantics=('parallel',), vmem_limit_bytes=VMEM_LIMIT),
    )(c, w_ada, b_ada.reshape(1, N))


def _merge_kernel(x_ref, oa_ref, ob_ref, ga_ref, shf_ref, scf_ref, gf_ref, na_ref, nb_ref, nffn_ref,
                  wa_ref, wb_ref, wr_ref, wgs_ref, wus_ref, wds_ref, xs_ref, f_ref, sc_ref):
    n_real = N_HEADS_A * HEAD_DIM
    oa = oa_ref[0]
    ob = ob_ref[0]
    na = oa * lax.rsqrt(jnp.sum(oa * oa, axis=-1, keepdims=True) * (1.0 / n_real) + EPS) * na_ref[...]
    nb = ob * lax.rsqrt(jnp.sum(ob * ob, axis=-1, keepdims=True) * (1.0 / n_real) + EPS) * nb_ref[...]
    mix = _dot(na.astype(BF16), wa_ref[...]) + _dot(nb.astype(BF16), wb_ref[...])
    x1 = x_ref[0] + ga_ref[0] * mix
    f = x1 * lax.rsqrt(jnp.mean(x1 * x1, axis=-1, keepdims=True) + EPS) * nffn_ref[...]
    f = f * (1.0 + scf_ref[0]) + shf_ref[0]
    for c in range(f.shape[1] // LANE):
        f_ref[0, :, c, :] = f[:, c * LANE:(c + 1) * LANE]
    fb = f.astype(BF16)
    sc_ref[0] = 1.0 / (1.0 + jnp.exp(-_dot(fb, wr_ref[...])))
    g = _dot(fb, wgs_ref[...])
    u = _dot(fb, wus_ref[...])
    hsh = (g * (1.0 / (1.0 + jnp.exp(-g))) * u).astype(BF16)
    xs_ref[0] = x1 + gf_ref[0] * _dot(hsh, wds_ref[...])


def _merge_weights(out_norm_a, out_norm_b, w_out, norm_ffn, w_router, w_gate_s, w_up_s, w_down_s):
    D = w_out.shape[1]
    na = out_norm_a.reshape(N_HEADS_A, 1, HEAD_DIM)
    grp = (jnp.arange(N_HEADS_A) // HPG)[:, None, None]
    half = jnp.arange(2)[None, :, None]
    na_pad = jnp.where(grp == half, na, 0.0).reshape(1, -1)
    nb_pad = jnp.pad(out_norm_b.reshape(N_HEADS_B, V_DIM), ((0, 0), (0, LANE - V_DIM))).reshape(1, -1)
    wa = w_out[:N_HEADS_A * HEAD_DIM].reshape(N_HEADS_A, 1, HEAD_DIM, D)
    wa_pad = jnp.where((grp == half)[..., None], wa, 0.0).reshape(-1, D)
    wb = w_out[N_HEADS_A * HEAD_DIM:].reshape(N_HEADS_B, V_DIM, D)
    wb_pad = jnp.pad(wb, ((0, 0), (0, LANE - V_DIM), (0, 0))).reshape(-1, D)
    bf = lambda a: a.astype(BF16)
    return [na_pad, nb_pad, norm_ffn.reshape(1, D), bf(wa_pad), bf(wb_pad), bf(w_router), bf(w_gate_s),
            bf(w_up_s), bf(w_down_s)]


def _mod_spec(a, tr):
    if a.shape[1] > 1:
        return pl.BlockSpec((1, tr, a.shape[2]), lambda b, t: (b, t, 0))
    return pl.BlockSpec((1, 1, a.shape[2]), lambda b, t: (b, 0, 0))


def _merge(x, oa, ob, gate_a, shift_f, scale_f, gate_f, mw, tr):
    B, T, D = x.shape
    row = lambda n: pl.BlockSpec((1, tr, n), lambda b, t: (b, t, 0))
    out_shape = [jax.ShapeDtypeStruct((B, T, D), F32), jax.ShapeDtypeStruct((B, T, D // LANE, LANE), F32),
                 jax.ShapeDtypeStruct((B, T, N_EXPERTS), F32)]
    return pl.pallas_call(
        _merge_kernel, grid=(B, T // tr),
        in_specs=[row(D), row(1024), row(1024)] + [_mod_spec(a, tr) for a in (gate_a, shift_f, scale_f, gate_f)]
        + [_const_spec(a.shape) for a in mw],
        out_specs=[row(D), pl.BlockSpec((1, tr, D // LANE, LANE), lambda b, t: (b, t, 0, 0)), row(N_EXPERTS)],
        out_shape=out_shape, name='merge',
        compiler_params=pltpu.CompilerParams(dimension_semantics=('parallel', 'parallel'),
                                             vmem_limit_bytes=VMEM_LIMIT),
    )(x, oa, ob, gate_a, shift_f, scale_f, gate_f, *mw)


MOE_CHUNK = 256
MOE_LOOK = 2


def _moe_kernel(nused_ref, bexp_ref, seg_ref, segexp_ref, nseg_ref, rtok_hbm, f_hbm, wg_hbm, wu_hbm, wd_hbm, y_ref,
                xbuf, wgf, wuf, wdf, wgb, wub, wdb, rtok, sem, isem, wsem):
    i = pl.program_id(0)
    nused = nused_ref[0]
    nchunk = rtok_hbm.shape[0] // MOE_CHUNK

    def weight_copies(j):
        e = segexp_ref[j]
        return [pltpu.make_async_copy(src.at[e], dst.at[j % 2], wsem.at[j % 2])
                for src, dst in ((wg_hbm, wgf), (wu_hbm, wuf), (wd_hbm, wdf))]

    def ids_copy(c):
        return pltpu.make_async_copy(rtok_hbm.at[pl.ds(c * MOE_CHUNK, MOE_CHUNK)], rtok.at[c % 2], isem.at[c % 2])

    nbuf = MOE_LOOK + 1
    nct = f_hbm.shape[1]

    def gather(blk):
        ids = rtok.at[(blk // MOE_CHUNK) % 2]
        row = blk % MOE_CHUNK
        for r in range(MOE_BLOCK):
            pltpu.make_async_copy(f_hbm.at[pl.ds(ids[row, r], 1)], xbuf.at[blk % nbuf, pl.ds(r, 1)],
                                  sem.at[blk % nbuf]).start()

    def wait_gather(blk):
        pltpu.make_async_copy(f_hbm.at[pl.ds(0, MOE_BLOCK)], xbuf.at[blk % nbuf], sem.at[blk % nbuf]).wait()

    @pl.when(i == 0)
    def _():
        ids_copy(0).start()
        ids_copy(0).wait()
        if nchunk > 1:
            ids_copy(1).start()
        for b in range(MOE_LOOK):
            gather(b)
        for cp in weight_copies(0):
            cp.start()

    @pl.when((i == 0) | (bexp_ref[i] != bexp_ref[jnp.maximum(i - 1, 0)]))
    def _():
        j = seg_ref[i]
        for cp in weight_copies(j):
            cp.wait()

        @pl.when(j + 1 < nseg_ref[0])
        def _():
            for cp in weight_copies(j + 1):
                cp.start()
        wgb[...] = wgf[j % 2].astype(BF16)
        wub[...] = wuf[j % 2].astype(BF16)
        wdb[...] = wdf[j % 2].astype(BF16)

    @pl.when(((i + MOE_LOOK) % MOE_CHUNK == 0) & ((i + MOE_LOOK) // MOE_CHUNK < nchunk))
    def _():
        c = (i + MOE_LOOK) // MOE_CHUNK
        ids_copy(c).wait()

        @pl.when(c + 1 < nchunk)
        def _():
            ids_copy(c + 1).start()

    @pl.when(i < nused)
    def _():
        wait_gather(i)
        gather(i + MOE_LOOK)
        slot = i % nbuf
        g = jnp.zeros((MOE_BLOCK, wgb.shape[1]), F32)
        u = jnp.zeros((MOE_BLOCK, wub.shape[1]), F32)
        for c in range(0, nct, 2):
            x = jnp.concatenate([xbuf[slot, :, c, :], xbuf[slot, :, c + 1, :]], axis=1).astype(BF16)
            g = g + _dot(x, wgb[c * LANE:(c + 2) * LANE, :])
            u = u + _dot(x, wub[c * LANE:(c + 2) * LANE, :])
        h = (g * (1.0 / (1.0 + jnp.exp(-g))) * u).astype(BF16)
        y = _dot(h, wdb[...])
        for c in range(nct):
            y_ref[:, c, :] = y[:, c * LANE:(c + 1) * LANE]

    @pl.when(i >= nused)
    def _():
        @pl.when(i < nused + MOE_LOOK)
        def _():
            wait_gather(i)
        y_ref[...] = jnp.zeros(y_ref.shape, F32)


def _moe_experts(nused, blk_exp, row_tok, f, w_gate_e, w_up_e, w_down_e):
    n_blocks = blk_exp.shape[0]
    n_steps = n_blocks + MOE_LOOK
    nb_pad = -(-n_steps // MOE_CHUNK) * MOE_CHUNK
    row_tok = jnp.pad(row_tok, (0, (nb_pad - n_blocks) * MOE_BLOCK)).reshape(nb_pad, MOE_BLOCK)
    blk_exp = jnp.pad(blk_exp, (0, MOE_LOOK), mode='edge')
    nct = f.shape[1]
    D = nct * LANE
    ne, _, de = w_gate_e.shape
    new_run = jnp.concatenate([jnp.ones((1,), jnp.int32), (blk_exp[1:] != blk_exp[:-1]).astype(jnp.int32)])
    seg = jnp.cumsum(new_run) - 1
    seg_exp = jnp.zeros((ne + 1,), jnp.int32).at[seg].set(blk_exp)
    nseg = (seg[-1] + 1).reshape(1)
    hbm = pl.BlockSpec(memory_space=pl.ANY)
    return pl.pallas_call(
        _moe_kernel,
        grid_spec=pltpu.PrefetchScalarGridSpec(
            num_scalar_prefetch=5, grid=(n_steps,),
            in_specs=[hbm, hbm, hbm, hbm, hbm],
            out_specs=pl.BlockSpec((MOE_BLOCK, nct, LANE), lambda i, *_: (i, 0, 0)),
            scratch_shapes=[pltpu.VMEM((MOE_LOOK + 1, MOE_BLOCK, nct, LANE), F32),
                            pltpu.VMEM((2, D, de), F32), pltpu.VMEM((2, D, de), F32), pltpu.VMEM((2, de, D), F32),
                            pltpu.VMEM((D, de), BF16), pltpu.VMEM((D, de), BF16), pltpu.VMEM((de, D), BF16),
                            pltpu.SMEM((2, MOE_CHUNK, MOE_BLOCK), jnp.int32),
                            pltpu.SemaphoreType.DMA((MOE_LOOK + 1,)), pltpu.SemaphoreType.DMA((2,)),
                            pltpu.SemaphoreType.DMA((2,))]),
        out_shape=jax.ShapeDtypeStruct((n_steps * MOE_BLOCK, nct, LANE), F32), name='moe_experts',
        compiler_params=pltpu.CompilerParams(dimension_semantics=('arbitrary',), vmem_limit_bytes=VMEM_LIMIT),
    )(nused, blk_exp, seg, seg_exp, nseg, row_tok, f, w_gate_e, w_up_e, w_down_e)


def _combine_kernel(pos_ref, yb_hbm, xs_ref, gf_ref, w_ref, nf_ref, o_ref, buf, sem):
    tile = pl.program_id(0) * pl.num_programs(1) + pl.program_id(1)
    ntile = pl.num_programs(0) * pl.num_programs(1)

    nct = yb_hbm.shape[1]

    def start(t, slot):
        def per_k(k, c):
            for r in range(MOE_BLOCK):
                pltpu.make_async_copy(yb_hbm.at[pl.ds(pos_ref[t * TOP_K + k, r], 1)],
                                      buf.at[slot, k, pl.ds(r, 1)], sem.at[slot]).start()
            return c
        lax.fori_loop(0, TOP_K, per_k, 0)

    @pl.when(tile == 0)
    def _():
        start(0, 0)

    @pl.when(tile + 1 < ntile)
    def _():
        start(tile + 1, (tile + 1) % 2)

    slot = tile % 2
    w = w_ref[0]
    for k in range(TOP_K):
        pltpu.make_async_copy(yb_hbm.at[pl.ds(0, MOE_BLOCK)], buf.at[slot, k], sem.at[slot]).wait()
    x2 = []
    ss = jnp.zeros((MOE_BLOCK, 1), F32)
    for c in range(nct):
        routed = jnp.zeros((MOE_BLOCK, LANE), F32)
        for k in range(TOP_K):
            routed = routed + w[:, k:k + 1] * buf[slot, k, :, c, :]
        cols = slice(c * LANE, (c + 1) * LANE)
        xc = xs_ref[0, :, cols] + gf_ref[0, :, cols] * routed
        ss = ss + jnp.sum(xc * xc, axis=-1, keepdims=True)
        x2.append(xc)
    inv = lax.rsqrt(ss * (1.0 / (nct * LANE)) + EPS)
    for c in range(nct):
        cols = slice(c * LANE, (c + 1) * LANE)
        o_ref[0, :, cols] = x2[c] * inv * nf_ref[:, cols]


def _combine(pos, yb, xs, gate_f, wts, norm_final):
    B, T, D = xs.shape
    tr = MOE_BLOCK
    gspec = (pl.BlockSpec((1, tr, D), lambda b, t, p: (b, t, 0)) if gate_f.shape[1] > 1
             else pl.BlockSpec((1, 1, D), lambda b, t, p: (b, 0, 0)))
    return pl.pallas_call(
        _combine_kernel,
        grid_spec=pltpu.PrefetchScalarGridSpec(
            num_scalar_prefetch=1, grid=(B, T // tr),
            in_specs=[pl.BlockSpec(memory_space=pl.ANY), pl.BlockSpec((1, tr, D), lambda b, t, p: (b, t, 0)), gspec,
                      pl.BlockSpec((1, tr, TOP_K), lambda b, t, p: (b, t, 0)),
                      pl.BlockSpec((1, D), lambda b, t, p: (0, 0))],
            out_specs=pl.BlockSpec((1, tr, D), lambda b, t, p: (b, t, 0)),
            scratch_shapes=[pltpu.VMEM((2, TOP_K, tr, D // LANE, LANE), F32), pltpu.SemaphoreType.DMA((2,))]),
        out_shape=jax.ShapeDtypeStruct((B, T, D), F32), name='combine',
        compiler_params=pltpu.CompilerParams(dimension_semantics=('arbitrary', 'arbitrary'),
                                             vmem_limit_bytes=VMEM_LIMIT),
    )(pos, yb, xs, gate_f, wts, norm_final.reshape(1, D))


def _assign_rows_kernel(idx_ref, rank_ref, start_ref, pos_ref):
    idx = idx_ref[...]
    tr = idx.shape[0]
    start = start_ref[...]
    lane = lax.broadcasted_iota(jnp.int32, (tr, start.shape[1]), 1)
    kcol = lax.broadcasted_iota(jnp.int32, (tr, TOP_K), 1)
    base = jnp.zeros((tr, TOP_K), F32)
    for k in range(TOP_K):
        sk = jnp.sum(jnp.where(lane == idx[:, k:k + 1], start, 0.0), axis=1, keepdims=True)
        base = jnp.where(kcol == k, sk, base)
    pos_ref[...] = base.astype(jnp.int32) + rank_ref[...]


def _assign_rows(idx, rank, start):
    n_tok = idx.shape[0]
    tspec = pl.BlockSpec((MOE_BLOCK, TOP_K), lambda i: (i, 0))
    return pl.pallas_call(
        _assign_rows_kernel, grid=(n_tok // MOE_BLOCK,),
        in_specs=[tspec, tspec, pl.BlockSpec(start.shape, lambda i: (0, 0))], out_specs=tspec,
        out_shape=jax.ShapeDtypeStruct((n_tok, TOP_K), jnp.int32), name='assign_rows',
        compiler_params=pltpu.CompilerParams(dimension_semantics=('parallel',), vmem_limit_bytes=VMEM_LIMIT),
    )(idx, rank, start)


def _route(scores, router_bias):
    n_tok = scores.shape[0]
    n_asg = n_tok * TOP_K
    idx, wts, rank, counts = _route_rank(scores, router_bias.astype(F32).reshape(1, N_EXPERTS))
    padded = (counts + MOE_BLOCK - 1) // MOE_BLOCK * MOE_BLOCK
    pad_end = jnp.cumsum(padded)
    pos = _assign_rows(idx, rank, (pad_end - padded).astype(F32).reshape(1, N_EXPERTS))
    n_blocks = -(-(n_asg + N_EXPERTS * (MOE_BLOCK - 1)) // MOE_BLOCK)
    n_rows = n_blocks * MOE_BLOCK
    tok = jnp.broadcast_to(jnp.arange(n_tok, dtype=jnp.int32)[:, None], pos.shape)
    row_tok = jnp.zeros((n_rows,), jnp.int32).at[pos.reshape(-1)].set(tok.reshape(-1))
    blk_start = jnp.arange(n_blocks, dtype=jnp.int32)[:, None] * MOE_BLOCK
    blk_exp = jnp.minimum(jnp.sum((pad_end[None, :] <= blk_start).astype(jnp.int32), axis=1), N_EXPERTS - 1)
    nused = (pad_end[-1] // MOE_BLOCK).astype(jnp.int32).reshape(1)
    return wts, pos, row_tok, blk_exp, nused


def _route_kernel(sc_ref, rb_ref, idx_ref, wts_ref, rank_ref, cnt_ref, carry):
    i = pl.program_id(0)

    @pl.when(i == 0)
    def _():
        carry[...] = jnp.zeros(carry.shape, F32)

    tr, ne = sc_ref.shape
    gsz = ne // N_GROUPS
    scores = sc_ref[...]
    biased = scores + rb_ref[...]
    lane = lax.broadcasted_iota(jnp.int32, (tr, ne), 1)
    lane_f = lane.astype(F32)
    lgrp = lane // gsz
    ninf = -jnp.inf

    def first_max(x):
        mx = jnp.max(x, axis=1, keepdims=True)
        return mx, jnp.min(jnp.where(x == mx, lane_f, float(ne)), axis=1, keepdims=True)

    gl = lax.broadcasted_iota(jnp.int32, (tr, LANE), 1)
    gscore = jnp.full((tr, LANE), ninf, F32)
    for g in range(N_GROUPS):
        xg = jnp.where(lgrp == g, biased, ninf)
        m1, f1 = first_max(xg)
        m2 = jnp.max(jnp.where(lane_f == f1, ninf, xg), axis=1, keepdims=True)
        gscore = jnp.where(gl == g, m1 + m2, gscore)
    gl_f = gl.astype(F32)
    keep = jnp.full((tr, ne), NEG, F32)
    for _ in range(TOP_GROUPS):
        mx = jnp.max(gscore, axis=1, keepdims=True)
        gf = jnp.min(jnp.where(gscore == mx, gl_f, float(LANE)), axis=1, keepdims=True)
        gscore = jnp.where(gl_f == gf, ninf, gscore)
        keep = jnp.where(lgrp.astype(F32) == gf, biased, keep)
    kcol = lax.broadcasted_iota(jnp.int32, (tr, TOP_K), 1)
    idx = jnp.zeros((tr, TOP_K), F32)
    wts = jnp.zeros((tr, TOP_K), F32)
    hot = []
    onehot = jnp.zeros((tr, ne), F32)
    for k in range(TOP_K):
        _, f = first_max(keep)
        hit = lane_f == f
        hot.append(hit)
        keep = jnp.where(hit, ninf, keep)
        onehot = jnp.where(hit, 1.0, onehot)
        idx = jnp.where(kcol == k, f, idx)
        wts = jnp.where(kcol == k, jnp.sum(jnp.where(hit, scores, 0.0), axis=1, keepdims=True), wts)
    idx_ref[...] = idx.astype(jnp.int32)
    wts_ref[...] = wts / jnp.sum(wts, axis=1, keepdims=True) * ROUTED_SCALE
    r_i = lax.broadcasted_iota(jnp.int32, (tr, tr), 0)
    c_i = lax.broadcasted_iota(jnp.int32, (tr, tr), 1)
    lower = jnp.where(c_i < r_i, 1.0, 0.0).astype(BF16)
    before = _dot(lower, onehot.astype(BF16)) + carry[0:1, :]
    rank = jnp.zeros((tr, TOP_K), F32)
    for k in range(TOP_K):
        rank = jnp.where(kcol == k, jnp.sum(jnp.where(hot[k], before, 0.0), axis=1, keepdims=True), rank)
    rank_ref[...] = rank.astype(jnp.int32)
    total = carry[0:1, :] + jnp.sum(onehot, axis=0, keepdims=True)
    carry[...] = jnp.broadcast_to(total, carry.shape)
    cnt_ref[...] = jnp.broadcast_to(total, cnt_ref.shape).astype(jnp.int32)


def _route_rank(scores, router_bias):
    n_tok, ne = scores.shape
    tr = MOE_BLOCK
    tk = lambda dt: jax.ShapeDtypeStruct((n_tok, TOP_K), dt)
    tspec = pl.BlockSpec((tr, TOP_K), lambda i: (i, 0))
    idx, wts, rank, cnt = pl.pallas_call(
        _route_kernel, grid=(n_tok // tr,),
        in_specs=[pl.BlockSpec((tr, ne), lambda i: (i, 0)), pl.BlockSpec((1, ne), lambda i: (0, 0))],
        out_specs=[tspec, tspec, tspec, pl.BlockSpec((8, ne), lambda i: (0, 0))],
        out_shape=[tk(jnp.int32), tk(F32), tk(jnp.int32), jax.ShapeDtypeStruct((8, ne), jnp.int32)],
        scratch_shapes=[pltpu.VMEM((8, ne), F32)], name='route',
        compiler_params=pltpu.CompilerParams(dimension_semantics=('arbitrary',), vmem_limit_bytes=VMEM_LIMIT),
    )(scores, router_bias)
    return idx, wts, rank, cnt[0]


PAGE = 128


def _softmax_with_new(s, s_new):
    m = jnp.maximum(jnp.max(s, axis=1, keepdims=True), s_new)
    p = jnp.exp(s - m)
    pn = jnp.exp(s_new - m)
    return p, pn, 1.0 / (jnp.sum(p, axis=1, keepdims=True) + pn)


def _samp_cmp_kernel(pt_ref, pool_hbm, q_ref, wbd_ref, pe_ref, w1c_ref, w2_ref, bc_ref, pf_ref,
                     oc_ref, idx_ref, buf_t, buf, peh_scr, sem, *, n_pages, n_cmp):
    s = pl.program_id(0)
    ns = pl.num_programs(0)
    nc = n_pages * PAGE // CMP_STRIDE

    def page_copy(page, slot, p):
        return pltpu.make_async_copy(pool_hbm.at[page], buf_t.at[slot, :, :, pl.ds(p * PAGE, PAGE)], sem.at[slot])

    def start(smp, slot):
        def body(p, c):
            page_copy(pt_ref[smp, p], slot, p).start()
            return c
        lax.fori_loop(0, n_pages, body, 0)

    @pl.when(s == 0)
    def _():
        start(0, 0)
        for j in range(2):
            pp = _dot(pe_ref[j], w1c_ref[j])
            peh_scr[j] = jnp.broadcast_to(pp[0:1, 0:CMP_HIDDEN] + pp[1:2, CMP_HIDDEN:], (8, CMP_HIDDEN))

    @pl.when(s + 1 < ns)
    def _():
        start(s + 1, (s + 1) % 2)

    slot = s % 2

    def wait_page(p, c):
        page_copy(0, slot, p).wait()
        return c
    lax.fori_loop(0, n_pages, wait_page, 0)

    tw = 4 * PAGE

    def to_rows(c, carry):
        l0 = pl.multiple_of(c * tw, tw)
        for j in range(2):
            xt = buf_t[slot, pl.ds(2 * j, 2), :, pl.ds(l0, tw)].reshape(2 * HEAD_DIM, tw)
            buf[j, pl.ds(l0, tw), :] = xt.T
        return carry
    lax.fori_loop(0, n_pages * PAGE // tw, to_rows, 0)

    rows = lax.broadcasted_iota(jnp.int32, (nc, LANE), 0)
    kvc = []
    for j in range(2):
        acc = jnp.zeros((nc, 4 * CMP_HIDDEN), F32)
        for r in range(0, CMP_STRIDE, 2):
            x = jnp.concatenate([buf[j, pl.ds(r, nc, stride=CMP_STRIDE), :],
                                 buf[j, pl.ds(r + 1, nc, stride=CMP_STRIDE), :]], axis=1).astype(BF16)
            acc = acc + _dot(x, wbd_ref[j, r // 2])
        out = jnp.zeros((nc, LANE), F32)
        for g in range(N_KV_A):
            a = acc[:, g * 2 * CMP_HIDDEN:g * 2 * CMP_HIDDEN + CMP_HIDDEN]
            bm = acc[:, g * 2 * CMP_HIDDEN + CMP_HIDDEN:(g + 1) * 2 * CMP_HIDDEN]
            hid = a + pltpu.roll(bm, nc - 1, 0) + peh_scr[j, 0:1]
            act = hid * (1.0 / (1.0 + jnp.exp(-hid)))
            out = out + _dot(act.astype(BF16), w2_ref[j, g])
        kvc.append(jnp.where(rows < n_cmp, out, 0.0).astype(BF16))
    kc, vc = kvc

    q8 = q_ref[0]
    sc = _dot_t(q8, kc) + bc_ref[...]
    p = jnp.exp(sc - jnp.max(sc, axis=1, keepdims=True))
    p = p * (1.0 / jnp.sum(p, axis=1, keepdims=True))
    oc_ref[0] = _dot(p.astype(BF16), vc)

    pg = jnp.concatenate([jnp.sum(p[0:HPG], axis=0, keepdims=True), jnp.sum(p[HPG:2 * HPG], axis=0, keepdims=True),
                          jnp.zeros((8 - N_KV_A, nc), F32)], axis=0)
    imp = _hilo_dot(pg, pf_ref[...])
    lane = lax.broadcasted_iota(jnp.int32, (8, LANE), 1).astype(F32)
    score = jnp.where(lane == 0.0, FORCED, jnp.where(lane >= float(LANE - N_LOCAL_SLC + 1), FORCED, imp))
    picks = jnp.full((8, LANE), float(LANE), F32)
    for k in range(N_SLC - 1):
        mx = jnp.max(score, axis=1, keepdims=True)
        first = jnp.min(jnp.where(score == mx, lane, float(LANE)), axis=1, keepdims=True)
        picks = jnp.where(lane == float(k), first, picks)
        score = jnp.where(lane == first, -jnp.inf, score)
    idx_ref[0] = picks.astype(jnp.int32)


def _samp_cmp(page_table, pool2d, q3, scw, bc, n_cmp):
    Bd, n_pages = page_table.shape
    nc = n_pages * PAGE // CMP_STRIDE
    wbd, pe, w1c, w2g = scw
    m = jnp.arange(nc)
    pool_m = ((m[:, None] // SLC_PER_CMP == jnp.arange(LANE)[None, :]) & (m[:, None] < n_cmp)).astype(BF16)
    one = pl.Buffered(1)
    cst = lambda a: pl.BlockSpec(a.shape, lambda s, pt: (0,) * a.ndim, pipeline_mode=one)
    return pl.pallas_call(
        functools.partial(_samp_cmp_kernel, n_pages=n_pages, n_cmp=n_cmp),
        grid_spec=pltpu.PrefetchScalarGridSpec(
            num_scalar_prefetch=1, grid=(Bd,),
            in_specs=[pl.BlockSpec(memory_space=pl.ANY), pl.BlockSpec((1, 8, LANE), lambda s, pt: (s, 0, 0)),
                      cst(wbd), cst(pe), cst(w1c), cst(w2g), cst(bc), cst(pool_m)],
            out_specs=[pl.BlockSpec((1, 8, LANE), lambda s, pt: (s, 0, 0)),
                       pl.BlockSpec((1, 8, LANE), lambda s, pt: (s, 0, 0))],
            scratch_shapes=[pltpu.VMEM((2, 4, HEAD_DIM, n_pages * PAGE), F32),
                            pltpu.VMEM((2, n_pages * PAGE, LANE), F32), pltpu.VMEM((2, 8, CMP_HIDDEN), F32),
                            pltpu.SemaphoreType.DMA((2,))]),
        out_shape=[jax.ShapeDtypeStruct((Bd, 8, LANE), F32), jax.ShapeDtypeStruct((Bd, 8, LANE), jnp.int32)],
        name='sample_cmp',
        compiler_params=pltpu.CompilerParams(dimension_semantics=('arbitrary',), vmem_limit_bytes=VMEM_LIMIT),
    )(page_table, pool2d, q3, wbd, pe, w1c, w2g, bc, pool_m)


def _samp_cmp_weights(cw):
    w1cat, pe, w2a, w2b = cw
    w = w1cat.reshape(2, CMP_STRIDE, HEAD_DIM, 2 * CMP_HIDDEN)
    z = jnp.zeros_like(w)
    wbd = jnp.concatenate([jnp.concatenate([w, z], axis=-1), jnp.concatenate([z, w], axis=-1)], axis=2)
    wbd = wbd.reshape(2, CMP_STRIDE // 2, 2 * LANE, 4 * CMP_HIDDEN)
    return wbd, pe, w1cat, jnp.stack([w2a, w2b], axis=1)


def _t5_bias_rows(rel_bias, rel, valid):
    b = _bucket_lookup(rel_bias.astype(F32), _t5_bucket(rel))
    return jnp.where(valid[None, :], b.T, NEG)


def _bucket_bias(rel, tbl_t):
    max_exact = N_BUCKETS // 2
    nf = jnp.maximum(rel, 1).astype(F32)
    large = max_exact + (jnp.log(nf / max_exact) / math.log(MAX_DISTANCE / max_exact)
                         * (N_BUCKETS - max_exact)).astype(jnp.int32)
    bucket = jnp.where(rel < max_exact, rel, jnp.minimum(large, N_BUCKETS - 1))
    bias = jnp.zeros(rel.shape, F32)
    for b in range(N_BUCKETS):
        bias = jnp.where(bucket == b, tbl_t[:, b:b + 1], bias)
    return bias


def _samp_sw_kernel(pt_ref, idx_ref, pool_hbm, q_ref, knew_ref, win_ref, wnew_ref, wcol_ref, oc_ref, gcol_ref,
                    tblt_ref, bw_ref, oa_ref, nwin_ref, kvbuf, sem, *, past_len):
    s = pl.program_id(0)
    ns = pl.num_programs(0)
    npb = past_len // SLC_BLOCK
    bpp = PAGE // SLC_BLOCK
    nk = N_SLC * PAGE

    def block_copy(page, slot, g, k, kv):
        return pltpu.make_async_copy(pool_hbm.at[page, kv * N_KV_A + g],
                                     kvbuf.at[slot, kv * N_KV_A + g, :, pl.ds(k * PAGE, PAGE)], sem.at[slot])

    def start(smp, slot):
        for g in range(N_KV_A):
            for k in range(N_SLC):
                j = jnp.minimum(idx_ref[(smp * N_KV_A + g) * N_SLC + k], npb - 1)
                page = pt_ref[smp, j // bpp]
                for kv in range(2):
                    block_copy(page, slot, g, k, kv).start()

    @pl.when(s == 0)
    def _():
        start(0, 0)

    @pl.when(s + 1 < ns)
    def _():
        start(s + 1, (s + 1) % 2)

    slot = s % 2
    q8 = q_ref[0]
    q32 = q8.astype(F32)
    qg = [q8[:, g * HEAD_DIM:(g + 1) * HEAD_DIM] for g in range(N_KV_A)]
    tbl_t = tblt_ref[...]
    row = lax.broadcasted_iota(jnp.int32, (8, LANE), 0)
    lane = lax.broadcasted_iota(jnp.int32, (8, LANE), 1)
    grp0 = row[:, 0:1] < HPG

    def new_token(kv_row):
        kn = kv_row[:, 0:LANE].astype(BF16).astype(F32)
        vn = kv_row[:, LANE:].astype(BF16).astype(F32)
        return jnp.sum(q32 * kn, axis=1, keepdims=True) + tbl_t[:, 0:1], vn

    def by_group(a0, a1):
        return jnp.concatenate([jnp.where(grp0, a0, 0.0), jnp.where(grp0, 0.0, a1)], axis=1)

    w = win_ref[0]
    wl = w.shape[-1]
    s_new, v_new = new_token(wnew_ref[0])
    sw = jnp.where(grp0, _dot(qg[0], w[0].astype(BF16)), _dot(qg[1], w[1].astype(BF16))) + bw_ref[...]
    p, pn, inv = _softmax_with_new(sw, s_new)
    pb = p.astype(BF16)
    o_w = (by_group(_dot_t(pb, w[2].astype(BF16)), _dot_t(pb, w[3].astype(BF16))) + pn * v_new) * inv
    wcol = lax.broadcasted_iota(jnp.int32, (HEAD_DIM, wl), 1)
    for c in range(2 * N_KV_A):
        nwin_ref[0, c] = jnp.where(wcol == wl - 1, wcol_ref[0, c], pltpu.roll(w[c], wl - 1, 1))

    for g in range(N_KV_A):
        for k in range(N_SLC):
            for kv in range(2):
                block_copy(0, slot, g, k, kv).wait()
    kl = lax.broadcasted_iota(jnp.int32, (8, nk), 1)
    kslot = kl >> 7
    kin = kl & (PAGE - 1)
    rowk = lax.broadcasted_iota(jnp.int32, (8, nk), 0) < HPG
    blk = jnp.zeros((8, nk), jnp.int32)
    for k in range(N_SLC):
        j0 = idx_ref[(s * N_KV_A) * N_SLC + k]
        j1 = idx_ref[(s * N_KV_A + 1) * N_SLC + k]
        blk = jnp.where(kslot == k, jnp.where(rowk, j0, j1), blk)
    rel = past_len - ((blk // bpp) * PAGE + kin)
    ok = (kin // SLC_BLOCK) == jnp.where(blk < npb, blk % bpp, -1)
    ss = jnp.where(rowk, _dot(qg[0], kvbuf[slot, 0].astype(BF16)), _dot(qg[1], kvbuf[slot, 1].astype(BF16)))
    ss = jnp.where(ok, ss + _bucket_bias(jnp.maximum(rel, 0), tbl_t), NEG)
    s_new, v_new = new_token(knew_ref[0])
    p, pn, inv = _softmax_with_new(ss, s_new)
    pb = p.astype(BF16)
    o_s = (by_group(_dot_t(pb, kvbuf[slot, 2].astype(BF16)), _dot_t(pb, kvbuf[slot, 3].astype(BF16)))
           + pn * v_new) * inv

    gc = gcol_ref[0]
    o = gc[:, 0:1] * oc_ref[0] + gc[:, 1:2] * o_s + gc[:, 2:3] * o_w
    oa_ref[0] = jnp.where((lane >= HEAD_DIM) == (row >= HPG), o, 0.0)


def _samp_sw(page_table, idx_flat, pool_t, q3, knew, win_t, wnew, wcol, o_c, gcol, tbl_t, bw, past_len):
    Bd = page_table.shape[0]
    wl = win_t.shape[-1]
    per = lambda shp: pl.BlockSpec((1,) + shp, lambda s, pt, ix: (s,) + (0,) * len(shp))
    cst = lambda a: pl.BlockSpec(a.shape, lambda s, pt, ix: (0,) * a.ndim)
    return pl.pallas_call(
        functools.partial(_samp_sw_kernel, past_len=past_len),
        grid_spec=pltpu.PrefetchScalarGridSpec(
            num_scalar_prefetch=2, grid=(Bd,),
            in_specs=[pl.BlockSpec(memory_space=pl.ANY), per((8, LANE)), per((1, 2 * LANE)),
                      per((2 * N_KV_A, HEAD_DIM, wl)), per((1, 2 * LANE)), per((2 * N_KV_A, HEAD_DIM, 1)),
                      per((8, LANE)), per((8, LANE)), cst(tbl_t), cst(bw)],
            out_specs=[per((8, LANE)), per((2 * N_KV_A, HEAD_DIM, wl))],
            scratch_shapes=[pltpu.VMEM((2, 2 * N_KV_A, HEAD_DIM, N_SLC * PAGE), F32),
                            pltpu.SemaphoreType.DMA((2,))]),
        out_shape=[jax.ShapeDtypeStruct((Bd, 8, LANE), F32), jax.ShapeDtypeStruct(win_t.shape, F32)],
        name='sample_slc_win',
        compiler_params=pltpu.CompilerParams(dimension_semantics=('arbitrary',), vmem_limit_bytes=VMEM_LIMIT),
    )(page_table, idx_flat, pool_t, q3, knew, win_t, wnew, wcol, o_c, gcol, tbl_t, bw)


def _samp_mla_kernel(pt_ref, pool_hbm, q_ref, lnew_ref, o_ref, buf, sem, *, n_pages):
    s = pl.program_id(0)
    ns = pl.num_programs(0)

    def page_copy(page, slot, p):
        return pltpu.make_async_copy(pool_hbm.at[page], buf.at[slot, :, pl.ds(p * PAGE, PAGE)], sem.at[slot])

    def start(smp, slot):
        def body(p, c):
            page_copy(pt_ref[smp, p], slot, p).start()
            return c
        lax.fori_loop(0, n_pages, body, 0)

    @pl.when(s == 0)
    def _():
        start(0, 0)

    @pl.when(s + 1 < ns)
    def _():
        start(s + 1, (s + 1) % 2)

    slot = s % 2

    def wait_page(p, c):
        page_copy(0, slot, p).wait()
        return c
    lax.fori_loop(0, n_pages, wait_page, 0)

    q8 = q_ref[0]
    ckv_t = buf[slot, 0:KV_LORA, :].astype(BF16)
    kr_t = buf[slot, KV_LORA:LATENT_DIM, :].astype(BF16)
    ln = lnew_ref[0].astype(F32)
    sc = _dot(q8[:, 0:KV_LORA], ckv_t) + _dot(q8[:, KV_LORA:LATENT_DIM], kr_t)
    s_new = jnp.sum(q8.astype(F32) * ln, axis=1, keepdims=True)
    p, pn, inv = _softmax_with_new(sc, s_new)
    o_ref[0] = (_dot_t(p.astype(BF16), ckv_t) + pn * ln[:, 0:KV_LORA]) * inv


def _samp_mla(page_table, pool2d, qm3, lnew):
    Bd, n_pages = page_table.shape
    per = lambda shp: pl.BlockSpec((1,) + shp, lambda s, pt: (s, 0, 0))
    return pl.pallas_call(
        functools.partial(_samp_mla_kernel, n_pages=n_pages),
        grid_spec=pltpu.PrefetchScalarGridSpec(
            num_scalar_prefetch=1, grid=(Bd,),
            in_specs=[pl.BlockSpec(memory_space=pl.ANY), per((8, 256)), per((1, 256))],
            out_specs=per((8, KV_LORA)),
            scratch_shapes=[pltpu.VMEM((2, LATENT_DIM, n_pages * PAGE), F32), pltpu.SemaphoreType.DMA((2,))]),
        out_shape=jax.ShapeDtypeStruct((Bd, 8, KV_LORA), F32), name='sample_mla',
        compiler_params=pltpu.CompilerParams(dimension_semantics=('arbitrary',), vmem_limit_bytes=VMEM_LIMIT),
    )(page_table, pool2d, qm3, lnew)


def _uv_kernel(o_ref, w_ref, y_ref):
    y_ref[...] = _dot(o_ref[...].astype(BF16), w_ref[0])


def _samp_uv(o_lat2d, w_uv):
    Bd = o_lat2d.shape[0]
    return pl.pallas_call(
        _uv_kernel, grid=(N_HEADS_B,),
        in_specs=[pl.BlockSpec((Bd, LANE), lambda h: (0, h)), pl.BlockSpec((1, KV_LORA, LANE), lambda h: (h, 0, 0))],
        out_specs=pl.BlockSpec((Bd, LANE), lambda h: (0, h)),
        out_shape=jax.ShapeDtypeStruct((Bd, N_HEADS_B * LANE), F32), name='sample_uv',
        compiler_params=pltpu.CompilerParams(dimension_semantics=('parallel',), vmem_limit_bytes=VMEM_LIMIT),
    )(o_lat2d, w_uv)


def _sample_mix(xs3, msm, pool_cmp, pool_slc, win_buf, pool_mla, page_table, gain, rel_bias, w, cw):
    Bd = xs3.shape[1]
    past_len = page_table.shape[1] * PAGE
    cos_s, sin_s = _rope_tables(jnp.full((Bd,), past_len, jnp.int32))
    qa_s, cmp_s, slc_s, win_s, _, gt_s, qm_s, lat_s, lat16_s = _inproj(xs3, msm[0], msm[1], gain, cos_s, sin_s, w, Bd)
    q3 = qa_s.reshape(Bd, N_HEADS_A, LANE)
    nc = past_len // CMP_STRIDE
    n_cmp = (past_len + 1 - CMP_BLOCK) // CMP_STRIDE + 1
    m = jnp.arange(nc)
    bc = _t5_bias_rows(rel_bias, past_len - (m * CMP_STRIDE + CMP_BLOCK - 1), m < n_cmp)
    fm = lambda a: a.transpose(0, 2, 3, 4, 1).reshape(a.shape[0], 2 * N_KV_A, HEAD_DIM, a.shape[1])
    o_c, idx = _samp_cmp(page_table, fm(pool_cmp), q3, _samp_cmp_weights(cw), bc, n_cmp)
    idx_flat = idx[:, :N_KV_A, :N_SLC].reshape(-1)
    wl = win_buf.shape[1]
    wi = jnp.arange(wl)
    bw = _t5_bias_rows(rel_bias, wl - wi, (wl - wi < WINDOW) & (past_len - wl + wi >= 0))
    tbl_t = jnp.pad(rel_bias.astype(F32).T, ((0, 0), (0, LANE - N_BUCKETS)))
    gcol = gt_s[0, :, :3 * N_HEADS_A].reshape(Bd, 3, N_HEADS_A).transpose(0, 2, 1)
    gcol = jnp.pad(gcol, ((0, 0), (0, 0), (0, LANE - 3)))
    oa_s, new_win = _samp_sw(page_table, idx_flat, fm(pool_slc), q3, slc_s.reshape(Bd, 1, 2 * LANE), fm(win_buf),
                             win_s.reshape(Bd, 1, 2 * LANE), win_s.reshape(Bd, 2 * N_KV_A, HEAD_DIM, 1), o_c, gcol,
                             tbl_t, bw, past_len)
    o_lat = _samp_mla(page_table, pool_mla.transpose(0, 2, 1), qm_s.reshape(Bd, N_HEADS_B, 256),
                      lat16_s.reshape(Bd, 1, 256))
    ob_s = _samp_uv(o_lat.reshape(Bd, N_HEADS_B * KV_LORA), w['w_uv'])
    new_win = new_win.reshape(Bd, 2, N_KV_A, HEAD_DIM, wl).transpose(0, 4, 1, 2, 3)
    return oa_s.reshape(1, Bd, -1), ob_s.reshape(1, Bd, -1), cmp_s, slc_s, new_win, lat_s


def kernel(x_prompt, x_sample, cache_nsa_cmp, cache_nsa_slc, cache_nsa_win, cache_mla, page_table, c_prompt, c_sample, rel_bias, w_ada, b_ada, norm_attn, norm_ffn, w_in, cmp_pe, cmp_w1, cmp_w2, q_norm, w_q_up, kv_norm, w_kv_up, out_norm_a, out_norm_b, w_out, w_router, router_bias, w_gate_e, w_up_e, w_down_e, w_gate_s, w_up_s, w_down_s, norm_final):
    B, S, D = x_prompt.shape
    Bd = x_sample.shape[0]
    l = 0
    n_mod = B + Bd
    c_all = jnp.pad(jnp.concatenate([c_prompt, c_sample], axis=0), ((0, -n_mod % 8), (0, 0)))
    mod = _adaln(c_all, w_ada[l], b_ada[l]).reshape(-1, 6, D)
    mp = [mod[:B, i][:, None, :] for i in range(6)]
    msm = [mod[B:n_mod, i][None] for i in range(6)]

    w = _inproj_weights(w_in[l], q_norm[l], w_q_up[l], kv_norm[l], w_kv_up[l])
    cw = _compress_weights(cmp_pe[l], cmp_w1[l], cmp_w2[l])
    mw = _merge_weights(out_norm_a[l], out_norm_b[l], w_out[l], norm_ffn[l], w_router[l], w_gate_s[l], w_up_s[l],
                        w_down_s[l])
    tb, lb = _bias_tables(rel_bias)

    cos, sin = _rope_tables(jnp.arange(S))
    qa, cmp32, slc32, win32, kv16, gt, qm, lat32, lat16 = _inproj(
        x_prompt, mp[0], mp[1], norm_attn[l], cos, sin, w, 256)
    nc = S // CMP_STRIDE
    n_cmp = (S - CMP_BLOCK) // CMP_STRIDE + 1
    xc = cmp32.reshape(B, nc, CMP_STRIDE, 4, HEAD_DIM).transpose(0, 3, 1, 2, 4).reshape(B, 4, nc, -1).astype(BF16)
    kcv = _compress(xc, cw, n_cmp)
    oa_p = _nsa_prompt(qa, gt, kv16, kcv, tb, lb)
    ob_p = _mla_prompt(qm, lat16, w['w_uv'])
    xs_p, f_p, sc_p = _merge(x_prompt, oa_p, ob_p, mp[2], mp[3], mp[4], mp[5], mw, 256)

    xs3 = x_sample.reshape(1, Bd, D)
    oa_s, ob_s, cmp_s, slc_s, new_win, lat_s = _sample_mix(
        xs3, msm, cache_nsa_cmp[l], cache_nsa_slc[l], cache_nsa_win[l], cache_mla[l], page_table, norm_attn[l],
        rel_bias, w, cw)
    xs_s, f_s, sc_s = _merge(xs3, oa_s, ob_s, msm[2], msm[3], msm[4], msm[5], mw, Bd)

    n_p = B * S
    f_all = jnp.concatenate([f_p.reshape(n_p, D // LANE, LANE), f_s.reshape(Bd, D // LANE, LANE)], axis=0)
    sc_all = jnp.concatenate([sc_p.reshape(n_p, N_EXPERTS), sc_s.reshape(Bd, N_EXPERTS)], axis=0)
    wts, pos, row_tok, blk_exp, nused = _route(sc_all, router_bias[l])
    yb = _moe_experts(nused, blk_exp, row_tok, f_all, w_gate_e[l], w_up_e[l], w_down_e[l])
    tile_pos = lambda p: p.reshape(-1, MOE_BLOCK, TOP_K).transpose(0, 2, 1).reshape(-1, MOE_BLOCK)
    y_p = _combine(tile_pos(pos[:n_p]), yb, xs_p, mp[5], wts[:n_p].reshape(B, S, TOP_K), norm_final)
    y_s = _combine(tile_pos(pos[n_p:]), yb, xs_s, msm[5], wts[n_p:].reshape(1, Bd, TOP_K), norm_final)

    sh6 = lambda a, b, t: a.reshape(1, b, t, 2, N_KV_A, HEAD_DIM)
    return (y_p, y_s.reshape(Bd, 1, D), sh6(cmp32, B, S), sh6(cmp_s, Bd, 1), sh6(slc32, B, S), sh6(slc_s, Bd, 1),
            sh6(win32[:, S - WINDOW:], B, WINDOW), new_win[None], lat32[None], lat_s.reshape(1, Bd, 1, LATENT_DIM))
```
